```python
import jax, jax.numpy as jnp
from jax import lax
import numpy as np

D_MODEL = 1024
BATCH = 2
SEQ = 8192
DEPTH = 1

GRID_W = 64
D_MIX = 2 * D_MODEL
SSM_HEADS = 16
SSM_HEAD_DIM = 64
D_SSM = SSM_HEADS * SSM_HEAD_DIM
SSM_GROUPS = 2
HEADS_PER_GROUP = SSM_HEADS // SSM_GROUPS
D_STATE = 128
D_CONV = 5
CHUNK = 128
CONV_DIM = D_SSM + 2 * SSM_GROUPS * D_STATE
NA_HEADS = 16
NA_HEAD_DIM = 64
D_NA = NA_HEADS * NA_HEAD_DIM
NA_MAX_ROWS = 8
NA_COLS = 16
D_FF = 4 * D_MODEL
PROJ_DIM = D_SSM + CONV_DIM + 2 * SSM_HEADS + 3 * D_NA
EPS = 1e-5

kernel_name = "hymba_ssd_natten_sqrelu_block"


def rmsnorm(x, w):
    xf = x.astype(jnp.float32)
    y = xf * lax.rsqrt(jnp.mean(xf * xf, axis=-1, keepdims=True) + EPS)
    return (y * w.astype(jnp.float32)).astype(x.dtype)


def depthwise_conv_centred(u, w, b):
    pad_l = (D_CONV - 1) // 2
    out = lax.conv_general_dilated(
        u, w.astype(u.dtype)[:, None, :], window_strides=(1,),
        padding=[(pad_l, D_CONV - 1 - pad_l)],
        dimension_numbers=("NWC", "WIO", "NWC"),
        feature_group_count=u.shape[-1])
    return out + b.astype(u.dtype)


def ssd_chunked(xdt, log_a, b_mat, c_mat):
    bsz, t = xdt.shape[:2]
    nc = t // CHUNK
    xc = xdt.astype(jnp.float32).reshape(bsz, nc, CHUNK, SSM_GROUPS, HEADS_PER_GROUP, SSM_HEAD_DIM)
    ac = log_a.astype(jnp.float32).reshape(bsz, nc, CHUNK, SSM_GROUPS, HEADS_PER_GROUP)
    bc = b_mat.astype(jnp.float32).reshape(bsz, nc, CHUNK, SSM_GROUPS, D_STATE)
    cc = c_mat.astype(jnp.float32).reshape(bsz, nc, CHUNK, SSM_GROUPS, D_STATE)
    a_cum = jnp.cumsum(ac, axis=2)
    seg = a_cum[:, :, :, None] - a_cum[:, :, None, :]
    lower = jnp.tril(jnp.ones((CHUNK, CHUNK), dtype=bool))[None, None, :, :, None, None]
    decay = jnp.exp(jnp.where(lower, seg, -jnp.inf))
    cb = jnp.einsum("bclgn,bcsgn->bclsg", cc, bc)
    y_diag = jnp.einsum("bclsge,bcsgep->bclgep", cb[..., None] * decay, xc)
    decay_to_end = jnp.exp(a_cum[:, :, -1:] - a_cum)
    chunk_states = jnp.einsum("bclgn,bclgep->bcgepn", bc, xc * decay_to_end[..., None])
    chunk_decay = jnp.exp(a_cum[:, :, -1])

    def step(h, inp):
        s_c, d_c = inp
        return d_c[..., None, None] * h + s_c, h

    h0 = jnp.zeros((bsz, SSM_GROUPS, HEADS_PER_GROUP, SSM_HEAD_DIM, D_STATE), jnp.float32)
    _, h_in = lax.scan(step, h0, (jnp.moveaxis(chunk_states, 1, 0), jnp.moveaxis(chunk_decay, 1, 0)))
    h_in = jnp.moveaxis(h_in, 0, 1)
    y_off = jnp.einsum("bclgn,bcgepn->bclgep", cc, h_in) * jnp.exp(a_cum)[..., None]
    return (y_diag + y_off).reshape(bsz, t, SSM_GROUPS, HEADS_PER_GROUP, SSM_HEAD_DIM)


def ssd_mixer(z, xbc, dt_raw, conv_w, conv_b, dt_bias_f, dt_bias_b, a_log_f, a_log_b, d_skip, norm_w):
    bsz, t, _ = xbc.shape
    xbc = jax.nn.silu(depthwise_conv_centred(xbc, conv_w, conv_b))
    xs, b_mat, c_mat = jnp.split(xbc, [D_SSM, D_SSM + SSM_GROUPS * D_STATE], axis=-1)
    xs = xs.astype(jnp.float32).reshape(bsz, t, SSM_GROUPS, HEADS_PER_GROUP, SSM_HEAD_DIM)
    b_mat = b_mat.reshape(bsz, t, SSM_GROUPS, D_STATE)
    c_mat = c_mat.reshape(bsz, t, SSM_GROUPS, D_STATE)
    dt_f_raw, dt_b_raw = jnp.split(dt_raw.astype(jnp.float32), 2, axis=-1)
    dt_f = jax.nn.softplus(dt_f_raw + dt_bias_f.astype(jnp.float32)).reshape(bsz, t, SSM_GROUPS, HEADS_PER_GROUP)
    dt_b = jax.nn.softplus(dt_b_raw + dt_bias_b.astype(jnp.float32)).reshape(bsz, t, SSM_GROUPS, HEADS_PER_GROUP)
    a_f = -jnp.exp(a_log_f.astype(jnp.float32)).reshape(SSM_GROUPS, HEADS_PER_GROUP)
    a_b = -jnp.exp(a_log_b.astype(jnp.float32)).reshape(SSM_GROUPS, HEADS_PER_GROUP)
    flip = lambda u: jnp.flip(u, axis=1)
    y_f = ssd_chunked(xs * dt_f[..., None], dt_f * a_f, b_mat, c_mat)
    y_b = flip(ssd_chunked(flip(xs * dt_b[..., None]), flip(dt_b * a_b), flip(b_mat), flip(c_mat)))
    y = y_f + y_b + d_skip.astype(jnp.float32).reshape(SSM_GROUPS, HEADS_PER_GROUP)[..., None] * xs
    gw = D_SSM // SSM_GROUPS
    g = y.reshape(bsz, t, SSM_GROUPS, gw) * jax.nn.silu(z.astype(jnp.float32)).reshape(bsz, t, SSM_GROUPS, gw)
    g = g * lax.rsqrt(jnp.mean(g * g, axis=-1, keepdims=True) + EPS)
    g = g * norm_w.astype(jnp.float32).reshape(SSM_GROUPS, gw)
    return g.reshape(bsz, t, D_SSM).astype(z.dtype)


def neighbourhood_attention(q, k, v, q_norm_w, k_norm_w, rpb):
    bsz, t, _ = q.shape
    rows = t // GRID_W
    kh = min(NA_MAX_ROWS, rows)
    q = rmsnorm(q.reshape(bsz, t, NA_HEADS, NA_HEAD_DIM), q_norm_w) * (NA_HEAD_DIM ** -0.5)
    k = rmsnorm(k.reshape(bsz, t, NA_HEADS, NA_HEAD_DIM), k_norm_w)
    qg = q.reshape(bsz, rows, GRID_W, NA_HEADS, NA_HEAD_DIM)
    kg = k.reshape(bsz, rows, GRID_W, NA_HEADS, NA_HEAD_DIM)
    vg = v.reshape(bsz, rows, GRID_W, NA_HEADS, NA_HEAD_DIM)
    col = jnp.arange(GRID_W)
    c0 = jnp.clip(col - NA_COLS // 2, 0, GRID_W - NA_COLS)
    col_win = c0[:, None] + jnp.arange(NA_COLS)
    col_rel = col_win - col[:, None] + (NA_COLS - 1)

    def row_block(i):
        r0 = jnp.clip(i - kh // 2, 0, rows - kh)
        q_i = lax.dynamic_index_in_dim(qg, i, axis=1, keepdims=False)
        k_rows = lax.dynamic_slice_in_dim(kg, r0, kh, axis=1)
        v_rows = lax.dynamic_slice_in_dim(vg, r0, kh, axis=1)
        k_win = k_rows[:, :, col_win]
        v_win = v_rows[:, :, col_win]
        row_rel = r0 + jnp.arange(kh) - i + (NA_MAX_ROWS - 1)
        bias = rpb[:, row_rel[None, :, None], col_rel[:, None, :]]
        s = jnp.einsum("bjhd,bajchd->bhjac", q_i, k_win).astype(jnp.float32) + bias.astype(jnp.float32)[None]
        p = jax.nn.softmax(s.reshape(bsz, NA_HEADS, GRID_W, kh * NA_COLS), axis=-1)
        p = p.reshape(bsz, NA_HEADS, GRID_W, kh, NA_COLS).astype(v.dtype)
        return jnp.einsum("bhjac,bajchd->bjhd", p, v_win)

    out = lax.map(row_block, jnp.arange(rows))
    return jnp.moveaxis(out, 0, 1).reshape(bsz, t, D_NA)


def hybrid_layer(x, norm_mix_w, w_in, conv_w, conv_b, dt_bias_fwd, dt_bias_bwd, a_log_fwd, a_log_bwd,
                 d_skip, ssm_norm_w, q_norm_w, k_norm_w, rel_pos_bias, w_out, norm_mlp_w, w_mlp_in, w_mlp_out):
    h = rmsnorm(x, norm_mix_w)
    proj = h @ w_in
    offs = np.cumsum([D_SSM, CONV_DIM, 2 * SSM_HEADS, D_NA, D_NA]).tolist()
    z, xbc, dt_raw, q, k, v = jnp.split(proj, offs, axis=-1)
    y_ssm = ssd_mixer(z, xbc, dt_raw, conv_w, conv_b, dt_bias_fwd, dt_bias_bwd,
                      a_log_fwd, a_log_bwd, d_skip, ssm_norm_w)
    y_na = neighbourhood_attention(q, k, v, q_norm_w, k_norm_w, rel_pos_bias)
    x = x + jnp.concatenate([y_ssm, y_na.astype(y_ssm.dtype)], axis=-1) @ w_out
    h = rmsnorm(x, norm_mlp_w)
    return x + jnp.square(jax.nn.relu(h @ w_mlp_in)) @ w_mlp_out


def setup_inputs(seed: int = 0) -> dict:
    key = jax.random.key(seed)
    ks = jax.random.split(key, 20)
    f32 = jnp.float32
    L = DEPTH
    nrm = lambda k, shape, s: jax.random.normal(k, shape, f32) * s
    dt0 = jnp.exp(jax.random.uniform(ks[4], (L, 2, SSM_HEADS), f32, np.log(1e-3), np.log(1e-1)))
    dt_bias = dt0 + jnp.log(-jnp.expm1(-dt0))
    a_log = jnp.log(jax.random.uniform(ks[5], (L, 2, SSM_HEADS), f32, 1.0, 16.0))
    return {
        "x": jax.random.normal(ks[0], (BATCH, SEQ, D_MODEL), f32),
        "norm_mix_w": 1.0 + nrm(ks[1], (L, D_MODEL), 0.02),
        "w_in": nrm(ks[2], (L, D_MODEL, PROJ_DIM), D_MODEL ** -0.5),
        "conv_w": nrm(ks[3], (L, D_CONV, CONV_DIM), D_CONV ** -0.5),
        "conv_b": nrm(ks[6], (L, CONV_DIM), 0.02),
        "dt_bias_fwd": dt_bias[:, 0],
        "dt_bias_bwd": dt_bias[:, 1],
        "a_log_fwd": a_log[:, 0],
        "a_log_bwd": a_log[:, 1],
        "d_skip": 1.0 + nrm(ks[7], (L, SSM_HEADS), 0.02),
        "ssm_norm_w": 1.0 + nrm(ks[8], (L, D_SSM), 0.02),
        "q_norm_w": 1.0 + nrm(ks[9], (L, NA_HEAD_DIM), 0.02),
        "k_norm_w": 1.0 + nrm(ks[10], (L, NA_HEAD_DIM), 0.02),
        "rel_pos_bias": nrm(ks[11], (L, NA_HEADS, 2 * NA_MAX_ROWS - 1, 2 * NA_COLS - 1), 0.1),
        "w_out": nrm(ks[12], (L, D_MIX, D_MODEL), D_MIX ** -0.5),
        "norm_mlp_w": 1.0 + nrm(ks[13], (L, D_MODEL), 0.02),
        "w_mlp_in": nrm(ks[14], (L, D_MODEL, D_FF), D_MODEL ** -0.5),
        "w_mlp_out": nrm(ks[15], (L, D_FF, D_MODEL), D_FF ** -0.5),
    }


def reference(x, norm_mix_w, w_in, conv_w, conv_b, dt_bias_fwd, dt_bias_bwd, a_log_fwd, a_log_bwd,
              d_skip, ssm_norm_w, q_norm_w, k_norm_w, rel_pos_bias, w_out, norm_mlp_w, w_mlp_in, w_mlp_out):
    for layer in range(DEPTH):
        x = hybrid_layer(x, norm_mix_w[layer], w_in[layer], conv_w[layer], conv_b[layer],
                         dt_bias_fwd[layer], dt_bias_bwd[layer], a_log_fwd[layer], a_log_bwd[layer],
                         d_skip[layer], ssm_norm_w[layer], q_norm_w[layer], k_norm_w[layer],
                         rel_pos_bias[layer], w_out[layer], norm_mlp_w[layer],
                         w_mlp_in[layer], w_mlp_out[layer])
    return x
```

```python
import functools

import numpy as np
import jax
import jax.numpy as jnp
from jax import lax
from jax.experimental import pallas as pl
from jax.experimental.pallas import tpu as pltpu

F32 = jnp.float32
BF16 = jnp.bfloat16

LANES = 128
GRID_W = 64
SSM_HEADS = 16
SSM_HEAD_DIM = 64
D_SSM = SSM_HEADS * SSM_HEAD_DIM
SSM_GROUPS = 2
GROUP_W = D_SSM // SSM_GROUPS
D_STATE = 128
D_CONV = 5
CHUNK = 128
CONV_DIM = D_SSM + 2 * SSM_GROUPS * D_STATE
NA_HEADS = 16
NA_HEAD_DIM = 64
D_NA = NA_HEADS * NA_HEAD_DIM
NA_ROWS = 8
NA_COLS = 16
NA_PAIRS = NA_HEADS // 2
NA_KEYS = NA_ROWS * GRID_W
EPS = 1e-5
MASKED = -1e30
HALO = 8
VMEM_LIMIT = 56 * 1024 * 1024


def _dot(a, b):
    return jnp.dot(a, b, preferred_element_type=F32)


def _dot_nt(a, b):
    return lax.dot_general(a, b, (((1,), (1,)), ((), ())), preferred_element_type=F32)


def _split3(x):
    hi = x.astype(BF16)
    r1 = x - hi.astype(F32)
    mid = r1.astype(BF16)
    lo = (r1 - mid.astype(F32)).astype(BF16)
    return hi, mid, lo


def _dot_sel_rhs(x, sel):
    hi, mid, lo = _split3(x)
    return _dot(hi, sel) + _dot(mid, sel) + _dot(lo, sel)


def _dot_sel_lhs(sel, x):
    hi, mid, lo = _split3(x)
    return _dot(sel, hi) + _dot(sel, mid) + _dot(sel, lo)


def _softplus(x):
    return jnp.maximum(x, 0.0) + jnp.log1p(jnp.exp(-jnp.abs(x)))


def _silu(x):
    return x * (1.0 / (1.0 + jnp.exp(-x)))


def _const_spec(shape):
    nd = len(shape)
    return pl.BlockSpec(shape, lambda *_: (0,) * nd, pipeline_mode=pl.Buffered(1))


def _inproj_body(x_ref, nw_ref, w_ref, z_ref, xbc_ref, q_ref, k_ref, v_ref, dt_ref):
    x = x_ref[...]
    ms = jnp.mean(x * x, axis=-1, keepdims=True)
    h = (x * lax.rsqrt(ms + EPS) * nw_ref[...]).astype(BF16)
    off = 0
    for ref in (z_ref, xbc_ref, q_ref, k_ref, v_ref, dt_ref):
        width = ref.shape[-1]
        ref[...] = _dot(h, w_ref[:, off:off + width]).astype(ref.dtype)
        off += width


def _inproj(x2, norm_w, w_cat, tm):
    n, d = x2.shape
    widths = (D_SSM, CONV_DIM, D_NA, D_NA, D_NA, LANES)
    dtypes = (F32, F32, F32, F32, BF16, F32)
    row = lambda i: (i, 0)
    return pl.pallas_call(
        _inproj_body,
        grid=(n // tm,),
        in_specs=[pl.BlockSpec((tm, d), row), _const_spec((1, d)), _const_spec(w_cat.shape)],
        out_specs=[pl.BlockSpec((tm, w), row) for w in widths],
        out_shape=[jax.ShapeDtypeStruct((n, w), dt) for w, dt in zip(widths, dtypes)],
        compiler_params=pltpu.CompilerParams(
            dimension_semantics=("arbitrary",), vmem_limit_bytes=VMEM_LIMIT),
    )(x2, norm_w, w_cat)


def _tri_masks():
    t = lax.broadcasted_iota(jnp.int32, (CHUNK, CHUNK), 0)
    s = lax.broadcasted_iota(jnp.int32, (CHUNK, CHUNK), 1)
    return s <= t, s >= t


def _dt_and_loga(dt_ref, dtb_ref, alog_ref):
    dtv = _softplus(dt_ref[0] + dtb_ref[...])
    return dtv, dtv * (-jnp.exp(alog_ref[...]))


def _ssd_bwd_body(cur_ref, prev_ref, next_ref, dt_ref, cw_ref, cb_ref, dtb_ref, alog_ref, eb_ref,
                  xc_ref, hb_ref, uext, state):
    c = pl.program_id(1)
    nc = pl.num_programs(1)
    chunk = nc - 1 - c

    @pl.when(c == 0)
    def _():
        state[...] = jnp.zeros_like(state)

    uext[0:HALO, :] = jnp.where(chunk > 0, prev_ref[0], 0.0)
    uext[HALO:HALO + CHUNK, :] = cur_ref[0]
    uext[HALO + CHUNK:2 * HALO + CHUNK, :] = jnp.where(chunk < nc - 1, next_ref[0], 0.0)
    first = HALO - (D_CONV - 1) // 2
    cstep = 2 * LANES
    for col in range(0, CONV_DIM, cstep):
        acc = jnp.broadcast_to(cb_ref[:, col:col + cstep], (CHUNK, cstep))
        for k in range(D_CONV):
            acc = acc + cw_ref[k:k + 1, col:col + cstep] * uext[first + k:first + k + CHUNK, col:col + cstep]
        xc_ref[0, :, col:col + cstep] = _silu(acc)

    dtv, loga = _dt_and_loga(dt_ref, dtb_ref, alog_ref)
    _, upper = _tri_masks()
    rcum = _dot_sel_lhs(jnp.where(upper, 1.0, 0.0).astype(BF16), loga)
    r0 = rcum[0:1, :]
    wgt = dtv * jnp.exp(r0 - rcum)
    wgt_x = _dot_sel_rhs(wgt, eb_ref[...])
    dec_x = _dot_sel_rhs(jnp.broadcast_to(jnp.exp(r0), (HALO, LANES)), eb_ref[...])[0:1]

    hb_ref[0, 0] = state[...].astype(BF16)
    xw = (xc_ref[0, :, 0:D_SSM] * wgt_x).astype(BF16)
    for g in range(SSM_GROUPS):
        gs = slice(g * GROUP_W, (g + 1) * GROUP_W)
        b_t = xc_ref[0, :, D_SSM + g * D_STATE:D_SSM + (g + 1) * D_STATE].T.astype(BF16)
        state[:, gs] = state[:, gs] * dec_x[:, gs] + _dot(b_t, xw[:, gs])


def _ssd_bwd(xbc, dt, conv_w, conv_b, dt_bias, a_log, sel_b):
    bsz, t, _ = xbc.shape
    nc = t // CHUNK
    hpc = CHUNK // HALO
    nhalo = t // HALO
    rev = lambda c: nc - 1 - c
    return pl.pallas_call(
        _ssd_bwd_body,
        grid=(bsz, nc),
        in_specs=[
            pl.BlockSpec((1, CHUNK, CONV_DIM), lambda b, c: (b, rev(c), 0)),
            pl.BlockSpec((1, HALO, CONV_DIM), lambda b, c: (b, jnp.maximum(rev(c) * hpc - 1, 0), 0)),
            pl.BlockSpec((1, HALO, CONV_DIM), lambda b, c: (b, jnp.minimum((rev(c) + 1) * hpc, nhalo - 1), 0)),
            pl.BlockSpec((1, CHUNK, LANES), lambda b, c: (b, rev(c), 0)),
            _const_spec(conv_w.shape), _const_spec(conv_b.shape),
            _const_spec(dt_bias.shape), _const_spec(a_log.shape), _const_spec(sel_b.shape),
        ],
        out_specs=[
            pl.BlockSpec((1, CHUNK, CONV_DIM), lambda b, c: (b, rev(c), 0)),
            pl.BlockSpec((1, 1, D_STATE, D_SSM), lambda b, c: (b, rev(c), 0, 0)),
        ],
        out_shape=[
            jax.ShapeDtypeStruct((bsz, t, CONV_DIM), F32),
            jax.ShapeDtypeStruct((bsz, nc, D_STATE, D_SSM), BF16),
        ],
        scratch_shapes=[pltpu.VMEM((CHUNK + 2 * HALO, CONV_DIM), F32), pltpu.VMEM((D_STATE, D_SSM), F32)],
        compiler_params=pltpu.CompilerParams(
            dimension_semantics=("arbitrary", "arbitrary"), vmem_limit_bytes=VMEM_LIMIT),
    )(xbc, xbc, xbc, dt, conv_w, conv_b, dt_bias, a_log, sel_b)


def _ssd_fwd_body(xc_ref, dt_ref, z_ref, hb_ref, dtb_ref, alog_ref, dsk_ref, nw_ref, ef_ref, eb_ref,
                  y_ref, state, ybuf):
    @pl.when(pl.program_id(1) == 0)
    def _():
        state[...] = jnp.zeros_like(state)

    dtv, loga = _dt_and_loga(dt_ref, dtb_ref, alog_ref)
    lower, upper = _tri_masks()
    cum_f = _dot_sel_lhs(jnp.where(lower, 1.0, 0.0).astype(BF16), loga)
    cum_b = _dot_sel_lhs(jnp.where(upper, 1.0, 0.0).astype(BF16), loga)
    lane = lax.broadcasted_iota(jnp.int32, (CHUNK, LANES), 1)
    cum = jnp.where(lane < SSM_HEADS, cum_f, cum_b)
    ecum = jnp.exp(cum)
    wgt_f = dtv * jnp.exp(cum_f[CHUNK - 1:CHUNK, :] - cum_f)
    ea_x = _dot_sel_rhs(ecum, ef_ref[...])
    er_x = _dot_sel_rhs(ecum, eb_ref[...])
    wf_x = _dot_sel_rhs(wgt_f, ef_ref[...])
    cum_t = cum.T
    dt_t = dtv.T
    first_half = lane < SSM_HEAD_DIM

    for g in range(SSM_GROUPS):
        gs = slice(g * GROUP_W, (g + 1) * GROUP_W)
        b_g = xc_ref[0, :, D_SSM + g * D_STATE:D_SSM + (g + 1) * D_STATE].astype(BF16)
        c0 = D_SSM + SSM_GROUPS * D_STATE + g * D_STATE
        c_g = xc_ref[0, :, c0:c0 + D_STATE].astype(BF16)
        cb = _dot_nt(c_g, b_g)
        y_off = (_dot(c_g, state[:, gs].astype(BF16)) * ea_x[:, gs]
                 + _dot(c_g, hb_ref[0, 0, :, gs]) * er_x[:, gs])
        pairs = GROUP_W // LANES
        for pp in range(pairs):
            pr = g * pairs + pp
            ps = slice(pr * LANES, (pr + 1) * LANES)
            mats = []
            for h in (2 * pr, 2 * pr + 1):
                hb = SSM_HEADS + h
                seg_f = jnp.broadcast_to(cum[:, h:h + 1], (CHUNK, CHUNK)) - cum_t[h:h + 1, :]
                seg_b = jnp.broadcast_to(cum[:, hb:hb + 1], (CHUNK, CHUNK)) - cum_t[hb:hb + 1, :]
                d_f = jnp.exp(jnp.where(lower, seg_f, MASKED)) * dt_t[h:h + 1, :]
                d_b = jnp.exp(jnp.where(upper, seg_b, MASKED)) * dt_t[hb:hb + 1, :]
                mats.append(((d_f + d_b) * cb).astype(BF16))
            xs = xc_ref[0, :, ps]
            xp = xs.astype(BF16)
            zero = jnp.zeros_like(xp)
            rhs = jnp.concatenate([jnp.where(first_half, xp, zero), jnp.where(first_half, zero, xp)], axis=0)
            y_diag = _dot(jnp.concatenate(mats, axis=1), rhs)
            ybuf[:, ps] = y_diag + y_off[:, pp * LANES:(pp + 1) * LANES] + dsk_ref[:, ps] * xs
        gated = ybuf[:, gs] * _silu(z_ref[0, :, gs])
        ms = jnp.mean(gated * gated, axis=-1, keepdims=True)
        y_ref[0, :, gs] = (gated * lax.rsqrt(ms + EPS) * nw_ref[:, gs]).astype(y_ref.dtype)

    xw = (xc_ref[0, :, 0:D_SSM] * wf_x).astype(BF16)
    for g in range(SSM_GROUPS):
        gs = slice(g * GROUP_W, (g + 1) * GROUP_W)
        b_t = xc_ref[0, :, D_SSM + g * D_STATE:D_SSM + (g + 1) * D_STATE].T.astype(BF16)
        state[:, gs] = state[:, gs] * ea_x[CHUNK - 1:CHUNK, gs] + _dot(b_t, xw[:, gs])


def _ssd_fwd(xc, dt, z, hb, dt_bias, a_log, d_skip_x, norm_w, sel_f, sel_b):
    bsz, t, _ = xc.shape
    nc = t // CHUNK
    blk = lambda w: pl.BlockSpec((1, CHUNK, w), lambda b, c: (b, c, 0))
    return pl.pallas_call(
        _ssd_fwd_body,
        grid=(bsz, nc),
        in_specs=[
            blk(CONV_DIM), blk(LANES), blk(D_SSM),
            pl.BlockSpec((1, 1, D_STATE, D_SSM), lambda b, c: (b, c, 0, 0)),
            _const_spec(dt_bias.shape), _const_spec(a_log.shape), _const_spec(d_skip_x.shape),
            _const_spec(norm_w.shape), _const_spec(sel_f.shape), _const_spec(sel_b.shape),
        ],
        out_specs=blk(D_SSM),
        out_shape=jax.ShapeDtypeStruct((bsz, t, D_SSM), BF16),
        scratch_shapes=[pltpu.VMEM((D_STATE, D_SSM), F32), pltpu.VMEM((CHUNK, D_SSM), F32)],
        compiler_params=pltpu.CompilerParams(
            dimension_semantics=("arbitrary", "arbitrary"), vmem_limit_bytes=VMEM_LIMIT),
    )(xc, dt, z, hb, dt_bias, a_log, d_skip_x, norm_w, sel_f, sel_b)


NORM_ROWS = 512


def _na_body(q_ref, k_ref, v_ref, bias_ref, qw_ref, kw_ref, seg_ref, o_ref, qn, kn):
    t = q_ref.shape[1]
    rows = t // GRID_W

    def norm_step(i, carry):
        sl = pl.ds(pl.multiple_of(i * NORM_ROWS, NORM_ROWS), NORM_ROWS)
        for src, w_ref, dst in ((q_ref, qw_ref, qn), (k_ref, kw_ref, kn)):
            xx = src[0, sl, :]
            sq = xx * xx
            hi = sq.astype(BF16)
            lo = (sq - hi.astype(F32)).astype(BF16)
            ssq = _dot(hi, seg_ref[...]) + _dot(lo, seg_ref[...])
            dst[sl, :] = (xx * lax.rsqrt(ssq * (1.0 / NA_HEAD_DIM) + EPS) * w_ref[...]).astype(BF16)
        return carry

    lax.fori_loop(0, t // NORM_ROWS, norm_step, 0)

    first_half = lax.broadcasted_iota(jnp.int32, (GRID_W, LANES), 1) < NA_HEAD_DIM

    def row_step(i, carry):
        r0 = jnp.clip(i - NA_ROWS // 2, 0, rows - NA_ROWS)
        qoff = pl.multiple_of(i * GRID_W, GRID_W)
        koff = pl.multiple_of(r0 * GRID_W, GRID_W)
        qi = qn[pl.ds(qoff, GRID_W), :]
        zero = jnp.zeros_like(qi)
        qs = jnp.concatenate([jnp.where(first_half, qi, zero), jnp.where(first_half, zero, qi)], axis=0)
        s = _dot_nt(qs, kn[pl.ds(koff, NA_KEYS), :]) + bias_ref[0, i - r0]
        m = jnp.max(s, axis=-1, keepdims=True)
        p = jnp.exp(s - m)
        denom = jnp.sum(p, axis=-1, keepdims=True)
        o2 = _dot(p.astype(BF16), v_ref[0, pl.ds(koff, NA_KEYS), :]) * (1.0 / denom)
        o_ref[0, pl.ds(qoff, GRID_W), :] = jnp.where(first_half, o2[0:GRID_W], o2[GRID_W:]).astype(o_ref.dtype)
        return carry

    lax.fori_loop(0, rows, row_step, 0)


def _natten(q, k, v, bias, qw, kw, seg):
    bsz, t, _ = q.shape
    blk = pl.BlockSpec((1, t, LANES), lambda p, b: (b, 0, p))
    return pl.pallas_call(
        _na_body,
        grid=(NA_PAIRS, bsz),
        in_specs=[
            blk, blk, blk,
            pl.BlockSpec((1, NA_ROWS, 2 * GRID_W, NA_KEYS), lambda p, b: (p, 0, 0, 0)),
            _const_spec(qw.shape), _const_spec(kw.shape), _const_spec(seg.shape),
        ],
        out_specs=blk,
        out_shape=jax.ShapeDtypeStruct((bsz, t, D_NA), BF16),
        scratch_shapes=[pltpu.VMEM((t, LANES), BF16), pltpu.VMEM((t, LANES), BF16)],
        compiler_params=pltpu.CompilerParams(
            dimension_semantics=("arbitrary", "arbitrary"), vmem_limit_bytes=VMEM_LIMIT),
    )(q, k, v, bias, qw, kw, seg)


def _na_bias_table(rpb):
    j = np.arange(GRID_W)
    c0 = np.clip(j - NA_COLS // 2, 0, GRID_W - NA_COLS)
    cc = np.arange(GRID_W)
    valid = (cc[None, :] >= c0[:, None]) & (cc[None, :] < c0[:, None] + NA_COLS)
    col_rel = np.clip(cc[None, :] - j[:, None] + NA_COLS - 1, 0, 2 * NA_COLS - 2)
    row_rel = np.arange(NA_ROWS)[None, :] - np.arange(NA_ROWS)[:, None] + NA_ROWS - 1
    tab = rpb[:, row_rel[:, :, None, None], col_rel[None, None, :, :]]
    tab = jnp.where(valid[None, None, None], tab.astype(F32), MASKED)
    tab = tab.transpose(0, 1, 3, 2, 4).reshape(NA_PAIRS, 2, NA_ROWS, GRID_W, NA_KEYS)
    return tab.transpose(0, 2, 1, 3, 4).reshape(NA_PAIRS, NA_ROWS, 2 * GRID_W, NA_KEYS)


FF_STEP = 512


def _outmlp_body(x_ref, ys_ref, yn_ref, wo_ref, nw_ref, w1_ref, w2_ref, o_ref):
    dmix = ys_ref.shape[-1]
    x1 = x_ref[...] + _dot(ys_ref[...], wo_ref[0:dmix, :]) + _dot(yn_ref[...], wo_ref[dmix:, :])
    ms = jnp.mean(x1 * x1, axis=-1, keepdims=True)
    h = (x1 * lax.rsqrt(ms + EPS) * nw_ref[...]).astype(BF16)
    o_ref[...] = x1
    for f in range(0, w1_ref.shape[1], FF_STEP):
        u = jnp.maximum(_dot(h, w1_ref[:, f:f + FF_STEP]), 0.0)
        o_ref[...] += _dot((u * u).astype(BF16), w2_ref[f:f + FF_STEP, :])


def _outmlp(x2, ys, yn, w_out, norm_w, w1, w2, tm):
    n, d = x2.shape
    row = lambda i: (i, 0)
    return pl.pallas_call(
        _outmlp_body,
        grid=(n // tm,),
        in_specs=[
            pl.BlockSpec((tm, d), row), pl.BlockSpec((tm, ys.shape[1]), row), pl.BlockSpec((tm, yn.shape[1]), row),
            _const_spec(w_out.shape), _const_spec((1, d)), _const_spec(w1.shape), _const_spec(w2.shape),
        ],
        out_specs=pl.BlockSpec((tm, d), row),
        out_shape=jax.ShapeDtypeStruct((n, d), F32),
        compiler_params=pltpu.CompilerParams(
            dimension_semantics=("arbitrary",), vmem_limit_bytes=VMEM_LIMIT),
    )(x2, ys, yn, w_out, norm_w, w1, w2)


def _head_select(first_lane):
    sel = np.zeros((LANES, D_SSM), np.float32)
    for h in range(SSM_HEADS):
        sel[first_lane + h, h * SSM_HEAD_DIM:(h + 1) * SSM_HEAD_DIM] = 1.0
    return jnp.asarray(sel, BF16)


def _pad_lanes(v):
    return jnp.pad(v.astype(F32), (0, LANES - v.shape[0]))[None, :]


def _layer(x, norm_mix_w, w_in, conv_w, conv_b, dt_bias_fwd, dt_bias_bwd, a_log_fwd, a_log_bwd, d_skip,
           ssm_norm_w, q_norm_w, k_norm_w, rel_pos_bias, w_out, norm_mlp_w, w_mlp_in, w_mlp_out, tm):
    bsz, t, d = x.shape
    n = bsz * t
    x2 = x.reshape(n, d)

    o_xbc, o_dt = D_SSM, D_SSM + CONV_DIM
    o_q = o_dt + 2 * SSM_HEADS
    w_cat = jnp.concatenate(
        [w_in[:, :o_dt], w_in[:, o_q:], w_in[:, o_dt:o_q], jnp.zeros((d, LANES - 2 * SSM_HEADS), w_in.dtype)],
        axis=1).astype(BF16)
    z, xbc, q, k, v, dt = _inproj(x2, norm_mix_w[None, :], w_cat, tm)
    r3 = lambda a: a.reshape(bsz, t, a.shape[-1])

    dt_bias = _pad_lanes(jnp.concatenate([dt_bias_fwd, dt_bias_bwd]))
    a_log = _pad_lanes(jnp.concatenate([a_log_fwd, a_log_bwd]))
    sel_f, sel_b = _head_select(0), _head_select(SSM_HEADS)
    xc, hb = _ssd_bwd(r3(xbc), r3(dt), conv_w, conv_b[None, :], dt_bias, a_log, sel_b)
    y_ssm = _ssd_fwd(xc, r3(dt), r3(z), hb, dt_bias, a_log,
                     jnp.repeat(d_skip.astype(F32), SSM_HEAD_DIM)[None, :], ssm_norm_w[None, :], sel_f, sel_b)

    lane_head = np.arange(LANES) // NA_HEAD_DIM
    seg = jnp.asarray(lane_head[:, None] == lane_head[None, :], BF16)
    qw = (jnp.tile(q_norm_w.astype(F32), 2) * (NA_HEAD_DIM ** -0.5))[None, :]
    kw = jnp.tile(k_norm_w.astype(F32), 2)[None, :]
    y_na = _natten(r3(q), r3(k), r3(v), _na_bias_table(rel_pos_bias), qw, kw, seg)

    out = _outmlp(x2, y_ssm.reshape(n, D_SSM), y_na.reshape(n, D_NA), w_out.astype(BF16),
                  norm_mlp_w[None, :], w_mlp_in.astype(BF16), w_mlp_out.astype(BF16), tm)
    return out.reshape(bsz, t, d)


def kernel(x, norm_mix_w, w_in, conv_w, conv_b, dt_bias_fwd, dt_bias_bwd, a_log_fwd, a_log_bwd, d_skip,
           ssm_norm_w, q_norm_w, k_norm_w, rel_pos_bias, w_out, norm_mlp_w, w_mlp_in, w_mlp_out):
    tm = min(512, x.shape[0] * x.shape[1])
    for layer in range(norm_mix_w.shape[0]):
        x = _layer(x, norm_mix_w[layer], w_in[layer], conv_w[layer], conv_b[layer], dt_bias_fwd[layer],
                   dt_bias_bwd[layer], a_log_fwd[layer], a_log_bwd[layer], d_skip[layer], ssm_norm_w[layer],
                   q_norm_w[layer], k_norm_w[layer], rel_pos_bias[layer], w_out[layer], norm_mlp_w[layer],
                   w_mlp_in[layer], w_mlp_out[layer], tm)
    return x
```

```python
import functools

import numpy as np
import jax
import jax.numpy as jnp
from jax import lax
from jax.experimental import pallas as pl
from jax.experimental.pallas import tpu as pltpu

F32 = jnp.float32
BF16 = jnp.bfloat16

LANES = 128
GRID_W = 64
SSM_HEADS = 16
SSM_HEAD_DIM = 64
D_SSM = SSM_HEADS * SSM_HEAD_DIM
SSM_GROUPS = 2
GROUP_W = D_SSM // SSM_GROUPS
D_STATE = 128
D_CONV = 5
CHUNK = 128
CONV_DIM = D_SSM + 2 * SSM_GROUPS * D_STATE
NA_HEADS = 16
NA_HEAD_DIM = 64
D_NA = NA_HEADS * NA_HEAD_DIM
NA_ROWS = 8
NA_COLS = 16
NA_PAIRS = NA_HEADS // 2
NA_KEYS = NA_ROWS * GRID_W
EPS = 1e-5
MASKED = -1e30
HALO = 8
VMEM_LIMIT = 56 * 1024 * 1024


def _dot(a, b):
    return jnp.dot(a, b, preferred_element_type=F32)


def _dot_nt(a, b):
    return lax.dot_general(a, b, (((1,), (1,)), ((), ())), preferred_element_type=F32)


def _split3(x):
    hi = x.astype(BF16)
    r1 = x - hi.astype(F32)
    mid = r1.astype(BF16)
    lo = (r1 - mid.astype(F32)).astype(BF16)
    return hi, mid, lo


def _dot_sel_rhs(x, sel):
    hi, mid, lo = _split3(x)
    return _dot(hi, sel) + _dot(mid, sel) + _dot(lo, sel)


def _dot_sel_lhs(sel, x):
    hi, mid, lo = _split3(x)
    return _dot(sel, hi) + _dot(sel, mid) + _dot(sel, lo)


def _softplus(x):
    return jnp.maximum(x, 0.0) + jnp.log1p(jnp.exp(-jnp.abs(x)))


def _silu(x):
    return x * (1.0 / (1.0 + jnp.exp(-x)))


def _const_spec(shape):
    nd = len(shape)
    return pl.BlockSpec(shape, lambda *_: (0,) * nd, pipeline_mode=pl.Buffered(1))


def _inproj_body(x_ref, nw_ref, w_ref, z_ref, xbc_ref, q_ref, k_ref, v_ref, dt_ref):
    x = x_ref[...]
    ms = jnp.mean(x * x, axis=-1, keepdims=True)
    h = (x * lax.rsqrt(ms + EPS) * nw_ref[...]).astype(BF16)
    off = 0
    for ref in (z_ref, xbc_ref, q_ref, k_ref, v_ref, dt_ref):
        width = ref.shape[-1]
        ref[...] = _dot(h, w_ref[:, off:off + width]).astype(ref.dtype)
        off += width


def _inproj(x2, norm_w, w_cat, tm):
    n, d = x2.shape
    widths = (D_SSM, CONV_DIM, D_NA, D_NA, D_NA, LANES)
    dtypes = (F32, F32, F32, F32, BF16, F32)
    row = lambda i: (i, 0)
    return pl.pallas_call(
        _inproj_body,
        grid=(n // tm,),
        in_specs=[pl.BlockSpec((tm, d), row), _const_spec((1, d)), _const_spec(w_cat.shape)],
        out_specs=[pl.BlockSpec((tm, w), row) for w in widths],
        out_shape=[jax.ShapeDtypeStruct((n, w), dt) for w, dt in zip(widths, dtypes)],
        compiler_params=pltpu.CompilerParams(
            dimension_semantics=("arbitrary",), vmem_limit_bytes=VMEM_LIMIT),
    )(x2, norm_w, w_cat)


def _tri_masks():
    t = lax.broadcasted_iota(jnp.int32, (CHUNK, CHUNK), 0)
    s = lax.broadcasted_iota(jnp.int32, (CHUNK, CHUNK), 1)
    return s <= t, s >= t


def _dt_and_loga(dt_ref, dtb_ref, alog_ref):
    dtv = _softplus(dt_ref[0] + dtb_ref[...])
    return dtv, dtv * (-jnp.exp(alog_ref[...]))


def _ssd_bwd_body(cur_ref, prev_ref, next_ref, dt_ref, cw_ref, cb_ref, dtb_ref, alog_ref, eb_ref,
                  xc_ref, hb_ref, uext, state):
    c = pl.program_id(1)
    nc = pl.num_programs(1)
    chunk = nc - 1 - c

    @pl.when(c == 0)
    def _():
        state[...] = jnp.zeros_like(state)

    uext[0:HALO, :] = jnp.where(chunk > 0, prev_ref[0], 0.0)
    uext[HALO:HALO + CHUNK, :] = cur_ref[0]
    uext[HALO + CHUNK:2 * HALO + CHUNK, :] = jnp.where(chunk < nc - 1, next_ref[0], 0.0)
    first = HALO - (D_CONV - 1) // 2
    cstep = 2 * LANES
    for col in range(0, CONV_DIM, cstep):
        acc = jnp.broadcast_to(cb_ref[:, col:col + cstep], (CHUNK, cstep))
        for k in range(D_CONV):
            acc = acc + cw_ref[k:k + 1, col:col + cstep] * uext[first + k:first + k + CHUNK, col:col + cstep]
        xc_ref[0, :, col:col + cstep] = _silu(acc)

    dtv, loga = _dt_and_loga(dt_ref, dtb_ref, alog_ref)
    _, upper = _tri_masks()
    rcum = _dot_sel_lhs(jnp.where(upper, 1.0, 0.0).astype(BF16), loga)
    r0 = rcum[0:1, :]
    wgt = dtv * jnp.exp(r0 - rcum)
    wgt_x = _dot_sel_rhs(wgt, eb_ref[...])
    dec_x = _dot_sel_rhs(jnp.broadcast_to(jnp.exp(r0), (HALO, LANES)), eb_ref[...])[0:1]

    hb_ref[0, 0] = state[...].astype(BF16)
    xw = (xc_ref[0, :, 0:D_SSM] * wgt_x).astype(BF16)
    for g in range(SSM_GROUPS):
        gs = slice(g * GROUP_W, (g + 1) * GROUP_W)
        b_t = xc_ref[0, :, D_SSM + g * D_STATE:D_SSM + (g + 1) * D_STATE].T.astype(BF16)
        state[:, gs] = state[:, gs] * dec_x[:, gs] + _dot(b_t, xw[:, gs])


def _ssd_bwd(xbc, dt, conv_w, conv_b, dt_bias, a_log, sel_b):
    bsz, t, _ = xbc.shape
    nc = t // CHUNK
    hpc = CHUNK // HALO
    nhalo = t // HALO
    rev = lambda c: nc - 1 - c
    return pl.pallas_call(
        _ssd_bwd_body,
        grid=(bsz, nc),
        in_specs=[
            pl.BlockSpec((1, CHUNK, CONV_DIM), lambda b, c: (b, rev(c), 0)),
            pl.BlockSpec((1, HALO, CONV_DIM), lambda b, c: (b, jnp.maximum(rev(c) * hpc - 1, 0), 0)),
            pl.BlockSpec((1, HALO, CONV_DIM), lambda b, c: (b, jnp.minimum((rev(c) + 1) * hpc, nhalo - 1), 0)),
            pl.BlockSpec((1, CHUNK, LANES), lambda b, c: (b, rev(c), 0)),
            _const_spec(conv_w.shape), _const_spec(conv_b.shape),
            _const_spec(dt_bias.shape), _const_spec(a_log.shape), _const_spec(sel_b.shape),
        ],
        out_specs=[
            pl.BlockSpec((1, CHUNK, CONV_DIM), lambda b, c: (b, rev(c), 0)),
            pl.BlockSpec((1, 1, D_STATE, D_SSM), lambda b, c: (b, rev(c), 0, 0)),
        ],
        out_shape=[
            jax.ShapeDtypeStruct((bsz, t, CONV_DIM), F32),
            jax.ShapeDtypeStruct((bsz, nc, D_STATE, D_SSM), BF16),
        ],
        scratch_shapes=[pltpu.VMEM((CHUNK + 2 * HALO, CONV_DIM), F32), pltpu.VMEM((D_STATE, D_SSM), F32)],
        compiler_params=pltpu.CompilerParams(
            dimension_semantics=("arbitrary", "arbitrary"), vmem_limit_bytes=VMEM_LIMIT),
    )(xbc, xbc, xbc, dt, conv_w, conv_b, dt_bias, a_log, sel_b)


def _ssd_fwd_body(xc_ref, dt_ref, z_ref, hb_ref, dtb_ref, alog_ref, dsk_ref, nw_ref, ef_ref, eb_ref,
                  y_ref, state, ybuf):
    @pl.when(pl.program_id(1) == 0)
    def _():
        state[...] = jnp.zeros_like(state)

    dtv, loga = _dt_and_loga(dt_ref, dtb_ref, alog_ref)
    lower, upper = _tri_masks()
    cum_f = _dot_sel_lhs(jnp.where(lower, 1.0, 0.0).astype(BF16), loga)
    cum_b = _dot_sel_lhs(jnp.where(upper, 1.0, 0.0).astype(BF16), loga)
    lane = lax.broadcasted_iota(jnp.int32, (CHUNK, LANES), 1)
    cum = jnp.where(lane < SSM_HEADS, cum_f, cum_b)
    ecum = jnp.exp(cum)
    wgt_f = dtv * jnp.exp(cum_f[CHUNK - 1:CHUNK, :] - cum_f)
    ea_x = _dot_sel_rhs(ecum, ef_ref[...])
    er_x = _dot_sel_rhs(ecum, eb_ref[...])
    wf_x = _dot_sel_rhs(wgt_f, ef_ref[...])
    cum_t = cum.T
    dt_t = dtv.T
    first_half = lane < SSM_HEAD_DIM

    for g in range(SSM_GROUPS):
        gs = slice(g * GROUP_W, (g + 1) * GROUP_W)
        b_g = xc_ref[0, :, D_SSM + g * D_STATE:D_SSM + (g + 1) * D_STATE].astype(BF16)
        c0 = D_SSM + SSM_GROUPS * D_STATE + g * D_STATE
        c_g = xc_ref[0, :, c0:c0 + D_STATE].astype(BF16)
        cb = _dot_nt(c_g, b_g)
        y_off = (_dot(c_g, state[:, gs].astype(BF16)) * ea_x[:, gs]
                 + _dot(c_g, hb_ref[0, 0, :, gs]) * er_x[:, gs])
        pairs = GROUP_W // LANES
        for pp in range(pairs):
            pr = g * pairs + pp
            ps = slice(pr * LANES, (pr + 1) * LANES)
            mats = []
            for h in (2 * pr, 2 * pr + 1):
                hb = SSM_HEADS + h
                seg_f = jnp.broadcast_to(cum[:, h:h + 1], (CHUNK, CHUNK)) - cum_t[h:h + 1, :]
                seg_b = jnp.broadcast_to(cum[:, hb:hb + 1], (CHUNK, CHUNK)) - cum_t[hb:hb + 1, :]
                d_f = jnp.exp(jnp.where(lower, seg_f, MASKED)) * dt_t[h:h + 1, :]
                d_b = jnp.exp(jnp.where(upper, seg_b, MASKED)) * dt_t[hb:hb + 1, :]
                mats.append(((d_f + d_b) * cb).astype(BF16))
            xs = xc_ref[0, :, ps]
            xp = xs.astype(BF16)
            zero = jnp.zeros_like(xp)
            rhs = jnp.concatenate([jnp.where(first_half, xp, zero), jnp.where(first_half, zero, xp)], axis=0)
            y_diag = _dot(jnp.concatenate(mats, axis=1), rhs)
            ybuf[:, ps] = y_diag + y_off[:, pp * LANES:(pp + 1) * LANES] + dsk_ref[:, ps] * xs
        gated = ybuf[:, gs] * _silu(z_ref[0, :, gs])
        ms = jnp.mean(gated * gated, axis=-1, keepdims=True)
        y_ref[0, :, gs] = (gated * lax.rsqrt(ms + EPS) * nw_ref[:, gs]).astype(y_ref.dtype)

    xw = (xc_ref[0, :, 0:D_SSM] * wf_x).astype(BF16)
    for g in range(SSM_GROUPS):
        gs = slice(g * GROUP_W, (g + 1) * GROUP_W)
        b_t = xc_ref[0, :, D_SSM + g * D_STATE:D_SSM + (g + 1) * D_STATE].T.astype(BF16)
        state[:, gs] = state[:, gs] * ea_x[CHUNK - 1:CHUNK, gs] + _dot(b_t, xw[:, gs])


def _ssd_fwd(xc, dt, z, hb, dt_bias, a_log, d_skip_x, norm_w, sel_f, sel_b):
    bsz, t, _ = xc.shape
    nc = t // CHUNK
    blk = lambda w: pl.BlockSpec((1, CHUNK, w), lambda b, c: (b, c, 0))
    return pl.pallas_call(
        _ssd_fwd_body,
        grid=(bsz, nc),
        in_specs=[
            blk(CONV_DIM), blk(LANES), blk(D_SSM),
            pl.BlockSpec((1, 1, D_STATE, D_SSM), lambda b, c: (b, c, 0, 0)),
            _const_spec(dt_bias.shape), _const_spec(a_log.shape), _const_spec(d_skip_x.shape),
            _const_spec(norm_w.shape), _const_spec(sel_f.shape), _const_spec(sel_b.shape),
        ],
        out_specs=blk(D_SSM),
        out_shape=jax.ShapeDtypeStruct((bsz, t, D_SSM), BF16),
        scratch_shapes=[pltpu.VMEM((D_STATE, D_SSM), F32), pltpu.VMEM((CHUNK, D_SSM), F32)],
        compiler_params=pltpu.CompilerParams(
            dimension_semantics=("arbitrary", "arbitrary"), vmem_limit_bytes=VMEM_LIMIT),
    )(xc, dt, z, hb, dt_bias, a_log, d_skip_x, norm_w, sel_f, sel_b)


NORM_ROWS = 512


def _na_body(q_ref, k_ref, v_ref, bias_ref, qw_ref, kw_ref, seg_ref, o_ref, qn, kn):
    t = q_ref.shape[1]
    rows = t // GRID_W

    def norm_step(i, carry):
        sl = pl.ds(pl.multiple_of(i * NORM_ROWS, NORM_ROWS), NORM_ROWS)
        for src, w_ref, dst in ((q_ref, qw_ref, qn), (k_ref, kw_ref, kn)):
            xx = src[0, sl, :]
            sq = xx * xx
            hi = sq.astype(BF16)
            lo = (sq - hi.astype(F32)).astype(BF16)
            ssq = _dot(hi, seg_ref[...]) + _dot(lo, seg_ref[...])
            dst[sl, :] = (xx * lax.rsqrt(ssq * (1.0 / NA_HEAD_DIM) + EPS) * w_ref[...]).astype(BF16)
        return carry

    lax.fori_loop(0, t // NORM_ROWS, norm_step, 0)

    first_half = lax.broadcasted_iota(jnp.int32, (GRID_W, LANES), 1) < NA_HEAD_DIM

    def row_step(i, carry):
        r0 = jnp.clip(i - NA_ROWS // 2, 0, rows - NA_ROWS)
        qoff = pl.multiple_of(i * GRID_W, GRID_W)
        koff = pl.multiple_of(r0 * GRID_W, GRID_W)
        qi = qn[pl.ds(qoff, GRID_W), :]
        zero = jnp.zeros_like(qi)
        qs = jnp.concatenate([jnp.where(first_half, qi, zero), jnp.where(first_half, zero, qi)], axis=0)
        s = _dot_nt(qs, kn[pl.ds(koff, NA_KEYS), :]) + bias_ref[0, i - r0]
        m = jnp.max(s, axis=-1, keepdims=True)
        p = jnp.exp(s - m)
        denom = jnp.sum(p, axis=-1, keepdims=True)
        o2 = _dot(p.astype(BF16), v_ref[0, pl.ds(koff, NA_KEYS), :]) * (1.0 / denom)
        o_ref[0, pl.ds(qoff, GRID_W), :] = jnp.where(first_half, o2[0:GRID_W], o2[GRID_W:]).astype(o_ref.dtype)
        return carry

    lax.fori_loop(0, rows, row_step, 0, unroll=4)


def _natten(q, k, v, bias, qw, kw, seg):
    bsz, t, _ = q.shape
    blk = pl.BlockSpec((1, t, LANES), lambda p, b: (b, 0, p))
    return pl.pallas_call(
        _na_body,
        grid=(NA_PAIRS, bsz),
        in_specs=[
            blk, blk, blk,
            pl.BlockSpec((1, NA_ROWS, 2 * GRID_W, NA_KEYS), lambda p, b: (p, 0, 0, 0)),
            _const_spec(qw.shape), _const_spec(kw.shape), _const_spec(seg.shape),
        ],
        out_specs=blk,
        out_shape=jax.ShapeDtypeStruct((bsz, t, D_NA), BF16),
        scratch_shapes=[pltpu.VMEM((t, LANES), BF16), pltpu.VMEM((t, LANES), BF16)],
        compiler_params=pltpu.CompilerParams(
            dimension_semantics=("arbitrary", "arbitrary"), vmem_limit_bytes=VMEM_LIMIT),
    )(q, k, v, bias, qw, kw, seg)


def _bias_rows_body(rpb_ref, onehot_ref, mask_ref, o_ref):
    o_ref[0] = _dot_sel_rhs(rpb_ref[0], onehot_ref[...]) + mask_ref[...]


def _na_bias_table(rpb):
    j = np.arange(GRID_W)
    c0 = np.clip(j - NA_COLS // 2, 0, GRID_W - NA_COLS)
    cc = np.arange(GRID_W)
    valid = (cc[None, :] >= c0[:, None]) & (cc[None, :] < c0[:, None] + NA_COLS)
    col_rel = cc[None, :] - j[:, None] + NA_COLS - 1
    onehot = (np.arange(LANES)[:, None, None] == col_rel[None]) & valid[None]
    onehot = jnp.asarray(onehot.reshape(LANES, GRID_W * GRID_W), BF16)
    mask = jnp.asarray(np.where(valid, 0.0, MASKED).reshape(1, GRID_W * GRID_W), F32)
    nrel = 2 * NA_ROWS
    rpb_p = jnp.pad(rpb.astype(F32), ((0, 0), (0, nrel - rpb.shape[1]), (0, LANES - rpb.shape[2])))
    rows = pl.pallas_call(
        _bias_rows_body,
        grid=(NA_HEADS,),
        in_specs=[pl.BlockSpec((1, nrel, LANES), lambda h: (h, 0, 0)),
                  _const_spec(onehot.shape), _const_spec(mask.shape)],
        out_specs=pl.BlockSpec((1, nrel, GRID_W * GRID_W), lambda h: (h, 0, 0)),
        out_shape=jax.ShapeDtypeStruct((NA_HEADS, nrel, GRID_W * GRID_W), F32),
        compiler_params=pltpu.CompilerParams(dimension_semantics=("arbitrary",)),
    )(rpb_p, onehot, mask)
    rows = rows.reshape(NA_HEADS, nrel, GRID_W, GRID_W)
    tab = jnp.stack([rows[:, NA_ROWS - 1 - dd:2 * NA_ROWS - 1 - dd] for dd in range(NA_ROWS)], axis=1)
    tab = tab.transpose(0, 1, 3, 2, 4).reshape(NA_PAIRS, 2, NA_ROWS, GRID_W, NA_KEYS)
    return tab.transpose(0, 2, 1, 3, 4).reshape(NA_PAIRS, NA_ROWS, 2 * GRID_W, NA_KEYS)


FF_STEP = 512


def _outmlp_body(x_ref, ys_ref, yn_ref, wo_ref, nw_ref, w1_ref, w2_ref, o_ref):
    dmix = ys_ref.shape[-1]
    x1 = x_ref[...] + _dot(ys_ref[...], wo_ref[0:dmix, :]) + _dot(yn_ref[...], wo_ref[dmix:, :])
    ms = jnp.mean(x1 * x1, axis=-1, keepdims=True)
    h = (x1 * lax.rsqrt(ms + EPS) * nw_ref[...]).astype(BF16)
    o_ref[...] = x1
    for f in range(0, w1_ref.shape[1], FF_STEP):
        u = jnp.maximum(_dot(h, w1_ref[:, f:f + FF_STEP]), 0.0)
        o_ref[...] += _dot((u * u).astype(BF16), w2_ref[f:f + FF_STEP, :])


def _outmlp(x2, ys, yn, w_out, norm_w, w1, w2, tm):
    n, d = x2.shape
    row = lambda i: (i, 0)
    return pl.pallas_call(
        _outmlp_body,
        grid=(n // tm,),
        in_specs=[
            pl.BlockSpec((tm, d), row), pl.BlockSpec((tm, ys.shape[1]), row), pl.BlockSpec((tm, yn.shape[1]), row),
            _const_spec(w_out.shape), _const_spec((1, d)), _const_spec(w1.shape), _const_spec(w2.shape),
        ],
        out_specs=pl.BlockSpec((tm, d), row),
        out_shape=jax.ShapeDtypeStruct((n, d), F32),
        compiler_params=pltpu.CompilerParams(
            dimension_semantics=("arbitrary",), vmem_limit_bytes=VMEM_LIMIT),
    )(x2, ys, yn, w_out, norm_w, w1, w2)


def _head_select(first_lane):
    sel = np.zeros((LANES, D_SSM), np.float32)
    for h in range(SSM_HEADS):
        sel[first_lane + h, h * SSM_HEAD_DIM:(h + 1) * SSM_HEAD_DIM] = 1.0
    return jnp.asarray(sel, BF16)


def _pad_lanes(v):
    return jnp.pad(v.astype(F32), (0, LANES - v.shape[0]))[None, :]


def _layer(x, norm_mix_w, w_in, conv_w, conv_b, dt_bias_fwd, dt_bias_bwd, a_log_fwd, a_log_bwd, d_skip,
           ssm_norm_w, q_norm_w, k_norm_w, rel_pos_bias, w_out, norm_mlp_w, w_mlp_in, w_mlp_out, tm):
    bsz, t, d = x.shape
    n = bsz * t
    x2 = x.reshape(n, d)

    o_xbc, o_dt = D_SSM, D_SSM + CONV_DIM
    o_q = o_dt + 2 * SSM_HEADS
    w_cat = jnp.concatenate(
        [w_in[:, :o_dt], w_in[:, o_q:], w_in[:, o_dt:o_q], jnp.zeros((d, LANES - 2 * SSM_HEADS), w_in.dtype)],
        axis=1).astype(BF16)
    z, xbc, q, k, v, dt = _inproj(x2, norm_mix_w[None, :], w_cat, tm)
    r3 = lambda a: a.reshape(bsz, t, a.shape[-1])

    dt_bias = _pad_lanes(jnp.concatenate([dt_bias_fwd, dt_bias_bwd]))
    a_log = _pad_lanes(jnp.concatenate([a_log_fwd, a_log_bwd]))
    sel_f, sel_b = _head_select(0), _head_select(SSM_HEADS)
    xc, hb = _ssd_bwd(r3(xbc), r3(dt), conv_w, conv_b[None, :], dt_bias, a_log, sel_b)
    y_ssm = _ssd_fwd(xc, r3(dt), r3(z), hb, dt_bias, a_log,
                     jnp.repeat(d_skip.astype(F32), SSM_HEAD_DIM)[None, :], ssm_norm_w[None, :], sel_f, sel_b)

    lane_head = np.arange(LANES) // NA_HEAD_DIM
    seg = jnp.asarray(lane_head[:, None] == lane_head[None, :], BF16)
    qw = (jnp.tile(q_norm_w.astype(F32), 2) * (NA_HEAD_DIM ** -0.5))[None, :]
    kw = jnp.tile(k_norm_w.astype(F32), 2)[None, :]
    y_na = _natten(r3(q), r3(k), r3(v), _na_bias_table(rel_pos_bias), qw, kw, seg)

    out = _outmlp(x2, y_ssm.reshape(n, D_SSM), y_na.reshape(n, D_NA), w_out.astype(BF16),
                  norm_mlp_w[None, :], w_mlp_in.astype(BF16), w_mlp_out.astype(BF16), tm)
    return out.reshape(bsz, t, d)


def kernel(x, norm_mix_w, w_in, conv_w, conv_b, dt_bias_fwd, dt_bias_bwd, a_log_fwd, a_log_bwd, d_skip,
           ssm_norm_w, q_norm_w, k_norm_w, rel_pos_bias, w_out, norm_mlp_w, w_mlp_in, w_mlp_out):
    tm = min(512, x.shape[0] * x.shape[1])
    for layer in range(norm_mix_w.shape[0]):
        x = _layer(x, norm_mix_w[layer], w_in[layer], conv_w[layer], conv_b[layer], dt_bias_fwd[layer],
                   dt_bias_bwd[layer], a_log_fwd[layer], a_log_bwd[layer], d_skip[layer], ssm_norm_w[layer],
                   q_norm_w[layer], k_norm_w[layer], rel_pos_bias[layer], w_out[layer], norm_mlp_w[layer],
                   w_mlp_in[layer], w_mlp_out[layer], tm)
    return x
```

```python
import functools

import numpy as np
import jax
import jax.numpy as jnp
from jax import lax
from jax.experimental import pallas as pl
from jax.experimental.pallas import tpu as pltpu

F32 = jnp.float32
BF16 = jnp.bfloat16

LANES = 128
GRID_W = 64
SSM_HEADS = 16
SSM_HEAD_DIM = 64
D_SSM = SSM_HEADS * SSM_HEAD_DIM
SSM_GROUPS = 2
GROUP_W = D_SSM // SSM_GROUPS
D_STATE = 128
D_CONV = 5
CHUNK = 128
CONV_DIM = D_SSM + 2 * SSM_GROUPS * D_STATE
DT_LANES = 2 * SSM_HEADS
DT_COPIES = 3
NA_HEADS = 16
NA_HEAD_DIM = 64
D_NA = NA_HEADS * NA_HEAD_DIM
NA_ROWS = 8
NA_COLS = 16
NA_PAIRS = NA_HEADS // 2
NA_BLOCK = 4
NA_SPAN = NA_ROWS + NA_BLOCK
NA_BKEYS = NA_SPAN * GRID_W
LOG2E = 1.4426950408889634
EPS = 1e-5
MASKED = -1e30
HALO = 8
VMEM_LIMIT = 56 * 1024 * 1024


def _dot(a, b):
    return jnp.dot(a, b, preferred_element_type=F32)


def _dot_nt(a, b):
    return lax.dot_general(a, b, (((1,), (1,)), ((), ())), preferred_element_type=F32)


def _split3(x):
    hi = x.astype(BF16)
    r1 = x - hi.astype(F32)
    mid = r1.astype(BF16)
    lo = (r1 - mid.astype(F32)).astype(BF16)
    return hi, mid, lo


def _dot_sel_rhs(x, sel):
    hi, mid, lo = _split3(x)
    return _dot(hi, sel) + _dot(mid, sel) + _dot(lo, sel)


def _dot_sel_lhs(sel, x):
    hi, mid, lo = _split3(x)
    return _dot(sel, hi) + _dot(sel, mid) + _dot(sel, lo)


def _pack3(x):
    lane = lax.broadcasted_iota(jnp.int32, x.shape, x.ndim - 1)
    hi = x.astype(BF16).astype(F32)
    r1 = x - hi
    mid = r1.astype(BF16).astype(F32)
    return jnp.where(lane < DT_LANES, hi, jnp.where(lane < 2 * DT_LANES, mid, r1 - mid)).astype(BF16)


def _softplus(x):
    return jnp.maximum(x, 0.0) + jnp.log(1.0 + jnp.exp(-jnp.abs(x)))


def _silu(x):
    return x * (1.0 / (1.0 + jnp.exp(-x)))


def _const_spec(shape):
    nd = len(shape)
    return pl.BlockSpec(shape, lambda *_: (0,) * nd, pipeline_mode=pl.Buffered(1))


def _inproj_body(x_ref, nw_ref, w_ref, z_ref, xbc_ref, q_ref, k_ref, v_ref, dt_ref):
    x = x_ref[...]
    ms = jnp.mean(x * x, axis=-1, keepdims=True)
    h = (x * lax.rsqrt(ms + EPS) * nw_ref[...]).astype(BF16)
    off = 0
    for ref in (z_ref, xbc_ref, q_ref, k_ref, v_ref, dt_ref):
        width = ref.shape[-1]
        ref[...] = _dot(h, w_ref[:, off:off + width]).astype(ref.dtype)
        off += width


def _inproj(x2, norm_w, w_cat, tm):
    n, d = x2.shape
    widths = (D_SSM, CONV_DIM, D_NA, D_NA, D_NA, LANES)
    dtypes = (F32, F32, F32, F32, BF16, F32)
    row = lambda i: (i, 0)
    return pl.pallas_call(
        _inproj_body,
        grid=(n // tm,),
        in_specs=[pl.BlockSpec((tm, d), row), _const_spec((1, d)), _const_spec(w_cat.shape)],
        out_specs=[pl.BlockSpec((tm, w), row) for w in widths],
        out_shape=[jax.ShapeDtypeStruct((n, w), dt) for w, dt in zip(widths, dtypes)],
        compiler_params=pltpu.CompilerParams(
            dimension_semantics=("arbitrary",), vmem_limit_bytes=VMEM_LIMIT),
    )(x2, norm_w, w_cat)


def _tri_masks():
    t = lax.broadcasted_iota(jnp.int32, (CHUNK, CHUNK), 0)
    s = lax.broadcasted_iota(jnp.int32, (CHUNK, CHUNK), 1)
    return s <= t, s >= t


def _dt_and_loga(dt_ref, dtb_ref, alog_ref):
    dtv = _softplus(dt_ref[0] + dtb_ref[...])
    return dtv, dtv * (-jnp.exp(alog_ref[...]))


def _ssd_bwd_body(cur_ref, prev_ref, next_ref, dt_ref, cw_ref, cb_ref, dtb_ref, alog_ref, eb_ref,
                  xc_ref, hb_ref, uext, state):
    c = pl.program_id(1)
    nc = pl.num_programs(1)
    chunk = nc - 1 - c

    @pl.when(c == 0)
    def _():
        state[...] = jnp.zeros_like(state)

    uext[0:HALO, :] = jnp.where(chunk > 0, prev_ref[0], 0.0)
    uext[HALO:HALO + CHUNK, :] = cur_ref[0]
    uext[HALO + CHUNK:2 * HALO + CHUNK, :] = jnp.where(chunk < nc - 1, next_ref[0], 0.0)
    first = HALO - (D_CONV - 1) // 2
    cstep = 2 * LANES
    for col in range(0, CONV_DIM, cstep):
        acc = jnp.broadcast_to(cb_ref[:, col:col + cstep], (CHUNK, cstep))
        for k in range(D_CONV):
            acc = acc + cw_ref[k:k + 1, col:col + cstep] * uext[first + k:first + k + CHUNK, col:col + cstep]
        xc_ref[0, :, col:col + cstep] = _silu(acc)

    dtv, loga = _dt_and_loga(dt_ref, dtb_ref, alog_ref)
    _, upper = _tri_masks()
    rcum = _dot_sel_lhs(jnp.where(upper, 1.0, 0.0).astype(BF16), loga)
    r0 = rcum[0:1, :]
    wgt = dtv * jnp.exp(r0 - rcum)
    wgt_x = _dot(_pack3(wgt), eb_ref[...])
    dec_x = _dot(_pack3(jnp.broadcast_to(jnp.exp(r0), (HALO, LANES))), eb_ref[...])[0:1]

    hb_ref[0, 0] = state[...].astype(BF16)
    xw = (xc_ref[0, :, 0:D_SSM] * wgt_x).astype(BF16)
    for g in range(SSM_GROUPS):
        gs = slice(g * GROUP_W, (g + 1) * GROUP_W)
        b_t = xc_ref[0, :, D_SSM + g * D_STATE:D_SSM + (g + 1) * D_STATE].T.astype(BF16)
        state[:, gs] = state[:, gs] * dec_x[:, gs] + _dot(b_t, xw[:, gs])


def _ssd_bwd(xbc, dt, conv_w, conv_b, dt_bias, a_log, sel_b):
    bsz, t, _ = xbc.shape
    nc = t // CHUNK
    hpc = CHUNK // HALO
    nhalo = t // HALO
    rev = lambda c: nc - 1 - c
    return pl.pallas_call(
        _ssd_bwd_body,
        grid=(bsz, nc),
        in_specs=[
            pl.BlockSpec((1, CHUNK, CONV_DIM), lambda b, c: (b, rev(c), 0)),
            pl.BlockSpec((1, HALO, CONV_DIM), lambda b, c: (b, jnp.maximum(rev(c) * hpc - 1, 0), 0)),
            pl.BlockSpec((1, HALO, CONV_DIM), lambda b, c: (b, jnp.minimum((rev(c) + 1) * hpc, nhalo - 1), 0)),
            pl.BlockSpec((1, CHUNK, LANES), lambda b, c: (b, rev(c), 0)),
            _const_spec(conv_w.shape), _const_spec(conv_b.shape),
            _const_spec(dt_bias.shape), _const_spec(a_log.shape), _const_spec(sel_b.shape),
        ],
        out_specs=[
            pl.BlockSpec((1, CHUNK, CONV_DIM), lambda b, c: (b, rev(c), 0)),
            pl.BlockSpec((1, 1, D_STATE, D_SSM), lambda b, c: (b, rev(c), 0, 0)),
        ],
        out_shape=[
            jax.ShapeDtypeStruct((bsz, t, CONV_DIM), F32),
            jax.ShapeDtypeStruct((bsz, nc, D_STATE, D_SSM), BF16),
        ],
        scratch_shapes=[pltpu.VMEM((CHUNK + 2 * HALO, CONV_DIM), F32), pltpu.VMEM((D_STATE, D_SSM), F32)],
        compiler_params=pltpu.CompilerParams(
            dimension_semantics=("arbitrary", "arbitrary"), vmem_limit_bytes=VMEM_LIMIT),
    )(xbc, xbc, xbc, dt, conv_w, conv_b, dt_bias, a_log, sel_b)


def _ssd_fwd_body(xc_ref, dt_ref, z_ref, hb_ref, dtb_ref, alog_ref, dsk_ref, nw_ref, ef_ref, eb_ref,
                  y_ref, state, ybuf):
    @pl.when(pl.program_id(1) == 0)
    def _():
        state[...] = jnp.zeros_like(state)

    dtv, loga = _dt_and_loga(dt_ref, dtb_ref, alog_ref)
    lower, upper = _tri_masks()
    cum_f = _dot_sel_lhs(jnp.where(lower, 1.0, 0.0).astype(BF16), loga)
    cum_b = _dot_sel_lhs(jnp.where(upper, 1.0, 0.0).astype(BF16), loga)
    lane = lax.broadcasted_iota(jnp.int32, (CHUNK, LANES), 1)
    cum = jnp.where((lane & SSM_HEADS) == 0, cum_f, cum_b)
    ecum_p = _pack3(jnp.exp(cum))
    wgt_p = _pack3(dtv * jnp.exp(cum_f[CHUNK - 1:CHUNK, :] - cum_f))
    cum_t = cum.T
    dt_t = dtv.T
    first_half = lane < SSM_HEAD_DIM

    for g in range(SSM_GROUPS):
        gs = slice(g * GROUP_W, (g + 1) * GROUP_W)
        b_f32 = xc_ref[0, :, D_SSM + g * D_STATE:D_SSM + (g + 1) * D_STATE]
        b_g = b_f32.astype(BF16)
        c0 = D_SSM + SSM_GROUPS * D_STATE + g * D_STATE
        c_g = xc_ref[0, :, c0:c0 + D_STATE].astype(BF16)
        cb = _dot_nt(c_g, b_g)
        ea_g = _dot(ecum_p, ef_ref[:, gs])
        er_g = _dot(ecum_p, eb_ref[:, gs])
        y_off = _dot(c_g, state[:, gs].astype(BF16)) * ea_g + _dot(c_g, hb_ref[0, 0, :, gs]) * er_g
        xw = (xc_ref[0, :, gs] * _dot(wgt_p, ef_ref[:, gs])).astype(BF16)
        state[:, gs] = state[:, gs] * ea_g[CHUNK - 1:CHUNK, :] + _dot(b_f32.T.astype(BF16), xw)
        pairs = GROUP_W // LANES
        for pp in range(pairs):
            pr = g * pairs + pp
            ps = slice(pr * LANES, (pr + 1) * LANES)
            mats = []
            for h in (2 * pr, 2 * pr + 1):
                hb = SSM_HEADS + h
                seg_f = jnp.broadcast_to(cum[:, h:h + 1], (CHUNK, CHUNK)) - cum_t[h:h + 1, :]
                seg_b = jnp.broadcast_to(cum[:, hb:hb + 1], (CHUNK, CHUNK)) - cum_t[hb:hb + 1, :]
                d_f = jnp.exp(jnp.where(lower, seg_f, MASKED)) * dt_t[h:h + 1, :]
                d_b = jnp.exp(jnp.where(upper, seg_b, MASKED)) * dt_t[hb:hb + 1, :]
                mats.append(((d_f + d_b) * cb).astype(BF16))
            xs = xc_ref[0, :, ps]
            xp = xs.astype(BF16)
            zero = jnp.zeros_like(xp)
            rhs = jnp.concatenate([jnp.where(first_half, xp, zero), jnp.where(first_half, zero, xp)], axis=0)
            y_diag = _dot(jnp.concatenate(mats, axis=1), rhs)
            ybuf[:, ps] = y_diag + y_off[:, pp * LANES:(pp + 1) * LANES] + dsk_ref[:, ps] * xs
        gated = ybuf[:, gs] * _silu(z_ref[0, :, gs])
        ms = jnp.mean(gated * gated, axis=-1, keepdims=True)
        y_ref[0, :, gs] = (gated * lax.rsqrt(ms + EPS) * nw_ref[:, gs]).astype(y_ref.dtype)


def _ssd_fwd(xc, dt, z, hb, dt_bias, a_log, d_skip_x, norm_w, sel_f, sel_b):
    bsz, t, _ = xc.shape
    nc = t // CHUNK
    blk = lambda w: pl.BlockSpec((1, CHUNK, w), lambda b, c: (b, c, 0))
    return pl.pallas_call(
        _ssd_fwd_body,
        grid=(bsz, nc),
        in_specs=[
            blk(CONV_DIM), blk(LANES), blk(D_SSM),
            pl.BlockSpec((1, 1, D_STATE, D_SSM), lambda b, c: (b, c, 0, 0)),
            _const_spec(dt_bias.shape), _const_spec(a_log.shape), _const_spec(d_skip_x.shape),
            _const_spec(norm_w.shape), _const_spec(sel_f.shape), _const_spec(sel_b.shape),
        ],
        out_specs=blk(D_SSM),
        out_shape=jax.ShapeDtypeStruct((bsz, t, D_SSM), BF16),
        scratch_shapes=[pltpu.VMEM((D_STATE, D_SSM), F32), pltpu.VMEM((CHUNK, D_SSM), F32)],
        compiler_params=pltpu.CompilerParams(
            dimension_semantics=("arbitrary", "arbitrary"), vmem_limit_bytes=VMEM_LIMIT),
    )(xc, dt, z, hb, dt_bias, a_log, d_skip_x, norm_w, sel_f, sel_b)


NORM_ROWS = 512


def _na_body(q_ref, k_ref, v_ref, bias_ref, qw_ref, kw_ref, seg_ref, o_ref, qn, kn, sbuf):
    t = q_ref.shape[1]
    rows = t // GRID_W

    def norm_step(i, carry):
        sl = pl.ds(pl.multiple_of(i * NORM_ROWS, NORM_ROWS), NORM_ROWS)
        for src, w_ref, dst in ((q_ref, qw_ref, qn), (k_ref, kw_ref, kn)):
            xx = src[0, sl, :]
            sq = xx * xx
            hi = sq.astype(BF16)
            lo = (sq - hi.astype(F32)).astype(BF16)
            ssq = _dot(hi, seg_ref[...]) + _dot(lo, seg_ref[...])
            dst[sl, :] = (xx * lax.rsqrt(ssq * (1.0 / NA_HEAD_DIM) + EPS) * w_ref[...]).astype(BF16)
        return carry

    lax.fori_loop(0, t // NORM_ROWS, norm_step, 0)

    nblk = rows // NA_BLOCK
    bq = NA_BLOCK * GRID_W
    first_half = lax.broadcasted_iota(jnp.int32, (bq, LANES), 1) < NA_HEAD_DIM

    def offsets(bi):
        i0 = bi * NA_BLOCK
        rlo = jnp.clip(i0 - NA_ROWS // 2, 0, rows - NA_SPAN)
        return pl.multiple_of(i0 * GRID_W, bq), pl.multiple_of(rlo * GRID_W, 2 * GRID_W)

    def scores(bi, slot):
        kind = jnp.where(bi == 0, 0, jnp.where(bi == nblk - 1, 2, 1))
        qoff, koff = offsets(bi)
        qb = qn[pl.ds(qoff, bq), :]
        zero = jnp.zeros_like(qb)
        qs = jnp.concatenate([jnp.where(first_half, qb, zero), jnp.where(first_half, zero, qb)], axis=0)
        sbuf[slot] = _dot_nt(qs, kn[pl.ds(koff, NA_BKEYS), :]) + bias_ref[0, kind]

    def attend(bi, slot):
        qoff, koff = offsets(bi)
        s = sbuf[slot]
        p = jnp.exp2(s - jnp.max(s, axis=-1, keepdims=True))
        denom = jnp.sum(p, axis=-1, keepdims=True)
        o2 = _dot(p.astype(BF16), v_ref[0, pl.ds(koff, NA_BKEYS), :]) * (1.0 / denom)
        o_ref[0, pl.ds(qoff, bq), :] = jnp.where(first_half, o2[0:bq], o2[bq:]).astype(o_ref.dtype)

    scores(0, 0)

    def pair_step(i, carry):
        b0 = 2 * i
        scores(b0 + 1, 1)
        attend(b0, 0)
        scores(jnp.minimum(b0 + 2, nblk - 1), 0)
        attend(b0 + 1, 1)
        return carry

    lax.fori_loop(0, nblk // 2, pair_step, 0)


def _natten(q, k, v, bias, qw, kw, seg):
    bsz, t, _ = q.shape
    blk = pl.BlockSpec((1, t, LANES), lambda p, b: (b, 0, p))
    return pl.pallas_call(
        _na_body,
        grid=(NA_PAIRS, bsz),
        in_specs=[
            blk, blk, blk,
            pl.BlockSpec((1,) + bias.shape[1:], lambda p, b: (p, 0, 0, 0)),
            _const_spec(qw.shape), _const_spec(kw.shape), _const_spec(seg.shape),
        ],
        out_specs=blk,
        out_shape=jax.ShapeDtypeStruct((bsz, t, D_NA), BF16),
        scratch_shapes=[pltpu.VMEM((t, LANES), BF16), pltpu.VMEM((t, LANES), BF16),
                        pltpu.VMEM((2, 2 * NA_BLOCK * GRID_W, NA_BKEYS), F32)],
        compiler_params=pltpu.CompilerParams(
            dimension_semantics=("arbitrary", "arbitrary"), vmem_limit_bytes=VMEM_LIMIT),
    )(q, k, v, bias, qw, kw, seg)


def _bias_rows_body(rpb_ref, onehot_ref, mask_ref, o_ref):
    o_ref[0] = _dot_sel_rhs(rpb_ref[0], onehot_ref[...]) * LOG2E + mask_ref[...]


def _block_row_rel():
    rel = -np.ones((3, NA_BLOCK, NA_SPAN), np.int64)
    for u in range(NA_BLOCK):
        for a in range(NA_SPAN):
            if a < NA_ROWS:
                rel[0, u, a] = a - u + NA_ROWS - 1
            if u <= a < u + NA_ROWS:
                rel[1, u, a] = a - u + NA_ROWS - 1 - NA_ROWS // 2
            if a >= NA_SPAN - NA_ROWS:
                rel[2, u, a] = a - u + NA_BLOCK - NA_SPAN + NA_ROWS - 1
    return rel


def _na_bias_table(rpb):
    j = np.arange(GRID_W)
    c0 = np.clip(j - NA_COLS // 2, 0, GRID_W - NA_COLS)
    cc = np.arange(GRID_W)
    valid = (cc[None, :] >= c0[:, None]) & (cc[None, :] < c0[:, None] + NA_COLS)
    col_rel = cc[None, :] - j[:, None] + NA_COLS - 1
    onehot = (np.arange(LANES)[:, None, None] == col_rel[None]) & valid[None]
    onehot = jnp.asarray(onehot.reshape(LANES, GRID_W * GRID_W), BF16)
    mask = jnp.asarray(np.where(valid, 0.0, MASKED).reshape(1, GRID_W * GRID_W), F32)
    nrel = 2 * NA_ROWS
    rpb_p = jnp.pad(rpb.astype(F32), ((0, 0), (0, nrel - rpb.shape[1]), (0, LANES - rpb.shape[2])))
    rows = pl.pallas_call(
        _bias_rows_body,
        grid=(NA_HEADS,),
        in_specs=[pl.BlockSpec((1, nrel, LANES), lambda h: (h, 0, 0)),
                  _const_spec(onehot.shape), _const_spec(mask.shape)],
        out_specs=pl.BlockSpec((1, nrel, GRID_W * GRID_W), lambda h: (h, 0, 0)),
        out_shape=jax.ShapeDtypeStruct((NA_HEADS, nrel, GRID_W * GRID_W), F32),
        compiler_params=pltpu.CompilerParams(dimension_semantics=("arbitrary",)),
    )(rpb_p, onehot, mask)
    rows = rows.reshape(NA_HEADS, nrel, GRID_W, GRID_W)
    masked = jnp.full((NA_HEADS, GRID_W, GRID_W), MASKED, F32)
    rel = _block_row_rel()
    tab = jnp.stack([rows[:, r] if r >= 0 else masked for r in rel.reshape(-1)], axis=1)
    tab = tab.reshape(NA_PAIRS, 2, 3, NA_BLOCK, NA_SPAN, GRID_W, GRID_W)
    return tab.transpose(0, 2, 1, 3, 5, 4, 6).reshape(NA_PAIRS, 3, 2 * NA_BLOCK * GRID_W, NA_BKEYS)


FF_STEP = 512


def _outmlp_body(x_ref, ys_ref, yn_ref, wo_ref, nw_ref, w1_ref, w2_ref, o_ref):
    dmix = ys_ref.shape[-1]
    x1 = x_ref[...] + _dot(ys_ref[...], wo_ref[0:dmix, :]) + _dot(yn_ref[...], wo_ref[dmix:, :])
    ms = jnp.mean(x1 * x1, axis=-1, keepdims=True)
    h = (x1 * lax.rsqrt(ms + EPS) * nw_ref[...]).astype(BF16)
    o_ref[...] = x1
    for f in range(0, w1_ref.shape[1], FF_STEP):
        u = jnp.maximum(_dot(h, w1_ref[:, f:f + FF_STEP]), 0.0)
        o_ref[...] += _dot((u * u).astype(BF16), w2_ref[f:f + FF_STEP, :])


def _outmlp(x2, ys, yn, w_out, norm_w, w1, w2, tm):
    n, d = x2.shape
    row = lambda i: (i, 0)
    return pl.pallas_call(
        _outmlp_body,
        grid=(n // tm,),
        in_specs=[
            pl.BlockSpec((tm, d), row), pl.BlockSpec((tm, ys.shape[1]), row), pl.BlockSpec((tm, yn.shape[1]), row),
            _const_spec(w_out.shape), _const_spec((1, d)), _const_spec(w1.shape), _const_spec(w2.shape),
        ],
        out_specs=pl.BlockSpec((tm, d), row),
        out_shape=jax.ShapeDtypeStruct((n, d), F32),
        compiler_params=pltpu.CompilerParams(
            dimension_semantics=("arbitrary",), vmem_limit_bytes=VMEM_LIMIT),
    )(x2, ys, yn, w_out, norm_w, w1, w2)


def _head_select(first_lane):
    sel = np.zeros((LANES, D_SSM), np.float32)
    for rep in range(DT_COPIES):
        for h in range(SSM_HEADS):
            sel[rep * DT_LANES + first_lane + h, h * SSM_HEAD_DIM:(h + 1) * SSM_HEAD_DIM] = 1.0
    return jnp.asarray(sel, BF16)


def _dt_lanes(fwd, bwd):
    v = jnp.tile(jnp.concatenate([fwd, bwd]).astype(F32), DT_COPIES)
    return jnp.pad(v, (0, LANES - v.shape[0]))[None, :]


def _layer(x, norm_mix_w, w_in, conv_w, conv_b, dt_bias_fwd, dt_bias_bwd, a_log_fwd, a_log_bwd, d_skip,
           ssm_norm_w, q_norm_w, k_norm_w, rel_pos_bias, w_out, norm_mlp_w, w_mlp_in, w_mlp_out, tm):
    bsz, t, d = x.shape
    n = bsz * t
    x2 = x.reshape(n, d)

    o_dt = D_SSM + CONV_DIM
    o_q = o_dt + DT_LANES
    w_dt = w_in[:, o_dt:o_q]
    w_cat = jnp.concatenate(
        [w_in[:, :o_dt], w_in[:, o_q:]] + [w_dt] * DT_COPIES
        + [jnp.zeros((d, LANES - DT_COPIES * DT_LANES), w_in.dtype)], axis=1).astype(BF16)
    z, xbc, q, k, v, dt = _inproj(x2, norm_mix_w[None, :], w_cat, tm)
    r3 = lambda a: a.reshape(bsz, t, a.shape[-1])

    dt_bias = _dt_lanes(dt_bias_fwd, dt_bias_bwd)
    a_log = _dt_lanes(a_log_fwd, a_log_bwd)
    sel_f, sel_b = _head_select(0), _head_select(SSM_HEADS)
    xc, hb = _ssd_bwd(r3(xbc), r3(dt), conv_w, conv_b[None, :], dt_bias, a_log, sel_b)
    y_ssm = _ssd_fwd(xc, r3(dt), r3(z), hb, dt_bias, a_log,
                     jnp.repeat(d_skip.astype(F32), SSM_HEAD_DIM)[None, :], ssm_norm_w[None, :], sel_f, sel_b)

    lane_head = np.arange(LANES) // NA_HEAD_DIM
    seg = jnp.asarray(lane_head[:, None] == lane_head[None, :], BF16)
    qw = (jnp.tile(q_norm_w.astype(F32), 2) * (NA_HEAD_DIM ** -0.5 * LOG2E))[None, :]
    kw = jnp.tile(k_norm_w.astype(F32), 2)[None, :]
    y_na = _natten(r3(q), r3(k), r3(v), _na_bias_table(rel_pos_bias), qw, kw, seg)

    out = _outmlp(x2, y_ssm.reshape(n, D_SSM), y_na.reshape(n, D_NA), w_out.astype(BF16),
                  norm_mlp_w[None, :], w_mlp_in.astype(BF16), w_mlp_out.astype(BF16), tm)
    return out.reshape(bsz, t, d)


def kernel(x, norm_mix_w, w_in, conv_w, conv_b, dt_bias_fwd, dt_bias_bwd, a_log_fwd, a_log_bwd, d_skip,
           ssm_norm_w, q_norm_w, k_norm_w, rel_pos_bias, w_out, norm_mlp_w, w_mlp_in, w_mlp_out):
    tm = min(512, x.shape[0] * x.shape[1])
    for layer in range(norm_mix_w.shape[0]):
        x = _layer(x, norm_mix_w[layer], w_in[layer], conv_w[layer], conv_b[layer], dt_bias_fwd[layer],
                   dt_bias_bwd[layer], a_log_fwd[layer], a_log_bwd[layer], d_skip[layer], ssm_norm_w[layer],
                   q_norm_w[layer], k_norm_w[layer], rel_pos_bias[layer], w_out[layer], norm_mlp_w[layer],
                   w_mlp_in[layer], w_mlp_out[layer], tm)
    return x
```

```python
import functools

import numpy as np
import jax
import jax.numpy as jnp
from jax import lax
from jax.experimental import pallas as pl
from jax.experimental.pallas import tpu as pltpu

F32 = jnp.float32
BF16 = jnp.bfloat16

LANES = 128
GRID_W = 64
SSM_HEADS = 16
SSM_HEAD_DIM = 64
D_SSM = SSM_HEADS * SSM_HEAD_DIM
SSM_GROUPS = 2
GROUP_W = D_SSM // SSM_GROUPS
D_STATE = 128
D_CONV = 5
CHUNK = 128
CONV_DIM = D_SSM + 2 * SSM_GROUPS * D_STATE
DT_LANES = 2 * SSM_HEADS
DT_COPIES = 3
NA_HEADS = 16
NA_HEAD_DIM = 64
D_NA = NA_HEADS * NA_HEAD_DIM
NA_ROWS = 8
NA_COLS = 16
NA_PAIRS = NA_HEADS // 2
NA_BLOCK = 4
NA_SPAN = NA_ROWS + NA_BLOCK
NA_BKEYS = NA_SPAN * GRID_W
LOG2E = 1.4426950408889634
EPS = 1e-5
MASKED = -1e30
HALO = 8
VMEM_LIMIT = 56 * 1024 * 1024


def _dot(a, b):
    return jnp.dot(a, b, preferred_element_type=F32)


def _dot_nt(a, b):
    return lax.dot_general(a, b, (((1,), (1,)), ((), ())), preferred_element_type=F32)


def _split3(x):
    hi = x.astype(BF16)
    r1 = x - hi.astype(F32)
    mid = r1.astype(BF16)
    lo = (r1 - mid.astype(F32)).astype(BF16)
    return hi, mid, lo


def _dot_sel_rhs(x, sel):
    hi, mid, lo = _split3(x)
    return _dot(hi, sel) + _dot(mid, sel) + _dot(lo, sel)


def _dot_sel_lhs(sel, x):
    hi, mid, lo = _split3(x)
    return _dot(sel, hi) + _dot(sel, mid) + _dot(sel, lo)


def _pack3(x):
    lane = lax.broadcasted_iota(jnp.int32, x.shape, x.ndim - 1)
    hi = x.astype(BF16).astype(F32)
    r1 = x - hi
    mid = r1.astype(BF16).astype(F32)
    return jnp.where(lane < DT_LANES, hi, jnp.where(lane < 2 * DT_LANES, mid, r1 - mid)).astype(BF16)


def _softplus(x):
    return jnp.maximum(x, 0.0) + jnp.log(1.0 + jnp.exp(-jnp.abs(x)))


def _silu(x):
    return x * (1.0 / (1.0 + jnp.exp(-x)))


def _const_spec(shape):
    nd = len(shape)
    return pl.BlockSpec(shape, lambda *_: (0,) * nd, pipeline_mode=pl.Buffered(1))


def _inproj_body(x_ref, nw_ref, w_ref, z_ref, xbc_ref, q_ref, k_ref, v_ref, dt_ref):
    x = x_ref[...]
    ms = jnp.mean(x * x, axis=-1, keepdims=True)
    h = (x * lax.rsqrt(ms + EPS) * nw_ref[...]).astype(BF16)
    off = 0
    for ref in (z_ref, xbc_ref, q_ref, k_ref, v_ref, dt_ref):
        width = ref.shape[-1]
        ref[...] = _dot(h, w_ref[:, off:off + width]).astype(ref.dtype)
        off += width


def _inproj(x2, norm_w, w_cat, tm):
    n, d = x2.shape
    widths = (D_SSM, CONV_DIM, D_NA, D_NA, D_NA, LANES)
    dtypes = (F32, F32, F32, F32, BF16, F32)
    row = lambda i: (i, 0)
    return pl.pallas_call(
        _inproj_body,
        grid=(n // tm,),
        in_specs=[pl.BlockSpec((tm, d), row), _const_spec((1, d)), _const_spec(w_cat.shape)],
        out_specs=[pl.BlockSpec((tm, w), row) for w in widths],
        out_shape=[jax.ShapeDtypeStruct((n, w), dt) for w, dt in zip(widths, dtypes)],
        compiler_params=pltpu.CompilerParams(
            dimension_semantics=("arbitrary",), vmem_limit_bytes=VMEM_LIMIT),
    )(x2, norm_w, w_cat)


def _tri_masks():
    t = lax.broadcasted_iota(jnp.int32, (CHUNK, CHUNK), 0)
    s = lax.broadcasted_iota(jnp.int32, (CHUNK, CHUNK), 1)
    return s <= t, s >= t


def _dt_and_loga(dt_ref, dtb_ref, alog_ref):
    dtv = _softplus(dt_ref[0] + dtb_ref[...])
    return dtv, dtv * (-jnp.exp(alog_ref[...]))


def _ssd_bwd_body(cur_ref, prev_ref, next_ref, dt_ref, cw_ref, cb_ref, dtb_ref, alog_ref, eb_ref,
                  xc_ref, hb_ref, uext, state):
    c = pl.program_id(1)
    nc = pl.num_programs(1)
    chunk = nc - 1 - c

    @pl.when(c == 0)
    def _():
        state[...] = jnp.zeros_like(state)

    uext[0:HALO, :] = jnp.where(chunk > 0, prev_ref[0], 0.0)
    uext[HALO:HALO + CHUNK, :] = cur_ref[0]
    uext[HALO + CHUNK:2 * HALO + CHUNK, :] = jnp.where(chunk < nc - 1, next_ref[0], 0.0)
    first = HALO - (D_CONV - 1) // 2
    cstep = 2 * LANES
    for col in range(0, CONV_DIM, cstep):
        acc = jnp.broadcast_to(cb_ref[:, col:col + cstep], (CHUNK, cstep))
        for k in range(D_CONV):
            acc = acc + cw_ref[k:k + 1, col:col + cstep] * uext[first + k:first + k + CHUNK, col:col + cstep]
        xc_ref[0, :, col:col + cstep] = _silu(acc)

    dtv, loga = _dt_and_loga(dt_ref, dtb_ref, alog_ref)
    _, upper = _tri_masks()
    rcum = _dot_sel_lhs(jnp.where(upper, 1.0, 0.0).astype(BF16), loga)
    r0 = rcum[0:1, :]
    wgt = dtv * jnp.exp(r0 - rcum)
    wgt_x = _dot(_pack3(wgt), eb_ref[...])
    dec_x = _dot(_pack3(jnp.broadcast_to(jnp.exp(r0), (HALO, LANES))), eb_ref[...])[0:1]

    hb_ref[0, 0] = state[...].astype(BF16)
    xw = (xc_ref[0, :, 0:D_SSM] * wgt_x).astype(BF16)
    for g in range(SSM_GROUPS):
        gs = slice(g * GROUP_W, (g + 1) * GROUP_W)
        b_t = xc_ref[0, :, D_SSM + g * D_STATE:D_SSM + (g + 1) * D_STATE].T.astype(BF16)
        state[:, gs] = state[:, gs] * dec_x[:, gs] + _dot(b_t, xw[:, gs])


def _ssd_bwd(xbc, dt, conv_w, conv_b, dt_bias, a_log, sel_b):
    bsz, t, _ = xbc.shape
    nc = t // CHUNK
    hpc = CHUNK // HALO
    nhalo = t // HALO
    rev = lambda c: nc - 1 - c
    return pl.pallas_call(
        _ssd_bwd_body,
        grid=(bsz, nc),
        in_specs=[
            pl.BlockSpec((1, CHUNK, CONV_DIM), lambda b, c: (b, rev(c), 0)),
            pl.BlockSpec((1, HALO, CONV_DIM), lambda b, c: (b, jnp.maximum(rev(c) * hpc - 1, 0), 0)),
            pl.BlockSpec((1, HALO, CONV_DIM), lambda b, c: (b, jnp.minimum((rev(c) + 1) * hpc, nhalo - 1), 0)),
            pl.BlockSpec((1, CHUNK, LANES), lambda b, c: (b, rev(c), 0)),
            _const_spec(conv_w.shape), _const_spec(conv_b.shape),
            _const_spec(dt_bias.shape), _const_spec(a_log.shape), _const_spec(sel_b.shape),
        ],
        out_specs=[
            pl.BlockSpec((1, CHUNK, CONV_DIM), lambda b, c: (b, rev(c), 0)),
            pl.BlockSpec((1, 1, D_STATE, D_SSM), lambda b, c: (b, rev(c), 0, 0)),
        ],
        out_shape=[
            jax.ShapeDtypeStruct((bsz, t, CONV_DIM), F32),
            jax.ShapeDtypeStruct((bsz, nc, D_STATE, D_SSM), BF16),
        ],
        scratch_shapes=[pltpu.VMEM((CHUNK + 2 * HALO, CONV_DIM), F32), pltpu.VMEM((D_STATE, D_SSM), F32)],
        compiler_params=pltpu.CompilerParams(
            dimension_semantics=("arbitrary", "arbitrary"), vmem_limit_bytes=VMEM_LIMIT),
    )(xbc, xbc, xbc, dt, conv_w, conv_b, dt_bias, a_log, sel_b)


def _ssd_fwd_body(xc_ref, dt_ref, z_ref, hb_ref, dtb_ref, alog_ref, dsk_ref, nw_ref, ef_ref, eb_ref,
                  y_ref, state, ybuf):
    @pl.when(pl.program_id(1) == 0)
    def _():
        state[...] = jnp.zeros_like(state)

    dtv, loga = _dt_and_loga(dt_ref, dtb_ref, alog_ref)
    lower, upper = _tri_masks()
    cum_f = _dot_sel_lhs(jnp.where(lower, 1.0, 0.0).astype(BF16), loga)
    cum_b = _dot_sel_lhs(jnp.where(upper, 1.0, 0.0).astype(BF16), loga)
    lane = lax.broadcasted_iota(jnp.int32, (CHUNK, LANES), 1)
    cum = jnp.where((lane & SSM_HEADS) == 0, cum_f, cum_b)
    ecum_p = _pack3(jnp.exp(cum))
    wgt_p = _pack3(dtv * jnp.exp(cum_f[CHUNK - 1:CHUNK, :] - cum_f))
    cum_t = cum.T
    dt_t = dtv.T
    first_half = lane < SSM_HEAD_DIM

    for g in range(SSM_GROUPS):
        gs = slice(g * GROUP_W, (g + 1) * GROUP_W)
        b_f32 = xc_ref[0, :, D_SSM + g * D_STATE:D_SSM + (g + 1) * D_STATE]
        b_g = b_f32.astype(BF16)
        c0 = D_SSM + SSM_GROUPS * D_STATE + g * D_STATE
        c_g = xc_ref[0, :, c0:c0 + D_STATE].astype(BF16)
        cb = _dot_nt(c_g, b_g)
        ea_g = _dot(ecum_p, ef_ref[:, gs])
        er_g = _dot(ecum_p, eb_ref[:, gs])
        y_off = _dot(c_g, state[:, gs].astype(BF16)) * ea_g + _dot(c_g, hb_ref[0, 0, :, gs]) * er_g
        xw = (xc_ref[0, :, gs] * _dot(wgt_p, ef_ref[:, gs])).astype(BF16)
        state[:, gs] = state[:, gs] * ea_g[CHUNK - 1:CHUNK, :] + _dot(b_f32.T.astype(BF16), xw)
        pairs = GROUP_W // LANES
        for pp in range(pairs):
            pr = g * pairs + pp
            ps = slice(pr * LANES, (pr + 1) * LANES)
            mats = []
            for h in (2 * pr, 2 * pr + 1):
                hb = SSM_HEADS + h
                seg_f = jnp.broadcast_to(cum[:, h:h + 1], (CHUNK, CHUNK)) - cum_t[h:h + 1, :]
                seg_b = jnp.broadcast_to(cum[:, hb:hb + 1], (CHUNK, CHUNK)) - cum_t[hb:hb + 1, :]
                d_f = jnp.exp(jnp.where(lower, seg_f, MASKED)) * dt_t[h:h + 1, :]
                d_b = jnp.exp(jnp.where(upper, seg_b, MASKED)) * dt_t[hb:hb + 1, :]
                mats.append(((d_f + d_b) * cb).astype(BF16))
            xs = xc_ref[0, :, ps]
            xp = xs.astype(BF16)
            zero = jnp.zeros_like(xp)
            rhs = jnp.concatenate([jnp.where(first_half, xp, zero), jnp.where(first_half, zero, xp)], axis=0)
            y_diag = _dot(jnp.concatenate(mats, axis=1), rhs)
            ybuf[:, ps] = y_diag + y_off[:, pp * LANES:(pp + 1) * LANES] + dsk_ref[:, ps] * xs
        gated = ybuf[:, gs] * _silu(z_ref[0, :, gs])
        ms = jnp.mean(gated * gated, axis=-1, keepdims=True)
        y_ref[0, :, gs] = (gated * lax.rsqrt(ms + EPS) * nw_ref[:, gs]).astype(y_ref.dtype)


def _ssd_fwd(xc, dt, z, hb, dt_bias, a_log, d_skip_x, norm_w, sel_f, sel_b):
    bsz, t, _ = xc.shape
    nc = t // CHUNK
    blk = lambda w: pl.BlockSpec((1, CHUNK, w), lambda b, c: (b, c, 0))
    return pl.pallas_call(
        _ssd_fwd_body,
        grid=(bsz, nc),
        in_specs=[
            blk(CONV_DIM), blk(LANES), blk(D_SSM),
            pl.BlockSpec((1, 1, D_STATE, D_SSM), lambda b, c: (b, c, 0, 0)),
            _const_spec(dt_bias.shape), _const_spec(a_log.shape), _const_spec(d_skip_x.shape),
            _const_spec(norm_w.shape), _const_spec(sel_f.shape), _const_spec(sel_b.shape),
        ],
        out_specs=blk(D_SSM),
        out_shape=jax.ShapeDtypeStruct((bsz, t, D_SSM), BF16),
        scratch_shapes=[pltpu.VMEM((D_STATE, D_SSM), F32), pltpu.VMEM((CHUNK, D_SSM), F32)],
        compiler_params=pltpu.CompilerParams(
            dimension_semantics=("arbitrary", "arbitrary"), vmem_limit_bytes=VMEM_LIMIT),
    )(xc, dt, z, hb, dt_bias, a_log, d_skip_x, norm_w, sel_f, sel_b)


NORM_ROWS = 512


def _build_bias(g_ref, colmask_ref, bias):
    rel = _block_row_rel()
    half = lax.broadcasted_iota(jnp.int32, (GRID_W, LANES), 1) < GRID_W
    nrel = 2 * NA_ROWS

    @functools.cache
    def toeplitz(hh, r, upper_half):
        if upper_half:
            return pltpu.roll(toeplitz(hh, r, False), GRID_W, 1)
        row = g_ref[0, hh * nrel + r:hh * nrel + r + 1, :] * LOG2E
        return pltpu.roll(jnp.broadcast_to(row, (GRID_W, LANES)), 0, 1, stride=1, stride_axis=0)

    for kind in range(rel.shape[0]):
        for hh in range(2):
            for u in range(NA_BLOCK):
                row0 = (hh * NA_BLOCK + u) * GRID_W
                for a in range(0, NA_SPAN, 2):
                    r_lo, r_hi = int(rel[kind, u, a]), int(rel[kind, u, a + 1])
                    lo = toeplitz(hh, r_lo, False) if r_lo >= 0 else MASKED
                    hi = toeplitz(hh, r_hi, True) if r_hi >= 0 else MASKED
                    tile = jnp.where(half, lo, hi) + colmask_ref[...]
                    bias[kind, row0:row0 + GRID_W, a * GRID_W:(a + 2) * GRID_W] = tile


def _na_body(q_ref, k_ref, v_ref, g_ref, colmask_ref, qw_ref, kw_ref, seg_ref, o_ref, qn, kn, sbuf, bias):
    t = q_ref.shape[1]
    rows = t // GRID_W

    @pl.when(pl.program_id(1) == 0)
    def _():
        _build_bias(g_ref, colmask_ref, bias)

    def norm_step(i, carry):
        sl = pl.ds(pl.multiple_of(i * NORM_ROWS, NORM_ROWS), NORM_ROWS)
        for src, w_ref, dst in ((q_ref, qw_ref, qn), (k_ref, kw_ref, kn)):
            xx = src[0, sl, :]
            sq = xx * xx
            hi = sq.astype(BF16)
            lo = (sq - hi.astype(F32)).astype(BF16)
            ssq = _dot(hi, seg_ref[...]) + _dot(lo, seg_ref[...])
            dst[sl, :] = (xx * lax.rsqrt(ssq * (1.0 / NA_HEAD_DIM) + EPS) * w_ref[...]).astype(BF16)
        return carry

    lax.fori_loop(0, t // NORM_ROWS, norm_step, 0)

    nblk = rows // NA_BLOCK
    bq = NA_BLOCK * GRID_W
    first_half = lax.broadcasted_iota(jnp.int32, (bq, LANES), 1) < NA_HEAD_DIM

    def offsets(bi):
        i0 = bi * NA_BLOCK
        rlo = jnp.clip(i0 - NA_ROWS // 2, 0, rows - NA_SPAN)
        return pl.multiple_of(i0 * GRID_W, bq), pl.multiple_of(rlo * GRID_W, 2 * GRID_W)

    def scores(bi, slot):
        kind = jnp.where(bi == 0, 0, jnp.where(bi == nblk - 1, 2, 1))
        qoff, koff = offsets(bi)
        qb = qn[pl.ds(qoff, bq), :]
        zero = jnp.zeros_like(qb)
        qs = jnp.concatenate([jnp.where(first_half, qb, zero), jnp.where(first_half, zero, qb)], axis=0)
        sbuf[slot] = _dot_nt(qs, kn[pl.ds(koff, NA_BKEYS), :]) + bias[kind]

    def attend(bi, slot):
        qoff, koff = offsets(bi)
        s = sbuf[slot]
        p = jnp.exp2(s - jnp.max(s, axis=-1, keepdims=True))
        denom = jnp.sum(p, axis=-1, keepdims=True)
        o2 = _dot(p.astype(BF16), v_ref[0, pl.ds(koff, NA_BKEYS), :]) * (1.0 / denom)
        o_ref[0, pl.ds(qoff, bq), :] = jnp.where(first_half, o2[0:bq], o2[bq:]).astype(o_ref.dtype)

    scores(0, 0)

    def pair_step(i, carry):
        b0 = 2 * i
        scores(b0 + 1, 1)
        attend(b0, 0)
        scores(jnp.minimum(b0 + 2, nblk - 1), 0)
        attend(b0 + 1, 1)
        return carry

    lax.fori_loop(0, nblk // 2, pair_step, 0)


def _natten(q, k, v, g, colmask, qw, kw, seg):
    bsz, t, _ = q.shape
    nq = 2 * NA_BLOCK * GRID_W
    blk = pl.BlockSpec((1, t, LANES), lambda p, b: (b, 0, p))
    return pl.pallas_call(
        _na_body,
        grid=(NA_PAIRS, bsz),
        in_specs=[
            blk, blk, blk,
            pl.BlockSpec((1,) + g.shape[1:], lambda p, b: (p, 0, 0)), _const_spec(colmask.shape),
            _const_spec(qw.shape), _const_spec(kw.shape), _const_spec(seg.shape),
        ],
        out_specs=blk,
        out_shape=jax.ShapeDtypeStruct((bsz, t, D_NA), BF16),
        scratch_shapes=[pltpu.VMEM((t, LANES), BF16), pltpu.VMEM((t, LANES), BF16),
                        pltpu.VMEM((2, nq, NA_BKEYS), F32), pltpu.VMEM((3, nq, NA_BKEYS), F32)],
        compiler_params=pltpu.CompilerParams(
            dimension_semantics=("arbitrary", "arbitrary"), vmem_limit_bytes=VMEM_LIMIT),
    )(q, k, v, g, colmask, qw, kw, seg)


def _block_row_rel():
    rel = -np.ones((3, NA_BLOCK, NA_SPAN), np.int64)
    for u in range(NA_BLOCK):
        for a in range(NA_SPAN):
            if a < NA_ROWS:
                rel[0, u, a] = a - u + NA_ROWS - 1
            if u <= a < u + NA_ROWS:
                rel[1, u, a] = a - u + NA_ROWS - 1 - NA_ROWS // 2
            if a >= NA_SPAN - NA_ROWS:
                rel[2, u, a] = a - u + NA_BLOCK - NA_SPAN + NA_ROWS - 1
    return rel


def _na_bias_rows(rpb):
    centre = NA_COLS - 1
    rows = jnp.concatenate(
        [rpb[:, :, centre:], jnp.zeros(rpb.shape[:2] + (LANES - rpb.shape[2],), rpb.dtype), rpb[:, :, :centre]],
        axis=2).astype(F32)
    rows = jnp.pad(rows, ((0, 0), (0, 2 * NA_ROWS - rpb.shape[1]), (0, 0)))
    return rows.reshape(NA_PAIRS, 2 * 2 * NA_ROWS, LANES)


def _na_col_mask():
    j = np.arange(GRID_W)
    c0 = np.clip(j - NA_COLS // 2, 0, GRID_W - NA_COLS)
    cc = np.arange(GRID_W)
    valid = (cc[None, :] >= c0[:, None]) & (cc[None, :] < c0[:, None] + NA_COLS)
    return jnp.asarray(np.tile(np.where(valid, 0.0, MASKED), (1, 2)), F32)


FF_STEP = 512


def _outmlp_body(x_ref, ys_ref, yn_ref, wo_ref, nw_ref, w1_ref, w2_ref, o_ref):
    dmix = ys_ref.shape[-1]
    x1 = x_ref[...] + _dot(ys_ref[...], wo_ref[0:dmix, :]) + _dot(yn_ref[...], wo_ref[dmix:, :])
    ms = jnp.mean(x1 * x1, axis=-1, keepdims=True)
    h = (x1 * lax.rsqrt(ms + EPS) * nw_ref[...]).astype(BF16)
    o_ref[...] = x1
    for f in range(0, w1_ref.shape[1], FF_STEP):
        u = jnp.maximum(_dot(h, w1_ref[:, f:f + FF_STEP]), 0.0)
        o_ref[...] += _dot((u * u).astype(BF16), w2_ref[f:f + FF_STEP, :])


def _outmlp(x2, ys, yn, w_out, norm_w, w1, w2, tm):
    n, d = x2.shape
    row = lambda i: (i, 0)
    return pl.pallas_call(
        _outmlp_body,
        grid=(n // tm,),
        in_specs=[
            pl.BlockSpec((tm, d), row), pl.BlockSpec((tm, ys.shape[1]), row), pl.BlockSpec((tm, yn.shape[1]), row),
            _const_spec(w_out.shape), _const_spec((1, d)), _const_spec(w1.shape), _const_spec(w2.shape),
        ],
        out_specs=pl.BlockSpec((tm, d), row),
        out_shape=jax.ShapeDtypeStruct((n, d), F32),
        compiler_params=pltpu.CompilerParams(
            dimension_semantics=("arbitrary",), vmem_limit_bytes=VMEM_LIMIT),
    )(x2, ys, yn, w_out, norm_w, w1, w2)


def _head_select(first_lane):
    sel = np.zeros((LANES, D_SSM), np.float32)
    for rep in range(DT_COPIES):
        for h in range(SSM_HEADS):
            sel[rep * DT_LANES + first_lane + h, h * SSM_HEAD_DIM:(h + 1) * SSM_HEAD_DIM] = 1.0
    return jnp.asarray(sel, BF16)


def _dt_lanes(fwd, bwd):
    v = jnp.tile(jnp.concatenate([fwd, bwd]).astype(F32), DT_COPIES)
    return jnp.pad(v, (0, LANES - v.shape[0]))[None, :]


def _layer(x, norm_mix_w, w_in, conv_w, conv_b, dt_bias_fwd, dt_bias_bwd, a_log_fwd, a_log_bwd, d_skip,
           ssm_norm_w, q_norm_w, k_norm_w, rel_pos_bias, w_out, norm_mlp_w, w_mlp_in, w_mlp_out, tm):
    bsz, t, d = x.shape
    n = bsz * t
    x2 = x.reshape(n, d)

    o_dt = D_SSM + CONV_DIM
    o_q = o_dt + DT_LANES
    w_dt = w_in[:, o_dt:o_q]
    w_cat = jnp.concatenate(
        [w_in[:, :o_dt], w_in[:, o_q:]] + [w_dt] * DT_COPIES
        + [jnp.zeros((d, LANES - DT_COPIES * DT_LANES), w_in.dtype)], axis=1).astype(BF16)
    z, xbc, q, k, v, dt = _inproj(x2, norm_mix_w[None, :], w_cat, tm)
    r3 = lambda a: a.reshape(bsz, t, a.shape[-1])

    dt_bias = _dt_lanes(dt_bias_fwd, dt_bias_bwd)
    a_log = _dt_lanes(a_log_fwd, a_log_bwd)
    sel_f, sel_b = _head_select(0), _head_select(SSM_HEADS)
    xc, hb = _ssd_bwd(r3(xbc), r3(dt), conv_w, conv_b[None, :], dt_bias, a_log, sel_b)
    y_ssm = _ssd_fwd(xc, r3(dt), r3(z), hb, dt_bias, a_log,
                     jnp.repeat(d_skip.astype(F32), SSM_HEAD_DIM)[None, :], ssm_norm_w[None, :], sel_f, sel_b)

    lane_head = np.arange(LANES) // NA_HEAD_DIM
    seg = jnp.asarray(lane_head[:, None] == lane_head[None, :], BF16)
    qw = (jnp.tile(q_norm_w.astype(F32), 2) * (NA_HEAD_DIM ** -0.5 * LOG2E))[None, :]
    kw = jnp.tile(k_norm_w.astype(F32), 2)[None, :]
    y_na = _natten(r3(q), r3(k), r3(v), _na_bias_rows(rel_pos_bias), _na_col_mask(), qw, kw, seg)

    out = _outmlp(x2, y_ssm.reshape(n, D_SSM), y_na.reshape(n, D_NA), w_out.astype(BF16),
                  norm_mlp_w[None, :], w_mlp_in.astype(BF16), w_mlp_out.astype(BF16), tm)
    return out.reshape(bsz, t, d)


def kernel(x, norm_mix_w, w_in, conv_w, conv_b, dt_bias_fwd, dt_bias_bwd, a_log_fwd, a_log_bwd, d_skip,
           ssm_norm_w, q_norm_w, k_norm_w, rel_pos_bias, w_out, norm_mlp_w, w_mlp_in, w_mlp_out):
    tm = min(512, x.shape[0] * x.shape[1])
    for layer in range(norm_mix_w.shape[0]):
        x = _layer(x, norm_mix_w[layer], w_in[layer], conv_w[layer], conv_b[layer], dt_bias_fwd[layer],
                   dt_bias_bwd[layer], a_log_fwd[layer], a_log_bwd[layer], d_skip[layer], ssm_norm_w[layer],
                   q_norm_w[layer], k_norm_w[layer], rel_pos_bias[layer], w_out[layer], norm_mlp_w[layer],
                   w_mlp_in[layer], w_mlp_out[layer], tm)
    return x
```

```python
import functools

import numpy as np
import jax
import jax.numpy as jnp
from jax import lax
from jax.experimental import pallas as pl
from jax.experimental.pallas import tpu as pltpu

F32 = jnp.float32
BF16 = jnp.bfloat16

LANES = 128
GRID_W = 64
SSM_HEADS = 16
SSM_HEAD_DIM = 64
D_SSM = SSM_HEADS * SSM_HEAD_DIM
SSM_GROUPS = 2
GROUP_W = D_SSM // SSM_GROUPS
D_STATE = 128
D_CONV = 5
CHUNK = 128
FWD_CHUNKS = 4
CONV_DIM = D_SSM + 2 * SSM_GROUPS * D_STATE
DT_LANES = 2 * SSM_HEADS
DT_COPIES = 3
NA_HEADS = 16
NA_HEAD_DIM = 64
D_NA = NA_HEADS * NA_HEAD_DIM
NA_ROWS = 8
NA_COLS = 16
NA_PAIRS = NA_HEADS // 2
NA_BLOCK = 4
NA_SPAN = NA_ROWS + NA_BLOCK
NA_BKEYS = NA_SPAN * GRID_W
LOG2E = 1.4426950408889634
EPS = 1e-5
MASKED = -1e30
HALO = 8
CONV_STRIDE = CHUNK // 8 + 1
CONV_ROWS = 152
VMEM_LIMIT = 56 * 1024 * 1024


def _dot(a, b):
    return jnp.dot(a, b, preferred_element_type=F32)


def _dot_nt(a, b):
    return lax.dot_general(a, b, (((1,), (1,)), ((), ())), preferred_element_type=F32)


def _split3(x):
    hi = x.astype(BF16)
    r1 = x - hi.astype(F32)
    mid = r1.astype(BF16)
    lo = (r1 - mid.astype(F32)).astype(BF16)
    return hi, mid, lo


def _dot_sel_rhs(x, sel):
    hi, mid, lo = _split3(x)
    return _dot(hi, sel) + _dot(mid, sel) + _dot(lo, sel)


def _dot_sel_lhs(sel, x):
    hi, mid, lo = _split3(x)
    return _dot(sel, hi) + _dot(sel, mid) + _dot(sel, lo)


def _pack3(x):
    lane = lax.broadcasted_iota(jnp.int32, x.shape, x.ndim - 1)
    hi = x.astype(BF16).astype(F32)
    r1 = x - hi
    mid = r1.astype(BF16).astype(F32)
    return jnp.where(lane < DT_LANES, hi, jnp.where(lane < 2 * DT_LANES, mid, r1 - mid)).astype(BF16)


def _softplus(x):
    return jnp.maximum(x, 0.0) + jnp.log(1.0 + jnp.exp(-jnp.abs(x)))


def _silu(x):
    return x * (1.0 / (1.0 + jnp.exp(-x)))


def _const_spec(shape):
    nd = len(shape)
    return pl.BlockSpec(shape, lambda *_: (0,) * nd, pipeline_mode=pl.Buffered(1))


def _inproj_body(x_ref, nw_ref, w_ref, z_ref, xbc_ref, q_ref, k_ref, v_ref, dt_ref):
    x = x_ref[...]
    ms = jnp.mean(x * x, axis=-1, keepdims=True)
    h = (x * lax.rsqrt(ms + EPS) * nw_ref[...]).astype(BF16)
    off = 0
    for ref in (z_ref, xbc_ref, q_ref, k_ref, v_ref, dt_ref):
        width = ref.shape[-1]
        ref[...] = _dot(h, w_ref[:, off:off + width]).astype(ref.dtype)
        off += width


def _inproj(x2, norm_w, w_cat, tm):
    n, d = x2.shape
    widths = (D_SSM, CONV_DIM, D_NA, D_NA, D_NA, LANES)
    dtypes = (F32, F32, F32, F32, BF16, F32)
    row = lambda i: (i, 0)
    return pl.pallas_call(
        _inproj_body,
        grid=(n // tm,),
        in_specs=[pl.BlockSpec((tm, d), row), _const_spec((1, d)), _const_spec(w_cat.shape)],
        out_specs=[pl.BlockSpec((tm, w), row) for w in widths],
        out_shape=[jax.ShapeDtypeStruct((n, w), dt) for w, dt in zip(widths, dtypes)],
        compiler_params=pltpu.CompilerParams(
            dimension_semantics=("arbitrary",), vmem_limit_bytes=VMEM_LIMIT),
    )(x2, norm_w, w_cat)


def _tri_masks():
    t = lax.broadcasted_iota(jnp.int32, (CHUNK, CHUNK), 0)
    s = lax.broadcasted_iota(jnp.int32, (CHUNK, CHUNK), 1)
    return s <= t, s >= t


def _dt_and_loga(dt_ref, dtb_ref, alog_ref):
    dtv = _softplus(dt_ref[0] + dtb_ref[...])
    return dtv, dtv * (-LOG2E * jnp.exp(alog_ref[...]))


def _ssd_bwd_body(cur_ref, prev_ref, next_ref, dt_ref, cw_ref, cb_ref, dtb_ref, alog_ref, eb_ref,
                  xc_ref, hb_ref, uext, cout, state):
    c = pl.program_id(1)
    nc = pl.num_programs(1)
    chunk = nc - 1 - c

    @pl.when(c == 0)
    def _():
        state[...] = jnp.zeros_like(state)

    first = HALO - (D_CONV - 1) // 2
    sub = 8
    for slab in range(CONV_DIM // LANES):
        cols = slice(slab * LANES, (slab + 1) * LANES)
        uext[slab, 0:HALO, :] = jnp.where(chunk > 0, prev_ref[0, :, cols], 0.0)
        uext[slab, HALO:HALO + CHUNK, :] = cur_ref[0, :, cols]
        uext[slab, HALO + CHUNK:2 * HALO + CHUNK, :] = jnp.where(chunk < nc - 1, next_ref[0, :, cols], 0.0)
        uext[slab, 2 * HALO + CHUNK:, :] = jnp.zeros((CONV_ROWS - 2 * HALO - CHUNK, LANES), F32)
        bias = jnp.broadcast_to(cb_ref[:, cols], (sub, LANES))
        taps = [jnp.broadcast_to(cw_ref[k:k + 1, cols], (sub, LANES)) for k in range(D_CONV)]
        for i in range(CONV_STRIDE):
            acc = bias
            for k in range(D_CONV):
                acc = acc + taps[k] * uext[slab, pl.ds(first + k + i, sub, stride=CONV_STRIDE), :]
            cout[slab, pl.ds(i, sub, stride=CONV_STRIDE), :] = _silu(acc)
        xc_ref[0, :, cols] = cout[slab, 0:CHUNK, :]

    dtv, loga = _dt_and_loga(dt_ref, dtb_ref, alog_ref)
    _, upper = _tri_masks()
    rcum = _dot_sel_lhs(jnp.where(upper, 1.0, 0.0).astype(BF16), loga)
    r0 = rcum[0:1, :]
    wgt = dtv * jnp.exp2(r0 - rcum)
    wgt_x = _dot(_pack3(wgt), eb_ref[...])
    dec_x = _dot(_pack3(jnp.broadcast_to(jnp.exp2(r0), (HALO, LANES))), eb_ref[...])[0:1]

    hb_ref[0, 0] = state[...].astype(BF16)
    xw = (xc_ref[0, :, 0:D_SSM] * wgt_x).astype(BF16)
    for g in range(SSM_GROUPS):
        gs = slice(g * GROUP_W, (g + 1) * GROUP_W)
        b_t = xc_ref[0, :, D_SSM + g * D_STATE:D_SSM + (g + 1) * D_STATE].T.astype(BF16)
        state[:, gs] = state[:, gs] * dec_x[:, gs] + _dot(b_t, xw[:, gs])


def _ssd_bwd(xbc, dt, conv_w, conv_b, dt_bias, a_log, sel_b):
    bsz, t, _ = xbc.shape
    nc = t // CHUNK
    hpc = CHUNK // HALO
    nhalo = t // HALO
    rev = lambda c: nc - 1 - c
    return pl.pallas_call(
        _ssd_bwd_body,
        grid=(bsz, nc),
        in_specs=[
            pl.BlockSpec((1, CHUNK, CONV_DIM), lambda b, c: (b, rev(c), 0)),
            pl.BlockSpec((1, HALO, CONV_DIM), lambda b, c: (b, jnp.maximum(rev(c) * hpc - 1, 0), 0)),
            pl.BlockSpec((1, HALO, CONV_DIM), lambda b, c: (b, jnp.minimum((rev(c) + 1) * hpc, nhalo - 1), 0)),
            pl.BlockSpec((1, CHUNK, LANES), lambda b, c: (b, rev(c), 0)),
            _const_spec(conv_w.shape), _const_spec(conv_b.shape),
            _const_spec(dt_bias.shape), _const_spec(a_log.shape), _const_spec(sel_b.shape),
        ],
        out_specs=[
            pl.BlockSpec((1, CHUNK, CONV_DIM), lambda b, c: (b, rev(c), 0)),
            pl.BlockSpec((1, 1, D_STATE, D_SSM), lambda b, c: (b, rev(c), 0, 0)),
        ],
        out_shape=[
            jax.ShapeDtypeStruct((bsz, t, CONV_DIM), F32),
            jax.ShapeDtypeStruct((bsz, nc, D_STATE, D_SSM), BF16),
        ],
        scratch_shapes=[pltpu.VMEM((CONV_DIM // LANES, CONV_ROWS, LANES), F32),
                        pltpu.VMEM((CONV_DIM // LANES, 8 * CONV_STRIDE, LANES), F32),
                        pltpu.VMEM((D_STATE, D_SSM), F32)],
        compiler_params=pltpu.CompilerParams(
            dimension_semantics=("arbitrary", "arbitrary"), vmem_limit_bytes=VMEM_LIMIT),
    )(xbc, xbc, xbc, dt, conv_w, conv_b, dt_bias, a_log, sel_b)


def _ssd_fwd_body(xc_ref, dt_ref, z_ref, hb_ref, dtb_ref, alog_ref, dsk_ref, nw_ref, ef_ref, eb_ref,
                  y_ref, state, ybuf):
    @pl.when(pl.program_id(1) == 0)
    def _():
        state[...] = jnp.zeros_like(state)

    for sub in range(FWD_CHUNKS):
        rows = slice(sub * CHUNK, (sub + 1) * CHUNK)
        _ssd_fwd_chunk(xc_ref.at[0, rows], dt_ref.at[:, rows], z_ref.at[0, rows], hb_ref.at[0, sub], dtb_ref, alog_ref,
                       dsk_ref, nw_ref, ef_ref, eb_ref, y_ref.at[0, rows], state, ybuf.at[rows])


def _ssd_fwd_chunk(xc_ref, dt_ref, z_ref, hb_ref, dtb_ref, alog_ref, dsk_ref, nw_ref, ef_ref, eb_ref,
                   y_ref, state, ybuf):
    dtv, loga = _dt_and_loga(dt_ref, dtb_ref, alog_ref)
    lower, upper = _tri_masks()
    cum_f = _dot_sel_lhs(jnp.where(lower, 1.0, 0.0).astype(BF16), loga)
    cum_b = _dot_sel_lhs(jnp.where(upper, 1.0, 0.0).astype(BF16), loga)
    lane = lax.broadcasted_iota(jnp.int32, (CHUNK, LANES), 1)
    cum = jnp.where((lane & SSM_HEADS) == 0, cum_f, cum_b)
    ecum_p = _pack3(jnp.exp2(cum))
    wgt_p = _pack3(dtv * jnp.exp2(cum_f[CHUNK - 1:CHUNK, :] - cum_f))
    src_t = (cum - jnp.log2(dtv)).T
    first_half = lane < SSM_HEAD_DIM

    for g in range(SSM_GROUPS):
        gs = slice(g * GROUP_W, (g + 1) * GROUP_W)
        b_f32 = xc_ref[:, D_SSM + g * D_STATE:D_SSM + (g + 1) * D_STATE]
        b_g = b_f32.astype(BF16)
        c0 = D_SSM + SSM_GROUPS * D_STATE + g * D_STATE
        c_g = xc_ref[:, c0:c0 + D_STATE].astype(BF16)
        cb = _dot_nt(c_g, b_g)
        ea_g = _dot(ecum_p, ef_ref[:, gs])
        er_g = _dot(ecum_p, eb_ref[:, gs])
        y_off = _dot(c_g, state[:, gs].astype(BF16)) * ea_g + _dot(c_g, hb_ref[:, gs]) * er_g
        xw = (xc_ref[:, gs] * _dot(wgt_p, ef_ref[:, gs])).astype(BF16)
        state[:, gs] = state[:, gs] * ea_g[CHUNK - 1:CHUNK, :] + _dot(b_f32.T.astype(BF16), xw)
        pairs = GROUP_W // LANES
        for pp in range(pairs):
            pr = g * pairs + pp
            ps = slice(pr * LANES, (pr + 1) * LANES)
            mats = []
            for h in (2 * pr, 2 * pr + 1):
                hb = SSM_HEADS + h
                seg_f = jnp.broadcast_to(cum[:, h:h + 1], (CHUNK, CHUNK)) - src_t[h:h + 1, :]
                seg_b = jnp.broadcast_to(cum[:, hb:hb + 1], (CHUNK, CHUNK)) - src_t[hb:hb + 1, :]
                d_f = jnp.exp2(jnp.where(lower, seg_f, MASKED))
                d_b = jnp.exp2(jnp.where(upper, seg_b, MASKED))
                mats.append(((d_f + d_b) * cb).astype(BF16))
            xs = xc_ref[:, ps]
            xp = xs.astype(BF16)
            zero = jnp.zeros_like(xp)
            rhs = jnp.concatenate([jnp.where(first_half, xp, zero), jnp.where(first_half, zero, xp)], axis=0)
            y_diag = _dot(jnp.concatenate(mats, axis=1), rhs)
            ybuf[:, ps] = y_diag + y_off[:, pp * LANES:(pp + 1) * LANES] + dsk_ref[:, ps] * xs
        gated = ybuf[:, gs] * _silu(z_ref[:, gs])
        ms = jnp.mean(gated * gated, axis=-1, keepdims=True)
        y_ref[:, gs] = (gated * lax.rsqrt(ms + EPS) * nw_ref[:, gs]).astype(y_ref.dtype)


def _ssd_fwd(xc, dt, z, hb, dt_bias, a_log, d_skip_x, norm_w, sel_f, sel_b):
    bsz, t, _ = xc.shape
    step = FWD_CHUNKS * CHUNK
    blk = lambda w: pl.BlockSpec((1, step, w), lambda b, c: (b, c, 0))
    return pl.pallas_call(
        _ssd_fwd_body,
        grid=(bsz, t // step),
        in_specs=[
            blk(CONV_DIM), blk(LANES), blk(D_SSM),
            pl.BlockSpec((1, FWD_CHUNKS, D_STATE, D_SSM), lambda b, c: (b, c, 0, 0)),
            _const_spec(dt_bias.shape), _const_spec(a_log.shape), _const_spec(d_skip_x.shape),
            _const_spec(norm_w.shape), _const_spec(sel_f.shape), _const_spec(sel_b.shape),
        ],
        out_specs=blk(D_SSM),
        out_shape=jax.ShapeDtypeStruct((bsz, t, D_SSM), BF16),
        scratch_shapes=[pltpu.VMEM((D_STATE, D_SSM), F32), pltpu.VMEM((step, D_SSM), F32)],
        compiler_params=pltpu.CompilerParams(
            dimension_semantics=("arbitrary", "arbitrary"), vmem_limit_bytes=VMEM_LIMIT),
    )(xc, dt, z, hb, dt_bias, a_log, d_skip_x, norm_w, sel_f, sel_b)


NORM_ROWS = 512


def _build_bias(g_ref, colmask_ref, bias):
    rel = _block_row_rel()
    half = lax.broadcasted_iota(jnp.int32, (GRID_W, LANES), 1) < GRID_W
    nrel = 2 * NA_ROWS

    @functools.cache
    def toeplitz(hh, r, upper_half):
        if upper_half:
            return pltpu.roll(toeplitz(hh, r, False), GRID_W, 1)
        row = g_ref[0, hh * nrel + r:hh * nrel + r + 1, :] * LOG2E
        return pltpu.roll(jnp.broadcast_to(row, (GRID_W, LANES)), 0, 1, stride=1, stride_axis=0)

    for kind in range(rel.shape[0]):
        for hh in range(2):
            for u in range(NA_BLOCK):
                row0 = (hh * NA_BLOCK + u) * GRID_W
                for a in range(0, NA_SPAN, 2):
                    r_lo, r_hi = int(rel[kind, u, a]), int(rel[kind, u, a + 1])
                    lo = toeplitz(hh, r_lo, False) if r_lo >= 0 else MASKED
                    hi = toeplitz(hh, r_hi, True) if r_hi >= 0 else MASKED
                    tile = jnp.where(half, lo, hi) + colmask_ref[...]
                    bias[kind, row0:row0 + GRID_W, a * GRID_W:(a + 2) * GRID_W] = tile


def _na_body(q_ref, k_ref, v_ref, g_ref, colmask_ref, qw_ref, kw_ref, seg_ref, o_ref, qn, kn, sbuf, bias):
    t = q_ref.shape[1]
    rows = t // GRID_W

    @pl.when(pl.program_id(1) == 0)
    def _():
        _build_bias(g_ref, colmask_ref, bias)

    def norm_step(i, carry):
        sl = pl.ds(pl.multiple_of(i * NORM_ROWS, NORM_ROWS), NORM_ROWS)
        for src, w_ref, dst in ((q_ref, qw_ref, qn), (k_ref, kw_ref, kn)):
            xx = src[0, sl, :]
            sq = xx * xx
            hi = sq.astype(BF16)
            lo = (sq - hi.astype(F32)).astype(BF16)
            ssq = _dot(hi, seg_ref[...]) + _dot(lo, seg_ref[...])
            dst[sl, :] = (xx * lax.rsqrt(ssq * (1.0 / NA_HEAD_DIM) + EPS) * w_ref[...]).astype(BF16)
        return carry

    lax.fori_loop(0, t // NORM_ROWS, norm_step, 0)

    nblk = rows // NA_BLOCK
    bq = NA_BLOCK * GRID_W
    first_half = lax.broadcasted_iota(jnp.int32, (bq, LANES), 1) < NA_HEAD_DIM

    def offsets(bi):
        i0 = bi * NA_BLOCK
        rlo = jnp.clip(i0 - NA_ROWS // 2, 0, rows - NA_SPAN)
        return pl.multiple_of(i0 * GRID_W, bq), pl.multiple_of(rlo * GRID_W, 2 * GRID_W)

    def scores(bi, slot):
        kind = jnp.where(bi == 0, 0, jnp.where(bi == nblk - 1, 2, 1))
        qoff, koff = offsets(bi)
        qb = qn[pl.ds(qoff, bq), :]
        zero = jnp.zeros_like(qb)
        qs = jnp.concatenate([jnp.where(first_half, qb, zero), jnp.where(first_half, zero, qb)], axis=0)
        sbuf[slot] = _dot_nt(qs, kn[pl.ds(koff, NA_BKEYS), :]) + bias[kind]

    def attend(bi, slot):
        qoff, koff = offsets(bi)
        s = sbuf[slot]
        p = jnp.exp2(s - jnp.max(s, axis=-1, keepdims=True))
        denom = jnp.sum(p, axis=-1, keepdims=True)
        o2 = _dot(p.astype(BF16), v_ref[0, pl.ds(koff, NA_BKEYS), :]) * (1.0 / denom)
        o_ref[0, pl.ds(qoff, bq), :] = jnp.where(first_half, o2[0:bq], o2[bq:]).astype(o_ref.dtype)

    scores(0, 0)

    def pair_step(i, carry):
        b0 = 2 * i
        scores(b0 + 1, 1)
        attend(b0, 0)
        scores(jnp.minimum(b0 + 2, nblk - 1), 0)
        attend(b0 + 1, 1)
        return carry

    lax.fori_loop(0, nblk // 2, pair_step, 0)


def _natten(q, k, v, g, colmask, qw, kw, seg):
    bsz, t, _ = q.shape
    nq = 2 * NA_BLOCK * GRID_W
    blk = pl.BlockSpec((1, t, LANES), lambda p, b: (b, 0, p))
    return pl.pallas_call(
        _na_body,
        grid=(NA_PAIRS, bsz),
        in_specs=[
            blk, blk, blk,
            pl.BlockSpec((1,) + g.shape[1:], lambda p, b: (p, 0, 0)), _const_spec(colmask.shape),
            _const_spec(qw.shape), _const_spec(kw.shape), _const_spec(seg.shape),
        ],
        out_specs=blk,
        out_shape=jax.ShapeDtypeStruct((bsz, t, D_NA), BF16),
        scratch_shapes=[pltpu.VMEM((t, LANES), BF16), pltpu.VMEM((t, LANES), BF16),
                        pltpu.VMEM((2, nq, NA_BKEYS), F32), pltpu.VMEM((3, nq, NA_BKEYS), F32)],
        compiler_params=pltpu.CompilerParams(
            dimension_semantics=("arbitrary", "arbitrary"), vmem_limit_bytes=VMEM_LIMIT),
    )(q, k, v, g, colmask, qw, kw, seg)


def _block_row_rel():
    rel = -np.ones((3, NA_BLOCK, NA_SPAN), np.int64)
    for u in range(NA_BLOCK):
        for a in range(NA_SPAN):
            if a < NA_ROWS:
                rel[0, u, a] = a - u + NA_ROWS - 1
            if u <= a < u + NA_ROWS:
                rel[1, u, a] = a - u + NA_ROWS - 1 - NA_ROWS // 2
            if a >= NA_SPAN - NA_ROWS:
                rel[2, u, a] = a - u + NA_BLOCK - NA_SPAN + NA_ROWS - 1
    return rel


def _na_bias_rows(rpb):
    centre = NA_COLS - 1
    rows = jnp.concatenate(
        [rpb[:, :, centre:], jnp.zeros(rpb.shape[:2] + (LANES - rpb.shape[2],), rpb.dtype), rpb[:, :, :centre]],
        axis=2).astype(F32)
    rows = jnp.pad(rows, ((0, 0), (0, 2 * NA_ROWS - rpb.shape[1]), (0, 0)))
    return rows.reshape(NA_PAIRS, 2 * 2 * NA_ROWS, LANES)


def _na_col_mask():
    j = np.arange(GRID_W)
    c0 = np.clip(j - NA_COLS // 2, 0, GRID_W - NA_COLS)
    cc = np.arange(GRID_W)
    valid = (cc[None, :] >= c0[:, None]) & (cc[None, :] < c0[:, None] + NA_COLS)
    return jnp.asarray(np.tile(np.where(valid, 0.0, MASKED), (1, 2)), F32)


FF_STEP = 512


def _outmlp_body(x_ref, ys_ref, yn_ref, wo_ref, nw_ref, w1_ref, w2_ref, o_ref):
    dmix = ys_ref.shape[-1]
    x1 = x_ref[...] + _dot(ys_ref[...], wo_ref[0:dmix, :]) + _dot(yn_ref[...], wo_ref[dmix:, :])
    ms = jnp.mean(x1 * x1, axis=-1, keepdims=True)
    h = (x1 * lax.rsqrt(ms + EPS) * nw_ref[...]).astype(BF16)
    o_ref[...] = x1
    for f in range(0, w1_ref.shape[1], FF_STEP):
        u = jnp.maximum(_dot(h, w1_ref[:, f:f + FF_STEP]), 0.0)
        o_ref[...] += _dot((u * u).astype(BF16), w2_ref[f:f + FF_STEP, :])


def _outmlp(x2, ys, yn, w_out, norm_w, w1, w2, tm):
    n, d = x2.shape
    row = lambda i: (i, 0)
    return pl.pallas_call(
        _outmlp_body,
        grid=(n // tm,),
        in_specs=[
            pl.BlockSpec((tm, d), row), pl.BlockSpec((tm, ys.shape[1]), row), pl.BlockSpec((tm, yn.shape[1]), row),
            _const_spec(w_out.shape), _const_spec((1, d)), _const_spec(w1.shape), _const_spec(w2.shape),
        ],
        out_specs=pl.BlockSpec((tm, d), row),
        out_shape=jax.ShapeDtypeStruct((n, d), F32),
        compiler_params=pltpu.CompilerParams(
            dimension_semantics=("arbitrary",), vmem_limit_bytes=VMEM_LIMIT),
    )(x2, ys, yn, w_out, norm_w, w1, w2)


def _head_select(first_lane):
    sel = np.zeros((LANES, D_SSM), np.float32)
    for rep in range(DT_COPIES):
        for h in range(SSM_HEADS):
            sel[rep * DT_LANES + first_lane + h, h * SSM_HEAD_DIM:(h + 1) * SSM_HEAD_DIM] = 1.0
    return jnp.asarray(sel, BF16)


def _dt_lanes(fwd, bwd):
    v = jnp.tile(jnp.concatenate([fwd, bwd]).astype(F32), DT_COPIES)
    return jnp.pad(v, (0, LANES - v.shape[0]))[None, :]


def _layer(x, norm_mix_w, w_in, conv_w, conv_b, dt_bias_fwd, dt_bias_bwd, a_log_fwd, a_log_bwd, d_skip,
           ssm_norm_w, q_norm_w, k_norm_w, rel_pos_bias, w_out, norm_mlp_w, w_mlp_in, w_mlp_out, tm):
    bsz, t, d = x.shape
    n = bsz * t
    x2 = x.reshape(n, d)

    o_dt = D_SSM + CONV_DIM
    o_q = o_dt + DT_LANES
    w_dt = w_in[:, o_dt:o_q]
    w_cat = jnp.concatenate(
        [w_in[:, :o_dt], w_in[:, o_q:]] + [w_dt] * DT_COPIES
        + [jnp.zeros((d, LANES - DT_COPIES * DT_LANES), w_in.dtype)], axis=1).astype(BF16)
    z, xbc, q, k, v, dt = _inproj(x2, norm_mix_w[None, :], w_cat, tm)
    r3 = lambda a: a.reshape(bsz, t, a.shape[-1])

    dt_bias = _dt_lanes(dt_bias_fwd, dt_bias_bwd)
    a_log = _dt_lanes(a_log_fwd, a_log_bwd)
    sel_f, sel_b = _head_select(0), _head_select(SSM_HEADS)
    xc, hb = _ssd_bwd(r3(xbc), r3(dt), conv_w, conv_b[None, :], dt_bias, a_log, sel_b)
    y_ssm = _ssd_fwd(xc, r3(dt), r3(z), hb, dt_bias, a_log,
                     jnp.repeat(d_skip.astype(F32), SSM_HEAD_DIM)[None, :], ssm_norm_w[None, :], sel_f, sel_b)

    lane_head = np.arange(LANES) // NA_HEAD_DIM
    seg = jnp.asarray(lane_head[:, None] == lane_head[None, :], BF16)
    qw = (jnp.tile(q_norm_w.astype(F32), 2) * (NA_HEAD_DIM ** -0.5 * LOG2E))[None, :]
    kw = jnp.tile(k_norm_w.astype(F32), 2)[None, :]
    y_na = _natten(r3(q), r3(k), r3(v), _na_bias_rows(rel_pos_bias), _na_col_mask(), qw, kw, seg)

    out = _outmlp(x2, y_ssm.reshape(n, D_SSM), y_na.reshape(n, D_NA), w_out.astype(BF16),
                  norm_mlp_w[None, :], w_mlp_in.astype(BF16), w_mlp_out.astype(BF16), tm)
    return out.reshape(bsz, t, d)


def kernel(x, norm_mix_w, w_in, conv_w, conv_b, dt_bias_fwd, dt_bias_bwd, a_log_fwd, a_log_bwd, d_skip,
           ssm_norm_w, q_norm_w, k_norm_w, rel_pos_bias, w_out, norm_mlp_w, w_mlp_in, w_mlp_out):
    tm = min(512, x.shape[0] * x.shape[1])
    for layer in range(norm_mix_w.shape[0]):
        x = _layer(x, norm_mix_w[layer], w_in[layer], conv_w[layer], conv_b[layer], dt_bias_fwd[layer],
                   dt_bias_bwd[layer], a_log_fwd[layer], a_log_bwd[layer], d_skip[layer], ssm_norm_w[layer],
                   q_norm_w[layer], k_norm_w[layer], rel_pos_bias[layer], w_out[layer], norm_mlp_w[layer],
                   w_mlp_in[layer], w_mlp_out[layer], tm)
    return x
```

```python
import functools

import numpy as np
import jax
import jax.numpy as jnp
from jax import lax
from jax.experimental import pallas as pl
from jax.experimental.pallas import tpu as pltpu

F32 = jnp.float32
BF16 = jnp.bfloat16

LANES = 128
GRID_W = 64
SSM_HEADS = 16
SSM_HEAD_DIM = 64
D_SSM = SSM_HEADS * SSM_HEAD_DIM
SSM_GROUPS = 2
GROUP_W = D_SSM // SSM_GROUPS
D_STATE = 128
D_CONV = 5
CHUNK = 128
FWD_CHUNKS = 4
BWD_CHUNKS = 4
CONV_DIM = D_SSM + 2 * SSM_GROUPS * D_STATE
DT_LANES = 2 * SSM_HEADS
DT_COPIES = 3
NA_HEADS = 16
NA_HEAD_DIM = 64
D_NA = NA_HEADS * NA_HEAD_DIM
NA_ROWS = 8
NA_COLS = 16
NA_PAIRS = NA_HEADS // 2
NA_BLOCK = 4
NA_SPAN = NA_ROWS + NA_BLOCK
NA_BKEYS = NA_SPAN * GRID_W
LOG2E = 1.4426950408889634
EPS = 1e-5
MASKED = -1e30
HALO = 8
CONV_STRIDE = CHUNK // 8 + 1
VMEM_LIMIT = 56 * 1024 * 1024


def _dot(a, b):
    return jnp.dot(a, b, preferred_element_type=F32)


def _dot_nt(a, b):
    return lax.dot_general(a, b, (((1,), (1,)), ((), ())), preferred_element_type=F32)


def _split3(x):
    hi = x.astype(BF16)
    r1 = x - hi.astype(F32)
    mid = r1.astype(BF16)
    lo = (r1 - mid.astype(F32)).astype(BF16)
    return hi, mid, lo


def _dot_sel_rhs(x, sel):
    hi, mid, lo = _split3(x)
    return _dot(hi, sel) + _dot(mid, sel) + _dot(lo, sel)


def _dot_sel_lhs(sel, x):
    hi, mid, lo = _split3(x)
    return _dot(sel, hi) + _dot(sel, mid) + _dot(sel, lo)


def _pack3(x):
    lane = lax.broadcasted_iota(jnp.int32, x.shape, x.ndim - 1)
    hi = x.astype(BF16).astype(F32)
    r1 = x - hi
    mid = r1.astype(BF16).astype(F32)
    return jnp.where(lane < DT_LANES, hi, jnp.where(lane < 2 * DT_LANES, mid, r1 - mid)).astype(BF16)


def _softplus(x):
    return jnp.maximum(x, 0.0) + jnp.log(1.0 + jnp.exp(-jnp.abs(x)))


def _silu(x):
    return x * (1.0 / (1.0 + jnp.exp(-x)))


def _const_spec(shape):
    nd = len(shape)
    return pl.BlockSpec(shape, lambda *_: (0,) * nd, pipeline_mode=pl.Buffered(1))


def _inproj_body(x_ref, nw_ref, w_ref, z_ref, xbc_ref, q_ref, k_ref, v_ref, dt_ref):
    x = x_ref[...]
    ms = jnp.mean(x * x, axis=-1, keepdims=True)
    h = (x * lax.rsqrt(ms + EPS) * nw_ref[...]).astype(BF16)
    off = 0
    for ref in (z_ref, xbc_ref, q_ref, k_ref, v_ref, dt_ref):
        width = ref.shape[-1]
        ref[...] = _dot(h, w_ref[:, off:off + width]).astype(ref.dtype)
        off += width


def _inproj(x2, norm_w, w_cat, tm):
    n, d = x2.shape
    widths = (D_SSM, CONV_DIM, D_NA, D_NA, D_NA, LANES)
    dtypes = (F32, F32, F32, F32, BF16, F32)
    row = lambda i: (i, 0)
    return pl.pallas_call(
        _inproj_body,
        grid=(n // tm,),
        in_specs=[pl.BlockSpec((tm, d), row), _const_spec((1, d)), _const_spec(w_cat.shape)],
        out_specs=[pl.BlockSpec((tm, w), row) for w in widths],
        out_shape=[jax.ShapeDtypeStruct((n, w), dt) for w, dt in zip(widths, dtypes)],
        compiler_params=pltpu.CompilerParams(
            dimension_semantics=("arbitrary",), vmem_limit_bytes=VMEM_LIMIT),
    )(x2, norm_w, w_cat)


def _tri_masks():
    t = lax.broadcasted_iota(jnp.int32, (CHUNK, CHUNK), 0)
    s = lax.broadcasted_iota(jnp.int32, (CHUNK, CHUNK), 1)
    return s <= t, s >= t


def _dt_and_loga(dt_ref, dtb_ref, alog_ref):
    dtv = _softplus(dt_ref[0] + dtb_ref[...])
    return dtv, dtv * (-LOG2E * jnp.exp(alog_ref[...]))


def _ssd_bwd_body(cur_ref, prev_ref, next_ref, dt_ref, cw_ref, cb_ref, dtb_ref, alog_ref, eb_ref,
                  xc_ref, hb_ref, uext, cout, state):
    c = pl.program_id(1)
    nsteps = pl.num_programs(1)
    step = nsteps - 1 - c
    rows_in = BWD_CHUNKS * CHUNK

    @pl.when(c == 0)
    def _():
        state[...] = jnp.zeros_like(state)

    first = HALO - (D_CONV - 1) // 2
    sub = 8
    nslab = CONV_DIM // LANES
    for slab in range(nslab):
        cols = slice(slab * LANES, (slab + 1) * LANES)
        uext[slab, 0:HALO, :] = jnp.where(step > 0, prev_ref[0, :, cols], 0.0)
        uext[slab, HALO:HALO + rows_in, :] = cur_ref[0, :, cols]
        uext[slab, HALO + rows_in:2 * HALO + rows_in, :] = jnp.where(step < nsteps - 1, next_ref[0, :, cols], 0.0)
        uext[slab, 2 * HALO + rows_in:, :] = jnp.zeros((uext.shape[1] - 2 * HALO - rows_in, LANES), F32)
        bias = jnp.broadcast_to(cb_ref[:, cols], (sub, LANES))
        taps = [jnp.broadcast_to(cw_ref[k:k + 1, cols], (sub, LANES)) for k in range(D_CONV)]
        for ch in range(BWD_CHUNKS):
            for i in range(CONV_STRIDE):
                acc = bias
                for k in range(D_CONV):
                    start = ch * CHUNK + first + k + i
                    acc = acc + taps[k] * uext[slab, pl.ds(start, sub, stride=CONV_STRIDE), :]
                cout[ch * nslab + slab, pl.ds(i, sub, stride=CONV_STRIDE), :] = _silu(acc)
            xc_ref[0, ch * CHUNK:(ch + 1) * CHUNK, cols] = cout[ch * nslab + slab, 0:CHUNK, :]

    _, upper = _tri_masks()
    upper_sel = jnp.where(upper, 1.0, 0.0).astype(BF16)
    for ch in reversed(range(BWD_CHUNKS)):
        rows = slice(ch * CHUNK, (ch + 1) * CHUNK)
        dtv, loga = _dt_and_loga(dt_ref.at[:, rows], dtb_ref, alog_ref)
        rcum = _dot_sel_lhs(upper_sel, loga)
        r0 = rcum[0:1, :]
        wgt = dtv * jnp.exp2(r0 - rcum)
        wgt_x = _dot(_pack3(wgt), eb_ref[...])
        dec_x = _dot(_pack3(jnp.broadcast_to(jnp.exp2(r0), (HALO, LANES))), eb_ref[...])[0:1]

        hb_ref[0, ch] = state[...].astype(BF16)
        xw = (xc_ref[0, rows, 0:D_SSM] * wgt_x).astype(BF16)
        for g in range(SSM_GROUPS):
            gs = slice(g * GROUP_W, (g + 1) * GROUP_W)
            b_t = xc_ref[0, rows, D_SSM + g * D_STATE:D_SSM + (g + 1) * D_STATE].T.astype(BF16)
            state[:, gs] = state[:, gs] * dec_x[:, gs] + _dot(b_t, xw[:, gs])


def _ssd_bwd(xbc, dt, conv_w, conv_b, dt_bias, a_log, sel_b):
    bsz, t, _ = xbc.shape
    rows = BWD_CHUNKS * CHUNK
    nsteps = t // rows
    hps = rows // HALO
    nhalo = t // HALO
    nslab = CONV_DIM // LANES
    rev = lambda c: nsteps - 1 - c
    return pl.pallas_call(
        _ssd_bwd_body,
        grid=(bsz, nsteps),
        in_specs=[
            pl.BlockSpec((1, rows, CONV_DIM), lambda b, c: (b, rev(c), 0)),
            pl.BlockSpec((1, HALO, CONV_DIM), lambda b, c: (b, jnp.maximum(rev(c) * hps - 1, 0), 0)),
            pl.BlockSpec((1, HALO, CONV_DIM), lambda b, c: (b, jnp.minimum((rev(c) + 1) * hps, nhalo - 1), 0)),
            pl.BlockSpec((1, rows, LANES), lambda b, c: (b, rev(c), 0)),
            _const_spec(conv_w.shape), _const_spec(conv_b.shape),
            _const_spec(dt_bias.shape), _const_spec(a_log.shape), _const_spec(sel_b.shape),
        ],
        out_specs=[
            pl.BlockSpec((1, rows, CONV_DIM), lambda b, c: (b, rev(c), 0)),
            pl.BlockSpec((1, BWD_CHUNKS, D_STATE, D_SSM), lambda b, c: (b, rev(c), 0, 0)),
        ],
        out_shape=[
            jax.ShapeDtypeStruct((bsz, t, CONV_DIM), F32),
            jax.ShapeDtypeStruct((bsz, t // CHUNK, D_STATE, D_SSM), BF16),
        ],
        scratch_shapes=[pltpu.VMEM((nslab, rows + 3 * HALO, LANES), F32),
                        pltpu.VMEM((BWD_CHUNKS * nslab, 8 * CONV_STRIDE, LANES), F32),
                        pltpu.VMEM((D_STATE, D_SSM), F32)],
        compiler_params=pltpu.CompilerParams(
            dimension_semantics=("arbitrary", "arbitrary"), vmem_limit_bytes=VMEM_LIMIT),
    )(xbc, xbc, xbc, dt, conv_w, conv_b, dt_bias, a_log, sel_b)


def _ssd_fwd_body(xc_ref, dt_ref, z_ref, hb_ref, dtb_ref, alog_ref, dsk_ref, nw_ref, ef_ref, eb_ref,
                  y_ref, state, ybuf):
    @pl.when(pl.program_id(1) == 0)
    def _():
        state[...] = jnp.zeros_like(state)

    for sub in range(FWD_CHUNKS):
        rows = slice(sub * CHUNK, (sub + 1) * CHUNK)
        _ssd_fwd_chunk(xc_ref.at[0, rows], dt_ref.at[:, rows], z_ref.at[0, rows], hb_ref.at[0, sub], dtb_ref, alog_ref,
                       dsk_ref, nw_ref, ef_ref, eb_ref, y_ref.at[0, rows], state, ybuf.at[rows])


def _ssd_fwd_chunk(xc_ref, dt_ref, z_ref, hb_ref, dtb_ref, alog_ref, dsk_ref, nw_ref, ef_ref, eb_ref,
                   y_ref, state, ybuf):
    dtv, loga = _dt_and_loga(dt_ref, dtb_ref, alog_ref)
    lower, upper = _tri_masks()
    cum_f = _dot_sel_lhs(jnp.where(lower, 1.0, 0.0).astype(BF16), loga)
    cum_b = _dot_sel_lhs(jnp.where(upper, 1.0, 0.0).astype(BF16), loga)
    lane = lax.broadcasted_iota(jnp.int32, (CHUNK, LANES), 1)
    cum = jnp.where((lane & SSM_HEADS) == 0, cum_f, cum_b)
    ecum_p = _pack3(jnp.exp2(cum))
    wgt_p = _pack3(dtv * jnp.exp2(cum_f[CHUNK - 1:CHUNK, :] - cum_f))
    src_t = (cum - jnp.log2(dtv)).T
    first_half = lane < SSM_HEAD_DIM

    for g in range(SSM_GROUPS):
        gs = slice(g * GROUP_W, (g + 1) * GROUP_W)
        b_f32 = xc_ref[:, D_SSM + g * D_STATE:D_SSM + (g + 1) * D_STATE]
        b_g = b_f32.astype(BF16)
        c0 = D_SSM + SSM_GROUPS * D_STATE + g * D_STATE
        c_g = xc_ref[:, c0:c0 + D_STATE].astype(BF16)
        cb = _dot_nt(c_g, b_g)
        ea_g = _dot(ecum_p, ef_ref[:, gs])
        er_g = _dot(ecum_p, eb_ref[:, gs])
        y_off = _dot(c_g, state[:, gs].astype(BF16)) * ea_g + _dot(c_g, hb_ref[:, gs]) * er_g
        xw = (xc_ref[:, gs] * _dot(wgt_p, ef_ref[:, gs])).astype(BF16)
        state[:, gs] = state[:, gs] * ea_g[CHUNK - 1:CHUNK, :] + _dot(b_f32.T.astype(BF16), xw)
        pairs = GROUP_W // LANES
        for pp in range(pairs):
            pr = g * pairs + pp
            ps = slice(pr * LANES, (pr + 1) * LANES)
            mats = []
            for h in (2 * pr, 2 * pr + 1):
                hb = SSM_HEADS + h
                seg_f = jnp.broadcast_to(cum[:, h:h + 1], (CHUNK, CHUNK)) - src_t[h:h + 1, :]
                seg_b = jnp.broadcast_to(cum[:, hb:hb + 1], (CHUNK, CHUNK)) - src_t[hb:hb + 1, :]
                d_f = jnp.exp2(jnp.where(lower, seg_f, MASKED))
                d_b = jnp.exp2(jnp.where(upper, seg_b, MASKED))
                mats.append(((d_f + d_b) * cb).astype(BF16))
            xs = xc_ref[:, ps]
            xp = xs.astype(BF16)
            zero = jnp.zeros_like(xp)
            rhs = jnp.concatenate([jnp.where(first_half, xp, zero), jnp.where(first_half, zero, xp)], axis=0)
            y_diag = _dot(jnp.concatenate(mats, axis=1), rhs)
            ybuf[:, ps] = y_diag + y_off[:, pp * LANES:(pp + 1) * LANES] + dsk_ref[:, ps] * xs
        gated = ybuf[:, gs] * _silu(z_ref[:, gs])
        ms = jnp.mean(gated * gated, axis=-1, keepdims=True)
        y_ref[:, gs] = (gated * lax.rsqrt(ms + EPS) * nw_ref[:, gs]).astype(y_ref.dtype)


def _ssd_fwd(xc, dt, z, hb, dt_bias, a_log, d_skip_x, norm_w, sel_f, sel_b):
    bsz, t, _ = xc.shape
    step = FWD_CHUNKS * CHUNK
    blk = lambda w: pl.BlockSpec((1, step, w), lambda b, c: (b, c, 0))
    return pl.pallas_call(
        _ssd_fwd_body,
        grid=(bsz, t // step),
        in_specs=[
            blk(CONV_DIM), blk(LANES), blk(D_SSM),
            pl.BlockSpec((1, FWD_CHUNKS, D_STATE, D_SSM), lambda b, c: (b, c, 0, 0)),
            _const_spec(dt_bias.shape), _const_spec(a_log.shape), _const_spec(d_skip_x.shape),
            _const_spec(norm_w.shape), _const_spec(sel_f.shape), _const_spec(sel_b.shape),
        ],
        out_specs=blk(D_SSM),
        out_shape=jax.ShapeDtypeStruct((bsz, t, D_SSM), BF16),
        scratch_shapes=[pltpu.VMEM((D_STATE, D_SSM), F32), pltpu.VMEM((step, D_SSM), F32)],
        compiler_params=pltpu.CompilerParams(
            dimension_semantics=("arbitrary", "arbitrary"), vmem_limit_bytes=VMEM_LIMIT),
    )(xc, dt, z, hb, dt_bias, a_log, d_skip_x, norm_w, sel_f, sel_b)


NORM_ROWS = 512


def _build_bias(g_ref, colmask_ref, bias):
    rel = _block_row_rel()
    half = lax.broadcasted_iota(jnp.int32, (GRID_W, LANES), 1) < GRID_W
    nrel = 2 * NA_ROWS

    @functools.cache
    def toeplitz(hh, r, upper_half):
        if upper_half:
            return pltpu.roll(toeplitz(hh, r, False), GRID_W, 1)
        row = g_ref[0, hh * nrel + r:hh * nrel + r + 1, :] * LOG2E
        return pltpu.roll(jnp.broadcast_to(row, (GRID_W, LANES)), 0, 1, stride=1, stride_axis=0)

    for kind in range(rel.shape[0]):
        for hh in range(2):
            for u in range(NA_BLOCK):
                row0 = (hh * NA_BLOCK + u) * GRID_W
                for a in range(0, NA_SPAN, 2):
                    r_lo, r_hi = int(rel[kind, u, a]), int(rel[kind, u, a + 1])
                    lo = toeplitz(hh, r_lo, False) if r_lo >= 0 else MASKED
                    hi = toeplitz(hh, r_hi, True) if r_hi >= 0 else MASKED
                    tile = jnp.where(half, lo, hi) + colmask_ref[...]
                    bias[kind, row0:row0 + GRID_W, a * GRID_W:(a + 2) * GRID_W] = tile


def _na_body(q_ref, k_ref, v_ref, g_ref, colmask_ref, qw_ref, kw_ref, seg_ref, o_ref, qn, kn, sbuf, bias):
    t = q_ref.shape[1]
    rows = t // GRID_W

    @pl.when(pl.program_id(1) == 0)
    def _():
        _build_bias(g_ref, colmask_ref, bias)

    def norm_step(i, carry):
        sl = pl.ds(pl.multiple_of(i * NORM_ROWS, NORM_ROWS), NORM_ROWS)
        for src, w_ref, dst in ((q_ref, qw_ref, qn), (k_ref, kw_ref, kn)):
            xx = src[0, sl, :]
            sq = xx * xx
            hi = sq.astype(BF16)
            lo = (sq - hi.astype(F32)).astype(BF16)
            ssq = _dot(hi, seg_ref[...]) + _dot(lo, seg_ref[...])
            dst[sl, :] = (xx * lax.rsqrt(ssq * (1.0 / NA_HEAD_DIM) + EPS) * w_ref[...]).astype(BF16)
        return carry

    lax.fori_loop(0, t // NORM_ROWS, norm_step, 0)

    nblk = rows // NA_BLOCK
    bq = NA_BLOCK * GRID_W
    first_half = lax.broadcasted_iota(jnp.int32, (bq, LANES), 1) < NA_HEAD_DIM

    def offsets(bi):
        i0 = bi * NA_BLOCK
        rlo = jnp.clip(i0 - NA_ROWS // 2, 0, rows - NA_SPAN)
        return pl.multiple_of(i0 * GRID_W, bq), pl.multiple_of(rlo * GRID_W, 2 * GRID_W)

    def scores(bi, slot):
        kind = jnp.where(bi == 0, 0, jnp.where(bi == nblk - 1, 2, 1))
        qoff, koff = offsets(bi)
        qb = qn[pl.ds(qoff, bq), :]
        zero = jnp.zeros_like(qb)
        qs = jnp.concatenate([jnp.where(first_half, qb, zero), jnp.where(first_half, zero, qb)], axis=0)
        sbuf[slot] = _dot_nt(qs, kn[pl.ds(koff, NA_BKEYS), :]) + bias[kind]

    def attend(bi, slot):
        qoff, koff = offsets(bi)
        s = sbuf[slot]
        p = jnp.exp2(s - jnp.max(s, axis=-1, keepdims=True))
        denom = jnp.sum(p, axis=-1, keepdims=True)
        o2 = _dot(p.astype(BF16), v_ref[0, pl.ds(koff, NA_BKEYS), :]) * (1.0 / denom)
        o_ref[0, pl.ds(qoff, bq), :] = jnp.where(first_half, o2[0:bq], o2[bq:]).astype(o_ref.dtype)

    scores(0, 0)

    def pair_step(i, carry):
        b0 = 2 * i
        scores(b0 + 1, 1)
        attend(b0, 0)
        scores(jnp.minimum(b0 + 2, nblk - 1), 0)
        attend(b0 + 1, 1)
        return carry

    lax.fori_loop(0, nblk // 2, pair_step, 0)


def _natten(q, k, v, g, colmask, qw, kw, seg):
    bsz, t, _ = q.shape
    nq = 2 * NA_BLOCK * GRID_W
    blk = pl.BlockSpec((1, t, LANES), lambda p, b: (b, 0, p))
    return pl.pallas_call(
        _na_body,
        grid=(NA_PAIRS, bsz),
        in_specs=[
            blk, blk, blk,
            pl.BlockSpec((1,) + g.shape[1:], lambda p, b: (p, 0, 0)), _const_spec(colmask.shape),
            _const_spec(qw.shape), _const_spec(kw.shape), _const_spec(seg.shape),
        ],
        out_specs=blk,
        out_shape=jax.ShapeDtypeStruct((bsz, t, D_NA), BF16),
        scratch_shapes=[pltpu.VMEM((t, LANES), BF16), pltpu.VMEM((t, LANES), BF16),
                        pltpu.VMEM((2, nq, NA_BKEYS), F32), pltpu.VMEM((3, nq, NA_BKEYS), F32)],
        compiler_params=pltpu.CompilerParams(
            dimension_semantics=("arbitrary", "arbitrary"), vmem_limit_bytes=VMEM_LIMIT),
    )(q, k, v, g, colmask, qw, kw, seg)


def _block_row_rel():
    rel = -np.ones((3, NA_BLOCK, NA_SPAN), np.int64)
    for u in range(NA_BLOCK):
        for a in range(NA_SPAN):
            if a < NA_ROWS:
                rel[0, u, a] = a - u + NA_ROWS - 1
            if u <= a < u + NA_ROWS:
                rel[1, u, a] = a - u + NA_ROWS - 1 - NA_ROWS // 2
            if a >= NA_SPAN - NA_ROWS:
                rel[2, u, a] = a - u + NA_BLOCK - NA_SPAN + NA_ROWS - 1
    return rel


def _na_bias_rows(rpb):
    centre = NA_COLS - 1
    rows = jnp.concatenate(
        [rpb[:, :, centre:], jnp.zeros(rpb.shape[:2] + (LANES - rpb.shape[2],), rpb.dtype), rpb[:, :, :centre]],
        axis=2).astype(F32)
    rows = jnp.pad(rows, ((0, 0), (0, 2 * NA_ROWS - rpb.shape[1]), (0, 0)))
    return rows.reshape(NA_PAIRS, 2 * 2 * NA_ROWS, LANES)


def _na_col_mask():
    j = np.arange(GRID_W)
    c0 = np.clip(j - NA_COLS // 2, 0, GRID_W - NA_COLS)
    cc = np.arange(GRID_W)
    valid = (cc[None, :] >= c0[:, None]) & (cc[None, :] < c0[:, None] + NA_COLS)
    return jnp.asarray(np.tile(np.where(valid, 0.0, MASKED), (1, 2)), F32)


FF_STEP = 512


def _outmlp_body(x_ref, ys_ref, yn_ref, wo_ref, nw_ref, w1_ref, w2_ref, o_ref):
    dmix = ys_ref.shape[-1]
    x1 = x_ref[...] + _dot(ys_ref[...], wo_ref[0:dmix, :]) + _dot(yn_ref[...], wo_ref[dmix:, :])
    ms = jnp.mean(x1 * x1, axis=-1, keepdims=True)
    h = (x1 * lax.rsqrt(ms + EPS) * nw_ref[...]).astype(BF16)
    o_ref[...] = x1
    for f in range(0, w1_ref.shape[1], FF_STEP):
        u = jnp.maximum(_dot(h, w1_ref[:, f:f + FF_STEP]), 0.0)
        o_ref[...] += _dot((u * u).astype(BF16), w2_ref[f:f + FF_STEP, :])


def _outmlp(x2, ys, yn, w_out, norm_w, w1, w2, tm):
    n, d = x2.shape
    row = lambda i: (i, 0)
    return pl.pallas_call(
        _outmlp_body,
        grid=(n // tm,),
        in_specs=[
            pl.BlockSpec((tm, d), row), pl.BlockSpec((tm, ys.shape[1]), row), pl.BlockSpec((tm, yn.shape[1]), row),
            _const_spec(w_out.shape), _const_spec((1, d)), _const_spec(w1.shape), _const_spec(w2.shape),
        ],
        out_specs=pl.BlockSpec((tm, d), row),
        out_shape=jax.ShapeDtypeStruct((n, d), F32),
        compiler_params=pltpu.CompilerParams(
            dimension_semantics=("arbitrary",), vmem_limit_bytes=VMEM_LIMIT),
    )(x2, ys, yn, w_out, norm_w, w1, w2)


def _head_select(first_lane):
    sel = np.zeros((LANES, D_SSM), np.float32)
    for rep in range(DT_COPIES):
        for h in range(SSM_HEADS):
            sel[rep * DT_LANES + first_lane + h, h * SSM_HEAD_DIM:(h + 1) * SSM_HEAD_DIM] = 1.0
    return jnp.asarray(sel, BF16)


def _dt_lanes(fwd, bwd):
    v = jnp.tile(jnp.concatenate([fwd, bwd]).astype(F32), DT_COPIES)
    return jnp.pad(v, (0, LANES - v.shape[0]))[None, :]


def _layer(x, norm_mix_w, w_in, conv_w, conv_b, dt_bias_fwd, dt_bias_bwd, a_log_fwd, a_log_bwd, d_skip,
           ssm_norm_w, q_norm_w, k_norm_w, rel_pos_bias, w_out, norm_mlp_w, w_mlp_in, w_mlp_out, tm):
    bsz, t, d = x.shape
    n = bsz * t
    x2 = x.reshape(n, d)

    o_dt = D_SSM + CONV_DIM
    o_q = o_dt + DT_LANES
    w_bf = w_in.astype(BF16)
    w_cat = jnp.concatenate(
        [w_bf[:, :o_dt], w_bf[:, o_q:]] + [w_bf[:, o_dt:o_q]] * DT_COPIES
        + [jnp.zeros((d, LANES - DT_COPIES * DT_LANES), BF16)], axis=1)
    z, xbc, q, k, v, dt = _inproj(x2, norm_mix_w[None, :], w_cat, tm)
    r3 = lambda a: a.reshape(bsz, t, a.shape[-1])

    dt_bias = _dt_lanes(dt_bias_fwd, dt_bias_bwd)
    a_log = _dt_lanes(a_log_fwd, a_log_bwd)
    sel_f, sel_b = _head_select(0), _head_select(SSM_HEADS)
    xc, hb = _ssd_bwd(r3(xbc), r3(dt), conv_w, conv_b[None, :], dt_bias, a_log, sel_b)
    y_ssm = _ssd_fwd(xc, r3(dt), r3(z), hb, dt_bias, a_log,
                     jnp.repeat(d_skip.astype(F32), SSM_HEAD_DIM)[None, :], ssm_norm_w[None, :], sel_f, sel_b)

    lane_head = np.arange(LANES) // NA_HEAD_DIM
    seg = jnp.asarray(lane_head[:, None] == lane_head[None, :], BF16)
    qw = (jnp.tile(q_norm_w.astype(F32), 2) * (NA_HEAD_DIM ** -0.5 * LOG2E))[None, :]
    kw = jnp.tile(k_norm_w.astype(F32), 2)[None, :]
    y_na = _natten(r3(q), r3(k), r3(v), _na_bias_rows(rel_pos_bias), _na_col_mask(), qw, kw, seg)

    out = _outmlp(x2, y_ssm.reshape(n, D_SSM), y_na.reshape(n, D_NA), w_out.astype(BF16),
                  norm_mlp_w[None, :], w_mlp_in.astype(BF16), w_mlp_out.astype(BF16), tm)
    return out.reshape(bsz, t, d)


def kernel(x, norm_mix_w, w_in, conv_w, conv_b, dt_bias_fwd, dt_bias_bwd, a_log_fwd, a_log_bwd, d_skip,
           ssm_norm_w, q_norm_w, k_norm_w, rel_pos_bias, w_out, norm_mlp_w, w_mlp_in, w_mlp_out):
    tm = min(512, x.shape[0] * x.shape[1])
    for layer in range(norm_mix_w.shape[0]):
        x = _layer(x, norm_mix_w[layer], w_in[layer], conv_w[layer], conv_b[layer], dt_bias_fwd[layer],
                   dt_bias_bwd[layer], a_log_fwd[layer], a_log_bwd[layer], d_skip[layer], ssm_norm_w[layer],
                   q_norm_w[layer], k_norm_w[layer], rel_pos_bias[layer], w_out[layer], norm_mlp_w[layer],
                   w_mlp_in[layer], w_mlp_out[layer], tm)
    return x
```

```python
import functools

import numpy as np
import jax
import jax.numpy as jnp
from jax import lax
from jax.experimental import pallas as pl
from jax.experimental.pallas import tpu as pltpu

F32 = jnp.float32
BF16 = jnp.bfloat16

LANES = 128
GRID_W = 64
SSM_HEADS = 16
SSM_HEAD_DIM = 64
D_SSM = SSM_HEADS * SSM_HEAD_DIM
SSM_GROUPS = 2
GROUP_W = D_SSM // SSM_GROUPS
D_STATE = 128
D_CONV = 5
CHUNK = 128
FWD_CHUNKS = 4
BWD_CHUNKS = 4
CONV_DIM = D_SSM + 2 * SSM_GROUPS * D_STATE
DT_LANES = 2 * SSM_HEADS
DT_COPIES = 3
NA_HEADS = 16
NA_HEAD_DIM = 64
D_NA = NA_HEADS * NA_HEAD_DIM
NA_ROWS = 8
NA_COLS = 16
NA_PAIRS = NA_HEADS // 2
NA_BLOCK = 4
NA_SPAN = NA_ROWS + NA_BLOCK
NA_BKEYS = NA_SPAN * GRID_W
LOG2E = 1.4426950408889634
EPS = 1e-5
MASKED = -1e30
HALO = 8
CONV_STRIDE = CHUNK // 8 + 1
VMEM_LIMIT = 56 * 1024 * 1024


def _dot(a, b):
    return jnp.dot(a, b, preferred_element_type=F32)


def _dot_nt(a, b):
    return lax.dot_general(a, b, (((1,), (1,)), ((), ())), preferred_element_type=F32)


def _split3(x):
    hi = x.astype(BF16)
    r1 = x - hi.astype(F32)
    mid = r1.astype(BF16)
    lo = (r1 - mid.astype(F32)).astype(BF16)
    return hi, mid, lo


def _dot_sel_rhs(x, sel):
    hi, mid, lo = _split3(x)
    return _dot(hi, sel) + _dot(mid, sel) + _dot(lo, sel)


def _dot_sel_lhs(sel, x):
    hi, mid, lo = _split3(x)
    return _dot(sel, hi) + _dot(sel, mid) + _dot(sel, lo)


def _pack3(x):
    lane = lax.broadcasted_iota(jnp.int32, x.shape, x.ndim - 1)
    hi = x.astype(BF16).astype(F32)
    r1 = x - hi
    mid = r1.astype(BF16).astype(F32)
    return jnp.where(lane < DT_LANES, hi, jnp.where(lane < 2 * DT_LANES, mid, r1 - mid)).astype(BF16)


def _softplus(x):
    return jnp.maximum(x, 0.0) + jnp.log(1.0 + jnp.exp(-jnp.abs(x)))


def _silu(x):
    return x * (1.0 / (1.0 + jnp.exp(-x)))


def _const_spec(shape):
    nd = len(shape)
    return pl.BlockSpec(shape, lambda *_: (0,) * nd, pipeline_mode=pl.Buffered(1))


def _inproj_body(x_ref, nw_ref, w_ref, z_ref, xbc_ref, q_ref, k_ref, v_ref, dt_ref):
    x = x_ref[...]
    ms = jnp.mean(x * x, axis=-1, keepdims=True)
    h = (x * lax.rsqrt(ms + EPS) * nw_ref[...]).astype(BF16)
    off = 0
    for ref in (z_ref, xbc_ref, q_ref, k_ref, v_ref, dt_ref):
        width = ref.shape[-1]
        ref[...] = _dot(h, w_ref[:, off:off + width]).astype(ref.dtype)
        off += width


def _inproj(x2, norm_w, w_cat, tm):
    n, d = x2.shape
    widths = (D_SSM, CONV_DIM, D_NA, D_NA, D_NA, LANES)
    dtypes = (F32, F32, F32, F32, BF16, F32)
    row = lambda i: (i, 0)
    return pl.pallas_call(
        _inproj_body,
        grid=(n // tm,),
        in_specs=[pl.BlockSpec((tm, d), row), _const_spec((1, d)), _const_spec(w_cat.shape)],
        out_specs=[pl.BlockSpec((tm, w), row) for w in widths],
        out_shape=[jax.ShapeDtypeStruct((n, w), dt) for w, dt in zip(widths, dtypes)],
        compiler_params=pltpu.CompilerParams(
            dimension_semantics=("arbitrary",), vmem_limit_bytes=VMEM_LIMIT),
    )(x2, norm_w, w_cat)


def _tri_masks():
    t = lax.broadcasted_iota(jnp.int32, (CHUNK, CHUNK), 0)
    s = lax.broadcasted_iota(jnp.int32, (CHUNK, CHUNK), 1)
    return s <= t, s >= t


def _dt_and_loga(dt_ref, dtb_ref, alog_ref):
    dtv = _softplus(dt_ref[0] + dtb_ref[...])
    return dtv, dtv * (-LOG2E * jnp.exp(alog_ref[...]))


def _ssd_bwd_body(cur_ref, prev_ref, next_ref, dt_ref, cw_ref, cb_ref, dtb_ref, alog_ref, eb_ref,
                  xc_ref, hb_ref, uext, cout, state):
    c = pl.program_id(1)
    nsteps = pl.num_programs(1)
    step = nsteps - 1 - c
    rows_in = BWD_CHUNKS * CHUNK

    @pl.when(c == 0)
    def _():
        state[...] = jnp.zeros_like(state)

    first = HALO - (D_CONV - 1) // 2
    sub = 8
    nslab = CONV_DIM // LANES
    for slab in range(nslab):
        cols = slice(slab * LANES, (slab + 1) * LANES)
        uext[slab, 0:HALO, :] = jnp.where(step > 0, prev_ref[0, :, cols], 0.0)
        uext[slab, HALO:HALO + rows_in, :] = cur_ref[0, :, cols]
        uext[slab, HALO + rows_in:2 * HALO + rows_in, :] = jnp.where(step < nsteps - 1, next_ref[0, :, cols], 0.0)
        uext[slab, 2 * HALO + rows_in:, :] = jnp.zeros((uext.shape[1] - 2 * HALO - rows_in, LANES), F32)
        bias = jnp.broadcast_to(cb_ref[:, cols], (sub, LANES))
        taps = [jnp.broadcast_to(cw_ref[k:k + 1, cols], (sub, LANES)) for k in range(D_CONV)]
        for ch in range(BWD_CHUNKS):
            for i in range(CONV_STRIDE):
                acc = bias
                for k in range(D_CONV):
                    start = ch * CHUNK + first + k + i
                    acc = acc + taps[k] * uext[slab, pl.ds(start, sub, stride=CONV_STRIDE), :]
                cout[ch * nslab + slab, pl.ds(i, sub, stride=CONV_STRIDE), :] = _silu(acc)
            xc_ref[0, ch * CHUNK:(ch + 1) * CHUNK, cols] = cout[ch * nslab + slab, 0:CHUNK, :]

    _, upper = _tri_masks()
    upper_sel = jnp.where(upper, 1.0, 0.0).astype(BF16)
    for ch in reversed(range(BWD_CHUNKS)):
        rows = slice(ch * CHUNK, (ch + 1) * CHUNK)
        dtv, loga = _dt_and_loga(dt_ref.at[:, rows], dtb_ref, alog_ref)
        rcum = _dot_sel_lhs(upper_sel, loga)
        r0 = rcum[0:1, :]
        wgt = dtv * jnp.exp2(r0 - rcum)
        wgt_x = _dot(_pack3(wgt), eb_ref[...])
        dec_x = _dot(_pack3(jnp.broadcast_to(jnp.exp2(r0), (HALO, LANES))), eb_ref[...])[0:1]

        hb_ref[0, ch] = state[...].astype(BF16)
        xw = (xc_ref[0, rows, 0:D_SSM] * wgt_x).astype(BF16)
        for g in range(SSM_GROUPS):
            gs = slice(g * GROUP_W, (g + 1) * GROUP_W)
            b_t = xc_ref[0, rows, D_SSM + g * D_STATE:D_SSM + (g + 1) * D_STATE].T.astype(BF16)
            state[:, gs] = state[:, gs] * dec_x[:, gs] + _dot(b_t, xw[:, gs])


def _ssd_bwd(xbc, dt, conv_w, conv_b, dt_bias, a_log, sel_b):
    bsz, t, _ = xbc.shape
    rows = BWD_CHUNKS * CHUNK
    nsteps = t // rows
    hps = rows // HALO
    nhalo = t // HALO
    nslab = CONV_DIM // LANES
    rev = lambda c: nsteps - 1 - c
    return pl.pallas_call(
        _ssd_bwd_body,
        grid=(bsz, nsteps),
        in_specs=[
            pl.BlockSpec((1, rows, CONV_DIM), lambda b, c: (b, rev(c), 0)),
            pl.BlockSpec((1, HALO, CONV_DIM), lambda b, c: (b, jnp.maximum(rev(c) * hps - 1, 0), 0)),
            pl.BlockSpec((1, HALO, CONV_DIM), lambda b, c: (b, jnp.minimum((rev(c) + 1) * hps, nhalo - 1), 0)),
            pl.BlockSpec((1, rows, LANES), lambda b, c: (b, rev(c), 0)),
            _const_spec(conv_w.shape), _const_spec(conv_b.shape),
            _const_spec(dt_bias.shape), _const_spec(a_log.shape), _const_spec(sel_b.shape),
        ],
        out_specs=[
            pl.BlockSpec((1, rows, CONV_DIM), lambda b, c: (b, rev(c), 0)),
            pl.BlockSpec((1, BWD_CHUNKS, D_STATE, D_SSM), lambda b, c: (b, rev(c), 0, 0)),
        ],
        out_shape=[
            jax.ShapeDtypeStruct((bsz, t, CONV_DIM), F32),
            jax.ShapeDtypeStruct((bsz, t // CHUNK, D_STATE, D_SSM), BF16),
        ],
        scratch_shapes=[pltpu.VMEM((nslab, rows + 3 * HALO, LANES), F32),
                        pltpu.VMEM((BWD_CHUNKS * nslab, 8 * CONV_STRIDE, LANES), F32),
                        pltpu.VMEM((D_STATE, D_SSM), F32)],
        compiler_params=pltpu.CompilerParams(
            dimension_semantics=("arbitrary", "arbitrary"), vmem_limit_bytes=VMEM_LIMIT),
    )(xbc, xbc, xbc, dt, conv_w, conv_b, dt_bias, a_log, sel_b)


def _ssd_fwd_body(xc_ref, dt_ref, z_ref, hb_ref, dtb_ref, alog_ref, dsk_ref, nw_ref, ef_ref, eb_ref,
                  y_ref, state, ybuf):
    @pl.when(pl.program_id(1) == 0)
    def _():
        state[...] = jnp.zeros_like(state)

    for sub in range(FWD_CHUNKS):
        rows = slice(sub * CHUNK, (sub + 1) * CHUNK)
        _ssd_fwd_chunk(xc_ref.at[0, rows], dt_ref.at[:, rows], z_ref.at[0, rows], hb_ref.at[0, sub], dtb_ref, alog_ref,
                       dsk_ref, nw_ref, ef_ref, eb_ref, y_ref.at[0, rows], state, ybuf.at[rows])


def _ssd_fwd_chunk(xc_ref, dt_ref, z_ref, hb_ref, dtb_ref, alog_ref, dsk_ref, nw_ref, ef_ref, eb_ref,
                   y_ref, state, ybuf):
    dtv, loga = _dt_and_loga(dt_ref, dtb_ref, alog_ref)
    lower, upper = _tri_masks()
    cum_f = _dot_sel_lhs(jnp.where(lower, 1.0, 0.0).astype(BF16), loga)
    cum_b = _dot_sel_lhs(jnp.where(upper, 1.0, 0.0).astype(BF16), loga)
    lane = lax.broadcasted_iota(jnp.int32, (CHUNK, LANES), 1)
    cum = jnp.where((lane & SSM_HEADS) == 0, cum_f, cum_b)
    ecum_p = _pack3(jnp.exp2(cum))
    wgt_p = _pack3(dtv * jnp.exp2(cum_f[CHUNK - 1:CHUNK, :] - cum_f))
    src_t = (cum - jnp.log2(dtv)).T
    first_half = lane < SSM_HEAD_DIM

    for g in range(SSM_GROUPS):
        gs = slice(g * GROUP_W, (g + 1) * GROUP_W)
        b_f32 = xc_ref[:, D_SSM + g * D_STATE:D_SSM + (g + 1) * D_STATE]
        b_g = b_f32.astype(BF16)
        c0 = D_SSM + SSM_GROUPS * D_STATE + g * D_STATE
        c_g = xc_ref[:, c0:c0 + D_STATE].astype(BF16)
        cb = _dot_nt(c_g, b_g)
        ea_g = _dot(ecum_p, ef_ref[:, gs])
        er_g = _dot(ecum_p, eb_ref[:, gs])
        y_off = _dot(c_g, state[:, gs].astype(BF16)) * ea_g + _dot(c_g, hb_ref[:, gs]) * er_g
        xw = (xc_ref[:, gs] * _dot(wgt_p, ef_ref[:, gs])).astype(BF16)
        state[:, gs] = state[:, gs] * ea_g[CHUNK - 1:CHUNK, :] + _dot(b_f32.T.astype(BF16), xw)
        pairs = GROUP_W // LANES
        for pp in range(pairs):
            pr = g * pairs + pp
            ps = slice(pr * LANES, (pr + 1) * LANES)
            mats = []
            for h in (2 * pr, 2 * pr + 1):
                hb = SSM_HEADS + h
                seg_f = jnp.broadcast_to(cum[:, h:h + 1], (CHUNK, CHUNK)) - src_t[h:h + 1, :]
                seg_b = jnp.broadcast_to(cum[:, hb:hb + 1], (CHUNK, CHUNK)) - src_t[hb:hb + 1, :]
                d_f = jnp.exp2(jnp.where(lower, seg_f, MASKED))
                d_b = jnp.exp2(jnp.where(upper, seg_b, MASKED))
                mats.append(((d_f + d_b) * cb).astype(BF16))
            xs = xc_ref[:, ps]
            xp = xs.astype(BF16)
            zero = jnp.zeros_like(xp)
            rhs = jnp.concatenate([jnp.where(first_half, xp, zero), jnp.where(first_half, zero, xp)], axis=0)
            y_diag = _dot(jnp.concatenate(mats, axis=1), rhs)
            ybuf[:, ps] = y_diag + y_off[:, pp * LANES:(pp + 1) * LANES] + dsk_ref[:, ps] * xs
        gated = ybuf[:, gs] * _silu(z_ref[:, gs])
        ms = jnp.mean(gated * gated, axis=-1, keepdims=True)
        y_ref[:, gs] = (gated * lax.rsqrt(ms + EPS) * nw_ref[:, gs]).astype(y_ref.dtype)


def _ssd_fwd(xc, dt, z, hb, dt_bias, a_log, d_skip_x, norm_w, sel_f, sel_b):
    bsz, t, _ = xc.shape
    step = FWD_CHUNKS * CHUNK
    blk = lambda w: pl.BlockSpec((1, step, w), lambda b, c: (b, c, 0))
    return pl.pallas_call(
        _ssd_fwd_body,
        grid=(bsz, t // step),
        in_specs=[
            blk(CONV_DIM), blk(LANES), blk(D_SSM),
            pl.BlockSpec((1, FWD_CHUNKS, D_STATE, D_SSM), lambda b, c: (b, c, 0, 0)),
            _const_spec(dt_bias.shape), _const_spec(a_log.shape), _const_spec(d_skip_x.shape),
            _const_spec(norm_w.shape), _const_spec(sel_f.shape), _const_spec(sel_b.shape),
        ],
        out_specs=blk(D_SSM),
        out_shape=jax.ShapeDtypeStruct((bsz, t, D_SSM), BF16),
        scratch_shapes=[pltpu.VMEM((D_STATE, D_SSM), F32), pltpu.VMEM((step, D_SSM), F32)],
        compiler_params=pltpu.CompilerParams(
            dimension_semantics=("arbitrary", "arbitrary"), vmem_limit_bytes=VMEM_LIMIT),
    )(xc, dt, z, hb, dt_bias, a_log, d_skip_x, norm_w, sel_f, sel_b)


NORM_ROWS = 512


def _build_bias(g_ref, colmask_ref, bias):
    rel = _block_row_rel()
    half = lax.broadcasted_iota(jnp.int32, (GRID_W, LANES), 1) < GRID_W
    nrel = 2 * NA_ROWS

    @functools.cache
    def toeplitz(hh, r, upper_half):
        if upper_half:
            return pltpu.roll(toeplitz(hh, r, False), GRID_W, 1)
        row = g_ref[0, hh * nrel + r:hh * nrel + r + 1, :] * LOG2E
        return pltpu.roll(jnp.broadcast_to(row, (GRID_W, LANES)), 0, 1, stride=1, stride_axis=0)

    for kind in range(rel.shape[0]):
        for hh in range(2):
            for u in range(0, NA_BLOCK, 2):
                lane0 = (hh * NA_BLOCK + u) * GRID_W
                for a in range(NA_SPAN):
                    r_lo, r_hi = int(rel[kind, u, a]), int(rel[kind, u + 1, a])
                    lo = toeplitz(hh, r_lo, False) if r_lo >= 0 else MASKED
                    hi = toeplitz(hh, r_hi, True) if r_hi >= 0 else MASKED
                    tile = jnp.where(half, lo, hi) + colmask_ref[...]
                    bias[kind, a * GRID_W:(a + 1) * GRID_W, lane0:lane0 + 2 * GRID_W] = tile


def _na_body(q_ref, k_ref, v_ref, g_ref, colmask_ref, qw_ref, kw_ref, seg_ref, o_ref, qn, kn, vt, sbuf, bias):
    t = q_ref.shape[1]
    rows = t // GRID_W

    @pl.when(pl.program_id(1) == 0)
    def _():
        _build_bias(g_ref, colmask_ref, bias)

    def vt_step(i, carry):
        tile = v_ref[0, pl.ds(pl.multiple_of(i * LANES, LANES), LANES), :]
        vt[i] = tile.astype(F32).T.astype(BF16)
        return carry

    lax.fori_loop(0, t // LANES, vt_step, 0, unroll=4)

    def norm_step(i, carry):
        sl = pl.ds(pl.multiple_of(i * NORM_ROWS, NORM_ROWS), NORM_ROWS)
        for src, w_ref, dst in ((q_ref, qw_ref, qn), (k_ref, kw_ref, kn)):
            xx = src[0, sl, :]
            sq = xx * xx
            hi = sq.astype(BF16)
            lo = (sq - hi.astype(F32)).astype(BF16)
            msq = _dot(hi, seg_ref[...]) + _dot(lo, seg_ref[...])
            dst[sl, :] = (xx * lax.rsqrt(msq + EPS) * w_ref[...]).astype(BF16)
        return carry

    lax.fori_loop(0, t // NORM_ROWS, norm_step, 0, unroll=2)

    nblk = rows // NA_BLOCK
    bq = NA_BLOCK * GRID_W
    first_half = lax.broadcasted_iota(jnp.int32, (bq, LANES), 1) < NA_HEAD_DIM

    def offsets(bi):
        i0 = bi * NA_BLOCK
        rlo = jnp.clip(i0 - NA_ROWS // 2, 0, rows - NA_SPAN)
        return pl.multiple_of(i0 * GRID_W, bq), pl.multiple_of(rlo * GRID_W, 2 * GRID_W)

    def scores(bi, slot):
        kind = jnp.where(bi == 0, 0, jnp.where(bi == nblk - 1, 2, 1))
        qoff, koff = offsets(bi)
        qb = qn[pl.ds(qoff, bq), :]
        zero = jnp.zeros_like(qb)
        qs = jnp.concatenate([jnp.where(first_half, qb, zero), jnp.where(first_half, zero, qb)], axis=0)
        sbuf[slot] = _dot_nt(kn[pl.ds(koff, NA_BKEYS), :], qs) + bias[kind]

    head_a_dims = lax.broadcasted_iota(jnp.int32, (LANES, bq), 0) < NA_HEAD_DIM

    def attend(bi, slot):
        qoff, koff = offsets(bi)
        s = sbuf[slot]
        p = jnp.exp2(s - jnp.max(s, axis=0, keepdims=True))
        denom = jnp.sum(p, axis=0, keepdims=True)
        tile0 = koff // LANES
        v_t = jnp.concatenate([vt[tile0 + i] for i in range(NA_BKEYS // LANES)], axis=1)
        o_t = _dot(v_t, p.astype(BF16)) * (1.0 / denom)
        o_ref[0, pl.ds(qoff, bq), :] = jnp.where(head_a_dims, o_t[:, 0:bq], o_t[:, bq:]).T.astype(o_ref.dtype)

    scores(0, 0)

    def pair_step(i, carry):
        b0 = 2 * i
        scores(b0 + 1, 1)
        attend(b0, 0)
        scores(jnp.minimum(b0 + 2, nblk - 1), 0)
        attend(b0 + 1, 1)
        return carry

    lax.fori_loop(0, nblk // 2, pair_step, 0, unroll=4)


def _natten(q, k, v, g, colmask, qw, kw, seg):
    bsz, t, _ = q.shape
    nq = 2 * NA_BLOCK * GRID_W
    blk = pl.BlockSpec((1, t, LANES), lambda p, b: (b, 0, p))
    return pl.pallas_call(
        _na_body,
        grid=(NA_PAIRS, bsz),
        in_specs=[
            blk, blk, blk,
            pl.BlockSpec((1,) + g.shape[1:], lambda p, b: (p, 0, 0)), _const_spec(colmask.shape),
            _const_spec(qw.shape), _const_spec(kw.shape), _const_spec(seg.shape),
        ],
        out_specs=blk,
        out_shape=jax.ShapeDtypeStruct((bsz, t, D_NA), BF16),
        scratch_shapes=[pltpu.VMEM((t, LANES), BF16), pltpu.VMEM((t, LANES), BF16),
                        pltpu.VMEM((t // LANES, LANES, LANES), BF16),
                        pltpu.VMEM((2, NA_BKEYS, nq), F32), pltpu.VMEM((3, NA_BKEYS, nq), F32)],
        compiler_params=pltpu.CompilerParams(
            dimension_semantics=("arbitrary", "arbitrary"), vmem_limit_bytes=VMEM_LIMIT),
    )(q, k, v, g, colmask, qw, kw, seg)


def _block_row_rel():
    rel = -np.ones((3, NA_BLOCK, NA_SPAN), np.int64)
    for u in range(NA_BLOCK):
        for a in range(NA_SPAN):
            if a < NA_ROWS:
                rel[0, u, a] = a - u + NA_ROWS - 1
            if u <= a < u + NA_ROWS:
                rel[1, u, a] = a - u + NA_ROWS - 1 - NA_ROWS // 2
            if a >= NA_SPAN - NA_ROWS:
                rel[2, u, a] = a - u + NA_BLOCK - NA_SPAN + NA_ROWS - 1
    return rel


def _na_bias_rows(rpb):
    centre = NA_COLS - 1
    rev = rpb[:, :, ::-1]
    rows = jnp.concatenate(
        [rev[:, :, centre:], jnp.zeros(rpb.shape[:2] + (LANES - rpb.shape[2],), rpb.dtype), rev[:, :, :centre]],
        axis=2).astype(F32)
    rows = jnp.pad(rows, ((0, 0), (0, 2 * NA_ROWS - rpb.shape[1]), (0, 0)))
    return rows.reshape(NA_PAIRS, 2 * 2 * NA_ROWS, LANES)


def _na_col_mask():
    j = np.arange(GRID_W)
    c0 = np.clip(j - NA_COLS // 2, 0, GRID_W - NA_COLS)
    cc = np.arange(GRID_W)
    valid = (cc[:, None] >= c0[None, :]) & (cc[:, None] < c0[None, :] + NA_COLS)
    return jnp.asarray(np.tile(np.where(valid, 0.0, MASKED), (1, 2)), F32)


FF_STEP = 512


def _outmlp_body(x_ref, ys_ref, yn_ref, wo_ref, nw_ref, w1_ref, w2_ref, o_ref):
    dmix = ys_ref.shape[-1]
    x1 = x_ref[...] + _dot(ys_ref[...], wo_ref[0:dmix, :]) + _dot(yn_ref[...], wo_ref[dmix:, :])
    ms = jnp.mean(x1 * x1, axis=-1, keepdims=True)
    h = (x1 * lax.rsqrt(ms + EPS) * nw_ref[...]).astype(BF16)
    o_ref[...] = x1
    for f in range(0, w1_ref.shape[1], FF_STEP):
        u = jnp.maximum(_dot(h, w1_ref[:, f:f + FF_STEP]), 0.0)
        o_ref[...] += _dot((u * u).astype(BF16), w2_ref[f:f + FF_STEP, :])


def _outmlp(x2, ys, yn, w_out, norm_w, w1, w2, tm):
    n, d = x2.shape
    row = lambda i: (i, 0)
    return pl.pallas_call(
        _outmlp_body,
        grid=(n // tm,),
        in_specs=[
            pl.BlockSpec((tm, d), row), pl.BlockSpec((tm, ys.shape[1]), row), pl.BlockSpec((tm, yn.shape[1]), row),
            _const_spec(w_out.shape), _const_spec((1, d)), _const_spec(w1.shape), _const_spec(w2.shape),
        ],
        out_specs=pl.BlockSpec((tm, d), row),
        out_shape=jax.ShapeDtypeStruct((n, d), F32),
        compiler_params=pltpu.CompilerParams(
            dimension_semantics=("arbitrary",), vmem_limit_bytes=VMEM_LIMIT),
    )(x2, ys, yn, w_out, norm_w, w1, w2)


def _head_select(first_lane):
    sel = np.zeros((LANES, D_SSM), np.float32)
    for rep in range(DT_COPIES):
        for h in range(SSM_HEADS):
            sel[rep * DT_LANES + first_lane + h, h * SSM_HEAD_DIM:(h + 1) * SSM_HEAD_DIM] = 1.0
    return jnp.asarray(sel, BF16)


def _dt_lanes(fwd, bwd):
    v = jnp.tile(jnp.concatenate([fwd, bwd]).astype(F32), DT_COPIES)
    return jnp.pad(v, (0, LANES - v.shape[0]))[None, :]


def _layer(x, norm_mix_w, w_in, conv_w, conv_b, dt_bias_fwd, dt_bias_bwd, a_log_fwd, a_log_bwd, d_skip,
           ssm_norm_w, q_norm_w, k_norm_w, rel_pos_bias, w_out, norm_mlp_w, w_mlp_in, w_mlp_out, tm):
    bsz, t, d = x.shape
    n = bsz * t
    x2 = x.reshape(n, d)

    o_dt = D_SSM + CONV_DIM
    o_q = o_dt + DT_LANES
    w_bf = w_in.astype(BF16)
    w_cat = jnp.concatenate(
        [w_bf[:, :o_dt], w_bf[:, o_q:]] + [w_bf[:, o_dt:o_q]] * DT_COPIES
        + [jnp.zeros((d, LANES - DT_COPIES * DT_LANES), BF16)], axis=1)
    z, xbc, q, k, v, dt = _inproj(x2, norm_mix_w[None, :], w_cat, tm)
    r3 = lambda a: a.reshape(bsz, t, a.shape[-1])

    dt_bias = _dt_lanes(dt_bias_fwd, dt_bias_bwd)
    a_log = _dt_lanes(a_log_fwd, a_log_bwd)
    sel_f, sel_b = _head_select(0), _head_select(SSM_HEADS)
    xc, hb = _ssd_bwd(r3(xbc), r3(dt), conv_w, conv_b[None, :], dt_bias, a_log, sel_b)
    y_ssm = _ssd_fwd(xc, r3(dt), r3(z), hb, dt_bias, a_log,
                     jnp.repeat(d_skip.astype(F32), SSM_HEAD_DIM)[None, :], ssm_norm_w[None, :], sel_f, sel_b)

    lane_head = np.arange(LANES) // NA_HEAD_DIM
    seg = jnp.asarray((lane_head[:, None] == lane_head[None, :]) / NA_HEAD_DIM, BF16)
    qw = (jnp.tile(q_norm_w.astype(F32), 2) * (NA_HEAD_DIM ** -0.5 * LOG2E))[None, :]
    kw = jnp.tile(k_norm_w.astype(F32), 2)[None, :]
    y_na = _natten(r3(q), r3(k), r3(v), _na_bias_rows(rel_pos_bias), _na_col_mask(), qw, kw, seg)

    out = _outmlp(x2, y_ssm.reshape(n, D_SSM), y_na.reshape(n, D_NA), w_out.astype(BF16),
                  norm_mlp_w[None, :], w_mlp_in.astype(BF16), w_mlp_out.astype(BF16), tm)
    return out.reshape(bsz, t, d)


def kernel(x, norm_mix_w, w_in, conv_w, conv_b, dt_bias_fwd, dt_bias_bwd, a_log_fwd, a_log_bwd, d_skip,
           ssm_norm_w, q_norm_w, k_norm_w, rel_pos_bias, w_out, norm_mlp_w, w_mlp_in, w_mlp_out):
    tm = min(512, x.shape[0] * x.shape[1])
    for layer in range(norm_mix_w.shape[0]):
        x = _layer(x, norm_mix_w[layer], w_in[layer], conv_w[layer], conv_b[layer], dt_bias_fwd[layer],
                   dt_bias_bwd[layer], a_log_fwd[layer], a_log_bwd[layer], d_skip[layer], ssm_norm_w[layer],
                   q_norm_w[layer], k_norm_w[layer], rel_pos_bias[layer], w_out[layer], norm_mlp_w[layer],
                   w_mlp_in[layer], w_mlp_out[layer], tm)
    return x
```

```python
import functools

import numpy as np
import jax
import jax.numpy as jnp
from jax import lax
from jax.experimental import pallas as pl
from jax.experimental.pallas import tpu as pltpu

F32 = jnp.float32
BF16 = jnp.bfloat16

LANES = 128
GRID_W = 64
SSM_HEADS = 16
SSM_HEAD_DIM = 64
D_SSM = SSM_HEADS * SSM_HEAD_DIM
SSM_GROUPS = 2
GROUP_W = D_SSM // SSM_GROUPS
D_STATE = 128
D_CONV = 5
CHUNK = 128
FWD_CHUNKS = 4
BWD_CHUNKS = 4
CONV_DIM = D_SSM + 2 * SSM_GROUPS * D_STATE
DT_LANES = 2 * SSM_HEADS
DT_COPIES = 3
NA_HEADS = 16
NA_HEAD_DIM = 64
D_NA = NA_HEADS * NA_HEAD_DIM
NA_ROWS = 8
NA_COLS = 16
NA_PAIRS = NA_HEADS // 2
NA_BLOCK = 4
NA_SPAN = NA_ROWS + NA_BLOCK
NA_BKEYS = NA_SPAN * GRID_W
LOG2E = 1.4426950408889634
EPS = 1e-5
MASKED = -1e30
HALO = 8
CONV_STRIDE = CHUNK // 8 + 1
VMEM_LIMIT = 56 * 1024 * 1024


def _dot(a, b):
    return jnp.dot(a, b, preferred_element_type=F32)


def _dot_nt(a, b):
    return lax.dot_general(a, b, (((1,), (1,)), ((), ())), preferred_element_type=F32)


def _split3(x):
    hi = x.astype(BF16)
    r1 = x - hi.astype(F32)
    mid = r1.astype(BF16)
    lo = (r1 - mid.astype(F32)).astype(BF16)
    return hi, mid, lo


def _dot_sel_rhs(x, sel):
    hi, mid, lo = _split3(x)
    return _dot(hi, sel) + _dot(mid, sel) + _dot(lo, sel)


def _dot_sel_lhs(sel, x):
    hi, mid, lo = _split3(x)
    return _dot(sel, hi) + _dot(sel, mid) + _dot(sel, lo)


def _pack3(x):
    lane = lax.broadcasted_iota(jnp.int32, x.shape, x.ndim - 1)
    hi = x.astype(BF16).astype(F32)
    r1 = x - hi
    mid = r1.astype(BF16).astype(F32)
    return jnp.where(lane < DT_LANES, hi, jnp.where(lane < 2 * DT_LANES, mid, r1 - mid)).astype(BF16)


def _softplus(x):
    return jnp.maximum(x, 0.0) + jnp.log(1.0 + jnp.exp(-jnp.abs(x)))


def _silu(x):
    return x * (1.0 / (1.0 + jnp.exp(-x)))


def _const_spec(shape):
    nd = len(shape)
    return pl.BlockSpec(shape, lambda *_: (0,) * nd, pipeline_mode=pl.Buffered(1))


O_DT = D_SSM + CONV_DIM
O_V = O_DT + DT_LANES + 2 * D_NA
DQK_SPAN = -(-(DT_LANES + 2 * D_NA) // LANES) * LANES


def _inproj_body(x_ref, nw_ref, w_ref, wvt_ref, z_ref, xbc_ref, q_ref, k_ref, vt_ref, dt_ref):
    x = x_ref[...]
    ms = jnp.mean(x * x, axis=-1, keepdims=True)
    hf = x * lax.rsqrt(ms + EPS) * nw_ref[...]
    h = hf.astype(BF16)
    z_ref[...] = _dot(h, w_ref[:, 0:D_SSM])
    xbc_ref[...] = _dot(h, w_ref[:, D_SSM:O_DT])
    r = _dot(h, w_ref[:, O_DT:O_DT + DQK_SPAN])
    lane = lax.broadcasted_iota(jnp.int32, (x.shape[0], LANES), 1)
    d0 = jnp.where(lane < DT_LANES, r[:, 0:LANES], 0.0)
    dt = d0
    for rep in range(1, DT_COPIES):
        dt = dt + pltpu.roll(d0, rep * DT_LANES, 1)
    dt_ref[...] = dt
    q_ref[...] = r[:, DT_LANES:DT_LANES + D_NA]
    k_ref[...] = r[:, DT_LANES + D_NA:DT_LANES + 2 * D_NA]
    vt = _dot(wvt_ref[...], hf.T.astype(BF16))
    for j in range(vt_ref.shape[0]):
        vt_ref[j] = vt[:, j * LANES:(j + 1) * LANES].astype(vt_ref.dtype)


def _inproj(x2, norm_w, w_bf, wvt, tm):
    n, d = x2.shape
    widths = (D_SSM, CONV_DIM, D_NA, D_NA)
    row = lambda i: (i, 0)
    tiles = tm // LANES
    return pl.pallas_call(
        _inproj_body,
        grid=(n // tm,),
        in_specs=[pl.BlockSpec((tm, d), row), _const_spec((1, d)), _const_spec(w_bf.shape), _const_spec(wvt.shape)],
        out_specs=[pl.BlockSpec((tm, w), row) for w in widths]
        + [pl.BlockSpec((tiles, D_NA, LANES), lambda i: (i, 0, 0)), pl.BlockSpec((tm, LANES), row)],
        out_shape=[jax.ShapeDtypeStruct((n, w), F32) for w in widths]
        + [jax.ShapeDtypeStruct((n // LANES, D_NA, LANES), BF16), jax.ShapeDtypeStruct((n, LANES), F32)],
        compiler_params=pltpu.CompilerParams(
            dimension_semantics=("arbitrary",), vmem_limit_bytes=VMEM_LIMIT),
    )(x2, norm_w, w_bf, wvt)


def _tri_masks():
    t = lax.broadcasted_iota(jnp.int32, (CHUNK, CHUNK), 0)
    s = lax.broadcasted_iota(jnp.int32, (CHUNK, CHUNK), 1)
    return s <= t, s >= t


def _dt_and_loga(dt_ref, dtb_ref, alog_ref):
    dtv = _softplus(dt_ref[0] + dtb_ref[...])
    return dtv, dtv * (-LOG2E * jnp.exp(alog_ref[...]))


def _ssd_bwd_body(cur_ref, prev_ref, next_ref, dt_ref, cw_ref, cb_ref, dtb_ref, alog_ref, eb_ref,
                  xc_ref, hb_ref, uext, cout, state):
    c = pl.program_id(1)
    nsteps = pl.num_programs(1)
    step = nsteps - 1 - c
    rows_in = BWD_CHUNKS * CHUNK

    @pl.when(c == 0)
    def _():
        state[...] = jnp.zeros_like(state)

    first = HALO - (D_CONV - 1) // 2
    sub = 8
    nslab = CONV_DIM // LANES
    for slab in range(nslab):
        cols = slice(slab * LANES, (slab + 1) * LANES)
        uext[slab, 0:HALO, :] = jnp.where(step > 0, prev_ref[0, :, cols], 0.0)
        uext[slab, HALO:HALO + rows_in, :] = cur_ref[0, :, cols]
        uext[slab, HALO + rows_in:2 * HALO + rows_in, :] = jnp.where(step < nsteps - 1, next_ref[0, :, cols], 0.0)
        uext[slab, 2 * HALO + rows_in:, :] = jnp.zeros((uext.shape[1] - 2 * HALO - rows_in, LANES), F32)
        bias = jnp.broadcast_to(cb_ref[:, cols], (sub, LANES))
        taps = [jnp.broadcast_to(cw_ref[k:k + 1, cols], (sub, LANES)) for k in range(D_CONV)]
        for ch in range(BWD_CHUNKS):
            for i in range(CONV_STRIDE):
                acc = bias
                for k in range(D_CONV):
                    start = ch * CHUNK + first + k + i
                    acc = acc + taps[k] * uext[slab, pl.ds(start, sub, stride=CONV_STRIDE), :]
                cout[ch * nslab + slab, pl.ds(i, sub, stride=CONV_STRIDE), :] = _silu(acc)
            xc_ref[0, ch * CHUNK:(ch + 1) * CHUNK, cols] = cout[ch * nslab + slab, 0:CHUNK, :]

    _, upper = _tri_masks()
    upper_sel = jnp.where(upper, 1.0, 0.0).astype(BF16)
    for ch in reversed(range(BWD_CHUNKS)):
        rows = slice(ch * CHUNK, (ch + 1) * CHUNK)
        dtv, loga = _dt_and_loga(dt_ref.at[:, rows], dtb_ref, alog_ref)
        rcum = _dot_sel_lhs(upper_sel, loga)
        r0 = rcum[0:1, :]
        wgt = dtv * jnp.exp2(r0 - rcum)
        wgt_x = _dot(_pack3(wgt), eb_ref[...])
        dec_x = _dot(_pack3(jnp.broadcast_to(jnp.exp2(r0), (HALO, LANES))), eb_ref[...])[0:1]

        hb_ref[0, ch] = state[...].astype(BF16)
        xw = (xc_ref[0, rows, 0:D_SSM] * wgt_x).astype(BF16)
        for g in range(SSM_GROUPS):
            gs = slice(g * GROUP_W, (g + 1) * GROUP_W)
            b_t = xc_ref[0, rows, D_SSM + g * D_STATE:D_SSM + (g + 1) * D_STATE].T.astype(BF16)
            state[:, gs] = state[:, gs] * dec_x[:, gs] + _dot(b_t, xw[:, gs])


def _ssd_bwd(xbc, dt, conv_w, conv_b, dt_bias, a_log, sel_b):
    bsz, t, _ = xbc.shape
    rows = BWD_CHUNKS * CHUNK
    nsteps = t // rows
    hps = rows // HALO
    nhalo = t // HALO
    nslab = CONV_DIM // LANES
    rev = lambda c: nsteps - 1 - c
    return pl.pallas_call(
        _ssd_bwd_body,
        grid=(bsz, nsteps),
        in_specs=[
            pl.BlockSpec((1, rows, CONV_DIM), lambda b, c: (b, rev(c), 0)),
            pl.BlockSpec((1, HALO, CONV_DIM), lambda b, c: (b, jnp.maximum(rev(c) * hps - 1, 0), 0)),
            pl.BlockSpec((1, HALO, CONV_DIM), lambda b, c: (b, jnp.minimum((rev(c) + 1) * hps, nhalo - 1), 0)),
            pl.BlockSpec((1, rows, LANES), lambda b, c: (b, rev(c), 0)),
            _const_spec(conv_w.shape), _const_spec(conv_b.shape),
            _const_spec(dt_bias.shape), _const_spec(a_log.shape), _const_spec(sel_b.shape),
        ],
        out_specs=[
            pl.BlockSpec((1, rows, CONV_DIM), lambda b, c: (b, rev(c), 0)),
            pl.BlockSpec((1, BWD_CHUNKS, D_STATE, D_SSM), lambda b, c: (b, rev(c), 0, 0)),
        ],
        out_shape=[
            jax.ShapeDtypeStruct((bsz, t, CONV_DIM), F32),
            jax.ShapeDtypeStruct((bsz, t // CHUNK, D_STATE, D_SSM), BF16),
        ],
        scratch_shapes=[pltpu.VMEM((nslab, rows + 3 * HALO, LANES), F32),
                        pltpu.VMEM((BWD_CHUNKS * nslab, 8 * CONV_STRIDE, LANES), F32),
                        pltpu.VMEM((D_STATE, D_SSM), F32)],
        compiler_params=pltpu.CompilerParams(
            dimension_semantics=("arbitrary", "arbitrary"), vmem_limit_bytes=VMEM_LIMIT),
    )(xbc, xbc, xbc, dt, conv_w, conv_b, dt_bias, a_log, sel_b)


def _ssd_fwd_body(xc_ref, dt_ref, z_ref, hb_ref, dtb_ref, alog_ref, dsk_ref, nw_ref, ef_ref, eb_ref,
                  y_ref, state, ybuf):
    @pl.when(pl.program_id(1) == 0)
    def _():
        state[...] = jnp.zeros_like(state)

    for sub in range(FWD_CHUNKS):
        rows = slice(sub * CHUNK, (sub + 1) * CHUNK)
        _ssd_fwd_chunk(xc_ref.at[0, rows], dt_ref.at[:, rows], z_ref.at[0, rows], hb_ref.at[0, sub], dtb_ref, alog_ref,
                       dsk_ref, nw_ref, ef_ref, eb_ref, y_ref.at[0, rows], state, ybuf.at[rows])


def _ssd_fwd_chunk(xc_ref, dt_ref, z_ref, hb_ref, dtb_ref, alog_ref, dsk_ref, nw_ref, ef_ref, eb_ref,
                   y_ref, state, ybuf):
    dtv, loga = _dt_and_loga(dt_ref, dtb_ref, alog_ref)
    lower, upper = _tri_masks()
    cum_f = _dot_sel_lhs(jnp.where(lower, 1.0, 0.0).astype(BF16), loga)
    cum_b = _dot_sel_lhs(jnp.where(upper, 1.0, 0.0).astype(BF16), loga)
    lane = lax.broadcasted_iota(jnp.int32, (CHUNK, LANES), 1)
    cum = jnp.where((lane & SSM_HEADS) == 0, cum_f, cum_b)
    ecum_p = _pack3(jnp.exp2(cum))
    wgt_p = _pack3(dtv * jnp.exp2(cum_f[CHUNK - 1:CHUNK, :] - cum_f))
    src_t = (cum - jnp.log2(dtv)).T
    first_half = lane < SSM_HEAD_DIM

    for g in range(SSM_GROUPS):
        gs = slice(g * GROUP_W, (g + 1) * GROUP_W)
        b_f32 = xc_ref[:, D_SSM + g * D_STATE:D_SSM + (g + 1) * D_STATE]
        b_g = b_f32.astype(BF16)
        c0 = D_SSM + SSM_GROUPS * D_STATE + g * D_STATE
        c_g = xc_ref[:, c0:c0 + D_STATE].astype(BF16)
        cb = _dot_nt(c_g, b_g)
        ea_g = _dot(ecum_p, ef_ref[:, gs])
        er_g = _dot(ecum_p, eb_ref[:, gs])
        y_off = _dot(c_g, state[:, gs].astype(BF16)) * ea_g + _dot(c_g, hb_ref[:, gs]) * er_g
        xw = (xc_ref[:, gs] * _dot(wgt_p, ef_ref[:, gs])).astype(BF16)
        state[:, gs] = state[:, gs] * ea_g[CHUNK - 1:CHUNK, :] + _dot(b_f32.T.astype(BF16), xw)
        pairs = GROUP_W // LANES
        for pp in range(pairs):
            pr = g * pairs + pp
            ps = slice(pr * LANES, (pr + 1) * LANES)
            mats = []
            for h in (2 * pr, 2 * pr + 1):
                hb = SSM_HEADS + h
                seg_f = jnp.broadcast_to(cum[:, h:h + 1], (CHUNK, CHUNK)) - src_t[h:h + 1, :]
                seg_b = jnp.broadcast_to(cum[:, hb:hb + 1], (CHUNK, CHUNK)) - src_t[hb:hb + 1, :]
                d_f = jnp.exp2(jnp.where(lower, seg_f, MASKED))
                d_b = jnp.exp2(jnp.where(upper, seg_b, MASKED))
                mats.append(((d_f + d_b) * cb).astype(BF16))
            xs = xc_ref[:, ps]
            xp = xs.astype(BF16)
            zero = jnp.zeros_like(xp)
            rhs = jnp.concatenate([jnp.where(first_half, xp, zero), jnp.where(first_half, zero, xp)], axis=0)
            y_diag = _dot(jnp.concatenate(mats, axis=1), rhs)
            ybuf[:, ps] = y_diag + y_off[:, pp * LANES:(pp + 1) * LANES] + dsk_ref[:, ps] * xs
        gated = ybuf[:, gs] * _silu(z_ref[:, gs])
        ms = jnp.mean(gated * gated, axis=-1, keepdims=True)
        y_ref[:, gs] = (gated * lax.rsqrt(ms + EPS) * nw_ref[:, gs]).astype(y_ref.dtype)


def _ssd_fwd(xc, dt, z, hb, dt_bias, a_log, d_skip_x, norm_w, sel_f, sel_b):
    bsz, t, _ = xc.shape
    step = FWD_CHUNKS * CHUNK
    blk = lambda w: pl.BlockSpec((1, step, w), lambda b, c: (b, c, 0))
    return pl.pallas_call(
        _ssd_fwd_body,
        grid=(bsz, t // step),
        in_specs=[
            blk(CONV_DIM), blk(LANES), blk(D_SSM),
            pl.BlockSpec((1, FWD_CHUNKS, D_STATE, D_SSM), lambda b, c: (b, c, 0, 0)),
            _const_spec(dt_bias.shape), _const_spec(a_log.shape), _const_spec(d_skip_x.shape),
            _const_spec(norm_w.shape), _const_spec(sel_f.shape), _const_spec(sel_b.shape),
        ],
        out_specs=blk(D_SSM),
        out_shape=jax.ShapeDtypeStruct((bsz, t, D_SSM), BF16),
        scratch_shapes=[pltpu.VMEM((D_STATE, D_SSM), F32), pltpu.VMEM((step, D_SSM), F32)],
        compiler_params=pltpu.CompilerParams(
            dimension_semantics=("arbitrary", "arbitrary"), vmem_limit_bytes=VMEM_LIMIT),
    )(xc, dt, z, hb, dt_bias, a_log, d_skip_x, norm_w, sel_f, sel_b)


NORM_ROWS = 512


def _build_bias(g_ref, colmask_ref, bias):
    rel = _block_row_rel()
    half = lax.broadcasted_iota(jnp.int32, (GRID_W, LANES), 1) < GRID_W
    nrel = 2 * NA_ROWS

    @functools.cache
    def toeplitz(hh, r, upper_half):
        if upper_half:
            return pltpu.roll(toeplitz(hh, r, False), GRID_W, 1)
        row = g_ref[0, hh * nrel + r:hh * nrel + r + 1, :] * LOG2E
        return pltpu.roll(jnp.broadcast_to(row, (GRID_W, LANES)), 0, 1, stride=1, stride_axis=0)

    for kind in range(rel.shape[0]):
        for hh in range(2):
            for u in range(0, NA_BLOCK, 2):
                lane0 = (hh * NA_BLOCK + u) * GRID_W
                for a in range(NA_SPAN):
                    r_lo, r_hi = int(rel[kind, u, a]), int(rel[kind, u + 1, a])
                    lo = toeplitz(hh, r_lo, False) if r_lo >= 0 else MASKED
                    hi = toeplitz(hh, r_hi, True) if r_hi >= 0 else MASKED
                    tile = jnp.where(half, lo, hi) + colmask_ref[...]
                    bias[kind, a * GRID_W:(a + 1) * GRID_W, lane0:lane0 + 2 * GRID_W] = tile


def _na_body(q_ref, k_ref, vt_ref, g_ref, colmask_ref, qw_ref, kw_ref, seg_ref, o_ref, qn, kn, sbuf, bias):
    t = q_ref.shape[1]
    rows = t // GRID_W

    @pl.when(pl.program_id(1) == 0)
    def _():
        _build_bias(g_ref, colmask_ref, bias)

    def norm_step(i, carry):
        sl = pl.ds(pl.multiple_of(i * NORM_ROWS, NORM_ROWS), NORM_ROWS)
        for src, w_ref, dst in ((q_ref, qw_ref, qn), (k_ref, kw_ref, kn)):
            xx = src[0, sl, :]
            sq = xx * xx
            hi = sq.astype(BF16)
            lo = (sq - hi.astype(F32)).astype(BF16)
            msq = _dot(hi, seg_ref[...]) + _dot(lo, seg_ref[...])
            dst[sl, :] = (xx * lax.rsqrt(msq + EPS) * w_ref[...]).astype(BF16)
        return carry

    lax.fori_loop(0, t // NORM_ROWS, norm_step, 0, unroll=2)

    nblk = rows // NA_BLOCK
    bq = NA_BLOCK * GRID_W
    first_half = lax.broadcasted_iota(jnp.int32, (bq, LANES), 1) < NA_HEAD_DIM

    def offsets(bi):
        i0 = bi * NA_BLOCK
        rlo = jnp.clip(i0 - NA_ROWS // 2, 0, rows - NA_SPAN)
        return pl.multiple_of(i0 * GRID_W, bq), pl.multiple_of(rlo * GRID_W, 2 * GRID_W)

    def scores(bi, slot):
        kind = jnp.where(bi == 0, 0, jnp.where(bi == nblk - 1, 2, 1))
        qoff, koff = offsets(bi)
        qb = qn[pl.ds(qoff, bq), :]
        zero = jnp.zeros_like(qb)
        qs = jnp.concatenate([jnp.where(first_half, qb, zero), jnp.where(first_half, zero, qb)], axis=0)
        sbuf[slot] = _dot_nt(kn[pl.ds(koff, NA_BKEYS), :], qs) + bias[kind]

    head_a_dims = lax.broadcasted_iota(jnp.int32, (LANES, bq), 0) < NA_HEAD_DIM

    def attend(bi, slot):
        qoff, koff = offsets(bi)
        s = sbuf[slot]
        p = jnp.exp2(s - jnp.max(s, axis=0, keepdims=True))
        denom = jnp.sum(p, axis=0, keepdims=True)
        tile0 = koff // LANES
        v_t = jnp.concatenate([vt_ref[tile0 + i] for i in range(NA_BKEYS // LANES)], axis=1)
        o_t = _dot(v_t, p.astype(BF16)) * (1.0 / denom)
        o_ref[0, pl.ds(qoff, bq), :] = jnp.where(head_a_dims, o_t[:, 0:bq], o_t[:, bq:]).T.astype(o_ref.dtype)

    scores(0, 0)

    def pair_step(i, carry):
        b0 = 2 * i
        scores(b0 + 1, 1)
        attend(b0, 0)
        scores(jnp.minimum(b0 + 2, nblk - 1), 0)
        attend(b0 + 1, 1)
        return carry

    lax.fori_loop(0, nblk // 2, pair_step, 0, unroll=4)


def _natten(q, k, vt, g, colmask, qw, kw, seg):
    bsz, t, _ = q.shape
    nq = 2 * NA_BLOCK * GRID_W
    blk = pl.BlockSpec((1, t, LANES), lambda p, b: (b, 0, p))
    return pl.pallas_call(
        _na_body,
        grid=(NA_PAIRS, bsz),
        in_specs=[
            blk, blk, pl.BlockSpec((t // LANES, LANES, LANES), lambda p, b: (b, p, 0)),
            pl.BlockSpec((1,) + g.shape[1:], lambda p, b: (p, 0, 0)), _const_spec(colmask.shape),
            _const_spec(qw.shape), _const_spec(kw.shape), _const_spec(seg.shape),
        ],
        out_specs=blk,
        out_shape=jax.ShapeDtypeStruct((bsz, t, D_NA), BF16),
        scratch_shapes=[pltpu.VMEM((t, LANES), BF16), pltpu.VMEM((t, LANES), BF16),
                        pltpu.VMEM((2, NA_BKEYS, nq), F32), pltpu.VMEM((3, NA_BKEYS, nq), F32)],
        compiler_params=pltpu.CompilerParams(
            dimension_semantics=("arbitrary", "arbitrary"), vmem_limit_bytes=VMEM_LIMIT),
    )(q, k, vt, g, colmask, qw, kw, seg)


def _block_row_rel():
    rel = -np.ones((3, NA_BLOCK, NA_SPAN), np.int64)
    for u in range(NA_BLOCK):
        for a in range(NA_SPAN):
            if a < NA_ROWS:
                rel[0, u, a] = a - u + NA_ROWS - 1
            if u <= a < u + NA_ROWS:
                rel[1, u, a] = a - u + NA_ROWS - 1 - NA_ROWS // 2
            if a >= NA_SPAN - NA_ROWS:
                rel[2, u, a] = a - u + NA_BLOCK - NA_SPAN + NA_ROWS - 1
    return rel


def _na_bias_rows(rpb):
    centre = NA_COLS - 1
    rev = rpb[:, :, ::-1]
    rows = jnp.concatenate(
        [rev[:, :, centre:], jnp.zeros(rpb.shape[:2] + (LANES - rpb.shape[2],), rpb.dtype), rev[:, :, :centre]],
        axis=2).astype(F32)
    rows = jnp.pad(rows, ((0, 0), (0, 2 * NA_ROWS - rpb.shape[1]), (0, 0)))
    return rows.reshape(NA_PAIRS, 2 * 2 * NA_ROWS, LANES)


def _na_col_mask():
    j = np.arange(GRID_W)
    c0 = np.clip(j - NA_COLS // 2, 0, GRID_W - NA_COLS)
    cc = np.arange(GRID_W)
    valid = (cc[:, None] >= c0[None, :]) & (cc[:, None] < c0[None, :] + NA_COLS)
    return jnp.asarray(np.tile(np.where(valid, 0.0, MASKED), (1, 2)), F32)


FF_STEP = 512


def _outmlp_body(x_ref, ys_ref, yn_ref, wo_ref, nw_ref, w1_ref, w2_ref, o_ref):
    dmix = ys_ref.shape[-1]
    x1 = x_ref[...] + _dot(ys_ref[...], wo_ref[0:dmix, :]) + _dot(yn_ref[...], wo_ref[dmix:, :])
    ms = jnp.mean(x1 * x1, axis=-1, keepdims=True)
    h = (x1 * lax.rsqrt(ms + EPS) * nw_ref[...]).astype(BF16)
    o_ref[...] = x1
    for f in range(0, w1_ref.shape[1], FF_STEP):
        u = jnp.maximum(_dot(h, w1_ref[:, f:f + FF_STEP]), 0.0)
        o_ref[...] += _dot((u * u).astype(BF16), w2_ref[f:f + FF_STEP, :])


def _outmlp(x2, ys, yn, w_out, norm_w, w1, w2, tm):
    n, d = x2.shape
    row = lambda i: (i, 0)
    return pl.pallas_call(
        _outmlp_body,
        grid=(n // tm,),
        in_specs=[
            pl.BlockSpec((tm, d), row), pl.BlockSpec((tm, ys.shape[1]), row), pl.BlockSpec((tm, yn.shape[1]), row),
            _const_spec(w_out.shape), _const_spec((1, d)), _const_spec(w1.shape), _const_spec(w2.shape),
        ],
        out_specs=pl.BlockSpec((tm, d), row),
        out_shape=jax.ShapeDtypeStruct((n, d), F32),
        compiler_params=pltpu.CompilerParams(
            dimension_semantics=("arbitrary",), vmem_limit_bytes=VMEM_LIMIT),
    )(x2, ys, yn, w_out, norm_w, w1, w2)


def _head_select(first_lane):
    sel = np.zeros((LANES, D_SSM), np.float32)
    for rep in range(DT_COPIES):
        for h in range(SSM_HEADS):
            sel[rep * DT_LANES + first_lane + h, h * SSM_HEAD_DIM:(h + 1) * SSM_HEAD_DIM] = 1.0
    return jnp.asarray(sel, BF16)


def _dt_lanes(fwd, bwd):
    v = jnp.tile(jnp.concatenate([fwd, bwd]).astype(F32), DT_COPIES)
    return jnp.pad(v, (0, LANES - v.shape[0]))[None, :]


def _layer(x, norm_mix_w, w_in, conv_w, conv_b, dt_bias_fwd, dt_bias_bwd, a_log_fwd, a_log_bwd, d_skip,
           ssm_norm_w, q_norm_w, k_norm_w, rel_pos_bias, w_out, norm_mlp_w, w_mlp_in, w_mlp_out, tm):
    bsz, t, d = x.shape
    n = bsz * t
    x2 = x.reshape(n, d)

    z, xbc, q, k, vt, dt = _inproj(x2, norm_mix_w[None, :], w_in.astype(BF16), w_in[:, O_V:].T.astype(BF16), tm)
    r3 = lambda a: a.reshape(bsz, t, a.shape[-1])

    dt_bias = _dt_lanes(dt_bias_fwd, dt_bias_bwd)
    a_log = _dt_lanes(a_log_fwd, a_log_bwd)
    sel_f, sel_b = _head_select(0), _head_select(SSM_HEADS)
    xc, hb = _ssd_bwd(r3(xbc), r3(dt), conv_w, conv_b[None, :], dt_bias, a_log, sel_b)
    y_ssm = _ssd_fwd(xc, r3(dt), r3(z), hb, dt_bias, a_log,
                     jnp.repeat(d_skip.astype(F32), SSM_HEAD_DIM)[None, :], ssm_norm_w[None, :], sel_f, sel_b)

    lane_head = np.arange(LANES) // NA_HEAD_DIM
    seg = jnp.asarray((lane_head[:, None] == lane_head[None, :]) / NA_HEAD_DIM, BF16)
    qw = (jnp.tile(q_norm_w.astype(F32), 2) * (NA_HEAD_DIM ** -0.5 * LOG2E))[None, :]
    kw = jnp.tile(k_norm_w.astype(F32), 2)[None, :]
    y_na = _natten(r3(q), r3(k), vt, _na_bias_rows(rel_pos_bias), _na_col_mask(), qw, kw, seg)

    out = _outmlp(x2, y_ssm.reshape(n, D_SSM), y_na.reshape(n, D_NA), w_out.astype(BF16),
                  norm_mlp_w[None, :], w_mlp_in.astype(BF16), w_mlp_out.astype(BF16), tm)
    return out.reshape(bsz, t, d)


def kernel(x, norm_mix_w, w_in, conv_w, conv_b, dt_bias_fwd, dt_bias_bwd, a_log_fwd, a_log_bwd, d_skip,
           ssm_norm_w, q_norm_w, k_norm_w, rel_pos_bias, w_out, norm_mlp_w, w_mlp_in, w_mlp_out):
    tm = min(512, x.shape[0] * x.shape[1])
    for layer in range(norm_mix_w.shape[0]):
        x = _layer(x, norm_mix_w[layer], w_in[layer], conv_w[layer], conv_b[layer], dt_bias_fwd[layer],
                   dt_bias_bwd[layer], a_log_fwd[layer], a_log_bwd[layer], d_skip[layer], ssm_norm_w[layer],
                   q_norm_w[layer], k_norm_w[layer], rel_pos_bias[layer], w_out[layer], norm_mlp_w[layer],
                   w_mlp_in[layer], w_mlp_out[layer], tm)
    return x
```

```python
import functools

import numpy as np
import jax
import jax.numpy as jnp
from jax import lax
from jax.experimental import pallas as pl
from jax.experimental.pallas import tpu as pltpu

F32 = jnp.float32
BF16 = jnp.bfloat16

LANES = 128
GRID_W = 64
SSM_HEADS = 16
SSM_HEAD_DIM = 64
D_SSM = SSM_HEADS * SSM_HEAD_DIM
SSM_GROUPS = 2
GROUP_W = D_SSM // SSM_GROUPS
D_STATE = 128
D_CONV = 5
CHUNK = 128
FWD_CHUNKS = 4
BWD_CHUNKS = 4
CONV_DIM = D_SSM + 2 * SSM_GROUPS * D_STATE
DT_LANES = 2 * SSM_HEADS
DT_COPIES = 3
NA_HEADS = 16
NA_HEAD_DIM = 64
D_NA = NA_HEADS * NA_HEAD_DIM
NA_ROWS = 8
NA_COLS = 16
NA_PAIRS = NA_HEADS // 2
NA_BLOCK = 4
NA_SPAN = NA_ROWS + NA_BLOCK
NA_BKEYS = NA_SPAN * GRID_W
LOG2E = 1.4426950408889634
EPS = 1e-5
MASKED = -1e30
HALO = 8
CONV_STRIDE = CHUNK // 8 + 1
VMEM_LIMIT = 56 * 1024 * 1024


def _dot(a, b):
    return jnp.dot(a, b, preferred_element_type=F32)


def _dot_nt(a, b):
    return lax.dot_general(a, b, (((1,), (1,)), ((), ())), preferred_element_type=F32)


def _split3(x):
    hi = x.astype(BF16)
    r1 = x - hi.astype(F32)
    mid = r1.astype(BF16)
    lo = (r1 - mid.astype(F32)).astype(BF16)
    return hi, mid, lo


def _dot_sel_rhs(x, sel):
    hi, mid, lo = _split3(x)
    return _dot(hi, sel) + _dot(mid, sel) + _dot(lo, sel)


def _dot_sel_lhs(sel, x):
    hi, mid, lo = _split3(x)
    return _dot(sel, hi) + _dot(sel, mid) + _dot(sel, lo)


def _pack3(x):
    lane = lax.broadcasted_iota(jnp.int32, x.shape, x.ndim - 1)
    hi = x.astype(BF16).astype(F32)
    r1 = x - hi
    mid = r1.astype(BF16).astype(F32)
    return jnp.where(lane < DT_LANES, hi, jnp.where(lane < 2 * DT_LANES, mid, r1 - mid)).astype(BF16)


def _softplus(x):
    return jnp.maximum(x, 0.0) + jnp.log(1.0 + jnp.exp(-jnp.abs(x)))


def _silu(x):
    return x * (1.0 / (1.0 + jnp.exp(-x)))


def _const_spec(shape):
    nd = len(shape)
    return pl.BlockSpec(shape, lambda *_: (0,) * nd, pipeline_mode=pl.Buffered(1))


O_DT = D_SSM + CONV_DIM
O_V = O_DT + DT_LANES + 2 * D_NA
DQK_SPAN = -(-(DT_LANES + 2 * D_NA) // LANES) * LANES


def _inproj_body(x_ref, nw_ref, w_ref, wvt_ref, z_ref, xbc_ref, q_ref, k_ref, vt_ref, dt_ref):
    x = x_ref[...]
    ms = jnp.mean(x * x, axis=-1, keepdims=True)
    hf = x * lax.rsqrt(ms + EPS) * nw_ref[...]
    h = hf.astype(BF16)
    z_ref[...] = _dot(h, w_ref[:, 0:D_SSM])
    xbc_ref[...] = _dot(h, w_ref[:, D_SSM:O_DT])
    r = _dot(h, w_ref[:, O_DT:O_DT + DQK_SPAN])
    lane = lax.broadcasted_iota(jnp.int32, (x.shape[0], LANES), 1)
    d0 = jnp.where(lane < DT_LANES, r[:, 0:LANES], 0.0)
    dt = d0
    for rep in range(1, DT_COPIES):
        dt = dt + pltpu.roll(d0, rep * DT_LANES, 1)
    dt_ref[...] = dt
    q_ref[...] = r[:, DT_LANES:DT_LANES + D_NA]
    k_ref[...] = r[:, DT_LANES + D_NA:DT_LANES + 2 * D_NA]
    vt = _dot(wvt_ref[...], hf.T.astype(BF16))
    for j in range(vt_ref.shape[0]):
        vt_ref[j] = vt[:, j * LANES:(j + 1) * LANES].astype(vt_ref.dtype)


def _inproj(x2, norm_w, w_bf, wvt, tm):
    n, d = x2.shape
    widths = (D_SSM, CONV_DIM, D_NA, D_NA)
    row = lambda i: (i, 0)
    tiles = tm // LANES
    return pl.pallas_call(
        _inproj_body,
        grid=(n // tm,),
        in_specs=[pl.BlockSpec((tm, d), row), _const_spec((1, d)), _const_spec(w_bf.shape), _const_spec(wvt.shape)],
        out_specs=[pl.BlockSpec((tm, w), row) for w in widths]
        + [pl.BlockSpec((tiles, D_NA, LANES), lambda i: (i, 0, 0)), pl.BlockSpec((tm, LANES), row)],
        out_shape=[jax.ShapeDtypeStruct((n, w), F32) for w in widths]
        + [jax.ShapeDtypeStruct((n // LANES, D_NA, LANES), BF16), jax.ShapeDtypeStruct((n, LANES), F32)],
        compiler_params=pltpu.CompilerParams(
            dimension_semantics=("arbitrary",), vmem_limit_bytes=VMEM_LIMIT),
    )(x2, norm_w, w_bf, wvt)


def _tri_masks():
    t = lax.broadcasted_iota(jnp.int32, (CHUNK, CHUNK), 0)
    s = lax.broadcasted_iota(jnp.int32, (CHUNK, CHUNK), 1)
    return s <= t, s >= t


def _dt_and_loga(dt_ref, dtb_ref, alog_ref):
    dtv = _softplus(dt_ref[0] + dtb_ref[...])
    return dtv, dtv * (-LOG2E * jnp.exp(alog_ref[...]))


def _ssd_bwd_body(cur_ref, prev_ref, next_ref, dt_ref, cw_ref, cb_ref, dtb_ref, alog_ref, eb_ref,
                  xc_ref, hb_ref, uext, cout, state):
    c = pl.program_id(1)
    nsteps = pl.num_programs(1)
    step = nsteps - 1 - c
    rows_in = BWD_CHUNKS * CHUNK

    @pl.when(c == 0)
    def _():
        state[...] = jnp.zeros_like(state)

    first = HALO - (D_CONV - 1) // 2
    sub = 8
    nslab = CONV_DIM // LANES
    for slab in range(nslab):
        cols = slice(slab * LANES, (slab + 1) * LANES)
        uext[slab, 0:HALO, :] = jnp.where(step > 0, prev_ref[0, :, cols], 0.0)
        uext[slab, HALO:HALO + rows_in, :] = cur_ref[0, :, cols]
        uext[slab, HALO + rows_in:2 * HALO + rows_in, :] = jnp.where(step < nsteps - 1, next_ref[0, :, cols], 0.0)
        uext[slab, 2 * HALO + rows_in:, :] = jnp.zeros((uext.shape[1] - 2 * HALO - rows_in, LANES), F32)
        bias = jnp.broadcast_to(cb_ref[:, cols], (sub, LANES))
        taps = [jnp.broadcast_to(cw_ref[k:k + 1, cols], (sub, LANES)) for k in range(D_CONV)]
        for ch in range(BWD_CHUNKS):
            for i in range(CONV_STRIDE):
                acc = bias
                for k in range(D_CONV):
                    start = ch * CHUNK + first + k + i
                    acc = acc + taps[k] * uext[slab, pl.ds(start, sub, stride=CONV_STRIDE), :]
                cout[ch * nslab + slab, pl.ds(i, sub, stride=CONV_STRIDE), :] = _silu(acc)
            xc_ref[0, ch * CHUNK:(ch + 1) * CHUNK, cols] = cout[ch * nslab + slab, 0:CHUNK, :]

    _, upper = _tri_masks()
    upper_sel = jnp.where(upper, 1.0, 0.0).astype(BF16)
    for ch in reversed(range(BWD_CHUNKS)):
        rows = slice(ch * CHUNK, (ch + 1) * CHUNK)
        dtv, loga = _dt_and_loga(dt_ref.at[:, rows], dtb_ref, alog_ref)
        rcum = _dot_sel_lhs(upper_sel, loga)
        r0 = rcum[0:1, :]
        wgt = dtv * jnp.exp2(r0 - rcum)
        wgt_x = _dot(_pack3(wgt), eb_ref[...])
        dec_x = _dot(_pack3(jnp.broadcast_to(jnp.exp2(r0), (HALO, LANES))), eb_ref[...])[0:1]

        hb_ref[0, ch] = state[...].astype(BF16)
        xw = (xc_ref[0, rows, 0:D_SSM] * wgt_x).astype(BF16)
        for g in range(SSM_GROUPS):
            gs = slice(g * GROUP_W, (g + 1) * GROUP_W)
            b_t = xc_ref[0, rows, D_SSM + g * D_STATE:D_SSM + (g + 1) * D_STATE].T.astype(BF16)
            state[:, gs] = state[:, gs] * dec_x[:, gs] + _dot(b_t, xw[:, gs])


def _ssd_bwd(xbc, dt, conv_w, conv_b, dt_bias, a_log, sel_b):
    bsz, t, _ = xbc.shape
    rows = BWD_CHUNKS * CHUNK
    nsteps = t // rows
    hps = rows // HALO
    nhalo = t // HALO
    nslab = CONV_DIM // LANES
    rev = lambda c: nsteps - 1 - c
    return pl.pallas_call(
        _ssd_bwd_body,
        grid=(bsz, nsteps),
        in_specs=[
            pl.BlockSpec((1, rows, CONV_DIM), lambda b, c: (b, rev(c), 0)),
            pl.BlockSpec((1, HALO, CONV_DIM), lambda b, c: (b, jnp.maximum(rev(c) * hps - 1, 0), 0)),
            pl.BlockSpec((1, HALO, CONV_DIM), lambda b, c: (b, jnp.minimum((rev(c) + 1) * hps, nhalo - 1), 0)),
            pl.BlockSpec((1, rows, LANES), lambda b, c: (b, rev(c), 0)),
            _const_spec(conv_w.shape), _const_spec(conv_b.shape),
            _const_spec(dt_bias.shape), _const_spec(a_log.shape), _const_spec(sel_b.shape),
        ],
        out_specs=[
            pl.BlockSpec((1, rows, CONV_DIM), lambda b, c: (b, rev(c), 0)),
            pl.BlockSpec((1, BWD_CHUNKS, D_STATE, D_SSM), lambda b, c: (b, rev(c), 0, 0)),
        ],
        out_shape=[
            jax.ShapeDtypeStruct((bsz, t, CONV_DIM), F32),
            jax.ShapeDtypeStruct((bsz, t // CHUNK, D_STATE, D_SSM), BF16),
        ],
        scratch_shapes=[pltpu.VMEM((nslab, rows + 3 * HALO, LANES), F32),
                        pltpu.VMEM((BWD_CHUNKS * nslab, 8 * CONV_STRIDE, LANES), F32),
                        pltpu.VMEM((D_STATE, D_SSM), F32)],
        compiler_params=pltpu.CompilerParams(
            dimension_semantics=("arbitrary", "arbitrary"), vmem_limit_bytes=VMEM_LIMIT),
    )(xbc, xbc, xbc, dt, conv_w, conv_b, dt_bias, a_log, sel_b)


def _ssd_fwd_body(xc_ref, dt_ref, z_ref, hb_ref, dtb_ref, alog_ref, dsk_ref, nw_ref, ef_ref, eb_ref,
                  y_ref, state, ybuf):
    @pl.when(pl.program_id(1) == 0)
    def _():
        state[...] = jnp.zeros_like(state)

    for sub in range(FWD_CHUNKS):
        rows = slice(sub * CHUNK, (sub + 1) * CHUNK)
        _ssd_fwd_chunk(xc_ref.at[0, rows], dt_ref.at[:, rows], z_ref.at[0, rows], hb_ref.at[0, sub], dtb_ref, alog_ref,
                       dsk_ref, nw_ref, ef_ref, eb_ref, y_ref.at[0, rows], state, ybuf.at[rows])


def _ssd_fwd_chunk(xc_ref, dt_ref, z_ref, hb_ref, dtb_ref, alog_ref, dsk_ref, nw_ref, ef_ref, eb_ref,
                   y_ref, state, ybuf):
    dtv, loga = _dt_and_loga(dt_ref, dtb_ref, alog_ref)
    lower, upper = _tri_masks()
    cum_f = _dot_sel_lhs(jnp.where(lower, 1.0, 0.0).astype(BF16), loga)
    cum_b = _dot_sel_lhs(jnp.where(upper, 1.0, 0.0).astype(BF16), loga)
    lane = lax.broadcasted_iota(jnp.int32, (CHUNK, LANES), 1)
    cum = jnp.where((lane & SSM_HEADS) == 0, cum_f, cum_b)
    ecum_p = _pack3(jnp.exp2(cum))
    wgt_p = _pack3(dtv * jnp.exp2(cum_f[CHUNK - 1:CHUNK, :] - cum_f))
    src_t = (cum - jnp.log2(dtv)).T
    first_half = lane < SSM_HEAD_DIM

    for g in range(SSM_GROUPS):
        gs = slice(g * GROUP_W, (g + 1) * GROUP_W)
        b_f32 = xc_ref[:, D_SSM + g * D_STATE:D_SSM + (g + 1) * D_STATE]
        b_g = b_f32.astype(BF16)
        c0 = D_SSM + SSM_GROUPS * D_STATE + g * D_STATE
        c_g = xc_ref[:, c0:c0 + D_STATE].astype(BF16)
        cb = _dot_nt(c_g, b_g)
        ea_g = _dot(ecum_p, ef_ref[:, gs])
        er_g = _dot(ecum_p, eb_ref[:, gs])
        y_off = _dot(c_g, state[:, gs].astype(BF16)) * ea_g + _dot(c_g, hb_ref[:, gs]) * er_g
        xw = (xc_ref[:, gs] * _dot(wgt_p, ef_ref[:, gs])).astype(BF16)
        state[:, gs] = state[:, gs] * ea_g[CHUNK - 1:CHUNK, :] + _dot(b_f32.T.astype(BF16), xw)
        pairs = GROUP_W // LANES
        for pp in range(pairs):
            pr = g * pairs + pp
            ps = slice(pr * LANES, (pr + 1) * LANES)
            mats = []
            for h in (2 * pr, 2 * pr + 1):
                hb = SSM_HEADS + h
                seg_f = jnp.broadcast_to(cum[:, h:h + 1], (CHUNK, CHUNK)) - src_t[h:h + 1, :]
                seg_b = jnp.broadcast_to(cum[:, hb:hb + 1], (CHUNK, CHUNK)) - src_t[hb:hb + 1, :]
                d_f = jnp.exp2(jnp.where(lower, seg_f, MASKED))
                d_b = jnp.exp2(jnp.where(upper, seg_b, MASKED))
                mats.append(((d_f + d_b) * cb).astype(BF16))
            xs = xc_ref[:, ps]
            xp = xs.astype(BF16)
            zero = jnp.zeros_like(xp)
            rhs = jnp.concatenate([jnp.where(first_half, xp, zero), jnp.where(first_half, zero, xp)], axis=0)
            y_diag = _dot(jnp.concatenate(mats, axis=1), rhs)
            ybuf[:, ps] = y_diag + y_off[:, pp * LANES:(pp + 1) * LANES] + dsk_ref[:, ps] * xs
        gated = ybuf[:, gs] * _silu(z_ref[:, gs])
        ms = jnp.mean(gated * gated, axis=-1, keepdims=True)
        y_ref[:, gs] = (gated * lax.rsqrt(ms + EPS) * nw_ref[:, gs]).astype(y_ref.dtype)


def _ssd_fwd(xc, dt, z, hb, dt_bias, a_log, d_skip_x, norm_w, sel_f, sel_b):
    bsz, t, _ = xc.shape
    step = FWD_CHUNKS * CHUNK
    blk = lambda w: pl.BlockSpec((1, step, w), lambda b, c: (b, c, 0))
    return pl.pallas_call(
        _ssd_fwd_body,
        grid=(bsz, t // step),
        in_specs=[
            blk(CONV_DIM), blk(LANES), blk(D_SSM),
            pl.BlockSpec((1, FWD_CHUNKS, D_STATE, D_SSM), lambda b, c: (b, c, 0, 0)),
            _const_spec(dt_bias.shape), _const_spec(a_log.shape), _const_spec(d_skip_x.shape),
            _const_spec(norm_w.shape), _const_spec(sel_f.shape), _const_spec(sel_b.shape),
        ],
        out_specs=blk(D_SSM),
        out_shape=jax.ShapeDtypeStruct((bsz, t, D_SSM), BF16),
        scratch_shapes=[pltpu.VMEM((D_STATE, D_SSM), F32), pltpu.VMEM((step, D_SSM), F32)],
        compiler_params=pltpu.CompilerParams(
            dimension_semantics=("arbitrary", "arbitrary"), vmem_limit_bytes=VMEM_LIMIT),
    )(xc, dt, z, hb, dt_bias, a_log, d_skip_x, norm_w, sel_f, sel_b)


NORM_ROWS = 512


def _build_bias(g_ref, colmask_ref, bias):
    rel = _block_row_rel()
    half = lax.broadcasted_iota(jnp.int32, (GRID_W, LANES), 1) < GRID_W
    nrel = 2 * NA_ROWS

    @functools.cache
    def toeplitz(hh, r, upper_half):
        if upper_half:
            return pltpu.roll(toeplitz(hh, r, False), GRID_W, 1)
        row = g_ref[0, hh * nrel + r:hh * nrel + r + 1, :] * LOG2E
        return pltpu.roll(jnp.broadcast_to(row, (GRID_W, LANES)), 0, 1, stride=1, stride_axis=0)

    for kind in range(rel.shape[0]):
        for hh in range(2):
            for u in range(0, NA_BLOCK, 2):
                lane0 = (hh * NA_BLOCK + u) * GRID_W
                for a in range(NA_SPAN):
                    r_lo, r_hi = int(rel[kind, u, a]), int(rel[kind, u + 1, a])
                    lo = toeplitz(hh, r_lo, False) if r_lo >= 0 else MASKED
                    hi = toeplitz(hh, r_hi, True) if r_hi >= 0 else MASKED
                    tile = jnp.where(half, lo, hi) + colmask_ref[...]
                    bias[kind, a * GRID_W:(a + 1) * GRID_W, lane0:lane0 + 2 * GRID_W] = tile


def _na_body(q_ref, k_ref, vt_ref, g_ref, colmask_ref, qw_ref, kw_ref, seg_ref, o_ref, qn, kn, sbuf, bias):
    t = q_ref.shape[1]
    rows = t // GRID_W

    @pl.when(pl.program_id(1) == 0)
    def _():
        _build_bias(g_ref, colmask_ref, bias)

    def norm_step(i, carry):
        sl = pl.ds(pl.multiple_of(i * NORM_ROWS, NORM_ROWS), NORM_ROWS)
        for src, w_ref, dst in ((q_ref, qw_ref, qn), (k_ref, kw_ref, kn)):
            xx = src[0, sl, :]
            sq = xx * xx
            hi = sq.astype(BF16)
            lo = (sq - hi.astype(F32)).astype(BF16)
            msq = _dot(hi, seg_ref[...]) + _dot(lo, seg_ref[...])
            dst[sl, :] = (xx * lax.rsqrt(msq + EPS) * w_ref[...]).astype(BF16)
        return carry

    lax.fori_loop(0, t // NORM_ROWS, norm_step, 0, unroll=2)

    nblk = rows // NA_BLOCK
    bq = NA_BLOCK * GRID_W
    first_half = lax.broadcasted_iota(jnp.int32, (bq, LANES), 1) < NA_HEAD_DIM

    def offsets(bi):
        i0 = bi * NA_BLOCK
        rlo = jnp.clip(i0 - NA_ROWS // 2, 0, rows - NA_SPAN)
        return pl.multiple_of(i0 * GRID_W, bq), pl.multiple_of(rlo * GRID_W, 2 * GRID_W)

    def scores(bi, slot):
        kind = jnp.where(bi == 0, 0, jnp.where(bi == nblk - 1, 2, 1))
        qoff, koff = offsets(bi)
        qb = qn[pl.ds(qoff, bq), :]
        zero = jnp.zeros_like(qb)
        qs = jnp.concatenate([jnp.where(first_half, qb, zero), jnp.where(first_half, zero, qb)], axis=0)
        sbuf[slot] = _dot_nt(kn[pl.ds(koff, NA_BKEYS), :], qs) + bias[kind]

    head_a_dims = lax.broadcasted_iota(jnp.int32, (LANES, bq), 0) < NA_HEAD_DIM

    def attend(bi, slot):
        qoff, koff = offsets(bi)
        s = sbuf[slot]
        p = jnp.exp2(s - jnp.max(s, axis=0, keepdims=True))
        denom = jnp.sum(p, axis=0, keepdims=True)
        tile0 = koff // LANES
        v_t = jnp.concatenate([vt_ref[tile0 + i] for i in range(NA_BKEYS // LANES)], axis=1)
        o_t = _dot(v_t, p.astype(BF16)) * (1.0 / denom)
        o_ref[0, pl.ds(qoff, bq), :] = jnp.where(head_a_dims, o_t[:, 0:bq], o_t[:, bq:]).T.astype(o_ref.dtype)

    scores(0, 0)

    def pair_step(i, carry):
        b0 = 2 * i
        scores(b0 + 1, 1)
        attend(b0, 0)
        scores(jnp.minimum(b0 + 2, nblk - 1), 0)
        attend(b0 + 1, 1)
        return carry

    lax.fori_loop(0, nblk // 2, pair_step, 0, unroll=4)


def _natten(q, k, vt, g, colmask, qw, kw, seg):
    bsz, t, _ = q.shape
    nq = 2 * NA_BLOCK * GRID_W
    blk = pl.BlockSpec((1, t, LANES), lambda p, b: (b, 0, p))
    return pl.pallas_call(
        _na_body,
        grid=(NA_PAIRS, bsz),
        in_specs=[
            blk, blk, pl.BlockSpec((t // LANES, LANES, LANES), lambda p, b: (b, p, 0)),
            pl.BlockSpec((1,) + g.shape[1:], lambda p, b: (p, 0, 0)), _const_spec(colmask.shape),
            _const_spec(qw.shape), _const_spec(kw.shape), _const_spec(seg.shape),
        ],
        out_specs=blk,
        out_shape=jax.ShapeDtypeStruct((bsz, t, D_NA), BF16),
        scratch_shapes=[pltpu.VMEM((t, LANES), BF16), pltpu.VMEM((t, LANES), BF16),
                        pltpu.VMEM((2, NA_BKEYS, nq), F32), pltpu.VMEM((3, NA_BKEYS, nq), F32)],
        compiler_params=pltpu.CompilerParams(
            dimension_semantics=("arbitrary", "arbitrary"), vmem_limit_bytes=VMEM_LIMIT),
    )(q, k, vt, g, colmask, qw, kw, seg)


def _block_row_rel():
    rel = -np.ones((3, NA_BLOCK, NA_SPAN), np.int64)
    for u in range(NA_BLOCK):
        for a in range(NA_SPAN):
            if a < NA_ROWS:
                rel[0, u, a] = a - u + NA_ROWS - 1
            if u <= a < u + NA_ROWS:
                rel[1, u, a] = a - u + NA_ROWS - 1 - NA_ROWS // 2
            if a >= NA_SPAN - NA_ROWS:
                rel[2, u, a] = a - u + NA_BLOCK - NA_SPAN + NA_ROWS - 1
    return rel


def _na_bias_rows(rpb):
    centre = NA_COLS - 1
    rev = rpb[:, :, ::-1]
    rows = jnp.concatenate(
        [rev[:, :, centre:], jnp.zeros(rpb.shape[:2] + (LANES - rpb.shape[2],), rpb.dtype), rev[:, :, :centre]],
        axis=2).astype(F32)
    rows = jnp.pad(rows, ((0, 0), (0, 2 * NA_ROWS - rpb.shape[1]), (0, 0)))
    return rows.reshape(NA_PAIRS, 2 * 2 * NA_ROWS, LANES)


def _na_col_mask():
    j = np.arange(GRID_W)
    c0 = np.clip(j - NA_COLS // 2, 0, GRID_W - NA_COLS)
    cc = np.arange(GRID_W)
    valid = (cc[:, None] >= c0[None, :]) & (cc[:, None] < c0[None, :] + NA_COLS)
    return jnp.asarray(np.tile(np.where(valid, 0.0, MASKED), (1, 2)), F32)


FF_STEP = 512


def _outmlp_body(x_ref, ys_ref, yn_ref, wo_ref, nw_ref, w1_ref, w2_ref, o_ref):
    dmix = ys_ref.shape[-1]
    x1 = x_ref[...] + _dot(ys_ref[...], wo_ref[0:dmix, :]) + _dot(yn_ref[...], wo_ref[dmix:, :])
    ms = jnp.mean(x1 * x1, axis=-1, keepdims=True)
    h = (x1 * lax.rsqrt(ms + EPS) * nw_ref[...]).astype(BF16)
    o_ref[...] = x1
    for f in range(0, w1_ref.shape[1], FF_STEP):
        u = jnp.maximum(_dot(h, w1_ref[:, f:f + FF_STEP]), 0.0)
        o_ref[...] += _dot((u * u).astype(BF16), w2_ref[f:f + FF_STEP, :])


def _outmlp(x2, ys, yn, w_out, norm_w, w1, w2, tm):
    n, d = x2.shape
    row = lambda i: (i, 0)
    return pl.pallas_call(
        _outmlp_body,
        grid=(n // tm,),
        in_specs=[
            pl.BlockSpec((tm, d), row), pl.BlockSpec((tm, ys.shape[1]), row), pl.BlockSpec((tm, yn.shape[1]), row),
            _const_spec(w_out.shape), _const_spec((1, d)), _const_spec(w1.shape), _const_spec(w2.shape),
        ],
        out_specs=pl.BlockSpec((tm, d), row),
        out_shape=jax.ShapeDtypeStruct((n, d), F32),
        compiler_params=pltpu.CompilerParams(
            dimension_semantics=("arbitrary",), vmem_limit_bytes=VMEM_LIMIT),
    )(x2, ys, yn, w_out, norm_w, w1, w2)


def _head_select(first_lane):
    sel = np.zeros((LANES, D_SSM), np.float32)
    for rep in range(DT_COPIES):
        for h in range(SSM_HEADS):
            sel[rep * DT_LANES + first_lane + h, h * SSM_HEAD_DIM:(h + 1) * SSM_HEAD_DIM] = 1.0
    return jnp.asarray(sel, BF16)


def _dt_lanes(fwd, bwd):
    v = jnp.tile(jnp.concatenate([fwd, bwd]).astype(F32), DT_COPIES)
    return jnp.pad(v, (0, LANES - v.shape[0]))[None, :]


def _layer(x, norm_mix_w, w_in, conv_w, conv_b, dt_bias_fwd, dt_bias_bwd, a_log_fwd, a_log_bwd, d_skip,
           ssm_norm_w, q_norm_w, k_norm_w, rel_pos_bias, w_out, norm_mlp_w, w_mlp_in, w_mlp_out, tm):
    bsz, t, d = x.shape
    n = bsz * t
    x2 = x.reshape(n, d)

    z, xbc, q, k, vt, dt = _inproj(x2, norm_mix_w[None, :], w_in[:, :O_DT + DQK_SPAN].astype(BF16),
                                   w_in[:, O_V:].T.astype(BF16), tm)
    r3 = lambda a: a.reshape(bsz, t, a.shape[-1])

    dt_bias = _dt_lanes(dt_bias_fwd, dt_bias_bwd)
    a_log = _dt_lanes(a_log_fwd, a_log_bwd)
    sel_f, sel_b = _head_select(0), _head_select(SSM_HEADS)
    xc, hb = _ssd_bwd(r3(xbc), r3(dt), conv_w, conv_b[None, :], dt_bias, a_log, sel_b)
    y_ssm = _ssd_fwd(xc, r3(dt), r3(z), hb, dt_bias, a_log,
                     jnp.repeat(d_skip.astype(F32), SSM_HEAD_DIM)[None, :], ssm_norm_w[None, :], sel_f, sel_b)

    lane_head = np.arange(LANES) // NA_HEAD_DIM
    seg = jnp.asarray((lane_head[:, None] == lane_head[None, :]) / NA_HEAD_DIM, BF16)
    qw = (jnp.tile(q_norm_w.astype(F32), 2) * (NA_HEAD_DIM ** -0.5 * LOG2E))[None, :]
    kw = jnp.tile(k_norm_w.astype(F32), 2)[None, :]
    y_na = _natten(r3(q), r3(k), vt, _na_bias_rows(rel_pos_bias), _na_col_mask(), qw, kw, seg)

    out = _outmlp(x2, y_ssm.reshape(n, D_SSM), y_na.reshape(n, D_NA), w_out.astype(BF16),
                  norm_mlp_w[None, :], w_mlp_in.astype(BF16), w_mlp_out.astype(BF16), min(2 * tm, n))
    return out.reshape(bsz, t, d)


def kernel(x, norm_mix_w, w_in, conv_w, conv_b, dt_bias_fwd, dt_bias_bwd, a_log_fwd, a_log_bwd, d_skip,
           ssm_norm_w, q_norm_w, k_norm_w, rel_pos_bias, w_out, norm_mlp_w, w_mlp_in, w_mlp_out):
    tm = min(512, x.shape[0] * x.shape[1])
    for layer in range(norm_mix_w.shape[0]):
        x = _layer(x, norm_mix_w[layer], w_in[layer], conv_w[layer], conv_b[layer], dt_bias_fwd[layer],
                   dt_bias_bwd[layer], a_log_fwd[layer], a_log_bwd[layer], d_skip[layer], ssm_norm_w[layer],
                   q_norm_w[layer], k_norm_w[layer], rel_pos_bias[layer], w_out[layer], norm_mlp_w[layer],
                   w_mlp_in[layer], w_mlp_out[layer], tm)
    return x
```

```python
import functools

import numpy as np
import jax
import jax.numpy as jnp
from jax import lax
from jax.experimental import pallas as pl
from jax.experimental.pallas import tpu as pltpu

F32 = jnp.float32
BF16 = jnp.bfloat16

LANES = 128
GRID_W = 64
SSM_HEADS = 16
SSM_HEAD_DIM = 64
D_SSM = SSM_HEADS * SSM_HEAD_DIM
SSM_GROUPS = 2
GROUP_W = D_SSM // SSM_GROUPS
D_STATE = 128
D_CONV = 5
CHUNK = 128
FWD_CHUNKS = 4
BWD_CHUNKS = 4
CONV_DIM = D_SSM + 2 * SSM_GROUPS * D_STATE
DT_LANES = 2 * SSM_HEADS
DT_COPIES = 3
NA_HEADS = 16
NA_HEAD_DIM = 64
D_NA = NA_HEADS * NA_HEAD_DIM
NA_ROWS = 8
NA_COLS = 16
NA_PAIRS = NA_HEADS // 2
NA_BLOCK = 4
NA_SPAN = NA_ROWS + NA_BLOCK
NA_BKEYS = NA_SPAN * GRID_W
LOG2E = 1.4426950408889634
EPS = 1e-5
MASKED = -1e30
HALO = 8
CONV_STRIDE = CHUNK // 8 + 1
VMEM_LIMIT = 56 * 1024 * 1024


def _dot(a, b):
    return jnp.dot(a, b, preferred_element_type=F32)


def _dot_nt(a, b):
    return lax.dot_general(a, b, (((1,), (1,)), ((), ())), preferred_element_type=F32)


def _split3(x):
    hi = x.astype(BF16)
    r1 = x - hi.astype(F32)
    mid = r1.astype(BF16)
    lo = (r1 - mid.astype(F32)).astype(BF16)
    return hi, mid, lo


def _dot_sel_rhs(x, sel):
    hi, mid, lo = _split3(x)
    return _dot(hi, sel) + _dot(mid, sel) + _dot(lo, sel)


def _dot_sel_lhs(sel, x):
    hi, mid, lo = _split3(x)
    return _dot(sel, hi) + _dot(sel, mid) + _dot(sel, lo)


def _pack3(x):
    lane = lax.broadcasted_iota(jnp.int32, x.shape, x.ndim - 1)
    hi = x.astype(BF16).astype(F32)
    r1 = x - hi
    mid = r1.astype(BF16).astype(F32)
    return jnp.where(lane < DT_LANES, hi, jnp.where(lane < 2 * DT_LANES, mid, r1 - mid)).astype(BF16)


def _softplus(x):
    return jnp.maximum(x, 0.0) + jnp.log(1.0 + jnp.exp(-jnp.abs(x)))


def _silu(x):
    return x * (1.0 / (1.0 + jnp.exp(-x)))


def _const_spec(shape):
    nd = len(shape)
    return pl.BlockSpec(shape, lambda *_: (0,) * nd, pipeline_mode=pl.Buffered(1))


O_DT = D_SSM + CONV_DIM
PROJ_DIM = O_DT + DT_LANES + 3 * D_NA
PROJ_PAD = -(-PROJ_DIM // LANES) * LANES


def _inproj_body(x_ref, nw_ref, w_ref, z_ref, xbc_ref, q_ref, k_ref, vt_ref, dt_ref):
    x = x_ref[...]
    ms = jnp.mean(x * x, axis=-1, keepdims=True)
    h = (x * lax.rsqrt(ms + EPS) * nw_ref[...]).astype(BF16)
    z_ref[...] = _dot(h, w_ref[:, 0:D_SSM])
    xbc_ref[...] = _dot(h, w_ref[:, D_SSM:O_DT])
    r = _dot(h, w_ref[:, O_DT:PROJ_PAD])
    lane = lax.broadcasted_iota(jnp.int32, (x.shape[0], LANES), 1)
    d0 = jnp.where(lane < DT_LANES, r[:, 0:LANES], 0.0)
    dt = d0
    for rep in range(1, DT_COPIES):
        dt = dt + pltpu.roll(d0, rep * DT_LANES, 1)
    dt_ref[...] = dt
    q_ref[...] = r[:, DT_LANES:DT_LANES + D_NA]
    k_ref[...] = r[:, DT_LANES + D_NA:DT_LANES + 2 * D_NA]
    vt = r[:, DT_LANES + 2 * D_NA:DT_LANES + 3 * D_NA].T
    for j in range(vt_ref.shape[0]):
        vt_ref[j] = vt[:, j * LANES:(j + 1) * LANES].astype(vt_ref.dtype)


def _inproj(x2, norm_w, w_bf, tm):
    n, d = x2.shape
    widths = (D_SSM, CONV_DIM, D_NA, D_NA)
    row = lambda i: (i, 0)
    tiles = tm // LANES
    return pl.pallas_call(
        _inproj_body,
        grid=(n // tm,),
        in_specs=[pl.BlockSpec((tm, d), row), _const_spec((1, d)), _const_spec(w_bf.shape)],
        out_specs=[pl.BlockSpec((tm, w), row) for w in widths]
        + [pl.BlockSpec((tiles, D_NA, LANES), lambda i: (i, 0, 0)), pl.BlockSpec((tm, LANES), row)],
        out_shape=[jax.ShapeDtypeStruct((n, w), F32) for w in widths]
        + [jax.ShapeDtypeStruct((n // LANES, D_NA, LANES), BF16), jax.ShapeDtypeStruct((n, LANES), F32)],
        compiler_params=pltpu.CompilerParams(
            dimension_semantics=("arbitrary",), vmem_limit_bytes=VMEM_LIMIT),
    )(x2, norm_w, w_bf)


def _tri_masks():
    t = lax.broadcasted_iota(jnp.int32, (CHUNK, CHUNK), 0)
    s = lax.broadcasted_iota(jnp.int32, (CHUNK, CHUNK), 1)
    return s <= t, s >= t


def _dt_and_loga(dt_ref, dtb_ref, alog_ref):
    dtv = _softplus(dt_ref[0] + dtb_ref[...])
    return dtv, dtv * (-LOG2E * jnp.exp(alog_ref[...]))


def _ssd_bwd_body(cur_ref, prev_ref, next_ref, dt_ref, cw_ref, cb_ref, dtb_ref, alog_ref, eb_ref,
                  xc_ref, hb_ref, uext, cout, state):
    c = pl.program_id(1)
    nsteps = pl.num_programs(1)
    step = nsteps - 1 - c
    rows_in = BWD_CHUNKS * CHUNK

    @pl.when(c == 0)
    def _():
        state[...] = jnp.zeros_like(state)

    first = HALO - (D_CONV - 1) // 2
    sub = 8
    nslab = CONV_DIM // LANES
    for slab in range(nslab):
        cols = slice(slab * LANES, (slab + 1) * LANES)
        uext[slab, 0:HALO, :] = jnp.where(step > 0, prev_ref[0, :, cols], 0.0)
        uext[slab, HALO:HALO + rows_in, :] = cur_ref[0, :, cols]
        uext[slab, HALO + rows_in:2 * HALO + rows_in, :] = jnp.where(step < nsteps - 1, next_ref[0, :, cols], 0.0)
        uext[slab, 2 * HALO + rows_in:, :] = jnp.zeros((uext.shape[1] - 2 * HALO - rows_in, LANES), F32)
        bias = jnp.broadcast_to(cb_ref[:, cols], (sub, LANES))
        taps = [jnp.broadcast_to(cw_ref[k:k + 1, cols], (sub, LANES)) for k in range(D_CONV)]
        for ch in range(BWD_CHUNKS):
            for i in range(CONV_STRIDE):
                acc = bias
                for k in range(D_CONV):
                    start = ch * CHUNK + first + k + i
                    acc = acc + taps[k] * uext[slab, pl.ds(start, sub, stride=CONV_STRIDE), :]
                cout[ch * nslab + slab, pl.ds(i, sub, stride=CONV_STRIDE), :] = _silu(acc)
            xc_ref[0, ch * CHUNK:(ch + 1) * CHUNK, cols] = cout[ch * nslab + slab, 0:CHUNK, :]

    _, upper = _tri_masks()
    upper_sel = jnp.where(upper, 1.0, 0.0).astype(BF16)
    for ch in reversed(range(BWD_CHUNKS)):
        rows = slice(ch * CHUNK, (ch + 1) * CHUNK)
        dtv, loga = _dt_and_loga(dt_ref.at[:, rows], dtb_ref, alog_ref)
        rcum = _dot_sel_lhs(upper_sel, loga)
        r0 = rcum[0:1, :]
        wgt = dtv * jnp.exp2(r0 - rcum)
        wgt_x = _dot(_pack3(wgt), eb_ref[...])
        dec_x = _dot(_pack3(jnp.broadcast_to(jnp.exp2(r0), (HALO, LANES))), eb_ref[...])[0:1]

        hb_ref[0, ch] = state[...].astype(BF16)
        xw = (xc_ref[0, rows, 0:D_SSM] * wgt_x).astype(BF16)
        for g in range(SSM_GROUPS):
            gs = slice(g * GROUP_W, (g + 1) * GROUP_W)
            b_t = xc_ref[0, rows, D_SSM + g * D_STATE:D_SSM + (g + 1) * D_STATE].T.astype(BF16)
            state[:, gs] = state[:, gs] * dec_x[:, gs] + _dot(b_t, xw[:, gs])


def _ssd_bwd(xbc, dt, conv_w, conv_b, dt_bias, a_log, sel_b):
    bsz, t, _ = xbc.shape
    rows = BWD_CHUNKS * CHUNK
    nsteps = t // rows
    hps = rows // HALO
    nhalo = t // HALO
    nslab = CONV_DIM // LANES
    rev = lambda c: nsteps - 1 - c
    return pl.pallas_call(
        _ssd_bwd_body,
        grid=(bsz, nsteps),
        in_specs=[
            pl.BlockSpec((1, rows, CONV_DIM), lambda b, c: (b, rev(c), 0)),
            pl.BlockSpec((1, HALO, CONV_DIM), lambda b, c: (b, jnp.maximum(rev(c) * hps - 1, 0), 0)),
            pl.BlockSpec((1, HALO, CONV_DIM), lambda b, c: (b, jnp.minimum((rev(c) + 1) * hps, nhalo - 1), 0)),
            pl.BlockSpec((1, rows, LANES), lambda b, c: (b, rev(c), 0)),
            _const_spec(conv_w.shape), _const_spec(conv_b.shape),
            _const_spec(dt_bias.shape), _const_spec(a_log.shape), _const_spec(sel_b.shape),
        ],
        out_specs=[
            pl.BlockSpec((1, rows, CONV_DIM), lambda b, c: (b, rev(c), 0)),
            pl.BlockSpec((1, BWD_CHUNKS, D_STATE, D_SSM), lambda b, c: (b, rev(c), 0, 0)),
        ],
        out_shape=[
            jax.ShapeDtypeStruct((bsz, t, CONV_DIM), F32),
            jax.ShapeDtypeStruct((bsz, t // CHUNK, D_STATE, D_SSM), BF16),
        ],
        scratch_shapes=[pltpu.VMEM((nslab, rows + 3 * HALO, LANES), F32),
                        pltpu.VMEM((BWD_CHUNKS * nslab, 8 * CONV_STRIDE, LANES), F32),
                        pltpu.VMEM((D_STATE, D_SSM), F32)],
        compiler_params=pltpu.CompilerParams(
            dimension_semantics=("arbitrary", "arbitrary"), vmem_limit_bytes=VMEM_LIMIT),
    )(xbc, xbc, xbc, dt, conv_w, conv_b, dt_bias, a_log, sel_b)


def _ssd_fwd_body(xc_ref, dt_ref, z_ref, hb_ref, dtb_ref, alog_ref, dsk_ref, nw_ref, ef_ref, eb_ref,
                  y_ref, state, ybuf):
    @pl.when(pl.program_id(1) == 0)
    def _():
        state[...] = jnp.zeros_like(state)

    for sub in range(FWD_CHUNKS):
        rows = slice(sub * CHUNK, (sub + 1) * CHUNK)
        _ssd_fwd_chunk(xc_ref.at[0, rows], dt_ref.at[:, rows], z_ref.at[0, rows], hb_ref.at[0, sub], dtb_ref, alog_ref,
                       dsk_ref, nw_ref, ef_ref, eb_ref, y_ref.at[0, rows], state, ybuf.at[rows])


def _ssd_fwd_chunk(xc_ref, dt_ref, z_ref, hb_ref, dtb_ref, alog_ref, dsk_ref, nw_ref, ef_ref, eb_ref,
                   y_ref, state, ybuf):
    dtv, loga = _dt_and_loga(dt_ref, dtb_ref, alog_ref)
    lower, upper = _tri_masks()
    cum_f = _dot_sel_lhs(jnp.where(lower, 1.0, 0.0).astype(BF16), loga)
    cum_b = _dot_sel_lhs(jnp.where(upper, 1.0, 0.0).astype(BF16), loga)
    lane = lax.broadcasted_iota(jnp.int32, (CHUNK, LANES), 1)
    cum = jnp.where((lane & SSM_HEADS) == 0, cum_f, cum_b)
    ecum_p = _pack3(jnp.exp2(cum))
    wgt_p = _pack3(dtv * jnp.exp2(cum_f[CHUNK - 1:CHUNK, :] - cum_f))
    src_t = (cum - jnp.log2(dtv)).T
    first_half = lane < SSM_HEAD_DIM

    for g in range(SSM_GROUPS):
        gs = slice(g * GROUP_W, (g + 1) * GROUP_W)
        b_f32 = xc_ref[:, D_SSM + g * D_STATE:D_SSM + (g + 1) * D_STATE]
        b_g = b_f32.astype(BF16)
        c0 = D_SSM + SSM_GROUPS * D_STATE + g * D_STATE
        c_g = xc_ref[:, c0:c0 + D_STATE].astype(BF16)
        cb = _dot_nt(c_g, b_g)
        ea_g = _dot(ecum_p, ef_ref[:, gs])
        er_g = _dot(ecum_p, eb_ref[:, gs])
        y_off = _dot(c_g, state[:, gs].astype(BF16)) * ea_g + _dot(c_g, hb_ref[:, gs]) * er_g
        xw = (xc_ref[:, gs] * _dot(wgt_p, ef_ref[:, gs])).astype(BF16)
        state[:, gs] = state[:, gs] * ea_g[CHUNK - 1:CHUNK, :] + _dot(b_f32.T.astype(BF16), xw)
        pairs = GROUP_W // LANES
        for pp in range(pairs):
            pr = g * pairs + pp
            ps = slice(pr * LANES, (pr + 1) * LANES)
            mats = []
            for h in (2 * pr, 2 * pr + 1):
                hb = SSM_HEADS + h
                seg_f = jnp.broadcast_to(cum[:, h:h + 1], (CHUNK, CHUNK)) - src_t[h:h + 1, :]
                seg_b = jnp.broadcast_to(cum[:, hb:hb + 1], (CHUNK, CHUNK)) - src_t[hb:hb + 1, :]
                d_f = jnp.exp2(jnp.where(lower, seg_f, MASKED))
                d_b = jnp.exp2(jnp.where(upper, seg_b, MASKED))
                mats.append(((d_f + d_b) * cb).astype(BF16))
            xs = xc_ref[:, ps]
            xp = xs.astype(BF16)
            zero = jnp.zeros_like(xp)
            rhs = jnp.concatenate([jnp.where(first_half, xp, zero), jnp.where(first_half, zero, xp)], axis=0)
            y_diag = _dot(jnp.concatenate(mats, axis=1), rhs)
            ybuf[:, ps] = y_diag + y_off[:, pp * LANES:(pp + 1) * LANES] + dsk_ref[:, ps] * xs
        gated = ybuf[:, gs] * _silu(z_ref[:, gs])
        ms = jnp.mean(gated * gated, axis=-1, keepdims=True)
        y_ref[:, gs] = (gated * lax.rsqrt(ms + EPS) * nw_ref[:, gs]).astype(y_ref.dtype)


def _ssd_fwd(xc, dt, z, hb, dt_bias, a_log, d_skip_x, norm_w, sel_f, sel_b):
    bsz, t, _ = xc.shape
    step = FWD_CHUNKS * CHUNK
    blk = lambda w: pl.BlockSpec((1, step, w), lambda b, c: (b, c, 0))
    return pl.pallas_call(
        _ssd_fwd_body,
        grid=(bsz, t // step),
        in_specs=[
            blk(CONV_DIM), blk(LANES), blk(D_SSM),
            pl.BlockSpec((1, FWD_CHUNKS, D_STATE, D_SSM), lambda b, c: (b, c, 0, 0)),
            _const_spec(dt_bias.shape), _const_spec(a_log.shape), _const_spec(d_skip_x.shape),
            _const_spec(norm_w.shape), _const_spec(sel_f.shape), _const_spec(sel_b.shape),
        ],
        out_specs=blk(D_SSM),
        out_shape=jax.ShapeDtypeStruct((bsz, t, D_SSM), BF16),
        scratch_shapes=[pltpu.VMEM((D_STATE, D_SSM), F32), pltpu.VMEM((step, D_SSM), F32)],
        compiler_params=pltpu.CompilerParams(
            dimension_semantics=("arbitrary", "arbitrary"), vmem_limit_bytes=VMEM_LIMIT),
    )(xc, dt, z, hb, dt_bias, a_log, d_skip_x, norm_w, sel_f, sel_b)


NORM_ROWS = 512


def _build_bias(g_ref, colmask_ref, bias):
    rel = _block_row_rel()
    half = lax.broadcasted_iota(jnp.int32, (GRID_W, LANES), 1) < GRID_W
    nrel = 2 * NA_ROWS

    @functools.cache
    def toeplitz(hh, r, upper_half):
        if upper_half:
            return pltpu.roll(toeplitz(hh, r, False), GRID_W, 1)
        row = g_ref[0, hh * nrel + r:hh * nrel + r + 1, :] * LOG2E
        return pltpu.roll(jnp.broadcast_to(row, (GRID_W, LANES)), 0, 1, stride=1, stride_axis=0)

    for kind in range(rel.shape[0]):
        for hh in range(2):
            for u in range(0, NA_BLOCK, 2):
                lane0 = (hh * NA_BLOCK + u) * GRID_W
                for a in range(NA_SPAN):
                    r_lo, r_hi = int(rel[kind, u, a]), int(rel[kind, u + 1, a])
                    lo = toeplitz(hh, r_lo, False) if r_lo >= 0 else MASKED
                    hi = toeplitz(hh, r_hi, True) if r_hi >= 0 else MASKED
                    tile = jnp.where(half, lo, hi) + colmask_ref[...]
                    bias[kind, a * GRID_W:(a + 1) * GRID_W, lane0:lane0 + 2 * GRID_W] = tile


def _na_body(q_ref, k_ref, vt_ref, g_ref, colmask_ref, qw_ref, kw_ref, seg_ref, o_ref, qn, kn, sbuf, bias):
    t = q_ref.shape[1]
    rows = t // GRID_W

    @pl.when(pl.program_id(1) == 0)
    def _():
        _build_bias(g_ref, colmask_ref, bias)

    def norm_step(i, carry):
        sl = pl.ds(pl.multiple_of(i * NORM_ROWS, NORM_ROWS), NORM_ROWS)
        for src, w_ref, dst in ((q_ref, qw_ref, qn), (k_ref, kw_ref, kn)):
            xx = src[0, sl, :]
            sq = xx * xx
            hi = sq.astype(BF16)
            lo = (sq - hi.astype(F32)).astype(BF16)
            msq = _dot(hi, seg_ref[...]) + _dot(lo, seg_ref[...])
            dst[sl, :] = (xx * lax.rsqrt(msq + EPS) * w_ref[...]).astype(BF16)
        return carry

    lax.fori_loop(0, t // NORM_ROWS, norm_step, 0, unroll=2)

    nblk = rows // NA_BLOCK
    bq = NA_BLOCK * GRID_W
    first_half = lax.broadcasted_iota(jnp.int32, (bq, LANES), 1) < NA_HEAD_DIM

    def offsets(bi):
        i0 = bi * NA_BLOCK
        rlo = jnp.clip(i0 - NA_ROWS // 2, 0, rows - NA_SPAN)
        return pl.multiple_of(i0 * GRID_W, bq), pl.multiple_of(rlo * GRID_W, 2 * GRID_W)

    def scores(bi, slot):
        kind = jnp.where(bi == 0, 0, jnp.where(bi == nblk - 1, 2, 1))
        qoff, koff = offsets(bi)
        qb = qn[pl.ds(qoff, bq), :]
        zero = jnp.zeros_like(qb)
        qs = jnp.concatenate([jnp.where(first_half, qb, zero), jnp.where(first_half, zero, qb)], axis=0)
        sbuf[slot] = _dot_nt(kn[pl.ds(koff, NA_BKEYS), :], qs) + bias[kind]

    head_a_dims = lax.broadcasted_iota(jnp.int32, (LANES, bq), 0) < NA_HEAD_DIM

    def attend(bi, slot):
        qoff, koff = offsets(bi)
        s = sbuf[slot]
        p = jnp.exp2(s - jnp.max(s, axis=0, keepdims=True))
        denom = jnp.sum(p, axis=0, keepdims=True)
        tile0 = koff // LANES
        v_t = jnp.concatenate([vt_ref[tile0 + i] for i in range(NA_BKEYS // LANES)], axis=1)
        o_t = _dot(v_t, p.astype(BF16)) * (1.0 / denom)
        o_ref[0, pl.ds(qoff, bq), :] = jnp.where(head_a_dims, o_t[:, 0:bq], o_t[:, bq:]).T.astype(o_ref.dtype)

    scores(0, 0)

    def pair_step(i, carry):
        b0 = 2 * i
        scores(b0 + 1, 1)
        attend(b0, 0)
        scores(jnp.minimum(b0 + 2, nblk - 1), 0)
        attend(b0 + 1, 1)
        return carry

    lax.fori_loop(0, nblk // 2, pair_step, 0, unroll=4)


def _natten(q, k, vt, g, colmask, qw, kw, seg):
    bsz, t, _ = q.shape
    nq = 2 * NA_BLOCK * GRID_W
    blk = pl.BlockSpec((1, t, LANES), lambda p, b: (b, 0, p))
    return pl.pallas_call(
        _na_body,
        grid=(NA_PAIRS, bsz),
        in_specs=[
            blk, blk, pl.BlockSpec((t // LANES, LANES, LANES), lambda p, b: (b, p, 0)),
            pl.BlockSpec((1,) + g.shape[1:], lambda p, b: (p, 0, 0)), _const_spec(colmask.shape),
            _const_spec(qw.shape), _const_spec(kw.shape), _const_spec(seg.shape),
        ],
        out_specs=blk,
        out_shape=jax.ShapeDtypeStruct((bsz, t, D_NA), BF16),
        scratch_shapes=[pltpu.VMEM((t, LANES), BF16), pltpu.VMEM((t, LANES), BF16),
                        pltpu.VMEM((2, NA_BKEYS, nq), F32), pltpu.VMEM((3, NA_BKEYS, nq), F32)],
        compiler_params=pltpu.CompilerParams(
            dimension_semantics=("arbitrary", "arbitrary"), vmem_limit_bytes=VMEM_LIMIT),
    )(q, k, vt, g, colmask, qw, kw, seg)


def _block_row_rel():
    rel = -np.ones((3, NA_BLOCK, NA_SPAN), np.int64)
    for u in range(NA_BLOCK):
        for a in range(NA_SPAN):
            if a < NA_ROWS:
                rel[0, u, a] = a - u + NA_ROWS - 1
            if u <= a < u + NA_ROWS:
                rel[1, u, a] = a - u + NA_ROWS - 1 - NA_ROWS // 2
            if a >= NA_SPAN - NA_ROWS:
                rel[2, u, a] = a - u + NA_BLOCK - NA_SPAN + NA_ROWS - 1
    return rel


def _na_bias_rows(rpb):
    centre = NA_COLS - 1
    rev = rpb[:, :, ::-1]
    rows = jnp.concatenate(
        [rev[:, :, centre:], jnp.zeros(rpb.shape[:2] + (LANES - rpb.shape[2],), rpb.dtype), rev[:, :, :centre]],
        axis=2).astype(F32)
    rows = jnp.pad(rows, ((0, 0), (0, 2 * NA_ROWS - rpb.shape[1]), (0, 0)))
    return rows.reshape(NA_PAIRS, 2 * 2 * NA_ROWS, LANES)


def _na_col_mask():
    j = np.arange(GRID_W)
    c0 = np.clip(j - NA_COLS // 2, 0, GRID_W - NA_COLS)
    cc = np.arange(GRID_W)
    valid = (cc[:, None] >= c0[None, :]) & (cc[:, None] < c0[None, :] + NA_COLS)
    return jnp.asarray(np.tile(np.where(valid, 0.0, MASKED), (1, 2)), F32)


FF_STEP = 512


def _outmlp_body(x_ref, ys_ref, yn_ref, wo_ref, nw_ref, w1_ref, w2_ref, o_ref):
    dmix = ys_ref.shape[-1]
    x1 = x_ref[...] + _dot(ys_ref[...], wo_ref[0:dmix, :]) + _dot(yn_ref[...], wo_ref[dmix:, :])
    ms = jnp.mean(x1 * x1, axis=-1, keepdims=True)
    h = (x1 * lax.rsqrt(ms + EPS) * nw_ref[...]).astype(BF16)
    o_ref[...] = x1
    for f in range(0, w1_ref.shape[1], FF_STEP):
        u = jnp.maximum(_dot(h, w1_ref[:, f:f + FF_STEP]), 0.0)
        o_ref[...] += _dot((u * u).astype(BF16), w2_ref[f:f + FF_STEP, :])


def _outmlp(x2, ys, yn, w_out, norm_w, w1, w2, tm):
    n, d = x2.shape
    row = lambda i: (i, 0)
    return pl.pallas_call(
        _outmlp_body,
        grid=(n // tm,),
        in_specs=[
            pl.BlockSpec((tm, d), row), pl.BlockSpec((tm, ys.shape[1]), row), pl.BlockSpec((tm, yn.shape[1]), row),
            _const_spec(w_out.shape), _const_spec((1, d)), _const_spec(w1.shape), _const_spec(w2.shape),
        ],
        out_specs=pl.BlockSpec((tm, d), row),
        out_shape=jax.ShapeDtypeStruct((n, d), F32),
        compiler_params=pltpu.CompilerParams(
            dimension_semantics=("arbitrary",), vmem_limit_bytes=VMEM_LIMIT),
    )(x2, ys, yn, w_out, norm_w, w1, w2)


def _head_select(first_lane):
    sel = np.zeros((LANES, D_SSM), np.float32)
    for rep in range(DT_COPIES):
        for h in range(SSM_HEADS):
            sel[rep * DT_LANES + first_lane + h, h * SSM_HEAD_DIM:(h + 1) * SSM_HEAD_DIM] = 1.0
    return jnp.asarray(sel, BF16)


def _dt_lanes(fwd, bwd):
    v = jnp.tile(jnp.concatenate([fwd, bwd]).astype(F32), DT_COPIES)
    return jnp.pad(v, (0, LANES - v.shape[0]))[None, :]


def _layer(x, norm_mix_w, w_in, conv_w, conv_b, dt_bias_fwd, dt_bias_bwd, a_log_fwd, a_log_bwd, d_skip,
           ssm_norm_w, q_norm_w, k_norm_w, rel_pos_bias, w_out, norm_mlp_w, w_mlp_in, w_mlp_out, tm):
    bsz, t, d = x.shape
    n = bsz * t
    x2 = x.reshape(n, d)

    w_bf = jnp.pad(w_in.astype(BF16), ((0, 0), (0, PROJ_PAD - w_in.shape[1])))
    z, xbc, q, k, vt, dt = _inproj(x2, norm_mix_w[None, :], w_bf, tm)
    r3 = lambda a: a.reshape(bsz, t, a.shape[-1])

    dt_bias = _dt_lanes(dt_bias_fwd, dt_bias_bwd)
    a_log = _dt_lanes(a_log_fwd, a_log_bwd)
    sel_f, sel_b = _head_select(0), _head_select(SSM_HEADS)
    xc, hb = _ssd_bwd(r3(xbc), r3(dt), conv_w, conv_b[None, :], dt_bias, a_log, sel_b)
    y_ssm = _ssd_fwd(xc, r3(dt), r3(z), hb, dt_bias, a_log,
                     jnp.repeat(d_skip.astype(F32), SSM_HEAD_DIM)[None, :], ssm_norm_w[None, :], sel_f, sel_b)

    lane_head = np.arange(LANES) // NA_HEAD_DIM
    seg = jnp.asarray((lane_head[:, None] == lane_head[None, :]) / NA_HEAD_DIM, BF16)
    qw = (jnp.tile(q_norm_w.astype(F32), 2) * (NA_HEAD_DIM ** -0.5 * LOG2E))[None, :]
    kw = jnp.tile(k_norm_w.astype(F32), 2)[None, :]
    y_na = _natten(r3(q), r3(k), vt, _na_bias_rows(rel_pos_bias), _na_col_mask(), qw, kw, seg)

    out = _outmlp(x2, y_ssm.reshape(n, D_SSM), y_na.reshape(n, D_NA), w_out.astype(BF16),
                  norm_mlp_w[None, :], w_mlp_in.astype(BF16), w_mlp_out.astype(BF16), tm)
    return out.reshape(bsz, t, d)


def kernel(x, norm_mix_w, w_in, conv_w, conv_b, dt_bias_fwd, dt_bias_bwd, a_log_fwd, a_log_bwd, d_skip,
           ssm_norm_w, q_norm_w, k_norm_w, rel_pos_bias, w_out, norm_mlp_w, w_mlp_in, w_mlp_out):
    tm = min(512, x.shape[0] * x.shape[1])
    for layer in range(norm_mix_w.shape[0]):
        x = _layer(x, norm_mix_w[layer], w_in[layer], conv_w[layer], conv_b[layer], dt_bias_fwd[layer],
                   dt_bias_bwd[layer], a_log_fwd[layer], a_log_bwd[layer], d_skip[layer], ssm_norm_w[layer],
                   q_norm_w[layer], k_norm_w[layer], rel_pos_bias[layer], w_out[layer], norm_mlp_w[layer],
                   w_mlp_in[layer], w_mlp_out[layer], tm)
    return x
```

```python
import functools

import numpy as np
import jax
import jax.numpy as jnp
from jax import lax
from jax.experimental import pallas as pl
from jax.experimental.pallas import tpu as pltpu

F32 = jnp.float32
BF16 = jnp.bfloat16

LANES = 128
GRID_W = 64
SSM_HEADS = 16
SSM_HEAD_DIM = 64
D_SSM = SSM_HEADS * SSM_HEAD_DIM
SSM_GROUPS = 2
GROUP_W = D_SSM // SSM_GROUPS
D_STATE = 128
D_CONV = 5
CHUNK = 128
FWD_CHUNKS = 4
BWD_CHUNKS = 4
CONV_DIM = D_SSM + 2 * SSM_GROUPS * D_STATE
DT_LANES = 2 * SSM_HEADS
DT_COPIES = 3
NA_HEADS = 16
NA_HEAD_DIM = 64
D_NA = NA_HEADS * NA_HEAD_DIM
NA_ROWS = 8
NA_COLS = 16
NA_PAIRS = NA_HEADS // 2
NA_BLOCK = 4
NA_SPAN = NA_ROWS + NA_BLOCK
NA_BKEYS = NA_SPAN * GRID_W
LOG2E = 1.4426950408889634
EPS = 1e-5
MASKED = -1e30
HALO = 8
CONV_STRIDE = CHUNK // 8 + 1
VMEM_LIMIT = 56 * 1024 * 1024


def _dot(a, b):
    return jnp.dot(a, b, preferred_element_type=F32)


def _dot_nt(a, b):
    return lax.dot_general(a, b, (((1,), (1,)), ((), ())), preferred_element_type=F32)


def _split3(x):
    hi = x.astype(BF16)
    r1 = x - hi.astype(F32)
    mid = r1.astype(BF16)
    lo = (r1 - mid.astype(F32)).astype(BF16)
    return hi, mid, lo


def _dot_sel_rhs(x, sel):
    hi, mid, lo = _split3(x)
    return _dot(hi, sel) + _dot(mid, sel) + _dot(lo, sel)


def _dot_sel_lhs(sel, x):
    hi, mid, lo = _split3(x)
    return _dot(sel, hi) + _dot(sel, mid) + _dot(sel, lo)


def _pack3(x):
    lane = lax.broadcasted_iota(jnp.int32, x.shape, x.ndim - 1)
    hi = x.astype(BF16).astype(F32)
    r1 = x - hi
    mid = r1.astype(BF16).astype(F32)
    return jnp.where(lane < DT_LANES, hi, jnp.where(lane < 2 * DT_LANES, mid, r1 - mid)).astype(BF16)


def _softplus(x):
    return jnp.maximum(x, 0.0) + jnp.log(1.0 + jnp.exp(-jnp.abs(x)))


def _silu(x):
    return x * (1.0 / (1.0 + jnp.exp(-x)))


def _const_spec(shape):
    nd = len(shape)
    return pl.BlockSpec(shape, lambda *_: (0,) * nd, pipeline_mode=pl.Buffered(1))


O_DT = D_SSM + CONV_DIM
O_Q = O_DT + DT_LANES
O_K = O_Q + D_NA
O_V = O_K + D_NA


def _inproj_body(x_ref, nw_ref, wt_ref, z_ref, xbc_ref, q_ref, k_ref, vt_ref, dt_ref):
    x = x_ref[...]
    ms = jnp.mean(x * x, axis=-1, keepdims=True)
    h = (x * lax.rsqrt(ms + EPS) * nw_ref[...]).astype(BF16)
    z_ref[...] = _dot_nt(h, wt_ref[0:D_SSM, :])
    xbc_ref[...] = _dot_nt(h, wt_ref[D_SSM:O_DT, :])
    w_dt = wt_ref[O_DT:O_Q, :]
    pad = jnp.zeros((LANES - DT_COPIES * DT_LANES, w_dt.shape[1]), w_dt.dtype)
    dt_ref[...] = _dot_nt(h, jnp.concatenate([w_dt] * DT_COPIES + [pad], axis=0))
    q_ref[...] = _dot_nt(h, wt_ref[O_Q:O_K, :])
    k_ref[...] = _dot_nt(h, wt_ref[O_K:O_V, :])
    vt = _dot_nt(wt_ref[O_V:, :], h)
    for j in range(vt_ref.shape[0]):
        vt_ref[j] = vt[:, j * LANES:(j + 1) * LANES].astype(vt_ref.dtype)


def _inproj(x2, norm_w, w_bf, tm):
    n, d = x2.shape
    widths = (D_SSM, CONV_DIM, D_NA, D_NA)
    row = lambda i: (i, 0)
    tiles = tm // LANES
    return pl.pallas_call(
        _inproj_body,
        grid=(n // tm,),
        in_specs=[pl.BlockSpec((tm, d), row), _const_spec((1, d)), _const_spec(w_bf.shape)],
        out_specs=[pl.BlockSpec((tm, w), row) for w in widths]
        + [pl.BlockSpec((tiles, D_NA, LANES), lambda i: (i, 0, 0)), pl.BlockSpec((tm, LANES), row)],
        out_shape=[jax.ShapeDtypeStruct((n, w), F32) for w in widths]
        + [jax.ShapeDtypeStruct((n // LANES, D_NA, LANES), BF16), jax.ShapeDtypeStruct((n, LANES), F32)],
        compiler_params=pltpu.CompilerParams(
            dimension_semantics=("arbitrary",), vmem_limit_bytes=VMEM_LIMIT),
    )(x2, norm_w, w_bf)


def _tri_masks():
    t = lax.broadcasted_iota(jnp.int32, (CHUNK, CHUNK), 0)
    s = lax.broadcasted_iota(jnp.int32, (CHUNK, CHUNK), 1)
    return s <= t, s >= t


def _dt_and_loga(dt_ref, dtb_ref, alog_ref):
    dtv = _softplus(dt_ref[0] + dtb_ref[...])
    return dtv, dtv * (-LOG2E * jnp.exp(alog_ref[...]))


def _ssd_bwd_body(cur_ref, prev_ref, next_ref, dt_ref, cw_ref, cb_ref, dtb_ref, alog_ref, eb_ref,
                  xc_ref, hb_ref, uext, cout, state):
    c = pl.program_id(1)
    nsteps = pl.num_programs(1)
    step = nsteps - 1 - c
    rows_in = BWD_CHUNKS * CHUNK

    @pl.when(c == 0)
    def _():
        state[...] = jnp.zeros_like(state)

    first = HALO - (D_CONV - 1) // 2
    sub = 8
    nslab = CONV_DIM // LANES
    for slab in range(nslab):
        cols = slice(slab * LANES, (slab + 1) * LANES)
        uext[slab, 0:HALO, :] = jnp.where(step > 0, prev_ref[0, :, cols], 0.0)
        uext[slab, HALO:HALO + rows_in, :] = cur_ref[0, :, cols]
        uext[slab, HALO + rows_in:2 * HALO + rows_in, :] = jnp.where(step < nsteps - 1, next_ref[0, :, cols], 0.0)
        uext[slab, 2 * HALO + rows_in:, :] = jnp.zeros((uext.shape[1] - 2 * HALO - rows_in, LANES), F32)
        bias = jnp.broadcast_to(cb_ref[:, cols], (sub, LANES))
        taps = [jnp.broadcast_to(cw_ref[k:k + 1, cols], (sub, LANES)) for k in range(D_CONV)]
        for ch in range(BWD_CHUNKS):
            for i in range(CONV_STRIDE):
                acc = bias
                for k in range(D_CONV):
                    start = ch * CHUNK + first + k + i
                    acc = acc + taps[k] * uext[slab, pl.ds(start, sub, stride=CONV_STRIDE), :]
                cout[ch * nslab + slab, pl.ds(i, sub, stride=CONV_STRIDE), :] = _silu(acc)
            xc_ref[0, ch * CHUNK:(ch + 1) * CHUNK, cols] = cout[ch * nslab + slab, 0:CHUNK, :]

    _, upper = _tri_masks()
    upper_sel = jnp.where(upper, 1.0, 0.0).astype(BF16)
    for ch in reversed(range(BWD_CHUNKS)):
        rows = slice(ch * CHUNK, (ch + 1) * CHUNK)
        dtv, loga = _dt_and_loga(dt_ref.at[:, rows], dtb_ref, alog_ref)
        rcum = _dot_sel_lhs(upper_sel, loga)
        r0 = rcum[0:1, :]
        wgt = dtv * jnp.exp2(r0 - rcum)
        wgt_x = _dot(_pack3(wgt), eb_ref[...])
        dec_x = _dot(_pack3(jnp.broadcast_to(jnp.exp2(r0), (HALO, LANES))), eb_ref[...])[0:1]

        hb_ref[0, ch] = state[...].astype(BF16)
        xw = (xc_ref[0, rows, 0:D_SSM] * wgt_x).astype(BF16)
        for g in range(SSM_GROUPS):
            gs = slice(g * GROUP_W, (g + 1) * GROUP_W)
            b_t = xc_ref[0, rows, D_SSM + g * D_STATE:D_SSM + (g + 1) * D_STATE].T.astype(BF16)
            state[:, gs] = state[:, gs] * dec_x[:, gs] + _dot(b_t, xw[:, gs])


def _ssd_bwd(xbc, dt, conv_w, conv_b, dt_bias, a_log, sel_b):
    bsz, t, _ = xbc.shape
    rows = BWD_CHUNKS * CHUNK
    nsteps = t // rows
    hps = rows // HALO
    nhalo = t // HALO
    nslab = CONV_DIM // LANES
    rev = lambda c: nsteps - 1 - c
    return pl.pallas_call(
        _ssd_bwd_body,
        grid=(bsz, nsteps),
        in_specs=[
            pl.BlockSpec((1, rows, CONV_DIM), lambda b, c: (b, rev(c), 0)),
            pl.BlockSpec((1, HALO, CONV_DIM), lambda b, c: (b, jnp.maximum(rev(c) * hps - 1, 0), 0)),
            pl.BlockSpec((1, HALO, CONV_DIM), lambda b, c: (b, jnp.minimum((rev(c) + 1) * hps, nhalo - 1), 0)),
            pl.BlockSpec((1, rows, LANES), lambda b, c: (b, rev(c), 0)),
            _const_spec(conv_w.shape), _const_spec(conv_b.shape),
            _const_spec(dt_bias.shape), _const_spec(a_log.shape), _const_spec(sel_b.shape),
        ],
        out_specs=[
            pl.BlockSpec((1, rows, CONV_DIM), lambda b, c: (b, rev(c), 0)),
            pl.BlockSpec((1, BWD_CHUNKS, D_STATE, D_SSM), lambda b, c: (b, rev(c), 0, 0)),
        ],
        out_shape=[
            jax.ShapeDtypeStruct((bsz, t, CONV_DIM), F32),
            jax.ShapeDtypeStruct((bsz, t // CHUNK, D_STATE, D_SSM), BF16),
        ],
        scratch_shapes=[pltpu.VMEM((nslab, rows + 3 * HALO, LANES), F32),
                        pltpu.VMEM((BWD_CHUNKS * nslab, 8 * CONV_STRIDE, LANES), F32),
                        pltpu.VMEM((D_STATE, D_SSM), F32)],
        compiler_params=pltpu.CompilerParams(
            dimension_semantics=("arbitrary", "arbitrary"), vmem_limit_bytes=VMEM_LIMIT),
    )(xbc, xbc, xbc, dt, conv_w, conv_b, dt_bias, a_log, sel_b)


def _ssd_fwd_body(xc_ref, dt_ref, z_ref, hb_ref, dtb_ref, alog_ref, dsk_ref, nw_ref, ef_ref, eb_ref,
                  y_ref, state, ybuf):
    @pl.when(pl.program_id(1) == 0)
    def _():
        state[...] = jnp.zeros_like(state)

    for sub in range(FWD_CHUNKS):
        rows = slice(sub * CHUNK, (sub + 1) * CHUNK)
        _ssd_fwd_chunk(xc_ref.at[0, rows], dt_ref.at[:, rows], z_ref.at[0, rows], hb_ref.at[0, sub], dtb_ref, alog_ref,
                       dsk_ref, nw_ref, ef_ref, eb_ref, y_ref.at[0, rows], state, ybuf.at[rows])


def _ssd_fwd_chunk(xc_ref, dt_ref, z_ref, hb_ref, dtb_ref, alog_ref, dsk_ref, nw_ref, ef_ref, eb_ref,
                   y_ref, state, ybuf):
    dtv, loga = _dt_and_loga(dt_ref, dtb_ref, alog_ref)
    lower, upper = _tri_masks()
    cum_f = _dot_sel_lhs(jnp.where(lower, 1.0, 0.0).astype(BF16), loga)
    cum_b = _dot_sel_lhs(jnp.where(upper, 1.0, 0.0).astype(BF16), loga)
    lane = lax.broadcasted_iota(jnp.int32, (CHUNK, LANES), 1)
    cum = jnp.where((lane & SSM_HEADS) == 0, cum_f, cum_b)
    ecum_p = _pack3(jnp.exp2(cum))
    wgt_p = _pack3(dtv * jnp.exp2(cum_f[CHUNK - 1:CHUNK, :] - cum_f))
    src_t = (cum - jnp.log2(dtv)).T
    first_half = lane < SSM_HEAD_DIM

    for g in range(SSM_GROUPS):
        gs = slice(g * GROUP_W, (g + 1) * GROUP_W)
        b_f32 = xc_ref[:, D_SSM + g * D_STATE:D_SSM + (g + 1) * D_STATE]
        b_g = b_f32.astype(BF16)
        c0 = D_SSM + SSM_GROUPS * D_STATE + g * D_STATE
        c_g = xc_ref[:, c0:c0 + D_STATE].astype(BF16)
        cb = _dot_nt(c_g, b_g)
        ea_g = _dot(ecum_p, ef_ref[:, gs])
        er_g = _dot(ecum_p, eb_ref[:, gs])
        y_off = _dot(c_g, state[:, gs].astype(BF16)) * ea_g + _dot(c_g, hb_ref[:, gs]) * er_g
        xw = (xc_ref[:, gs] * _dot(wgt_p, ef_ref[:, gs])).astype(BF16)
        state[:, gs] = state[:, gs] * ea_g[CHUNK - 1:CHUNK, :] + _dot(b_f32.T.astype(BF16), xw)
        pairs = GROUP_W // LANES
        for pp in range(pairs):
            pr = g * pairs + pp
            ps = slice(pr * LANES, (pr + 1) * LANES)
            mats = []
            for h in (2 * pr, 2 * pr + 1):
                hb = SSM_HEADS + h
                seg_f = jnp.broadcast_to(cum[:, h:h + 1], (CHUNK, CHUNK)) - src_t[h:h + 1, :]
                seg_b = jnp.broadcast_to(cum[:, hb:hb + 1], (CHUNK, CHUNK)) - src_t[hb:hb + 1, :]
                d_f = jnp.exp2(jnp.where(lower, seg_f, MASKED))
                d_b = jnp.exp2(jnp.where(upper, seg_b, MASKED))
                mats.append(((d_f + d_b) * cb).astype(BF16))
            xs = xc_ref[:, ps]
            xp = xs.astype(BF16)
            zero = jnp.zeros_like(xp)
            rhs = jnp.concatenate([jnp.where(first_half, xp, zero), jnp.where(first_half, zero, xp)], axis=0)
            y_diag = _dot(jnp.concatenate(mats, axis=1), rhs)
            ybuf[:, ps] = y_diag + y_off[:, pp * LANES:(pp + 1) * LANES] + dsk_ref[:, ps] * xs
        gated = ybuf[:, gs] * _silu(z_ref[:, gs])
        ms = jnp.mean(gated * gated, axis=-1, keepdims=True)
        y_ref[:, gs] = (gated * lax.rsqrt(ms + EPS) * nw_ref[:, gs]).astype(y_ref.dtype)


def _ssd_fwd(xc, dt, z, hb, dt_bias, a_log, d_skip_x, norm_w, sel_f, sel_b):
    bsz, t, _ = xc.shape
    step = FWD_CHUNKS * CHUNK
    blk = lambda w: pl.BlockSpec((1, step, w), lambda b, c: (b, c, 0))
    return pl.pallas_call(
        _ssd_fwd_body,
        grid=(bsz, t // step),
        in_specs=[
            blk(CONV_DIM), blk(LANES), blk(D_SSM),
            pl.BlockSpec((1, FWD_CHUNKS, D_STATE, D_SSM), lambda b, c: (b, c, 0, 0)),
            _const_spec(dt_bias.shape), _const_spec(a_log.shape), _const_spec(d_skip_x.shape),
            _const_spec(norm_w.shape), _const_spec(sel_f.shape), _const_spec(sel_b.shape),
        ],
        out_specs=blk(D_SSM),
        out_shape=jax.ShapeDtypeStruct((bsz, t, D_SSM), BF16),
        scratch_shapes=[pltpu.VMEM((D_STATE, D_SSM), F32), pltpu.VMEM((step, D_SSM), F32)],
        compiler_params=pltpu.CompilerParams(
            dimension_semantics=("arbitrary", "arbitrary"), vmem_limit_bytes=VMEM_LIMIT),
    )(xc, dt, z, hb, dt_bias, a_log, d_skip_x, norm_w, sel_f, sel_b)


NORM_ROWS = 512


def _build_bias(g_ref, colmask_ref, bias):
    rel = _block_row_rel()
    half = lax.broadcasted_iota(jnp.int32, (GRID_W, LANES), 1) < GRID_W
    nrel = 2 * NA_ROWS

    @functools.cache
    def toeplitz(hh, r, upper_half):
        if upper_half:
            return pltpu.roll(toeplitz(hh, r, False), GRID_W, 1)
        row = g_ref[0, hh * nrel + r:hh * nrel + r + 1, :] * LOG2E
        return pltpu.roll(jnp.broadcast_to(row, (GRID_W, LANES)), 0, 1, stride=1, stride_axis=0)

    for kind in range(rel.shape[0]):
        for hh in range(2):
            for u in range(0, NA_BLOCK, 2):
                lane0 = (hh * NA_BLOCK + u) * GRID_W
                for a in range(NA_SPAN):
                    r_lo, r_hi = int(rel[kind, u, a]), int(rel[kind, u + 1, a])
                    lo = toeplitz(hh, r_lo, False) if r_lo >= 0 else MASKED
                    hi = toeplitz(hh, r_hi, True) if r_hi >= 0 else MASKED
                    tile = jnp.where(half, lo, hi) + colmask_ref[...]
                    bias[kind, a * GRID_W:(a + 1) * GRID_W, lane0:lane0 + 2 * GRID_W] = tile


def _na_body(q_ref, k_ref, vt_ref, g_ref, colmask_ref, qw_ref, kw_ref, seg_ref, o_ref, qn, kn, sbuf, bias):
    t = q_ref.shape[1]
    rows = t // GRID_W

    @pl.when(pl.program_id(1) == 0)
    def _():
        _build_bias(g_ref, colmask_ref, bias)

    def norm_step(i, carry):
        sl = pl.ds(pl.multiple_of(i * NORM_ROWS, NORM_ROWS), NORM_ROWS)
        for src, w_ref, dst in ((q_ref, qw_ref, qn), (k_ref, kw_ref, kn)):
            xx = src[0, sl, :]
            sq = xx * xx
            hi = sq.astype(BF16)
            lo = (sq - hi.astype(F32)).astype(BF16)
            msq = _dot(hi, seg_ref[...]) + _dot(lo, seg_ref[...])
            dst[sl, :] = (xx * lax.rsqrt(msq + EPS) * w_ref[...]).astype(BF16)
        return carry

    lax.fori_loop(0, t // NORM_ROWS, norm_step, 0, unroll=2)

    nblk = rows // NA_BLOCK
    bq = NA_BLOCK * GRID_W
    first_half = lax.broadcasted_iota(jnp.int32, (bq, LANES), 1) < NA_HEAD_DIM

    def offsets(bi):
        i0 = bi * NA_BLOCK
        rlo = jnp.clip(i0 - NA_ROWS // 2, 0, rows - NA_SPAN)
        return pl.multiple_of(i0 * GRID_W, bq), pl.multiple_of(rlo * GRID_W, 2 * GRID_W)

    def scores(bi, slot):
        kind = jnp.where(bi == 0, 0, jnp.where(bi == nblk - 1, 2, 1))
        qoff, koff = offsets(bi)
        qb = qn[pl.ds(qoff, bq), :]
        zero = jnp.zeros_like(qb)
        qs = jnp.concatenate([jnp.where(first_half, qb, zero), jnp.where(first_half, zero, qb)], axis=0)
        sbuf[slot] = _dot_nt(kn[pl.ds(koff, NA_BKEYS), :], qs) + bias[kind]

    head_a_dims = lax.broadcasted_iota(jnp.int32, (LANES, bq), 0) < NA_HEAD_DIM

    def attend(bi, slot):
        qoff, koff = offsets(bi)
        s = sbuf[slot]
        p = jnp.exp2(s - jnp.max(s, axis=0, keepdims=True))
        denom = jnp.sum(p, axis=0, keepdims=True)
        tile0 = koff // LANES
        v_t = jnp.concatenate([vt_ref[tile0 + i] for i in range(NA_BKEYS // LANES)], axis=1)
        o_t = _dot(v_t, p.astype(BF16)) * (1.0 / denom)
        o_ref[0, pl.ds(qoff, bq), :] = jnp.where(head_a_dims, o_t[:, 0:bq], o_t[:, bq:]).T.astype(o_ref.dtype)

    scores(0, 0)

    def pair_step(i, carry):
        b0 = 2 * i
        scores(b0 + 1, 1)
        attend(b0, 0)
        scores(jnp.minimum(b0 + 2, nblk - 1), 0)
        attend(b0 + 1, 1)
        return carry

    lax.fori_loop(0, nblk // 2, pair_step, 0, unroll=4)


def _natten(q, k, vt, g, colmask, qw, kw, seg):
    bsz, t, _ = q.shape
    nq = 2 * NA_BLOCK * GRID_W
    blk = pl.BlockSpec((1, t, LANES), lambda p, b: (b, 0, p))
    return pl.pallas_call(
        _na_body,
        grid=(NA_PAIRS, bsz),
        in_specs=[
            blk, blk, pl.BlockSpec((t // LANES, LANES, LANES), lambda p, b: (b, p, 0)),
            pl.BlockSpec((1,) + g.shape[1:], lambda p, b: (p, 0, 0)), _const_spec(colmask.shape),
            _const_spec(qw.shape), _const_spec(kw.shape), _const_spec(seg.shape),
        ],
        out_specs=blk,
        out_shape=jax.ShapeDtypeStruct((bsz, t, D_NA), BF16),
        scratch_shapes=[pltpu.VMEM((t, LANES), BF16), pltpu.VMEM((t, LANES), BF16),
                        pltpu.VMEM((2, NA_BKEYS, nq), F32), pltpu.VMEM((3, NA_BKEYS, nq), F32)],
        compiler_params=pltpu.CompilerParams(
            dimension_semantics=("arbitrary", "arbitrary"), vmem_limit_bytes=VMEM_LIMIT),
    )(q, k, vt, g, colmask, qw, kw, seg)


def _block_row_rel():
    rel = -np.ones((3, NA_BLOCK, NA_SPAN), np.int64)
    for u in range(NA_BLOCK):
        for a in range(NA_SPAN):
            if a < NA_ROWS:
                rel[0, u, a] = a - u + NA_ROWS - 1
            if u <= a < u + NA_ROWS:
                rel[1, u, a] = a - u + NA_ROWS - 1 - NA_ROWS // 2
            if a >= NA_SPAN - NA_ROWS:
                rel[2, u, a] = a - u + NA_BLOCK - NA_SPAN + NA_ROWS - 1
    return rel


def _na_bias_rows(rpb):
    centre = NA_COLS - 1
    rev = rpb[:, :, ::-1]
    rows = jnp.concatenate(
        [rev[:, :, centre:], jnp.zeros(rpb.shape[:2] + (LANES - rpb.shape[2],), rpb.dtype), rev[:, :, :centre]],
        axis=2).astype(F32)
    rows = jnp.pad(rows, ((0, 0), (0, 2 * NA_ROWS - rpb.shape[1]), (0, 0)))
    return rows.reshape(NA_PAIRS, 2 * 2 * NA_ROWS, LANES)


def _na_col_mask():
    j = np.arange(GRID_W)
    c0 = np.clip(j - NA_COLS // 2, 0, GRID_W - NA_COLS)
    cc = np.arange(GRID_W)
    valid = (cc[:, None] >= c0[None, :]) & (cc[:, None] < c0[None, :] + NA_COLS)
    return jnp.asarray(np.tile(np.where(valid, 0.0, MASKED), (1, 2)), F32)


FF_STEP = 512


def _outmlp_body(x_ref, ys_ref, yn_ref, wo_ref, nw_ref, w1_ref, w2_ref, o_ref):
    dmix = ys_ref.shape[-1]
    x1 = x_ref[...] + _dot(ys_ref[...], wo_ref[0:dmix, :]) + _dot(yn_ref[...], wo_ref[dmix:, :])
    ms = jnp.mean(x1 * x1, axis=-1, keepdims=True)
    h = (x1 * lax.rsqrt(ms + EPS) * nw_ref[...]).astype(BF16)
    o_ref[...] = x1
    for f in range(0, w1_ref.shape[1], FF_STEP):
        u = jnp.maximum(_dot(h, w1_ref[:, f:f + FF_STEP]), 0.0)
        o_ref[...] += _dot((u * u).astype(BF16), w2_ref[f:f + FF_STEP, :])


def _outmlp(x2, ys, yn, w_out, norm_w, w1, w2, tm):
    n, d = x2.shape
    row = lambda i: (i, 0)
    return pl.pallas_call(
        _outmlp_body,
        grid=(n // tm,),
        in_specs=[
            pl.BlockSpec((tm, d), row), pl.BlockSpec((tm, ys.shape[1]), row), pl.BlockSpec((tm, yn.shape[1]), row),
            _const_spec(w_out.shape), _const_spec((1, d)), _const_spec(w1.shape), _const_spec(w2.shape),
        ],
        out_specs=pl.BlockSpec((tm, d), row),
        out_shape=jax.ShapeDtypeStruct((n, d), F32),
        compiler_params=pltpu.CompilerParams(
            dimension_semantics=("arbitrary",), vmem_limit_bytes=VMEM_LIMIT),
    )(x2, ys, yn, w_out, norm_w, w1, w2)


def _head_select(first_lane):
    sel = np.zeros((LANES, D_SSM), np.float32)
    for rep in range(DT_COPIES):
        for h in range(SSM_HEADS):
            sel[rep * DT_LANES + first_lane + h, h * SSM_HEAD_DIM:(h + 1) * SSM_HEAD_DIM] = 1.0
    return jnp.asarray(sel, BF16)


def _dt_lanes(fwd, bwd):
    v = jnp.tile(jnp.concatenate([fwd, bwd]).astype(F32), DT_COPIES)
    return jnp.pad(v, (0, LANES - v.shape[0]))[None, :]


def _layer(x, norm_mix_w, w_in, conv_w, conv_b, dt_bias_fwd, dt_bias_bwd, a_log_fwd, a_log_bwd, d_skip,
           ssm_norm_w, q_norm_w, k_norm_w, rel_pos_bias, w_out, norm_mlp_w, w_mlp_in, w_mlp_out, tm):
    bsz, t, d = x.shape
    n = bsz * t
    x2 = x.reshape(n, d)

    z, xbc, q, k, vt, dt = _inproj(x2, norm_mix_w[None, :], w_in.T.astype(BF16), tm)
    r3 = lambda a: a.reshape(bsz, t, a.shape[-1])

    dt_bias = _dt_lanes(dt_bias_fwd, dt_bias_bwd)
    a_log = _dt_lanes(a_log_fwd, a_log_bwd)
    sel_f, sel_b = _head_select(0), _head_select(SSM_HEADS)
    xc, hb = _ssd_bwd(r3(xbc), r3(dt), conv_w, conv_b[None, :], dt_bias, a_log, sel_b)
    y_ssm = _ssd_fwd(xc, r3(dt), r3(z), hb, dt_bias, a_log,
                     jnp.repeat(d_skip.astype(F32), SSM_HEAD_DIM)[None, :], ssm_norm_w[None, :], sel_f, sel_b)

    lane_head = np.arange(LANES) // NA_HEAD_DIM
    seg = jnp.asarray((lane_head[:, None] == lane_head[None, :]) / NA_HEAD_DIM, BF16)
    qw = (jnp.tile(q_norm_w.astype(F32), 2) * (NA_HEAD_DIM ** -0.5 * LOG2E))[None, :]
    kw = jnp.tile(k_norm_w.astype(F32), 2)[None, :]
    y_na = _natten(r3(q), r3(k), vt, _na_bias_rows(rel_pos_bias), _na_col_mask(), qw, kw, seg)

    out = _outmlp(x2, y_ssm.reshape(n, D_SSM), y_na.reshape(n, D_NA), w_out.astype(BF16),
                  norm_mlp_w[None, :], w_mlp_in.astype(BF16), w_mlp_out.astype(BF16), tm)
    return out.reshape(bsz, t, d)


def kernel(x, norm_mix_w, w_in, conv_w, conv_b, dt_bias_fwd, dt_bias_bwd, a_log_fwd, a_log_bwd, d_skip,
           ssm_norm_w, q_norm_w, k_norm_w, rel_pos_bias, w_out, norm_mlp_w, w_mlp_in, w_mlp_out):
    tm = min(512, x.shape[0] * x.shape[1])
    for layer in range(norm_mix_w.shape[0]):
        x = _layer(x, norm_mix_w[layer], w_in[layer], conv_w[layer], conv_b[layer], dt_bias_fwd[layer],
                   dt_bias_bwd[layer], a_log_fwd[layer], a_log_bwd[layer], d_skip[layer], ssm_norm_w[layer],
                   q_norm_w[layer], k_norm_w[layer], rel_pos_bias[layer], w_out[layer], norm_mlp_w[layer],
                   w_mlp_in[layer], w_mlp_out[layer], tm)
    return x
```

```python
import functools

import numpy as np
import jax
import jax.numpy as jnp
from jax import lax
from jax.experimental import pallas as pl
from jax.experimental.pallas import tpu as pltpu

F32 = jnp.float32
BF16 = jnp.bfloat16

LANES = 128
GRID_W = 64
SSM_HEADS = 16
SSM_HEAD_DIM = 64
D_SSM = SSM_HEADS * SSM_HEAD_DIM
SSM_GROUPS = 2
GROUP_W = D_SSM // SSM_GROUPS
D_STATE = 128
D_CONV = 5
CHUNK = 128
FWD_CHUNKS = 4
BWD_CHUNKS = 4
CONV_DIM = D_SSM + 2 * SSM_GROUPS * D_STATE
DT_LANES = 2 * SSM_HEADS
DT_COPIES = 3
NA_HEADS = 16
NA_HEAD_DIM = 64
D_NA = NA_HEADS * NA_HEAD_DIM
NA_ROWS = 8
NA_COLS = 16
NA_PAIRS = NA_HEADS // 2
NA_BLOCK = 4
NA_SPAN = NA_ROWS + NA_BLOCK
NA_BKEYS = NA_SPAN * GRID_W
LOG2E = 1.4426950408889634
EPS = 1e-5
MASKED = -1e30
HALO = 8
CONV_STRIDE = CHUNK // 8 + 1
VMEM_LIMIT = 56 * 1024 * 1024


def _dot(a, b):
    return jnp.dot(a, b, preferred_element_type=F32)


def _dot_nt(a, b):
    return lax.dot_general(a, b, (((1,), (1,)), ((), ())), preferred_element_type=F32)


def _split3(x):
    hi = x.astype(BF16)
    r1 = x - hi.astype(F32)
    mid = r1.astype(BF16)
    lo = (r1 - mid.astype(F32)).astype(BF16)
    return hi, mid, lo


def _dot_sel_rhs(x, sel):
    hi, mid, lo = _split3(x)
    return _dot(hi, sel) + _dot(mid, sel) + _dot(lo, sel)


def _dot_sel_lhs(sel, x):
    hi, mid, lo = _split3(x)
    return _dot(sel, hi) + _dot(sel, mid) + _dot(sel, lo)


def _pack3(x):
    lane = lax.broadcasted_iota(jnp.int32, x.shape, x.ndim - 1)
    hi = x.astype(BF16).astype(F32)
    r1 = x - hi
    mid = r1.astype(BF16).astype(F32)
    return jnp.where(lane < DT_LANES, hi, jnp.where(lane < 2 * DT_LANES, mid, r1 - mid)).astype(BF16)


def _softplus(x):
    return jnp.maximum(x, 0.0) + jnp.log(1.0 + jnp.exp(-jnp.abs(x)))


def _silu(x):
    return x * (1.0 / (1.0 + jnp.exp(-x)))


def _const_spec(shape):
    nd = len(shape)
    return pl.BlockSpec(shape, lambda *_: (0,) * nd, pipeline_mode=pl.Buffered(1))


O_DT = D_SSM + CONV_DIM
O_Q = O_DT + DT_LANES
O_K = O_Q + D_NA
O_V = O_K + D_NA


def _inproj_body(x_ref, nw_ref, wt_ref, *refs):
    ncast = (len(refs) - 6) // 2
    cast_in, (z_ref, xbc_ref, q_ref, k_ref, vt_ref, dt_ref), cast_out = refs[:ncast], refs[ncast:ncast + 6], refs[ncast + 6:]
    for src, dst in zip(cast_in, cast_out):
        dst[...] = src[...].astype(dst.dtype)
    x = x_ref[...]
    ms = jnp.mean(x * x, axis=-1, keepdims=True)
    h = (x * lax.rsqrt(ms + EPS) * nw_ref[...]).astype(BF16)
    z_ref[...] = _dot_nt(h, wt_ref[0:D_SSM, :])
    xbc_ref[...] = _dot_nt(h, wt_ref[D_SSM:O_DT, :])
    w_dt = wt_ref[O_DT:O_Q, :]
    pad = jnp.zeros((LANES - DT_COPIES * DT_LANES, w_dt.shape[1]), w_dt.dtype)
    dt_ref[...] = _dot_nt(h, jnp.concatenate([w_dt] * DT_COPIES + [pad], axis=0))
    q_ref[...] = _dot_nt(h, wt_ref[O_Q:O_K, :])
    k_ref[...] = _dot_nt(h, wt_ref[O_K:O_V, :])
    vt = _dot_nt(wt_ref[O_V:, :], h)
    for j in range(vt_ref.shape[0]):
        vt_ref[j] = vt[:, j * LANES:(j + 1) * LANES].astype(vt_ref.dtype)


def _inproj(x2, norm_w, w_bf, later_weights, tm):
    n, d = x2.shape
    steps = n // tm
    widths = (D_SSM, CONV_DIM, D_NA, D_NA)
    row = lambda i: (i, 0)
    tiles = tm // LANES
    slabs = [pl.BlockSpec((w.shape[0] // steps, w.shape[1]), row) for w in later_weights]
    outs = pl.pallas_call(
        _inproj_body,
        grid=(steps,),
        in_specs=[pl.BlockSpec((tm, d), row), _const_spec((1, d)), _const_spec(w_bf.shape)] + slabs,
        out_specs=[pl.BlockSpec((tm, w), row) for w in widths]
        + [pl.BlockSpec((tiles, D_NA, LANES), lambda i: (i, 0, 0)), pl.BlockSpec((tm, LANES), row)] + slabs,
        out_shape=[jax.ShapeDtypeStruct((n, w), F32) for w in widths]
        + [jax.ShapeDtypeStruct((n // LANES, D_NA, LANES), BF16), jax.ShapeDtypeStruct((n, LANES), F32)]
        + [jax.ShapeDtypeStruct(w.shape, BF16) for w in later_weights],
        compiler_params=pltpu.CompilerParams(
            dimension_semantics=("arbitrary",), vmem_limit_bytes=VMEM_LIMIT),
    )(x2, norm_w, w_bf, *later_weights)
    return outs[:6], outs[6:]


def _tri_masks():
    t = lax.broadcasted_iota(jnp.int32, (CHUNK, CHUNK), 0)
    s = lax.broadcasted_iota(jnp.int32, (CHUNK, CHUNK), 1)
    return s <= t, s >= t


def _dt_and_loga(dt_ref, dtb_ref, alog_ref):
    dtv = _softplus(dt_ref[0] + dtb_ref[...])
    return dtv, dtv * (-LOG2E * jnp.exp(alog_ref[...]))


def _ssd_bwd_body(cur_ref, prev_ref, next_ref, dt_ref, cw_ref, cb_ref, dtb_ref, alog_ref, eb_ref,
                  xc_ref, hb_ref, uext, cout, state):
    c = pl.program_id(1)
    nsteps = pl.num_programs(1)
    step = nsteps - 1 - c
    rows_in = BWD_CHUNKS * CHUNK

    @pl.when(c == 0)
    def _():
        state[...] = jnp.zeros_like(state)

    first = HALO - (D_CONV - 1) // 2
    sub = 8
    nslab = CONV_DIM // LANES
    for slab in range(nslab):
        cols = slice(slab * LANES, (slab + 1) * LANES)
        uext[slab, 0:HALO, :] = jnp.where(step > 0, prev_ref[0, :, cols], 0.0)
        uext[slab, HALO:HALO + rows_in, :] = cur_ref[0, :, cols]
        uext[slab, HALO + rows_in:2 * HALO + rows_in, :] = jnp.where(step < nsteps - 1, next_ref[0, :, cols], 0.0)
        uext[slab, 2 * HALO + rows_in:, :] = jnp.zeros((uext.shape[1] - 2 * HALO - rows_in, LANES), F32)
        bias = jnp.broadcast_to(cb_ref[:, cols], (sub, LANES))
        taps = [jnp.broadcast_to(cw_ref[k:k + 1, cols], (sub, LANES)) for k in range(D_CONV)]
        for ch in range(BWD_CHUNKS):
            for i in range(CONV_STRIDE):
                acc = bias
                for k in range(D_CONV):
                    start = ch * CHUNK + first + k + i
                    acc = acc + taps[k] * uext[slab, pl.ds(start, sub, stride=CONV_STRIDE), :]
                cout[ch * nslab + slab, pl.ds(i, sub, stride=CONV_STRIDE), :] = _silu(acc)
            xc_ref[0, ch * CHUNK:(ch + 1) * CHUNK, cols] = cout[ch * nslab + slab, 0:CHUNK, :]

    _, upper = _tri_masks()
    upper_sel = jnp.where(upper, 1.0, 0.0).astype(BF16)
    for ch in reversed(range(BWD_CHUNKS)):
        rows = slice(ch * CHUNK, (ch + 1) * CHUNK)
        dtv, loga = _dt_and_loga(dt_ref.at[:, rows], dtb_ref, alog_ref)
        rcum = _dot_sel_lhs(upper_sel, loga)
        r0 = rcum[0:1, :]
        wgt = dtv * jnp.exp2(r0 - rcum)
        wgt_x = _dot(_pack3(wgt), eb_ref[...])
        dec_x = _dot(_pack3(jnp.broadcast_to(jnp.exp2(r0), (HALO, LANES))), eb_ref[...])[0:1]

        hb_ref[0, ch] = state[...].astype(BF16)
        xw = (xc_ref[0, rows, 0:D_SSM] * wgt_x).astype(BF16)
        for g in range(SSM_GROUPS):
            gs = slice(g * GROUP_W, (g + 1) * GROUP_W)
            b_t = xc_ref[0, rows, D_SSM + g * D_STATE:D_SSM + (g + 1) * D_STATE].T.astype(BF16)
            state[:, gs] = state[:, gs] * dec_x[:, gs] + _dot(b_t, xw[:, gs])


def _ssd_bwd(xbc, dt, conv_w, conv_b, dt_bias, a_log, sel_b):
    bsz, t, _ = xbc.shape
    rows = BWD_CHUNKS * CHUNK
    nsteps = t // rows
    hps = rows // HALO
    nhalo = t // HALO
    nslab = CONV_DIM // LANES
    rev = lambda c: nsteps - 1 - c
    return pl.pallas_call(
        _ssd_bwd_body,
        grid=(bsz, nsteps),
        in_specs=[
            pl.BlockSpec((1, rows, CONV_DIM), lambda b, c: (b, rev(c), 0)),
            pl.BlockSpec((1, HALO, CONV_DIM), lambda b, c: (b, jnp.maximum(rev(c) * hps - 1, 0), 0)),
            pl.BlockSpec((1, HALO, CONV_DIM), lambda b, c: (b, jnp.minimum((rev(c) + 1) * hps, nhalo - 1), 0)),
            pl.BlockSpec((1, rows, LANES), lambda b, c: (b, rev(c), 0)),
            _const_spec(conv_w.shape), _const_spec(conv_b.shape),
            _const_spec(dt_bias.shape), _const_spec(a_log.shape), _const_spec(sel_b.shape),
        ],
        out_specs=[
            pl.BlockSpec((1, rows, CONV_DIM), lambda b, c: (b, rev(c), 0)),
            pl.BlockSpec((1, BWD_CHUNKS, D_STATE, D_SSM), lambda b, c: (b, rev(c), 0, 0)),
        ],
        out_shape=[
            jax.ShapeDtypeStruct((bsz, t, CONV_DIM), F32),
            jax.ShapeDtypeStruct((bsz, t // CHUNK, D_STATE, D_SSM), BF16),
        ],
        scratch_shapes=[pltpu.VMEM((nslab, rows + 3 * HALO, LANES), F32),
                        pltpu.VMEM((BWD_CHUNKS * nslab, 8 * CONV_STRIDE, LANES), F32),
                        pltpu.VMEM((D_STATE, D_SSM), F32)],
        compiler_params=pltpu.CompilerParams(
            dimension_semantics=("arbitrary", "arbitrary"), vmem_limit_bytes=VMEM_LIMIT),
    )(xbc, xbc, xbc, dt, conv_w, conv_b, dt_bias, a_log, sel_b)


def _ssd_fwd_body(xc_ref, dt_ref, z_ref, hb_ref, dtb_ref, alog_ref, dsk_ref, nw_ref, ef_ref, eb_ref,
                  y_ref, state, ybuf):
    @pl.when(pl.program_id(1) == 0)
    def _():
        state[...] = jnp.zeros_like(state)

    for sub in range(FWD_CHUNKS):
        rows = slice(sub * CHUNK, (sub + 1) * CHUNK)
        _ssd_fwd_chunk(xc_ref.at[0, rows], dt_ref.at[:, rows], z_ref.at[0, rows], hb_ref.at[0, sub], dtb_ref, alog_ref,
                       dsk_ref, nw_ref, ef_ref, eb_ref, y_ref.at[0, rows], state, ybuf.at[rows])


def _ssd_fwd_chunk(xc_ref, dt_ref, z_ref, hb_ref, dtb_ref, alog_ref, dsk_ref, nw_ref, ef_ref, eb_ref,
                   y_ref, state, ybuf):
    dtv, loga = _dt_and_loga(dt_ref, dtb_ref, alog_ref)
    lower, upper = _tri_masks()
    cum_f = _dot_sel_lhs(jnp.where(lower, 1.0, 0.0).astype(BF16), loga)
    cum_b = _dot_sel_lhs(jnp.where(upper, 1.0, 0.0).astype(BF16), loga)
    lane = lax.broadcasted_iota(jnp.int32, (CHUNK, LANES), 1)
    cum = jnp.where((lane & SSM_HEADS) == 0, cum_f, cum_b)
    ecum_p = _pack3(jnp.exp2(cum))
    wgt_p = _pack3(dtv * jnp.exp2(cum_f[CHUNK - 1:CHUNK, :] - cum_f))
    src_t = (cum - jnp.log2(dtv)).T
    first_half = lane < SSM_HEAD_DIM

    for g in range(SSM_GROUPS):
        gs = slice(g * GROUP_W, (g + 1) * GROUP_W)
        b_f32 = xc_ref[:, D_SSM + g * D_STATE:D_SSM + (g + 1) * D_STATE]
        b_g = b_f32.astype(BF16)
        c0 = D_SSM + SSM_GROUPS * D_STATE + g * D_STATE
        c_g = xc_ref[:, c0:c0 + D_STATE].astype(BF16)
        cb = _dot_nt(c_g, b_g)
        ea_g = _dot(ecum_p, ef_ref[:, gs])
        er_g = _dot(ecum_p, eb_ref[:, gs])
        y_off = _dot(c_g, state[:, gs].astype(BF16)) * ea_g + _dot(c_g, hb_ref[:, gs]) * er_g
        xw = (xc_ref[:, gs] * _dot(wgt_p, ef_ref[:, gs])).astype(BF16)
        state[:, gs] = state[:, gs] * ea_g[CHUNK - 1:CHUNK, :] + _dot(b_f32.T.astype(BF16), xw)
        pairs = GROUP_W // LANES
        for pp in range(pairs):
            pr = g * pairs + pp
            ps = slice(pr * LANES, (pr + 1) * LANES)
            mats = []
            for h in (2 * pr, 2 * pr + 1):
                hb = SSM_HEADS + h
                seg_f = jnp.broadcast_to(cum[:, h:h + 1], (CHUNK, CHUNK)) - src_t[h:h + 1, :]
                seg_b = jnp.broadcast_to(cum[:, hb:hb + 1], (CHUNK, CHUNK)) - src_t[hb:hb + 1, :]
                d_f = jnp.exp2(jnp.where(lower, seg_f, MASKED))
                d_b = jnp.exp2(jnp.where(upper, seg_b, MASKED))
                mats.append(((d_f + d_b) * cb).astype(BF16))
            xs = xc_ref[:, ps]
            xp = xs.astype(BF16)
            zero = jnp.zeros_like(xp)
            rhs = jnp.concatenate([jnp.where(first_half, xp, zero), jnp.where(first_half, zero, xp)], axis=0)
            y_diag = _dot(jnp.concatenate(mats, axis=1), rhs)
            ybuf[:, ps] = y_diag + y_off[:, pp * LANES:(pp + 1) * LANES] + dsk_ref[:, ps] * xs
        gated = ybuf[:, gs] * _silu(z_ref[:, gs])
        ms = jnp.mean(gated * gated, axis=-1, keepdims=True)
        y_ref[:, gs] = (gated * lax.rsqrt(ms + EPS) * nw_ref[:, gs]).astype(y_ref.dtype)


def _ssd_fwd(xc, dt, z, hb, dt_bias, a_log, d_skip_x, norm_w, sel_f, sel_b):
    bsz, t, _ = xc.shape
    step = FWD_CHUNKS * CHUNK
    blk = lambda w: pl.BlockSpec((1, step, w), lambda b, c: (b, c, 0))
    return pl.pallas_call(
        _ssd_fwd_body,
        grid=(bsz, t // step),
        in_specs=[
            blk(CONV_DIM), blk(LANES), blk(D_SSM),
            pl.BlockSpec((1, FWD_CHUNKS, D_STATE, D_SSM), lambda b, c: (b, c, 0, 0)),
            _const_spec(dt_bias.shape), _const_spec(a_log.shape), _const_spec(d_skip_x.shape),
            _const_spec(norm_w.shape), _const_spec(sel_f.shape), _const_spec(sel_b.shape),
        ],
        out_specs=blk(D_SSM),
        out_shape=jax.ShapeDtypeStruct((bsz, t, D_SSM), BF16),
        scratch_shapes=[pltpu.VMEM((D_STATE, D_SSM), F32), pltpu.VMEM((step, D_SSM), F32)],
        compiler_params=pltpu.CompilerParams(
            dimension_semantics=("arbitrary", "arbitrary"), vmem_limit_bytes=VMEM_LIMIT),
    )(xc, dt, z, hb, dt_bias, a_log, d_skip_x, norm_w, sel_f, sel_b)


NORM_ROWS = 512


def _build_bias(g_ref, colmask_ref, bias):
    rel = _block_row_rel()
    half = lax.broadcasted_iota(jnp.int32, (GRID_W, LANES), 1) < GRID_W
    nrel = 2 * NA_ROWS

    @functools.cache
    def toeplitz(hh, r, upper_half):
        if upper_half:
            return pltpu.roll(toeplitz(hh, r, False), GRID_W, 1)
        row = g_ref[0, hh * nrel + r:hh * nrel + r + 1, :] * LOG2E
        return pltpu.roll(jnp.broadcast_to(row, (GRID_W, LANES)), 0, 1, stride=1, stride_axis=0)

    for kind in range(rel.shape[0]):
        for hh in range(2):
            for u in range(0, NA_BLOCK, 2):
                lane0 = (hh * NA_BLOCK + u) * GRID_W
                for a in range(NA_SPAN):
                    r_lo, r_hi = int(rel[kind, u, a]), int(rel[kind, u + 1, a])
                    lo = toeplitz(hh, r_lo, False) if r_lo >= 0 else MASKED
                    hi = toeplitz(hh, r_hi, True) if r_hi >= 0 else MASKED
                    tile = jnp.where(half, lo, hi) + colmask_ref[...]
                    bias[kind, a * GRID_W:(a + 1) * GRID_W, lane0:lane0 + 2 * GRID_W] = tile


def _na_body(q_ref, k_ref, vt_ref, g_ref, colmask_ref, qw_ref, kw_ref, seg_ref, o_ref, qn, kn, sbuf, bias):
    t = q_ref.shape[1]
    rows = t // GRID_W

    @pl.when(pl.program_id(1) == 0)
    def _():
        _build_bias(g_ref, colmask_ref, bias)

    def norm_step(i, carry):
        sl = pl.ds(pl.multiple_of(i * NORM_ROWS, NORM_ROWS), NORM_ROWS)
        for src, w_ref, dst in ((q_ref, qw_ref, qn), (k_ref, kw_ref, kn)):
            xx = src[0, sl, :]
            sq = xx * xx
            hi = sq.astype(BF16)
            lo = (sq - hi.astype(F32)).astype(BF16)
            msq = _dot(hi, seg_ref[...]) + _dot(lo, seg_ref[...])
            dst[sl, :] = (xx * lax.rsqrt(msq + EPS) * w_ref[...]).astype(BF16)
        return carry

    lax.fori_loop(0, t // NORM_ROWS, norm_step, 0, unroll=4)

    nblk = rows // NA_BLOCK
    bq = NA_BLOCK * GRID_W
    first_half = lax.broadcasted_iota(jnp.int32, (bq, LANES), 1) < NA_HEAD_DIM

    def offsets(bi):
        i0 = bi * NA_BLOCK
        rlo = jnp.clip(i0 - NA_ROWS // 2, 0, rows - NA_SPAN)
        return pl.multiple_of(i0 * GRID_W, bq), pl.multiple_of(rlo * GRID_W, 2 * GRID_W)

    def scores(bi, slot):
        kind = jnp.where(bi == 0, 0, jnp.where(bi == nblk - 1, 2, 1))
        qoff, koff = offsets(bi)
        qb = qn[pl.ds(qoff, bq), :]
        zero = jnp.zeros_like(qb)
        qs = jnp.concatenate([jnp.where(first_half, qb, zero), jnp.where(first_half, zero, qb)], axis=0)
        sbuf[slot] = _dot_nt(kn[pl.ds(koff, NA_BKEYS), :], qs) + bias[kind]

    head_a_dims = lax.broadcasted_iota(jnp.int32, (LANES, bq), 0) < NA_HEAD_DIM

    def attend(bi, slot):
        qoff, koff = offsets(bi)
        s = sbuf[slot]
        p = jnp.exp2(s - jnp.max(s, axis=0, keepdims=True))
        denom = jnp.sum(p, axis=0, keepdims=True)
        tile0 = koff // LANES
        v_t = jnp.concatenate([vt_ref[tile0 + i] for i in range(NA_BKEYS // LANES)], axis=1)
        o_t = _dot(v_t, p.astype(BF16)) * (1.0 / denom)
        o_ref[0, pl.ds(qoff, bq), :] = jnp.where(head_a_dims, o_t[:, 0:bq], o_t[:, bq:]).T.astype(o_ref.dtype)

    scores(0, 0)

    def pair_step(i, carry):
        b0 = 2 * i
        scores(b0 + 1, 1)
        attend(b0, 0)
        scores(jnp.minimum(b0 + 2, nblk - 1), 0)
        attend(b0 + 1, 1)
        return carry

    lax.fori_loop(0, nblk // 2, pair_step, 0, unroll=4)


def _natten(q, k, vt, g, colmask, qw, kw, seg):
    bsz, t, _ = q.shape
    nq = 2 * NA_BLOCK * GRID_W
    blk = pl.BlockSpec((1, t, LANES), lambda p, b: (b, 0, p))
    return pl.pallas_call(
        _na_body,
        grid=(NA_PAIRS, bsz),
        in_specs=[
            blk, blk, pl.BlockSpec((t // LANES, LANES, LANES), lambda p, b: (b, p, 0)),
            pl.BlockSpec((1,) + g.shape[1:], lambda p, b: (p, 0, 0)), _const_spec(colmask.shape),
            _const_spec(qw.shape), _const_spec(kw.shape), _const_spec(seg.shape),
        ],
        out_specs=blk,
        out_shape=jax.ShapeDtypeStruct((bsz, t, D_NA), BF16),
        scratch_shapes=[pltpu.VMEM((t, LANES), BF16), pltpu.VMEM((t, LANES), BF16),
                        pltpu.VMEM((2, NA_BKEYS, nq), F32), pltpu.VMEM((3, NA_BKEYS, nq), F32)],
        compiler_params=pltpu.CompilerParams(
            dimension_semantics=("arbitrary", "arbitrary"), vmem_limit_bytes=VMEM_LIMIT),
    )(q, k, vt, g, colmask, qw, kw, seg)


def _block_row_rel():
    rel = -np.ones((3, NA_BLOCK, NA_SPAN), np.int64)
    for u in range(NA_BLOCK):
        for a in range(NA_SPAN):
            if a < NA_ROWS:
                rel[0, u, a] = a - u + NA_ROWS - 1
            if u <= a < u + NA_ROWS:
                rel[1, u, a] = a - u + NA_ROWS - 1 - NA_ROWS // 2
            if a >= NA_SPAN - NA_ROWS:
                rel[2, u, a] = a - u + NA_BLOCK - NA_SPAN + NA_ROWS - 1
    return rel


def _na_bias_rows(rpb):
    centre = NA_COLS - 1
    rev = rpb[:, :, ::-1]
    rows = jnp.concatenate(
        [rev[:, :, centre:], jnp.zeros(rpb.shape[:2] + (LANES - rpb.shape[2],), rpb.dtype), rev[:, :, :centre]],
        axis=2).astype(F32)
    rows = jnp.pad(rows, ((0, 0), (0, 2 * NA_ROWS - rpb.shape[1]), (0, 0)))
    return rows.reshape(NA_PAIRS, 2 * 2 * NA_ROWS, LANES)


def _na_col_mask():
    j = np.arange(GRID_W)
    c0 = np.clip(j - NA_COLS // 2, 0, GRID_W - NA_COLS)
    cc = np.arange(GRID_W)
    valid = (cc[:, None] >= c0[None, :]) & (cc[:, None] < c0[None, :] + NA_COLS)
    return jnp.asarray(np.tile(np.where(valid, 0.0, MASKED), (1, 2)), F32)


FF_STEP = 512


def _outmlp_body(x_ref, ys_ref, yn_ref, wo_ref, nw_ref, w1_ref, w2_ref, o_ref):
    dmix = ys_ref.shape[-1]
    x1 = x_ref[...] + _dot(ys_ref[...], wo_ref[0:dmix, :]) + _dot(yn_ref[...], wo_ref[dmix:, :])
    ms = jnp.mean(x1 * x1, axis=-1, keepdims=True)
    h = (x1 * lax.rsqrt(ms + EPS) * nw_ref[...]).astype(BF16)
    o_ref[...] = x1
    for f in range(0, w1_ref.shape[1], FF_STEP):
        u = jnp.maximum(_dot(h, w1_ref[:, f:f + FF_STEP]), 0.0)
        o_ref[...] += _dot((u * u).astype(BF16), w2_ref[f:f + FF_STEP, :])


def _outmlp(x2, ys, yn, w_out, norm_w, w1, w2, tm):
    n, d = x2.shape
    row = lambda i: (i, 0)
    return pl.pallas_call(
        _outmlp_body,
        grid=(n // tm,),
        in_specs=[
            pl.BlockSpec((tm, d), row), pl.BlockSpec((tm, ys.shape[1]), row), pl.BlockSpec((tm, yn.shape[1]), row),
            _const_spec(w_out.shape), _const_spec((1, d)), _const_spec(w1.shape), _const_spec(w2.shape),
        ],
        out_specs=pl.BlockSpec((tm, d), row),
        out_shape=jax.ShapeDtypeStruct((n, d), F32),
        compiler_params=pltpu.CompilerParams(
            dimension_semantics=("arbitrary",), vmem_limit_bytes=VMEM_LIMIT),
    )(x2, ys, yn, w_out, norm_w, w1, w2)


def _head_select(first_lane):
    sel = np.zeros((LANES, D_SSM), np.float32)
    for rep in range(DT_COPIES):
        for h in range(SSM_HEADS):
            sel[rep * DT_LANES + first_lane + h, h * SSM_HEAD_DIM:(h + 1) * SSM_HEAD_DIM] = 1.0
    return jnp.asarray(sel, BF16)


def _dt_lanes(fwd, bwd):
    v = jnp.tile(jnp.concatenate([fwd, bwd]).astype(F32), DT_COPIES)
    return jnp.pad(v, (0, LANES - v.shape[0]))[None, :]


def _layer(x, norm_mix_w, w_in, conv_w, conv_b, dt_bias_fwd, dt_bias_bwd, a_log_fwd, a_log_bwd, d_skip,
           ssm_norm_w, q_norm_w, k_norm_w, rel_pos_bias, w_out, norm_mlp_w, w_mlp_in, w_mlp_out, tm):
    bsz, t, d = x.shape
    n = bsz * t
    x2 = x.reshape(n, d)

    (z, xbc, q, k, vt, dt), (w_out_bf, w1_bf, w2_bf) = _inproj(
        x2, norm_mix_w[None, :], w_in.T.astype(BF16), (w_out, w_mlp_in, w_mlp_out), tm)
    r3 = lambda a: a.reshape(bsz, t, a.shape[-1])

    dt_bias = _dt_lanes(dt_bias_fwd, dt_bias_bwd)
    a_log = _dt_lanes(a_log_fwd, a_log_bwd)
    sel_f, sel_b = _head_select(0), _head_select(SSM_HEADS)
    xc, hb = _ssd_bwd(r3(xbc), r3(dt), conv_w, conv_b[None, :], dt_bias, a_log, sel_b)
    y_ssm = _ssd_fwd(xc, r3(dt), r3(z), hb, dt_bias, a_log,
                     jnp.repeat(d_skip.astype(F32), SSM_HEAD_DIM)[None, :], ssm_norm_w[None, :], sel_f, sel_b)

    lane_head = np.arange(LANES) // NA_HEAD_DIM
    seg = jnp.asarray((lane_head[:, None] == lane_head[None, :]) / NA_HEAD_DIM, BF16)
    qw = (jnp.tile(q_norm_w.astype(F32), 2) * (NA_HEAD_DIM ** -0.5 * LOG2E))[None, :]
    kw = jnp.tile(k_norm_w.astype(F32), 2)[None, :]
    y_na = _natten(r3(q), r3(k), vt, _na_bias_rows(rel_pos_bias), _na_col_mask(), qw, kw, seg)

    out = _outmlp(x2, y_ssm.reshape(n, D_SSM), y_na.reshape(n, D_NA), w_out_bf, norm_mlp_w[None, :], w1_bf, w2_bf, tm)
    return out.reshape(bsz, t, d)


def kernel(x, norm_mix_w, w_in, conv_w, conv_b, dt_bias_fwd, dt_bias_bwd, a_log_fwd, a_log_bwd, d_skip,
           ssm_norm_w, q_norm_w, k_norm_w, rel_pos_bias, w_out, norm_mlp_w, w_mlp_in, w_mlp_out):
    tm = min(512, x.shape[0] * x.shape[1])
    for layer in range(norm_mix_w.shape[0]):
        x = _layer(x, norm_mix_w[layer], w_in[layer], conv_w[layer], conv_b[layer], dt_bias_fwd[layer],
                   dt_bias_bwd[layer], a_log_fwd[layer], a_log_bwd[layer], d_skip[layer], ssm_norm_w[layer],
                   q_norm_w[layer], k_norm_w[layer], rel_pos_bias[layer], w_out[layer], norm_mlp_w[layer],
                   w_mlp_in[layer], w_mlp_out[layer], tm)
    return x
```

```python
import functools

import numpy as np
import jax
import jax.numpy as jnp
from jax import lax
from jax.experimental import pallas as pl
from jax.experimental.pallas import tpu as pltpu

F32 = jnp.float32
BF16 = jnp.bfloat16

LANES = 128
SUBLANES = 8
GRID_W = 64
SSM_HEADS = 16
SSM_HEAD_DIM = 64
D_SSM = SSM_HEADS * SSM_HEAD_DIM
SSM_GROUPS = 2
GROUP_W = D_SSM // SSM_GROUPS
D_STATE = 128
D_CONV = 5
CHUNK = 128
FWD_CHUNKS = 4
BWD_CHUNKS = 4
CONV_DIM = D_SSM + 2 * SSM_GROUPS * D_STATE
DT_LANES = 2 * SSM_HEADS
DT_COPIES = 3
NA_HEADS = 16
NA_HEAD_DIM = 64
D_NA = NA_HEADS * NA_HEAD_DIM
NA_ROWS = 8
NA_COLS = 16
NA_PAIRS = NA_HEADS // 2
NA_BLOCK = 4
NA_SPAN = NA_ROWS + NA_BLOCK
NA_BKEYS = NA_SPAN * GRID_W
LOG2E = 1.4426950408889634
EPS = 1e-5
MASKED = -1e30
HALO = SUBLANES
CONV_STRIDE = CHUNK // SUBLANES + 1
VMEM_LIMIT = 56 * 1024 * 1024


def _dot(a, b):
    return jnp.dot(a, b, preferred_element_type=F32)


def _dot_nt(a, b):
    return lax.dot_general(a, b, (((1,), (1,)), ((), ())), preferred_element_type=F32)


def _split3(x):
    hi = x.astype(BF16)
    r1 = x - hi.astype(F32)
    mid = r1.astype(BF16)
    lo = (r1 - mid.astype(F32)).astype(BF16)
    return hi, mid, lo


def _dot_sel_lhs(sel, x):
    hi, mid, lo = _split3(x)
    return _dot(sel, hi) + _dot(sel, mid) + _dot(sel, lo)


def _pack3(x):
    lane = lax.broadcasted_iota(jnp.int32, x.shape, x.ndim - 1)
    hi = x.astype(BF16).astype(F32)
    r1 = x - hi
    mid = r1.astype(BF16).astype(F32)
    return jnp.where(lane < DT_LANES, hi, jnp.where(lane < 2 * DT_LANES, mid, r1 - mid)).astype(BF16)


def _softplus(x):
    return jnp.maximum(x, 0.0) + jnp.log(1.0 + jnp.exp(-jnp.abs(x)))


def _silu(x):
    return x * (1.0 / (1.0 + jnp.exp(-x)))


def _const_spec(shape):
    nd = len(shape)
    return pl.BlockSpec(shape, lambda *_: (0,) * nd, pipeline_mode=pl.Buffered(1))


O_DT = D_SSM + CONV_DIM
O_Q = O_DT + DT_LANES
O_K = O_Q + D_NA
O_V = O_K + D_NA


def _inproj_body(x_ref, nw_ref, wt_ref, *refs):
    ncast = (len(refs) - 6) // 2
    cast_in, (z_ref, xbc_ref, q_ref, k_ref, vt_ref, dt_ref), cast_out = refs[:ncast], refs[ncast:ncast + 6], refs[ncast + 6:]
    for src, dst in zip(cast_in, cast_out):
        dst[...] = src[...].astype(dst.dtype)
    x = x_ref[...]
    ms = jnp.mean(x * x, axis=-1, keepdims=True)
    h = (x * lax.rsqrt(ms + EPS) * nw_ref[...]).astype(BF16)
    z_ref[...] = _dot_nt(h, wt_ref[0:D_SSM, :])
    xbc_ref[...] = _dot_nt(h, wt_ref[D_SSM:O_DT, :])
    w_dt = wt_ref[O_DT:O_Q, :]
    pad = jnp.zeros((LANES - DT_COPIES * DT_LANES, w_dt.shape[1]), w_dt.dtype)
    dt_ref[...] = _dot_nt(h, jnp.concatenate([w_dt] * DT_COPIES + [pad], axis=0))
    q_ref[...] = _dot_nt(h, wt_ref[O_Q:O_K, :])
    k_ref[...] = _dot_nt(h, wt_ref[O_K:O_V, :])
    vt = _dot_nt(wt_ref[O_V:, :], h)
    for j in range(vt_ref.shape[0]):
        vt_ref[j] = vt[:, j * LANES:(j + 1) * LANES].astype(vt_ref.dtype)


def _inproj(x2, norm_w, w_bf, later_weights, tm):
    n, d = x2.shape
    steps = n // tm
    widths = (D_SSM, CONV_DIM, D_NA, D_NA)
    row = lambda i: (i, 0)
    tiles = tm // LANES
    bf16_rows = 2 * SUBLANES
    assert all(w.shape[0] % (steps * bf16_rows) == 0 for w in later_weights)
    slabs = [pl.BlockSpec((w.shape[0] // steps, w.shape[1]), row) for w in later_weights]
    outs = pl.pallas_call(
        _inproj_body,
        grid=(steps,),
        in_specs=[pl.BlockSpec((tm, d), row), _const_spec((1, d)), _const_spec(w_bf.shape)] + slabs,
        out_specs=[pl.BlockSpec((tm, w), row) for w in widths]
        + [pl.BlockSpec((tiles, D_NA, LANES), lambda i: (i, 0, 0)), pl.BlockSpec((tm, LANES), row)] + slabs,
        out_shape=[jax.ShapeDtypeStruct((n, w), F32) for w in widths]
        + [jax.ShapeDtypeStruct((n // LANES, D_NA, LANES), BF16), jax.ShapeDtypeStruct((n, LANES), F32)]
        + [jax.ShapeDtypeStruct(w.shape, BF16) for w in later_weights],
        compiler_params=pltpu.CompilerParams(
            dimension_semantics=("arbitrary",), vmem_limit_bytes=VMEM_LIMIT),
    )(x2, norm_w, w_bf, *later_weights)
    return outs[:6], outs[6:]


def _tri_masks():
    t = lax.broadcasted_iota(jnp.int32, (CHUNK, CHUNK), 0)
    s = lax.broadcasted_iota(jnp.int32, (CHUNK, CHUNK), 1)
    return s <= t, s >= t


def _dt_and_loga(dt_ref, dtb_ref, alog_ref):
    dtv = _softplus(dt_ref[0] + dtb_ref[...])
    return dtv, dtv * (-LOG2E * jnp.exp(alog_ref[...]))


def _ssd_bwd_body(cur_ref, prev_ref, next_ref, dt_ref, cw_ref, cb_ref, dtb_ref, alog_ref, eb_ref,
                  xc_ref, hb_ref, uext, cout, state):
    c = pl.program_id(1)
    nsteps = pl.num_programs(1)
    step = nsteps - 1 - c
    rows_in = BWD_CHUNKS * CHUNK

    @pl.when(c == 0)
    def _():
        state[...] = jnp.zeros_like(state)

    first = HALO - (D_CONV - 1) // 2
    sub = SUBLANES
    nslab = CONV_DIM // LANES
    for slab in range(nslab):
        cols = slice(slab * LANES, (slab + 1) * LANES)
        uext[slab, 0:HALO, :] = jnp.where(step > 0, prev_ref[0, :, cols], 0.0)
        uext[slab, HALO:HALO + rows_in, :] = cur_ref[0, :, cols]
        uext[slab, HALO + rows_in:2 * HALO + rows_in, :] = jnp.where(step < nsteps - 1, next_ref[0, :, cols], 0.0)
        uext[slab, 2 * HALO + rows_in:, :] = jnp.zeros((uext.shape[1] - 2 * HALO - rows_in, LANES), F32)
        bias = jnp.broadcast_to(cb_ref[:, cols], (sub, LANES))
        taps = [jnp.broadcast_to(cw_ref[k:k + 1, cols], (sub, LANES)) for k in range(D_CONV)]
        for ch in range(BWD_CHUNKS):
            for i in range(CONV_STRIDE):
                acc = bias
                for k in range(D_CONV):
                    start = ch * CHUNK + first + k + i
                    acc = acc + taps[k] * uext[slab, pl.ds(start, sub, stride=CONV_STRIDE), :]
                cout[ch * nslab + slab, pl.ds(i, sub, stride=CONV_STRIDE), :] = _silu(acc)
            xc_ref[0, ch * CHUNK:(ch + 1) * CHUNK, cols] = cout[ch * nslab + slab, 0:CHUNK, :]

    _, upper = _tri_masks()
    upper_sel = jnp.where(upper, 1.0, 0.0).astype(BF16)
    for ch in reversed(range(BWD_CHUNKS)):
        rows = slice(ch * CHUNK, (ch + 1) * CHUNK)
        dtv, loga = _dt_and_loga(dt_ref.at[:, rows], dtb_ref, alog_ref)
        rcum = _dot_sel_lhs(upper_sel, loga)
        r0 = rcum[0:1, :]
        wgt = dtv * jnp.exp2(r0 - rcum)
        wgt_x = _dot(_pack3(wgt), eb_ref[...])
        dec_x = _dot(_pack3(jnp.broadcast_to(jnp.exp2(r0), (HALO, LANES))), eb_ref[...])[0:1]

        hb_ref[0, ch] = state[...].astype(BF16)
        xw = (xc_ref[0, rows, 0:D_SSM] * wgt_x).astype(BF16)
        for g in range(SSM_GROUPS):
            gs = slice(g * GROUP_W, (g + 1) * GROUP_W)
            b_t = xc_ref[0, rows, D_SSM + g * D_STATE:D_SSM + (g + 1) * D_STATE].T.astype(BF16)
            state[:, gs] = state[:, gs] * dec_x[:, gs] + _dot(b_t, xw[:, gs])


def _ssd_bwd(xbc, dt, conv_w, conv_b, dt_bias, a_log, sel_b):
    bsz, t, _ = xbc.shape
    rows = BWD_CHUNKS * CHUNK
    nsteps = t // rows
    hps = rows // HALO
    nhalo = t // HALO
    nslab = CONV_DIM // LANES
    rev = lambda c: nsteps - 1 - c
    return pl.pallas_call(
        _ssd_bwd_body,
        grid=(bsz, nsteps),
        in_specs=[
            pl.BlockSpec((1, rows, CONV_DIM), lambda b, c: (b, rev(c), 0)),
            pl.BlockSpec((1, HALO, CONV_DIM), lambda b, c: (b, jnp.maximum(rev(c) * hps - 1, 0), 0)),
            pl.BlockSpec((1, HALO, CONV_DIM), lambda b, c: (b, jnp.minimum((rev(c) + 1) * hps, nhalo - 1), 0)),
            pl.BlockSpec((1, rows, LANES), lambda b, c: (b, rev(c), 0)),
            _const_spec(conv_w.shape), _const_spec(conv_b.shape),
            _const_spec(dt_bias.shape), _const_spec(a_log.shape), _const_spec(sel_b.shape),
        ],
        out_specs=[
            pl.BlockSpec((1, rows, CONV_DIM), lambda b, c: (b, rev(c), 0)),
            pl.BlockSpec((1, BWD_CHUNKS, D_STATE, D_SSM), lambda b, c: (b, rev(c), 0, 0)),
        ],
        out_shape=[
            jax.ShapeDtypeStruct((bsz, t, CONV_DIM), F32),
            jax.ShapeDtypeStruct((bsz, t // CHUNK, D_STATE, D_SSM), BF16),
        ],
        scratch_shapes=[pltpu.VMEM((nslab, rows + 3 * HALO, LANES), F32),
                        pltpu.VMEM((BWD_CHUNKS * nslab, SUBLANES * CONV_STRIDE, LANES), F32),
                        pltpu.VMEM((D_STATE, D_SSM), F32)],
        compiler_params=pltpu.CompilerParams(
            dimension_semantics=("arbitrary", "arbitrary"), vmem_limit_bytes=VMEM_LIMIT),
    )(xbc, xbc, xbc, dt, conv_w, conv_b, dt_bias, a_log, sel_b)


def _ssd_fwd_body(xc_ref, dt_ref, z_ref, hb_ref, dtb_ref, alog_ref, dsk_ref, nw_ref, ef_ref, eb_ref,
                  y_ref, state, ybuf):
    @pl.when(pl.program_id(1) == 0)
    def _():
        state[...] = jnp.zeros_like(state)

    for sub in range(FWD_CHUNKS):
        rows = slice(sub * CHUNK, (sub + 1) * CHUNK)
        _ssd_fwd_chunk(xc_ref.at[0, rows], dt_ref.at[:, rows], z_ref.at[0, rows], hb_ref.at[0, sub], dtb_ref, alog_ref,
                       dsk_ref, nw_ref, ef_ref, eb_ref, y_ref.at[0, rows], state, ybuf.at[rows])


def _ssd_fwd_chunk(xc_ref, dt_ref, z_ref, hb_ref, dtb_ref, alog_ref, dsk_ref, nw_ref, ef_ref, eb_ref,
                   y_ref, state, ybuf):
    dtv, loga = _dt_and_loga(dt_ref, dtb_ref, alog_ref)
    lower, upper = _tri_masks()
    cum_f = _dot_sel_lhs(jnp.where(lower, 1.0, 0.0).astype(BF16), loga)
    cum_b = _dot_sel_lhs(jnp.where(upper, 1.0, 0.0).astype(BF16), loga)
    lane = lax.broadcasted_iota(jnp.int32, (CHUNK, LANES), 1)
    cum = jnp.where((lane & SSM_HEADS) == 0, cum_f, cum_b)
    ecum_p = _pack3(jnp.exp2(cum))
    wgt_p = _pack3(dtv * jnp.exp2(cum_f[CHUNK - 1:CHUNK, :] - cum_f))
    src_t = (cum - jnp.log2(dtv)).T
    first_half = lane < SSM_HEAD_DIM

    for g in range(SSM_GROUPS):
        gs = slice(g * GROUP_W, (g + 1) * GROUP_W)
        b_f32 = xc_ref[:, D_SSM + g * D_STATE:D_SSM + (g + 1) * D_STATE]
        b_g = b_f32.astype(BF16)
        c0 = D_SSM + SSM_GROUPS * D_STATE + g * D_STATE
        c_g = xc_ref[:, c0:c0 + D_STATE].astype(BF16)
        cb = _dot_nt(c_g, b_g)
        ea_g = _dot(ecum_p, ef_ref[:, gs])
        er_g = _dot(ecum_p, eb_ref[:, gs])
        y_off = _dot(c_g, state[:, gs].astype(BF16)) * ea_g + _dot(c_g, hb_ref[:, gs]) * er_g
        xw = (xc_ref[:, gs] * _dot(wgt_p, ef_ref[:, gs])).astype(BF16)
        state[:, gs] = state[:, gs] * ea_g[CHUNK - 1:CHUNK, :] + _dot(b_f32.T.astype(BF16), xw)
        pairs = GROUP_W // LANES
        for pp in range(pairs):
            pr = g * pairs + pp
            ps = slice(pr * LANES, (pr + 1) * LANES)
            mats = []
            for h in (2 * pr, 2 * pr + 1):
                hb = SSM_HEADS + h
                seg_f = jnp.broadcast_to(cum[:, h:h + 1], (CHUNK, CHUNK)) - src_t[h:h + 1, :]
                seg_b = jnp.broadcast_to(cum[:, hb:hb + 1], (CHUNK, CHUNK)) - src_t[hb:hb + 1, :]
                d_f = jnp.exp2(jnp.where(lower, seg_f, MASKED))
                d_b = jnp.exp2(jnp.where(upper, seg_b, MASKED))
                mats.append(((d_f + d_b) * cb).astype(BF16))
            xs = xc_ref[:, ps]
            xp = xs.astype(BF16)
            zero = jnp.zeros_like(xp)
            rhs = jnp.concatenate([jnp.where(first_half, xp, zero), jnp.where(first_half, zero, xp)], axis=0)
            y_diag = _dot(jnp.concatenate(mats, axis=1), rhs)
            ybuf[:, ps] = y_diag + y_off[:, pp * LANES:(pp + 1) * LANES] + dsk_ref[:, ps] * xs
        gated = ybuf[:, gs] * _silu(z_ref[:, gs])
        ms = jnp.mean(gated * gated, axis=-1, keepdims=True)
        y_ref[:, gs] = (gated * lax.rsqrt(ms + EPS) * nw_ref[:, gs]).astype(y_ref.dtype)


def _ssd_fwd(xc, dt, z, hb, dt_bias, a_log, d_skip_x, norm_w, sel_f, sel_b):
    bsz, t, _ = xc.shape
    step = FWD_CHUNKS * CHUNK
    blk = lambda w: pl.BlockSpec((1, step, w), lambda b, c: (b, c, 0))
    return pl.pallas_call(
        _ssd_fwd_body,
        grid=(bsz, t // step),
        in_specs=[
            blk(CONV_DIM), blk(LANES), blk(D_SSM),
            pl.BlockSpec((1, FWD_CHUNKS, D_STATE, D_SSM), lambda b, c: (b, c, 0, 0)),
            _const_spec(dt_bias.shape), _const_spec(a_log.shape), _const_spec(d_skip_x.shape),
            _const_spec(norm_w.shape), _const_spec(sel_f.shape), _const_spec(sel_b.shape),
        ],
        out_specs=blk(D_SSM),
        out_shape=jax.ShapeDtypeStruct((bsz, t, D_SSM), BF16),
        scratch_shapes=[pltpu.VMEM((D_STATE, D_SSM), F32), pltpu.VMEM((step, D_SSM), F32)],
        compiler_params=pltpu.CompilerParams(
            dimension_semantics=("arbitrary", "arbitrary"), vmem_limit_bytes=VMEM_LIMIT),
    )(xc, dt, z, hb, dt_bias, a_log, d_skip_x, norm_w, sel_f, sel_b)


NORM_ROWS = 512


def _build_bias(g_ref, colmask_ref, bias):
    rel = _block_row_rel()
    half = lax.broadcasted_iota(jnp.int32, (GRID_W, LANES), 1) < GRID_W
    nrel = 2 * NA_ROWS

    @functools.cache
    def toeplitz(hh, r, upper_half):
        if upper_half:
            return pltpu.roll(toeplitz(hh, r, False), GRID_W, 1)
        row = g_ref[0, hh * nrel + r:hh * nrel + r + 1, :] * LOG2E
        return pltpu.roll(jnp.broadcast_to(row, (GRID_W, LANES)), 0, 1, stride=1, stride_axis=0)

    for kind in range(rel.shape[0]):
        for hh in range(2):
            for u in range(0, NA_BLOCK, 2):
                lane0 = (hh * NA_BLOCK + u) * GRID_W
                for a in range(NA_SPAN):
                    r_lo, r_hi = int(rel[kind, u, a]), int(rel[kind, u + 1, a])
                    lo = toeplitz(hh, r_lo, False) if r_lo >= 0 else MASKED
                    hi = toeplitz(hh, r_hi, True) if r_hi >= 0 else MASKED
                    tile = jnp.where(half, lo, hi) + colmask_ref[...]
                    bias[kind, a * GRID_W:(a + 1) * GRID_W, lane0:lane0 + 2 * GRID_W] = tile


def _na_body(q_ref, k_ref, vt_ref, g_ref, colmask_ref, qw_ref, kw_ref, seg_ref, o_ref, qn, kn, sbuf, bias):
    t = q_ref.shape[1]
    rows = t // GRID_W

    @pl.when(pl.program_id(1) == 0)
    def _():
        _build_bias(g_ref, colmask_ref, bias)

    def norm_step(i, carry):
        sl = pl.ds(pl.multiple_of(i * NORM_ROWS, NORM_ROWS), NORM_ROWS)
        for src, w_ref, dst in ((q_ref, qw_ref, qn), (k_ref, kw_ref, kn)):
            xx = src[0, sl, :]
            sq = xx * xx
            hi = sq.astype(BF16)
            lo = (sq - hi.astype(F32)).astype(BF16)
            msq = _dot(hi, seg_ref[...]) + _dot(lo, seg_ref[...])
            dst[sl, :] = (xx * lax.rsqrt(msq + EPS) * w_ref[...]).astype(BF16)
        return carry

    lax.fori_loop(0, t // NORM_ROWS, norm_step, 0, unroll=4)

    nblk = rows // NA_BLOCK
    bq = NA_BLOCK * GRID_W
    first_half = lax.broadcasted_iota(jnp.int32, (bq, LANES), 1) < NA_HEAD_DIM

    def offsets(bi):
        i0 = bi * NA_BLOCK
        rlo = jnp.clip(i0 - NA_ROWS // 2, 0, rows - NA_SPAN)
        return pl.multiple_of(i0 * GRID_W, bq), pl.multiple_of(rlo * GRID_W, 2 * GRID_W)

    def scores(bi, slot):
        kind = jnp.where(bi == 0, 0, jnp.where(bi == nblk - 1, 2, 1))
        qoff, koff = offsets(bi)
        qb = qn[pl.ds(qoff, bq), :]
        zero = jnp.zeros_like(qb)
        qs = jnp.concatenate([jnp.where(first_half, qb, zero), jnp.where(first_half, zero, qb)], axis=0)
        sbuf[slot] = _dot_nt(kn[pl.ds(koff, NA_BKEYS), :], qs) + bias[kind]

    head_a_dims = lax.broadcasted_iota(jnp.int32, (LANES, bq), 0) < NA_HEAD_DIM

    def attend(bi, slot):
        qoff, koff = offsets(bi)
        s = sbuf[slot]
        p = jnp.exp2(s - jnp.max(s, axis=0, keepdims=True))
        denom = jnp.sum(p, axis=0, keepdims=True)
        tile0 = koff // LANES
        v_t = jnp.concatenate([vt_ref[tile0 + i] for i in range(NA_BKEYS // LANES)], axis=1)
        o_t = _dot(v_t, p.astype(BF16)) * (1.0 / denom)
        o_ref[0, pl.ds(qoff, bq), :] = jnp.where(head_a_dims, o_t[:, 0:bq], o_t[:, bq:]).T.astype(o_ref.dtype)

    scores(0, 0)

    def pair_step(i, carry):
        b0 = 2 * i
        scores(b0 + 1, 1)
        attend(b0, 0)
        scores(jnp.minimum(b0 + 2, nblk - 1), 0)
        attend(b0 + 1, 1)
        return carry

    lax.fori_loop(0, nblk // 2, pair_step, 0, unroll=4)


def _natten(q, k, vt, g, colmask, qw, kw, seg):
    bsz, t, _ = q.shape
    nq = 2 * NA_BLOCK * GRID_W
    blk = pl.BlockSpec((1, t, LANES), lambda p, b: (b, 0, p))
    return pl.pallas_call(
        _na_body,
        grid=(NA_PAIRS, bsz),
        in_specs=[
            blk, blk, pl.BlockSpec((t // LANES, LANES, LANES), lambda p, b: (b, p, 0)),
            pl.BlockSpec((1,) + g.shape[1:], lambda p, b: (p, 0, 0)), _const_spec(colmask.shape),
            _const_spec(qw.shape), _const_spec(kw.shape), _const_spec(seg.shape),
        ],
        out_specs=blk,
        out_shape=jax.ShapeDtypeStruct((bsz, t, D_NA), BF16),
        scratch_shapes=[pltpu.VMEM((t, LANES), BF16), pltpu.VMEM((t, LANES), BF16),
                        pltpu.VMEM((2, NA_BKEYS, nq), F32), pltpu.VMEM((3, NA_BKEYS, nq), F32)],
        compiler_params=pltpu.CompilerParams(
            dimension_semantics=("arbitrary", "arbitrary"), vmem_limit_bytes=VMEM_LIMIT),
    )(q, k, vt, g, colmask, qw, kw, seg)


def _block_row_rel():
    rel = -np.ones((3, NA_BLOCK, NA_SPAN), np.int64)
    for u in range(NA_BLOCK):
        for a in range(NA_SPAN):
            if a < NA_ROWS:
                rel[0, u, a] = a - u + NA_ROWS - 1
            if u <= a < u + NA_ROWS:
                rel[1, u, a] = a - u + NA_ROWS - 1 - NA_ROWS // 2
            if a >= NA_SPAN - NA_ROWS:
                rel[2, u, a] = a - u + NA_BLOCK - NA_SPAN + NA_ROWS - 1
    return rel


def _na_bias_rows(rpb):
    centre = NA_COLS - 1
    rev = rpb[:, :, ::-1]
    rows = jnp.concatenate(
        [rev[:, :, centre:], jnp.zeros(rpb.shape[:2] + (LANES - rpb.shape[2],), rpb.dtype), rev[:, :, :centre]],
        axis=2).astype(F32)
    rows = jnp.pad(rows, ((0, 0), (0, 2 * NA_ROWS - rpb.shape[1]), (0, 0)))
    return rows.reshape(NA_PAIRS, 2 * 2 * NA_ROWS, LANES)


def _na_col_mask():
    j = np.arange(GRID_W)
    c0 = np.clip(j - NA_COLS // 2, 0, GRID_W - NA_COLS)
    cc = np.arange(GRID_W)
    valid = (cc[:, None] >= c0[None, :]) & (cc[:, None] < c0[None, :] + NA_COLS)
    return jnp.asarray(np.tile(np.where(valid, 0.0, MASKED), (1, 2)), F32)


FF_STEP = 512


def _outmlp_body(x_ref, ys_ref, yn_ref, wo_ref, nw_ref, w1_ref, w2_ref, o_ref):
    dmix = ys_ref.shape[-1]
    x1 = x_ref[...] + _dot(ys_ref[...], wo_ref[0:dmix, :]) + _dot(yn_ref[...], wo_ref[dmix:, :])
    ms = jnp.mean(x1 * x1, axis=-1, keepdims=True)
    h = (x1 * lax.rsqrt(ms + EPS) * nw_ref[...]).astype(BF16)
    o_ref[...] = x1
    for f in range(0, w1_ref.shape[1], FF_STEP):
        u = jnp.maximum(_dot(h, w1_ref[:, f:f + FF_STEP]), 0.0)
        o_ref[...] += _dot((u * u).astype(BF16), w2_ref[f:f + FF_STEP, :])


def _outmlp(x2, ys, yn, w_out, norm_w, w1, w2, tm):
    n, d = x2.shape
    row = lambda i: (i, 0)
    return pl.pallas_call(
        _outmlp_body,
        grid=(n // tm,),
        in_specs=[
            pl.BlockSpec((tm, d), row), pl.BlockSpec((tm, ys.shape[1]), row), pl.BlockSpec((tm, yn.shape[1]), row),
            _const_spec(w_out.shape), _const_spec((1, d)), _const_spec(w1.shape), _const_spec(w2.shape),
        ],
        out_specs=pl.BlockSpec((tm, d), row),
        out_shape=jax.ShapeDtypeStruct((n, d), F32),
        compiler_params=pltpu.CompilerParams(
            dimension_semantics=("arbitrary",), vmem_limit_bytes=VMEM_LIMIT),
    )(x2, ys, yn, w_out, norm_w, w1, w2)


def _head_select(first_lane):
    sel = np.zeros((LANES, D_SSM), np.float32)
    for rep in range(DT_COPIES):
        for h in range(SSM_HEADS):
            sel[rep * DT_LANES + first_lane + h, h * SSM_HEAD_DIM:(h + 1) * SSM_HEAD_DIM] = 1.0
    return jnp.asarray(sel, BF16)


def _dt_lanes(fwd, bwd):
    v = jnp.tile(jnp.concatenate([fwd, bwd]).astype(F32), DT_COPIES)
    return jnp.pad(v, (0, LANES - v.shape[0]))[None, :]


def _layer(x, norm_mix_w, w_in, conv_w, conv_b, dt_bias_fwd, dt_bias_bwd, a_log_fwd, a_log_bwd, d_skip,
           ssm_norm_w, q_norm_w, k_norm_w, rel_pos_bias, w_out, norm_mlp_w, w_mlp_in, w_mlp_out, tm):
    bsz, t, d = x.shape
    n = bsz * t
    x2 = x.reshape(n, d)
    rows = t // GRID_W
    assert n % tm == 0 and tm % LANES == 0, (n, tm)
    assert t % (max(FWD_CHUNKS, BWD_CHUNKS) * CHUNK) == 0 and t % NORM_ROWS == 0, t
    assert t % GRID_W == 0 and rows % (2 * NA_BLOCK) == 0 and rows >= NA_SPAN, t
    assert w_in.shape == (d, O_V + D_NA), w_in.shape

    (z, xbc, q, k, vt, dt), (w_out_bf, w1_bf, w2_bf) = _inproj(
        x2, norm_mix_w[None, :], w_in.T.astype(BF16), (w_out, w_mlp_in, w_mlp_out), tm)
    r3 = lambda a: a.reshape(bsz, t, a.shape[-1])

    dt_bias = _dt_lanes(dt_bias_fwd, dt_bias_bwd)
    a_log = _dt_lanes(a_log_fwd, a_log_bwd)
    sel_f, sel_b = _head_select(0), _head_select(SSM_HEADS)
    xc, hb = _ssd_bwd(r3(xbc), r3(dt), conv_w, conv_b[None, :], dt_bias, a_log, sel_b)
    y_ssm = _ssd_fwd(xc, r3(dt), r3(z), hb, dt_bias, a_log,
                     jnp.repeat(d_skip.astype(F32), SSM_HEAD_DIM)[None, :], ssm_norm_w[None, :], sel_f, sel_b)

    lane_head = np.arange(LANES) // NA_HEAD_DIM
    seg = jnp.asarray((lane_head[:, None] == lane_head[None, :]) / NA_HEAD_DIM, BF16)
    qw = (jnp.tile(q_norm_w.astype(F32), 2) * (NA_HEAD_DIM ** -0.5 * LOG2E))[None, :]
    kw = jnp.tile(k_norm_w.astype(F32), 2)[None, :]
    y_na = _natten(r3(q), r3(k), vt, _na_bias_rows(rel_pos_bias), _na_col_mask(), qw, kw, seg)

    out = _outmlp(x2, y_ssm.reshape(n, D_SSM), y_na.reshape(n, D_NA), w_out_bf, norm_mlp_w[None, :], w1_bf, w2_bf, tm)
    return out.reshape(bsz, t, d)


def kernel(x, norm_mix_w, w_in, conv_w, conv_b, dt_bias_fwd, dt_bias_bwd, a_log_fwd, a_log_bwd, d_skip,
           ssm_norm_w, q_norm_w, k_norm_w, rel_pos_bias, w_out, norm_mlp_w, w_mlp_in, w_mlp_out):
    tm = min(512, x.shape[0] * x.shape[1])
    for layer in range(norm_mix_w.shape[0]):
        x = _layer(x, norm_mix_w[layer], w_in[layer], conv_w[layer], conv_b[layer], dt_bias_fwd[layer],
                   dt_bias_bwd[layer], a_log_fwd[layer], a_log_bwd[layer], d_skip[layer], ssm_norm_w[layer],
                   q_norm_w[layer], k_norm_w[layer], rel_pos_bias[layer], w_out[layer], norm_mlp_w[layer],
                   w_mlp_in[layer], w_mlp_out[layer], tm)
    return x
```

```python
import functools

import numpy as np
import jax
import jax.numpy as jnp
from jax import lax
from jax.experimental import pallas as pl
from jax.experimental.pallas import tpu as pltpu

F32 = jnp.float32
BF16 = jnp.bfloat16

LANES = 128
SUBLANES = 8
GRID_W = 64
SSM_HEADS = 16
SSM_HEAD_DIM = 64
D_SSM = SSM_HEADS * SSM_HEAD_DIM
SSM_GROUPS = 2
GROUP_W = D_SSM // SSM_GROUPS
D_STATE = 128
D_CONV = 5
CHUNK = 128
FWD_CHUNKS = 4
BWD_CHUNKS = 4
CONV_DIM = D_SSM + 2 * SSM_GROUPS * D_STATE
DT_LANES = 2 * SSM_HEADS
DT_COPIES = 3
NA_HEADS = 16
NA_HEAD_DIM = 64
D_NA = NA_HEADS * NA_HEAD_DIM
NA_ROWS = 8
NA_COLS = 16
NA_PAIRS = NA_HEADS // 2
NA_BLOCK = 4
NA_SPAN = NA_ROWS + NA_BLOCK
NA_BKEYS = NA_SPAN * GRID_W
LOG2E = 1.4426950408889634
EPS = 1e-5
MASKED = -1e30
HALO = SUBLANES
CONV_STRIDE = CHUNK // SUBLANES + 1
VMEM_LIMIT = 56 * 1024 * 1024


def _dot(a, b):
    return jnp.dot(a, b, preferred_element_type=F32)


def _dot_nt(a, b):
    return lax.dot_general(a, b, (((1,), (1,)), ((), ())), preferred_element_type=F32)


def _split3(x):
    hi = x.astype(BF16)
    r1 = x - hi.astype(F32)
    mid = r1.astype(BF16)
    lo = (r1 - mid.astype(F32)).astype(BF16)
    return hi, mid, lo


def _dot_sel_lhs(sel, x):
    hi, mid, lo = _split3(x)
    return _dot(sel, hi) + _dot(sel, mid) + _dot(sel, lo)


def _pack3(x):
    lane = lax.broadcasted_iota(jnp.int32, x.shape, x.ndim - 1)
    hi = x.astype(BF16).astype(F32)
    r1 = x - hi
    mid = r1.astype(BF16).astype(F32)
    return jnp.where(lane < DT_LANES, hi, jnp.where(lane < 2 * DT_LANES, mid, r1 - mid)).astype(BF16)


def _softplus(x):
    return jnp.maximum(x, 0.0) + jnp.log(1.0 + jnp.exp(-jnp.abs(x)))


def _silu(x):
    return x * (1.0 / (1.0 + jnp.exp(-x)))


def _const_spec(shape):
    nd = len(shape)
    return pl.BlockSpec(shape, lambda *_: (0,) * nd, pipeline_mode=pl.Buffered(1))


O_DT = D_SSM + CONV_DIM
O_Q = O_DT + DT_LANES
O_K = O_Q + D_NA
O_V = O_K + D_NA


def _head_rmsnorm_t(xt, gain_ref):
    tokens = xt.shape[1]
    x3 = xt.reshape(NA_HEADS, NA_HEAD_DIM, tokens)
    ms = jnp.mean(x3 * x3, axis=1, keepdims=True)
    return (x3 * lax.rsqrt(ms + EPS) * gain_ref[...]).reshape(D_NA, tokens)


def _inproj_body(x_ref, nw_ref, wt_ref, qg_ref, kg_ref, *refs):
    ncast = (len(refs) - 6) // 2
    cast_in, (z_ref, xbc_ref, qt_ref, k_ref, vt_ref, dt_ref), cast_out = refs[:ncast], refs[ncast:ncast + 6], refs[ncast + 6:]
    for src, dst in zip(cast_in, cast_out):
        dst[...] = src[...].astype(dst.dtype)
    x = x_ref[...]
    ms = jnp.mean(x * x, axis=-1, keepdims=True)
    h = (x * lax.rsqrt(ms + EPS) * nw_ref[...]).astype(BF16)
    z_ref[...] = _dot_nt(h, wt_ref[0:D_SSM, :])
    xbc_ref[...] = _dot_nt(h, wt_ref[D_SSM:O_DT, :])
    w_dt = wt_ref[O_DT:O_Q, :]
    pad = jnp.zeros((LANES - DT_COPIES * DT_LANES, w_dt.shape[1]), w_dt.dtype)
    dt_ref[...] = _dot_nt(h, jnp.concatenate([w_dt] * DT_COPIES + [pad], axis=0))
    qt = _head_rmsnorm_t(_dot_nt(wt_ref[O_Q:O_K, :], h), qg_ref)
    k_ref[...] = _head_rmsnorm_t(_dot_nt(wt_ref[O_K:O_V, :], h), kg_ref).T.astype(k_ref.dtype)
    vt = _dot_nt(wt_ref[O_V:, :], h)
    for j in range(vt_ref.shape[0]):
        cols = slice(j * LANES, (j + 1) * LANES)
        qt_ref[j] = qt[:, cols].astype(qt_ref.dtype)
        vt_ref[j] = vt[:, cols].astype(vt_ref.dtype)


def _inproj(x2, norm_w, w_bf, q_gain, k_gain, later_weights, tm):
    n, d = x2.shape
    steps = n // tm
    row = lambda i: (i, 0)
    tiles = tm // LANES
    tile_spec = pl.BlockSpec((tiles, D_NA, LANES), lambda i: (i, 0, 0))
    tile_shape = jax.ShapeDtypeStruct((n // LANES, D_NA, LANES), BF16)
    bf16_rows = 2 * SUBLANES
    assert all(w.shape[0] % (steps * bf16_rows) == 0 for w in later_weights)
    slabs = [pl.BlockSpec((w.shape[0] // steps, w.shape[1]), row) for w in later_weights]
    gains = [jnp.broadcast_to(g.astype(F32)[:, None], (NA_HEAD_DIM, tm)) for g in (q_gain, k_gain)]
    outs = pl.pallas_call(
        _inproj_body,
        grid=(steps,),
        in_specs=[pl.BlockSpec((tm, d), row), _const_spec((1, d)), _const_spec(w_bf.shape),
                  _const_spec(gains[0].shape), _const_spec(gains[1].shape)] + slabs,
        out_specs=[pl.BlockSpec((tm, D_SSM), row), pl.BlockSpec((tm, CONV_DIM), row), tile_spec,
                   pl.BlockSpec((tm, D_NA), row), tile_spec, pl.BlockSpec((tm, LANES), row)] + slabs,
        out_shape=[jax.ShapeDtypeStruct((n, D_SSM), F32), jax.ShapeDtypeStruct((n, CONV_DIM), F32), tile_shape,
                   jax.ShapeDtypeStruct((n, D_NA), BF16), tile_shape, jax.ShapeDtypeStruct((n, LANES), F32)]
        + [jax.ShapeDtypeStruct(w.shape, BF16) for w in later_weights],
        compiler_params=pltpu.CompilerParams(
            dimension_semantics=("arbitrary",), vmem_limit_bytes=VMEM_LIMIT),
    )(x2, norm_w, w_bf, *gains, *later_weights)
    return outs[:6], outs[6:]


def _tri_masks():
    t = lax.broadcasted_iota(jnp.int32, (CHUNK, CHUNK), 0)
    s = lax.broadcasted_iota(jnp.int32, (CHUNK, CHUNK), 1)
    return s <= t, s >= t


def _dt_and_loga(dt_ref, dtb_ref, alog_ref):
    dtv = _softplus(dt_ref[0] + dtb_ref[...])
    return dtv, dtv * (-LOG2E * jnp.exp(alog_ref[...]))


def _ssd_bwd_body(cur_ref, prev_ref, next_ref, dt_ref, cw_ref, cb_ref, dtb_ref, alog_ref, eb_ref,
                  xc_ref, hb_ref, uext, cout, state):
    c = pl.program_id(1)
    nsteps = pl.num_programs(1)
    step = nsteps - 1 - c
    rows_in = BWD_CHUNKS * CHUNK

    @pl.when(c == 0)
    def _():
        state[...] = jnp.zeros_like(state)

    first = HALO - (D_CONV - 1) // 2
    sub = SUBLANES
    nslab = CONV_DIM // LANES
    for slab in range(nslab):
        cols = slice(slab * LANES, (slab + 1) * LANES)
        uext[slab, 0:HALO, :] = jnp.where(step > 0, prev_ref[0, :, cols], 0.0)
        uext[slab, HALO:HALO + rows_in, :] = cur_ref[0, :, cols]
        uext[slab, HALO + rows_in:2 * HALO + rows_in, :] = jnp.where(step < nsteps - 1, next_ref[0, :, cols], 0.0)
        uext[slab, 2 * HALO + rows_in:, :] = jnp.zeros((uext.shape[1] - 2 * HALO - rows_in, LANES), F32)
        bias = jnp.broadcast_to(cb_ref[:, cols], (sub, LANES))
        taps = [jnp.broadcast_to(cw_ref[k:k + 1, cols], (sub, LANES)) for k in range(D_CONV)]
        for ch in range(BWD_CHUNKS):
            for i in range(CONV_STRIDE):
                acc = bias
                for k in range(D_CONV):
                    start = ch * CHUNK + first + k + i
                    acc = acc + taps[k] * uext[slab, pl.ds(start, sub, stride=CONV_STRIDE), :]
                cout[ch * nslab + slab, pl.ds(i, sub, stride=CONV_STRIDE), :] = _silu(acc)
            xc_ref[0, ch * CHUNK:(ch + 1) * CHUNK, cols] = cout[ch * nslab + slab, 0:CHUNK, :]

    _, upper = _tri_masks()
    upper_sel = jnp.where(upper, 1.0, 0.0).astype(BF16)
    for ch in reversed(range(BWD_CHUNKS)):
        rows = slice(ch * CHUNK, (ch + 1) * CHUNK)
        dtv, loga = _dt_and_loga(dt_ref.at[:, rows], dtb_ref, alog_ref)
        rcum = _dot_sel_lhs(upper_sel, loga)
        r0 = rcum[0:1, :]
        wgt = dtv * jnp.exp2(r0 - rcum)
        wgt_x = _dot(_pack3(wgt), eb_ref[...])
        dec_x = _dot(_pack3(jnp.broadcast_to(jnp.exp2(r0), (HALO, LANES))), eb_ref[...])[0:1]

        hb_ref[0, ch] = state[...].astype(BF16)
        xw = (xc_ref[0, rows, 0:D_SSM] * wgt_x).astype(BF16)
        for g in range(SSM_GROUPS):
            gs = slice(g * GROUP_W, (g + 1) * GROUP_W)
            b_t = xc_ref[0, rows, D_SSM + g * D_STATE:D_SSM + (g + 1) * D_STATE].T.astype(BF16)
            state[:, gs] = state[:, gs] * dec_x[:, gs] + _dot(b_t, xw[:, gs])


def _ssd_bwd(xbc, dt, conv_w, conv_b, dt_bias, a_log, sel_b):
    bsz, t, _ = xbc.shape
    rows = BWD_CHUNKS * CHUNK
    nsteps = t // rows
    hps = rows // HALO
    nhalo = t // HALO
    nslab = CONV_DIM // LANES
    rev = lambda c: nsteps - 1 - c
    return pl.pallas_call(
        _ssd_bwd_body,
        grid=(bsz, nsteps),
        in_specs=[
            pl.BlockSpec((1, rows, CONV_DIM), lambda b, c: (b, rev(c), 0)),
            pl.BlockSpec((1, HALO, CONV_DIM), lambda b, c: (b, jnp.maximum(rev(c) * hps - 1, 0), 0)),
            pl.BlockSpec((1, HALO, CONV_DIM), lambda b, c: (b, jnp.minimum((rev(c) + 1) * hps, nhalo - 1), 0)),
            pl.BlockSpec((1, rows, LANES), lambda b, c: (b, rev(c), 0)),
            _const_spec(conv_w.shape), _const_spec(conv_b.shape),
            _const_spec(dt_bias.shape), _const_spec(a_log.shape), _const_spec(sel_b.shape),
        ],
        out_specs=[
            pl.BlockSpec((1, rows, CONV_DIM), lambda b, c: (b, rev(c), 0)),
            pl.BlockSpec((1, BWD_CHUNKS, D_STATE, D_SSM), lambda b, c: (b, rev(c), 0, 0)),
        ],
        out_shape=[
            jax.ShapeDtypeStruct((bsz, t, CONV_DIM), F32),
            jax.ShapeDtypeStruct((bsz, t // CHUNK, D_STATE, D_SSM), BF16),
        ],
        scratch_shapes=[pltpu.VMEM((nslab, rows + 3 * HALO, LANES), F32),
                        pltpu.VMEM((BWD_CHUNKS * nslab, SUBLANES * CONV_STRIDE, LANES), F32),
                        pltpu.VMEM((D_STATE, D_SSM), F32)],
        compiler_params=pltpu.CompilerParams(
            dimension_semantics=("arbitrary", "arbitrary"), vmem_limit_bytes=VMEM_LIMIT),
    )(xbc, xbc, xbc, dt, conv_w, conv_b, dt_bias, a_log, sel_b)


def _ssd_fwd_body(xc_ref, dt_ref, z_ref, hb_ref, dtb_ref, alog_ref, dsk_ref, nw_ref, ef_ref, eb_ref,
                  y_ref, state, ybuf):
    @pl.when(pl.program_id(1) == 0)
    def _():
        state[...] = jnp.zeros_like(state)

    for sub in range(FWD_CHUNKS):
        rows = slice(sub * CHUNK, (sub + 1) * CHUNK)
        _ssd_fwd_chunk(xc_ref.at[0, rows], dt_ref.at[:, rows], z_ref.at[0, rows], hb_ref.at[0, sub], dtb_ref, alog_ref,
                       dsk_ref, nw_ref, ef_ref, eb_ref, y_ref.at[0, rows], state, ybuf.at[rows])


def _ssd_fwd_chunk(xc_ref, dt_ref, z_ref, hb_ref, dtb_ref, alog_ref, dsk_ref, nw_ref, ef_ref, eb_ref,
                   y_ref, state, ybuf):
    dtv, loga = _dt_and_loga(dt_ref, dtb_ref, alog_ref)
    lower, upper = _tri_masks()
    cum_f = _dot_sel_lhs(jnp.where(lower, 1.0, 0.0).astype(BF16), loga)
    cum_b = _dot_sel_lhs(jnp.where(upper, 1.0, 0.0).astype(BF16), loga)
    lane = lax.broadcasted_iota(jnp.int32, (CHUNK, LANES), 1)
    cum = jnp.where((lane & SSM_HEADS) == 0, cum_f, cum_b)
    ecum_p = _pack3(jnp.exp2(cum))
    wgt_p = _pack3(dtv * jnp.exp2(cum_f[CHUNK - 1:CHUNK, :] - cum_f))
    src_t = (cum - jnp.log2(dtv)).T
    first_half = lane < SSM_HEAD_DIM

    for g in range(SSM_GROUPS):
        gs = slice(g * GROUP_W, (g + 1) * GROUP_W)
        b_f32 = xc_ref[:, D_SSM + g * D_STATE:D_SSM + (g + 1) * D_STATE]
        b_g = b_f32.astype(BF16)
        c0 = D_SSM + SSM_GROUPS * D_STATE + g * D_STATE
        c_g = xc_ref[:, c0:c0 + D_STATE].astype(BF16)
        cb = _dot_nt(c_g, b_g)
        ea_g = _dot(ecum_p, ef_ref[:, gs])
        er_g = _dot(ecum_p, eb_ref[:, gs])
        y_off = _dot(c_g, state[:, gs].astype(BF16)) * ea_g + _dot(c_g, hb_ref[:, gs]) * er_g
        xw = (xc_ref[:, gs] * _dot(wgt_p, ef_ref[:, gs])).astype(BF16)
        state[:, gs] = state[:, gs] * ea_g[CHUNK - 1:CHUNK, :] + _dot(b_f32.T.astype(BF16), xw)
        pairs = GROUP_W // LANES
        for pp in range(pairs):
            pr = g * pairs + pp
            ps = slice(pr * LANES, (pr + 1) * LANES)
            mats = []
            for h in (2 * pr, 2 * pr + 1):
                hb = SSM_HEADS + h
                seg_f = jnp.broadcast_to(cum[:, h:h + 1], (CHUNK, CHUNK)) - src_t[h:h + 1, :]
                seg_b = jnp.broadcast_to(cum[:, hb:hb + 1], (CHUNK, CHUNK)) - src_t[hb:hb + 1, :]
                d_f = jnp.exp2(jnp.where(lower, seg_f, MASKED))
                d_b = jnp.exp2(jnp.where(upper, seg_b, MASKED))
                mats.append(((d_f + d_b) * cb).astype(BF16))
            xs = xc_ref[:, ps]
            xp = xs.astype(BF16)
            zero = jnp.zeros_like(xp)
            rhs = jnp.concatenate([jnp.where(first_half, xp, zero), jnp.where(first_half, zero, xp)], axis=0)
            y_diag = _dot(jnp.concatenate(mats, axis=1), rhs)
            ybuf[:, ps] = y_diag + y_off[:, pp * LANES:(pp + 1) * LANES] + dsk_ref[:, ps] * xs
        gated = ybuf[:, gs] * _silu(z_ref[:, gs])
        ms = jnp.mean(gated * gated, axis=-1, keepdims=True)
        y_ref[:, gs] = (gated * lax.rsqrt(ms + EPS) * nw_ref[:, gs]).astype(y_ref.dtype)


def _ssd_fwd(xc, dt, z, hb, dt_bias, a_log, d_skip_x, norm_w, sel_f, sel_b):
    bsz, t, _ = xc.shape
    step = FWD_CHUNKS * CHUNK
    blk = lambda w: pl.BlockSpec((1, step, w), lambda b, c: (b, c, 0))
    return pl.pallas_call(
        _ssd_fwd_body,
        grid=(bsz, t // step),
        in_specs=[
            blk(CONV_DIM), blk(LANES), blk(D_SSM),
            pl.BlockSpec((1, FWD_CHUNKS, D_STATE, D_SSM), lambda b, c: (b, c, 0, 0)),
            _const_spec(dt_bias.shape), _const_spec(a_log.shape), _const_spec(d_skip_x.shape),
            _const_spec(norm_w.shape), _const_spec(sel_f.shape), _const_spec(sel_b.shape),
        ],
        out_specs=blk(D_SSM),
        out_shape=jax.ShapeDtypeStruct((bsz, t, D_SSM), BF16),
        scratch_shapes=[pltpu.VMEM((D_STATE, D_SSM), F32), pltpu.VMEM((step, D_SSM), F32)],
        compiler_params=pltpu.CompilerParams(
            dimension_semantics=("arbitrary", "arbitrary"), vmem_limit_bytes=VMEM_LIMIT),
    )(xc, dt, z, hb, dt_bias, a_log, d_skip_x, norm_w, sel_f, sel_b)


def _build_bias(g_ref, colmask_ref, bias):
    rel = _block_row_rel()
    half = lax.broadcasted_iota(jnp.int32, (GRID_W, LANES), 1) < GRID_W
    nrel = 2 * NA_ROWS

    @functools.cache
    def toeplitz(hh, r, upper_half):
        if upper_half:
            return pltpu.roll(toeplitz(hh, r, False), GRID_W, 1)
        row = g_ref[0, hh * nrel + r:hh * nrel + r + 1, :] * LOG2E
        return pltpu.roll(jnp.broadcast_to(row, (GRID_W, LANES)), 0, 1, stride=1, stride_axis=0)

    for kind in range(rel.shape[0]):
        for hh in range(2):
            for u in range(0, NA_BLOCK, 2):
                lane0 = (hh * NA_BLOCK + u) * GRID_W
                for a in range(NA_SPAN):
                    r_lo, r_hi = int(rel[kind, u, a]), int(rel[kind, u + 1, a])
                    lo = toeplitz(hh, r_lo, False) if r_lo >= 0 else MASKED
                    hi = toeplitz(hh, r_hi, True) if r_hi >= 0 else MASKED
                    tile = jnp.where(half, lo, hi) + colmask_ref[...]
                    bias[kind, a * GRID_W:(a + 1) * GRID_W, lane0:lane0 + 2 * GRID_W] = tile


def _na_body(qt_ref, k_ref, vt_ref, g_ref, colmask_ref, o_ref, sbuf, bias):
    t = k_ref.shape[1]
    rows = t // GRID_W

    @pl.when(pl.program_id(1) == 0)
    def _():
        _build_bias(g_ref, colmask_ref, bias)

    nblk = rows // NA_BLOCK
    bq = NA_BLOCK * GRID_W

    def offsets(bi):
        i0 = bi * NA_BLOCK
        rlo = jnp.clip(i0 - NA_ROWS // 2, 0, rows - NA_SPAN)
        return pl.multiple_of(i0 * GRID_W, bq), pl.multiple_of(rlo * GRID_W, 2 * GRID_W)

    def scores(bi, slot):
        kind = jnp.where(bi == 0, 0, jnp.where(bi == nblk - 1, 2, 1))
        _, koff = offsets(bi)
        tile0 = bi * (bq // LANES)
        q_t = jnp.concatenate([qt_ref[tile0 + i] for i in range(bq // LANES)], axis=1)
        zero = jnp.zeros((NA_HEAD_DIM, bq), q_t.dtype)
        qs_t = jnp.concatenate([jnp.concatenate([q_t[:NA_HEAD_DIM], zero], axis=0),
                                jnp.concatenate([zero, q_t[NA_HEAD_DIM:]], axis=0)], axis=1)
        sbuf[slot] = _dot(k_ref[0, pl.ds(koff, NA_BKEYS), :], qs_t) + bias[kind]

    head_a_dims = lax.broadcasted_iota(jnp.int32, (LANES, bq), 0) < NA_HEAD_DIM

    def attend(bi, slot):
        qoff, koff = offsets(bi)
        s = sbuf[slot]
        p = jnp.exp2(s - jnp.max(s, axis=0, keepdims=True))
        denom = jnp.sum(p, axis=0, keepdims=True)
        tile0 = koff // LANES
        v_t = jnp.concatenate([vt_ref[tile0 + i] for i in range(NA_BKEYS // LANES)], axis=1)
        o_t = _dot(v_t, p.astype(BF16)) * (1.0 / denom)
        o_ref[0, pl.ds(qoff, bq), :] = jnp.where(head_a_dims, o_t[:, 0:bq], o_t[:, bq:]).T.astype(o_ref.dtype)

    scores(0, 0)

    def pair_step(i, carry):
        b0 = 2 * i
        scores(b0 + 1, 1)
        attend(b0, 0)
        scores(jnp.minimum(b0 + 2, nblk - 1), 0)
        attend(b0 + 1, 1)
        return carry

    lax.fori_loop(0, nblk // 2, pair_step, 0, unroll=4)


def _natten(qt, k, vt, g, colmask):
    bsz, t, _ = k.shape
    nq = 2 * NA_BLOCK * GRID_W
    blk = pl.BlockSpec((1, t, LANES), lambda p, b: (b, 0, p))
    tiles = pl.BlockSpec((t // LANES, LANES, LANES), lambda p, b: (b, p, 0))
    return pl.pallas_call(
        _na_body,
        grid=(NA_PAIRS, bsz),
        in_specs=[tiles, blk, tiles,
                  pl.BlockSpec((1,) + g.shape[1:], lambda p, b: (p, 0, 0)), _const_spec(colmask.shape)],
        out_specs=blk,
        out_shape=jax.ShapeDtypeStruct((bsz, t, D_NA), BF16),
        scratch_shapes=[pltpu.VMEM((2, NA_BKEYS, nq), F32), pltpu.VMEM((3, NA_BKEYS, nq), F32)],
        compiler_params=pltpu.CompilerParams(
            dimension_semantics=("arbitrary", "arbitrary"), vmem_limit_bytes=VMEM_LIMIT),
    )(qt, k, vt, g, colmask)


def _block_row_rel():
    rel = -np.ones((3, NA_BLOCK, NA_SPAN), np.int64)
    for u in range(NA_BLOCK):
        for a in range(NA_SPAN):
            if a < NA_ROWS:
                rel[0, u, a] = a - u + NA_ROWS - 1
            if u <= a < u + NA_ROWS:
                rel[1, u, a] = a - u + NA_ROWS - 1 - NA_ROWS // 2
            if a >= NA_SPAN - NA_ROWS:
                rel[2, u, a] = a - u + NA_BLOCK - NA_SPAN + NA_ROWS - 1
    return rel


def _na_bias_rows(rpb):
    centre = NA_COLS - 1
    rev = rpb[:, :, ::-1]
    rows = jnp.concatenate(
        [rev[:, :, centre:], jnp.zeros(rpb.shape[:2] + (LANES - rpb.shape[2],), rpb.dtype), rev[:, :, :centre]],
        axis=2).astype(F32)
    rows = jnp.pad(rows, ((0, 0), (0, 2 * NA_ROWS - rpb.shape[1]), (0, 0)))
    return rows.reshape(NA_PAIRS, 2 * 2 * NA_ROWS, LANES)


def _na_col_mask():
    j = np.arange(GRID_W)
    c0 = np.clip(j - NA_COLS // 2, 0, GRID_W - NA_COLS)
    cc = np.arange(GRID_W)
    valid = (cc[:, None] >= c0[None, :]) & (cc[:, None] < c0[None, :] + NA_COLS)
    return jnp.asarray(np.tile(np.where(valid, 0.0, MASKED), (1, 2)), F32)


FF_STEP = 512


def _outmlp_body(x_ref, ys_ref, yn_ref, wo_ref, nw_ref, w1_ref, w2_ref, o_ref):
    dmix = ys_ref.shape[-1]
    x1 = x_ref[...] + _dot(ys_ref[...], wo_ref[0:dmix, :]) + _dot(yn_ref[...], wo_ref[dmix:, :])
    ms = jnp.mean(x1 * x1, axis=-1, keepdims=True)
    h = (x1 * lax.rsqrt(ms + EPS) * nw_ref[...]).astype(BF16)
    o_ref[...] = x1
    for f in range(0, w1_ref.shape[1], FF_STEP):
        u = jnp.maximum(_dot(h, w1_ref[:, f:f + FF_STEP]), 0.0)
        o_ref[...] += _dot((u * u).astype(BF16), w2_ref[f:f + FF_STEP, :])


def _outmlp(x2, ys, yn, w_out, norm_w, w1, w2, tm):
    n, d = x2.shape
    row = lambda i: (i, 0)
    return pl.pallas_call(
        _outmlp_body,
        grid=(n // tm,),
        in_specs=[
            pl.BlockSpec((tm, d), row), pl.BlockSpec((tm, ys.shape[1]), row), pl.BlockSpec((tm, yn.shape[1]), row),
            _const_spec(w_out.shape), _const_spec((1, d)), _const_spec(w1.shape), _const_spec(w2.shape),
        ],
        out_specs=pl.BlockSpec((tm, d), row),
        out_shape=jax.ShapeDtypeStruct((n, d), F32),
        compiler_params=pltpu.CompilerParams(
            dimension_semantics=("arbitrary",), vmem_limit_bytes=VMEM_LIMIT),
    )(x2, ys, yn, w_out, norm_w, w1, w2)


def _head_select(first_lane):
    sel = np.zeros((LANES, D_SSM), np.float32)
    for rep in range(DT_COPIES):
        for h in range(SSM_HEADS):
            sel[rep * DT_LANES + first_lane + h, h * SSM_HEAD_DIM:(h + 1) * SSM_HEAD_DIM] = 1.0
    return jnp.asarray(sel, BF16)


def _dt_lanes(fwd, bwd):
    v = jnp.tile(jnp.concatenate([fwd, bwd]).astype(F32), DT_COPIES)
    return jnp.pad(v, (0, LANES - v.shape[0]))[None, :]


def _layer(x, norm_mix_w, w_in, conv_w, conv_b, dt_bias_fwd, dt_bias_bwd, a_log_fwd, a_log_bwd, d_skip,
           ssm_norm_w, q_norm_w, k_norm_w, rel_pos_bias, w_out, norm_mlp_w, w_mlp_in, w_mlp_out, tm):
    bsz, t, d = x.shape
    n = bsz * t
    x2 = x.reshape(n, d)
    rows = t // GRID_W
    assert n % tm == 0 and tm % LANES == 0, (n, tm)
    assert t % (max(FWD_CHUNKS, BWD_CHUNKS) * CHUNK) == 0, t
    assert t % GRID_W == 0 and rows % (2 * NA_BLOCK) == 0 and rows >= NA_SPAN, t
    assert w_in.shape == (d, O_V + D_NA), w_in.shape

    (z, xbc, qt, k, vt, dt), (w_out_bf, w1_bf, w2_bf) = _inproj(
        x2, norm_mix_w[None, :], w_in.T.astype(BF16), q_norm_w.astype(F32) * (NA_HEAD_DIM ** -0.5 * LOG2E), k_norm_w,
        (w_out, w_mlp_in, w_mlp_out), tm)
    r3 = lambda a: a.reshape(bsz, t, a.shape[-1])

    dt_bias = _dt_lanes(dt_bias_fwd, dt_bias_bwd)
    a_log = _dt_lanes(a_log_fwd, a_log_bwd)
    sel_f, sel_b = _head_select(0), _head_select(SSM_HEADS)
    xc, hb = _ssd_bwd(r3(xbc), r3(dt), conv_w, conv_b[None, :], dt_bias, a_log, sel_b)
    y_ssm = _ssd_fwd(xc, r3(dt), r3(z), hb, dt_bias, a_log,
                     jnp.repeat(d_skip.astype(F32), SSM_HEAD_DIM)[None, :], ssm_norm_w[None, :], sel_f, sel_b)

    y_na = _natten(qt, r3(k), vt, _na_bias_rows(rel_pos_bias), _na_col_mask())

    out = _outmlp(x2, y_ssm.reshape(n, D_SSM), y_na.reshape(n, D_NA), w_out_bf, norm_mlp_w[None, :], w1_bf, w2_bf, tm)
    return out.reshape(bsz, t, d)


def kernel(x, norm_mix_w, w_in, conv_w, conv_b, dt_bias_fwd, dt_bias_bwd, a_log_fwd, a_log_bwd, d_skip,
           ssm_norm_w, q_norm_w, k_norm_w, rel_pos_bias, w_out, norm_mlp_w, w_mlp_in, w_mlp_out):
    tm = min(512, x.shape[0] * x.shape[1])
    for layer in range(norm_mix_w.shape[0]):
        x = _layer(x, norm_mix_w[layer], w_in[layer], conv_w[layer], conv_b[layer], dt_bias_fwd[layer],
                   dt_bias_bwd[layer], a_log_fwd[layer], a_log_bwd[layer], d_skip[layer], ssm_norm_w[layer],
                   q_norm_w[layer], k_norm_w[layer], rel_pos_bias[layer], w_out[layer], norm_mlp_w[layer],
                   w_mlp_in[layer], w_mlp_out[layer], tm)
    return x
```

```python
import functools

import numpy as np
import jax
import jax.numpy as jnp
from jax import lax
from jax.experimental import pallas as pl
from jax.experimental.pallas import tpu as pltpu

F32 = jnp.float32
BF16 = jnp.bfloat16

LANES = 128
SUBLANES = 8
GRID_W = 64
SSM_HEADS = 16
SSM_HEAD_DIM = 64
D_SSM = SSM_HEADS * SSM_HEAD_DIM
SSM_GROUPS = 2
GROUP_W = D_SSM // SSM_GROUPS
D_STATE = 128
D_CONV = 5
CHUNK = 128
FWD_CHUNKS = 4
BWD_CHUNKS = 4
CONV_DIM = D_SSM + 2 * SSM_GROUPS * D_STATE
DT_LANES = 2 * SSM_HEADS
DT_COPIES = 3
NA_HEADS = 16
NA_HEAD_DIM = 64
D_NA = NA_HEADS * NA_HEAD_DIM
NA_ROWS = 8
NA_COLS = 16
NA_PAIRS = NA_HEADS // 2
NA_BLOCK = 4
NA_SPAN = NA_ROWS + NA_BLOCK
NA_BKEYS = NA_SPAN * GRID_W
LOG2E = 1.4426950408889634
EPS = 1e-5
MASKED = -1e30
HALO = SUBLANES
CONV_STRIDE = CHUNK // SUBLANES + 1
VMEM_LIMIT = 56 * 1024 * 1024


def _dot(a, b):
    return jnp.dot(a, b, preferred_element_type=F32)


def _dot_nt(a, b):
    return lax.dot_general(a, b, (((1,), (1,)), ((), ())), preferred_element_type=F32)


def _split3(x):
    hi = x.astype(BF16)
    r1 = x - hi.astype(F32)
    mid = r1.astype(BF16)
    lo = (r1 - mid.astype(F32)).astype(BF16)
    return hi, mid, lo


def _dot_sel_lhs(sel, x):
    hi, mid, lo = _split3(x)
    return _dot(sel, hi) + _dot(sel, mid) + _dot(sel, lo)


def _pack3(x):
    lane = lax.broadcasted_iota(jnp.int32, x.shape, x.ndim - 1)
    hi = x.astype(BF16).astype(F32)
    r1 = x - hi
    mid = r1.astype(BF16).astype(F32)
    return jnp.where(lane < DT_LANES, hi, jnp.where(lane < 2 * DT_LANES, mid, r1 - mid)).astype(BF16)


def _softplus(x):
    return jnp.maximum(x, 0.0) + jnp.log(1.0 + jnp.exp(-jnp.abs(x)))


def _silu(x):
    return x * (1.0 / (1.0 + jnp.exp(-x)))


def _const_spec(shape):
    nd = len(shape)
    return pl.BlockSpec(shape, lambda *_: (0,) * nd, pipeline_mode=pl.Buffered(1))


O_DT = D_SSM + CONV_DIM
O_Q = O_DT + DT_LANES
O_K = O_Q + D_NA
O_V = O_K + D_NA


def _head_rmsnorm_t(xt, gain_ref):
    tokens = xt.shape[1]
    x3 = xt.reshape(NA_HEADS, NA_HEAD_DIM, tokens)
    ms = jnp.mean(x3 * x3, axis=1, keepdims=True)
    return (x3 * lax.rsqrt(ms + EPS) * gain_ref[...]).reshape(D_NA, tokens)


def _inproj_body(x_ref, nw_ref, wt_ref, qg_ref, kg_ref, *refs):
    ncast = (len(refs) - 6) // 2
    cast_in, (z_ref, xbc_ref, qt_ref, k_ref, vt_ref, dt_ref), cast_out = refs[:ncast], refs[ncast:ncast + 6], refs[ncast + 6:]
    for src, dst in zip(cast_in, cast_out):
        dst[...] = src[...].astype(dst.dtype)
    x = x_ref[...]
    ms = jnp.mean(x * x, axis=-1, keepdims=True)
    h = (x * lax.rsqrt(ms + EPS) * nw_ref[...]).astype(BF16)
    z_ref[...] = _silu(_dot_nt(h, wt_ref[0:D_SSM, :]))
    xbc_ref[...] = _dot_nt(h, wt_ref[D_SSM:O_DT, :])
    w_dt = wt_ref[O_DT:O_Q, :]
    pad = jnp.zeros((LANES - DT_COPIES * DT_LANES, w_dt.shape[1]), w_dt.dtype)
    dt_ref[...] = _dot_nt(h, jnp.concatenate([w_dt] * DT_COPIES + [pad], axis=0))
    qt = _head_rmsnorm_t(_dot_nt(wt_ref[O_Q:O_K, :], h), qg_ref)
    k_ref[...] = _head_rmsnorm_t(_dot_nt(wt_ref[O_K:O_V, :], h), kg_ref).T.astype(k_ref.dtype)
    vt = _dot_nt(wt_ref[O_V:, :], h)
    for j in range(vt_ref.shape[0]):
        cols = slice(j * LANES, (j + 1) * LANES)
        qt_ref[j] = qt[:, cols].astype(qt_ref.dtype)
        vt_ref[j] = vt[:, cols].astype(vt_ref.dtype)


def _inproj(x2, norm_w, w_bf, q_gain, k_gain, later_weights, tm):
    n, d = x2.shape
    steps = n // tm
    row = lambda i: (i, 0)
    tiles = tm // LANES
    tile_spec = pl.BlockSpec((tiles, D_NA, LANES), lambda i: (i, 0, 0))
    tile_shape = jax.ShapeDtypeStruct((n // LANES, D_NA, LANES), BF16)
    bf16_rows = 2 * SUBLANES
    assert all(w.shape[0] % (steps * bf16_rows) == 0 for w in later_weights)
    slabs = [pl.BlockSpec((w.shape[0] // steps, w.shape[1]), row) for w in later_weights]
    gains = [jnp.broadcast_to(g.astype(F32)[:, None], (NA_HEAD_DIM, tm)) for g in (q_gain, k_gain)]
    outs = pl.pallas_call(
        _inproj_body,
        grid=(steps,),
        in_specs=[pl.BlockSpec((tm, d), row), _const_spec((1, d)), _const_spec(w_bf.shape),
                  _const_spec(gains[0].shape), _const_spec(gains[1].shape)] + slabs,
        out_specs=[pl.BlockSpec((tm, D_SSM), row), pl.BlockSpec((tm, CONV_DIM), row), tile_spec,
                   pl.BlockSpec((tm, D_NA), row), tile_spec, pl.BlockSpec((tm, LANES), row)] + slabs,
        out_shape=[jax.ShapeDtypeStruct((n, D_SSM), F32), jax.ShapeDtypeStruct((n, CONV_DIM), F32), tile_shape,
                   jax.ShapeDtypeStruct((n, D_NA), BF16), tile_shape, jax.ShapeDtypeStruct((n, LANES), F32)]
        + [jax.ShapeDtypeStruct(w.shape, BF16) for w in later_weights],
        compiler_params=pltpu.CompilerParams(
            dimension_semantics=("arbitrary",), vmem_limit_bytes=VMEM_LIMIT),
    )(x2, norm_w, w_bf, *gains, *later_weights)
    return outs[:6], outs[6:]


def _tri_masks():
    t = lax.broadcasted_iota(jnp.int32, (CHUNK, CHUNK), 0)
    s = lax.broadcasted_iota(jnp.int32, (CHUNK, CHUNK), 1)
    return s <= t, s >= t


def _dt_and_loga(dt_ref, dtb_ref, alog_ref):
    dtv = _softplus(dt_ref[0] + dtb_ref[...])
    return dtv, dtv * (-LOG2E * jnp.exp(alog_ref[...]))


def _ssd_bwd_body(cur_ref, prev_ref, next_ref, dt_ref, cw_ref, cb_ref, dtb_ref, alog_ref, eb_ref,
                  xc_ref, hb_ref, uext, cout, state):
    c = pl.program_id(1)
    nsteps = pl.num_programs(1)
    step = nsteps - 1 - c
    rows_in = BWD_CHUNKS * CHUNK

    @pl.when(c == 0)
    def _():
        state[...] = jnp.zeros_like(state)

    first = HALO - (D_CONV - 1) // 2
    sub = SUBLANES
    nslab = CONV_DIM // LANES
    for slab in range(nslab):
        cols = slice(slab * LANES, (slab + 1) * LANES)
        uext[slab, 0:HALO, :] = jnp.where(step > 0, prev_ref[0, :, cols], 0.0)
        uext[slab, HALO:HALO + rows_in, :] = cur_ref[0, :, cols]
        uext[slab, HALO + rows_in:2 * HALO + rows_in, :] = jnp.where(step < nsteps - 1, next_ref[0, :, cols], 0.0)
        uext[slab, 2 * HALO + rows_in:, :] = jnp.zeros((uext.shape[1] - 2 * HALO - rows_in, LANES), F32)
        bias = jnp.broadcast_to(cb_ref[:, cols], (sub, LANES))
        taps = [jnp.broadcast_to(cw_ref[k:k + 1, cols], (sub, LANES)) for k in range(D_CONV)]
        for ch in range(BWD_CHUNKS):
            for i in range(CONV_STRIDE):
                acc = bias
                for k in range(D_CONV):
                    start = ch * CHUNK + first + k + i
                    acc = acc + taps[k] * uext[slab, pl.ds(start, sub, stride=CONV_STRIDE), :]
                cout[ch * nslab + slab, pl.ds(i, sub, stride=CONV_STRIDE), :] = _silu(acc)
            xc_ref[0, ch * CHUNK:(ch + 1) * CHUNK, cols] = cout[ch * nslab + slab, 0:CHUNK, :]

    _, upper = _tri_masks()
    upper_sel = jnp.where(upper, 1.0, 0.0).astype(BF16)
    for ch in reversed(range(BWD_CHUNKS)):
        rows = slice(ch * CHUNK, (ch + 1) * CHUNK)
        dtv, loga = _dt_and_loga(dt_ref.at[:, rows], dtb_ref, alog_ref)
        rcum = _dot_sel_lhs(upper_sel, loga)
        r0 = rcum[0:1, :]
        wgt = dtv * jnp.exp2(r0 - rcum)
        wgt_x = _dot(_pack3(wgt), eb_ref[...])
        dec_x = _dot(_pack3(jnp.broadcast_to(jnp.exp2(r0), (HALO, LANES))), eb_ref[...])[0:1]

        hb_ref[0, ch] = state[...].astype(BF16)
        xw = (xc_ref[0, rows, 0:D_SSM] * wgt_x).astype(BF16)
        for g in range(SSM_GROUPS):
            gs = slice(g * GROUP_W, (g + 1) * GROUP_W)
            b_t = xc_ref[0, rows, D_SSM + g * D_STATE:D_SSM + (g + 1) * D_STATE].T.astype(BF16)
            state[:, gs] = state[:, gs] * dec_x[:, gs] + _dot(b_t, xw[:, gs])


def _ssd_bwd(xbc, dt, conv_w, conv_b, dt_bias, a_log, sel_b):
    bsz, t, _ = xbc.shape
    rows = BWD_CHUNKS * CHUNK
    nsteps = t // rows
    hps = rows // HALO
    nhalo = t // HALO
    nslab = CONV_DIM // LANES
    rev = lambda c: nsteps - 1 - c
    return pl.pallas_call(
        _ssd_bwd_body,
        grid=(bsz, nsteps),
        in_specs=[
            pl.BlockSpec((1, rows, CONV_DIM), lambda b, c: (b, rev(c), 0)),
            pl.BlockSpec((1, HALO, CONV_DIM), lambda b, c: (b, jnp.maximum(rev(c) * hps - 1, 0), 0)),
            pl.BlockSpec((1, HALO, CONV_DIM), lambda b, c: (b, jnp.minimum((rev(c) + 1) * hps, nhalo - 1), 0)),
            pl.BlockSpec((1, rows, LANES), lambda b, c: (b, rev(c), 0)),
            _const_spec(conv_w.shape), _const_spec(conv_b.shape),
            _const_spec(dt_bias.shape), _const_spec(a_log.shape), _const_spec(sel_b.shape),
        ],
        out_specs=[
            pl.BlockSpec((1, rows, CONV_DIM), lambda b, c: (b, rev(c), 0)),
            pl.BlockSpec((1, BWD_CHUNKS, D_STATE, D_SSM), lambda b, c: (b, rev(c), 0, 0)),
        ],
        out_shape=[
            jax.ShapeDtypeStruct((bsz, t, CONV_DIM), F32),
            jax.ShapeDtypeStruct((bsz, t // CHUNK, D_STATE, D_SSM), BF16),
        ],
        scratch_shapes=[pltpu.VMEM((nslab, rows + 3 * HALO, LANES), F32),
                        pltpu.VMEM((BWD_CHUNKS * nslab, SUBLANES * CONV_STRIDE, LANES), F32),
                        pltpu.VMEM((D_STATE, D_SSM), F32)],
        compiler_params=pltpu.CompilerParams(
            dimension_semantics=("arbitrary", "arbitrary"), vmem_limit_bytes=VMEM_LIMIT),
    )(xbc, xbc, xbc, dt, conv_w, conv_b, dt_bias, a_log, sel_b)


def _ssd_fwd_body(xc_ref, dt_ref, hb_ref, dtb_ref, alog_ref, dsk_ref, ef_ref, eb_ref, y_ref, state):
    @pl.when(pl.program_id(1) == 0)
    def _():
        state[...] = jnp.zeros_like(state)

    for sub in range(FWD_CHUNKS):
        rows = slice(sub * CHUNK, (sub + 1) * CHUNK)
        _ssd_fwd_chunk(xc_ref.at[0, rows], dt_ref.at[:, rows], hb_ref.at[0, sub], dtb_ref, alog_ref,
                       dsk_ref, ef_ref, eb_ref, y_ref.at[0, rows], state)


def _ssd_fwd_chunk(xc_ref, dt_ref, hb_ref, dtb_ref, alog_ref, dsk_ref, ef_ref, eb_ref, y_ref, state):
    dtv, loga = _dt_and_loga(dt_ref, dtb_ref, alog_ref)
    lower, upper = _tri_masks()
    cum_f = _dot_sel_lhs(jnp.where(lower, 1.0, 0.0).astype(BF16), loga)
    cum_b = _dot_sel_lhs(jnp.where(upper, 1.0, 0.0).astype(BF16), loga)
    lane = lax.broadcasted_iota(jnp.int32, (CHUNK, LANES), 1)
    cum = jnp.where((lane & SSM_HEADS) == 0, cum_f, cum_b)
    ecum_p = _pack3(jnp.exp2(cum))
    wgt_p = _pack3(dtv * jnp.exp2(cum_f[CHUNK - 1:CHUNK, :] - cum_f))
    src_t = (cum - jnp.log2(dtv)).T
    first_half = lane < SSM_HEAD_DIM

    for g in range(SSM_GROUPS):
        gs = slice(g * GROUP_W, (g + 1) * GROUP_W)
        b_f32 = xc_ref[:, D_SSM + g * D_STATE:D_SSM + (g + 1) * D_STATE]
        b_g = b_f32.astype(BF16)
        c0 = D_SSM + SSM_GROUPS * D_STATE + g * D_STATE
        c_g = xc_ref[:, c0:c0 + D_STATE].astype(BF16)
        cb = _dot_nt(c_g, b_g)
        ea_g = _dot(ecum_p, ef_ref[:, gs])
        er_g = _dot(ecum_p, eb_ref[:, gs])
        y_off = _dot(c_g, state[:, gs].astype(BF16)) * ea_g + _dot(c_g, hb_ref[:, gs]) * er_g
        xw = (xc_ref[:, gs] * _dot(wgt_p, ef_ref[:, gs])).astype(BF16)
        state[:, gs] = state[:, gs] * ea_g[CHUNK - 1:CHUNK, :] + _dot(b_f32.T.astype(BF16), xw)
        pairs = GROUP_W // LANES
        for pp in range(pairs):
            pr = g * pairs + pp
            ps = slice(pr * LANES, (pr + 1) * LANES)
            mats = []
            for h in (2 * pr, 2 * pr + 1):
                hb = SSM_HEADS + h
                seg_f = jnp.broadcast_to(cum[:, h:h + 1], (CHUNK, CHUNK)) - src_t[h:h + 1, :]
                seg_b = jnp.broadcast_to(cum[:, hb:hb + 1], (CHUNK, CHUNK)) - src_t[hb:hb + 1, :]
                d_f = jnp.exp2(jnp.where(lower, seg_f, MASKED))
                d_b = jnp.exp2(jnp.where(upper, seg_b, MASKED))
                mats.append(((d_f + d_b) * cb).astype(BF16))
            xs = xc_ref[:, ps]
            xp = xs.astype(BF16)
            zero = jnp.zeros_like(xp)
            rhs = jnp.concatenate([jnp.where(first_half, xp, zero), jnp.where(first_half, zero, xp)], axis=0)
            y_diag = _dot(jnp.concatenate(mats, axis=1), rhs)
            y_ref[:, ps] = y_diag + y_off[:, pp * LANES:(pp + 1) * LANES] + dsk_ref[:, ps] * xs


def _ssd_fwd(xc, dt, hb, dt_bias, a_log, d_skip_x, sel_f, sel_b):
    bsz, t, _ = xc.shape
    step = FWD_CHUNKS * CHUNK
    blk = lambda w: pl.BlockSpec((1, step, w), lambda b, c: (b, c, 0))
    return pl.pallas_call(
        _ssd_fwd_body,
        grid=(bsz, t // step),
        in_specs=[
            blk(CONV_DIM), blk(LANES),
            pl.BlockSpec((1, FWD_CHUNKS, D_STATE, D_SSM), lambda b, c: (b, c, 0, 0)),
            _const_spec(dt_bias.shape), _const_spec(a_log.shape), _const_spec(d_skip_x.shape),
            _const_spec(sel_f.shape), _const_spec(sel_b.shape),
        ],
        out_specs=blk(D_SSM),
        out_shape=jax.ShapeDtypeStruct((bsz, t, D_SSM), F32),
        scratch_shapes=[pltpu.VMEM((D_STATE, D_SSM), F32)],
        compiler_params=pltpu.CompilerParams(
            dimension_semantics=("arbitrary", "arbitrary"), vmem_limit_bytes=VMEM_LIMIT),
    )(xc, dt, hb, dt_bias, a_log, d_skip_x, sel_f, sel_b)


def _build_bias(g_ref, colmask_ref, bias):
    rel = _block_row_rel()
    half = lax.broadcasted_iota(jnp.int32, (GRID_W, LANES), 1) < GRID_W
    nrel = 2 * NA_ROWS

    @functools.cache
    def toeplitz(hh, r, upper_half):
        if upper_half:
            return pltpu.roll(toeplitz(hh, r, False), GRID_W, 1)
        row = g_ref[0, hh * nrel + r:hh * nrel + r + 1, :] * LOG2E
        return pltpu.roll(jnp.broadcast_to(row, (GRID_W, LANES)), 0, 1, stride=1, stride_axis=0)

    for kind in range(rel.shape[0]):
        for hh in range(2):
            for u in range(0, NA_BLOCK, 2):
                lane0 = (hh * NA_BLOCK + u) * GRID_W
                for a in range(NA_SPAN):
                    r_lo, r_hi = int(rel[kind, u, a]), int(rel[kind, u + 1, a])
                    lo = toeplitz(hh, r_lo, False) if r_lo >= 0 else MASKED
                    hi = toeplitz(hh, r_hi, True) if r_hi >= 0 else MASKED
                    tile = jnp.where(half, lo, hi) + colmask_ref[...]
                    bias[kind, a * GRID_W:(a + 1) * GRID_W, lane0:lane0 + 2 * GRID_W] = tile


def _na_body(qt_ref, k_ref, vt_ref, g_ref, colmask_ref, o_ref, sbuf, bias):
    t = k_ref.shape[1]
    rows = t // GRID_W

    @pl.when(pl.program_id(1) == 0)
    def _():
        _build_bias(g_ref, colmask_ref, bias)

    nblk = rows // NA_BLOCK
    bq = NA_BLOCK * GRID_W

    def offsets(bi):
        i0 = bi * NA_BLOCK
        rlo = jnp.clip(i0 - NA_ROWS // 2, 0, rows - NA_SPAN)
        return pl.multiple_of(i0 * GRID_W, bq), pl.multiple_of(rlo * GRID_W, 2 * GRID_W)

    def scores(bi, slot):
        kind = jnp.where(bi == 0, 0, jnp.where(bi == nblk - 1, 2, 1))
        _, koff = offsets(bi)
        tile0 = bi * (bq // LANES)
        q_t = jnp.concatenate([qt_ref[tile0 + i] for i in range(bq // LANES)], axis=1)
        zero = jnp.zeros((NA_HEAD_DIM, bq), q_t.dtype)
        qs_t = jnp.concatenate([jnp.concatenate([q_t[:NA_HEAD_DIM], zero], axis=0),
                                jnp.concatenate([zero, q_t[NA_HEAD_DIM:]], axis=0)], axis=1)
        sbuf[slot] = _dot(k_ref[0, pl.ds(koff, NA_BKEYS), :], qs_t) + bias[kind]

    head_a_dims = lax.broadcasted_iota(jnp.int32, (LANES, bq), 0) < NA_HEAD_DIM

    def attend(bi, slot):
        qoff, koff = offsets(bi)
        s = sbuf[slot]
        p = jnp.exp2(s - jnp.max(s, axis=0, keepdims=True))
        denom = jnp.sum(p, axis=0, keepdims=True)
        tile0 = koff // LANES
        v_t = jnp.concatenate([vt_ref[tile0 + i] for i in range(NA_BKEYS // LANES)], axis=1)
        o_t = _dot(v_t, p.astype(BF16)) * (1.0 / denom)
        o_ref[0, pl.ds(qoff, bq), :] = jnp.where(head_a_dims, o_t[:, 0:bq], o_t[:, bq:]).T.astype(o_ref.dtype)

    scores(0, 0)

    def pair_step(i, carry):
        b0 = 2 * i
        scores(b0 + 1, 1)
        attend(b0, 0)
        scores(jnp.minimum(b0 + 2, nblk - 1), 0)
        attend(b0 + 1, 1)
        return carry

    lax.fori_loop(0, nblk // 2, pair_step, 0, unroll=4)


def _natten(qt, k, vt, g, colmask):
    bsz, t, _ = k.shape
    nq = 2 * NA_BLOCK * GRID_W
    blk = pl.BlockSpec((1, t, LANES), lambda p, b: (b, 0, p))
    tiles = pl.BlockSpec((t // LANES, LANES, LANES), lambda p, b: (b, p, 0))
    return pl.pallas_call(
        _na_body,
        grid=(NA_PAIRS, bsz),
        in_specs=[tiles, blk, tiles,
                  pl.BlockSpec((1,) + g.shape[1:], lambda p, b: (p, 0, 0)), _const_spec(colmask.shape)],
        out_specs=blk,
        out_shape=jax.ShapeDtypeStruct((bsz, t, D_NA), BF16),
        scratch_shapes=[pltpu.VMEM((2, NA_BKEYS, nq), F32), pltpu.VMEM((3, NA_BKEYS, nq), F32)],
        compiler_params=pltpu.CompilerParams(
            dimension_semantics=("arbitrary", "arbitrary"), vmem_limit_bytes=VMEM_LIMIT),
    )(qt, k, vt, g, colmask)


def _block_row_rel():
    rel = -np.ones((3, NA_BLOCK, NA_SPAN), np.int64)
    for u in range(NA_BLOCK):
        for a in range(NA_SPAN):
            if a < NA_ROWS:
                rel[0, u, a] = a - u + NA_ROWS - 1
            if u <= a < u + NA_ROWS:
                rel[1, u, a] = a - u + NA_ROWS - 1 - NA_ROWS // 2
            if a >= NA_SPAN - NA_ROWS:
                rel[2, u, a] = a - u + NA_BLOCK - NA_SPAN + NA_ROWS - 1
    return rel


def _na_bias_rows(rpb):
    centre = NA_COLS - 1
    rev = rpb[:, :, ::-1]
    rows = jnp.concatenate(
        [rev[:, :, centre:], jnp.zeros(rpb.shape[:2] + (LANES - rpb.shape[2],), rpb.dtype), rev[:, :, :centre]],
        axis=2).astype(F32)
    rows = jnp.pad(rows, ((0, 0), (0, 2 * NA_ROWS - rpb.shape[1]), (0, 0)))
    return rows.reshape(NA_PAIRS, 2 * 2 * NA_ROWS, LANES)


def _na_col_mask():
    j = np.arange(GRID_W)
    c0 = np.clip(j - NA_COLS // 2, 0, GRID_W - NA_COLS)
    cc = np.arange(GRID_W)
    valid = (cc[:, None] >= c0[None, :]) & (cc[:, None] < c0[None, :] + NA_COLS)
    return jnp.asarray(np.tile(np.where(valid, 0.0, MASKED), (1, 2)), F32)


FF_STEP = 512


def _outmlp_body(x_ref, ys_ref, gate_ref, gw_ref, yn_ref, wo_ref, nw_ref, w1_ref, w2_ref, o_ref):
    dmix = ys_ref.shape[-1]
    groups = []
    for g in range(SSM_GROUPS):
        gs = slice(g * GROUP_W, (g + 1) * GROUP_W)
        gated = ys_ref[:, gs] * gate_ref[:, gs]
        gms = jnp.mean(gated * gated, axis=-1, keepdims=True)
        groups.append((gated * lax.rsqrt(gms + EPS) * gw_ref[:, gs]).astype(BF16))
    y_ssm = jnp.concatenate(groups, axis=1)
    x1 = x_ref[...] + _dot(y_ssm, wo_ref[0:dmix, :]) + _dot(yn_ref[...], wo_ref[dmix:, :])
    ms = jnp.mean(x1 * x1, axis=-1, keepdims=True)
    h = (x1 * lax.rsqrt(ms + EPS) * nw_ref[...]).astype(BF16)
    o_ref[...] = x1
    for f in range(0, w1_ref.shape[1], FF_STEP):
        u = jnp.maximum(_dot(h, w1_ref[:, f:f + FF_STEP]), 0.0)
        o_ref[...] += _dot((u * u).astype(BF16), w2_ref[f:f + FF_STEP, :])


def _outmlp(x2, ys, gate, gate_norm_w, yn, w_out, norm_w, w1, w2, tm):
    n, d = x2.shape
    row = lambda i: (i, 0)
    return pl.pallas_call(
        _outmlp_body,
        grid=(n // tm,),
        in_specs=[
            pl.BlockSpec((tm, d), row), pl.BlockSpec((tm, ys.shape[1]), row), pl.BlockSpec((tm, gate.shape[1]), row),
            _const_spec(gate_norm_w.shape), pl.BlockSpec((tm, yn.shape[1]), row),
            _const_spec(w_out.shape), _const_spec((1, d)), _const_spec(w1.shape), _const_spec(w2.shape),
        ],
        out_specs=pl.BlockSpec((tm, d), row),
        out_shape=jax.ShapeDtypeStruct((n, d), F32),
        compiler_params=pltpu.CompilerParams(
            dimension_semantics=("arbitrary",), vmem_limit_bytes=VMEM_LIMIT),
    )(x2, ys, gate, gate_norm_w, yn, w_out, norm_w, w1, w2)


def _head_select(first_lane):
    sel = np.zeros((LANES, D_SSM), np.float32)
    for rep in range(DT_COPIES):
        for h in range(SSM_HEADS):
            sel[rep * DT_LANES + first_lane + h, h * SSM_HEAD_DIM:(h + 1) * SSM_HEAD_DIM] = 1.0
    return jnp.asarray(sel, BF16)


def _dt_lanes(fwd, bwd):
    v = jnp.tile(jnp.concatenate([fwd, bwd]).astype(F32), DT_COPIES)
    return jnp.pad(v, (0, LANES - v.shape[0]))[None, :]


def _layer(x, norm_mix_w, w_in, conv_w, conv_b, dt_bias_fwd, dt_bias_bwd, a_log_fwd, a_log_bwd, d_skip,
           ssm_norm_w, q_norm_w, k_norm_w, rel_pos_bias, w_out, norm_mlp_w, w_mlp_in, w_mlp_out, tm):
    bsz, t, d = x.shape
    n = bsz * t
    x2 = x.reshape(n, d)
    rows = t // GRID_W
    assert n % tm == 0 and tm % LANES == 0, (n, tm)
    assert t % (max(FWD_CHUNKS, BWD_CHUNKS) * CHUNK) == 0, t
    assert t % GRID_W == 0 and rows % (2 * NA_BLOCK) == 0 and rows >= NA_SPAN, t
    assert w_in.shape == (d, O_V + D_NA), w_in.shape

    (gate, xbc, qt, k, vt, dt), (w_out_bf, w1_bf, w2_bf) = _inproj(
        x2, norm_mix_w[None, :], w_in.T.astype(BF16), q_norm_w.astype(F32) * (NA_HEAD_DIM ** -0.5 * LOG2E), k_norm_w,
        (w_out, w_mlp_in, w_mlp_out), tm)
    r3 = lambda a: a.reshape(bsz, t, a.shape[-1])

    dt_bias = _dt_lanes(dt_bias_fwd, dt_bias_bwd)
    a_log = _dt_lanes(a_log_fwd, a_log_bwd)
    sel_f, sel_b = _head_select(0), _head_select(SSM_HEADS)
    xc, hb = _ssd_bwd(r3(xbc), r3(dt), conv_w, conv_b[None, :], dt_bias, a_log, sel_b)
    y_raw = _ssd_fwd(xc, r3(dt), hb, dt_bias, a_log, jnp.repeat(d_skip.astype(F32), SSM_HEAD_DIM)[None, :],
                     sel_f, sel_b)

    y_na = _natten(qt, r3(k), vt, _na_bias_rows(rel_pos_bias), _na_col_mask())

    out = _outmlp(x2, y_raw.reshape(n, D_SSM), gate, ssm_norm_w[None, :].astype(F32), y_na.reshape(n, D_NA),
                  w_out_bf, norm_mlp_w[None, :], w1_bf, w2_bf, tm)
    return out.reshape(bsz, t, d)


def kernel(x, norm_mix_w, w_in, conv_w, conv_b, dt_bias_fwd, dt_bias_bwd, a_log_fwd, a_log_bwd, d_skip,
           ssm_norm_w, q_norm_w, k_norm_w, rel_pos_bias, w_out, norm_mlp_w, w_mlp_in, w_mlp_out):
    tm = min(512, x.shape[0] * x.shape[1])
    for layer in range(norm_mix_w.shape[0]):
        x = _layer(x, norm_mix_w[layer], w_in[layer], conv_w[layer], conv_b[layer], dt_bias_fwd[layer],
                   dt_bias_bwd[layer], a_log_fwd[layer], a_log_bwd[layer], d_skip[layer], ssm_norm_w[layer],
                   q_norm_w[layer], k_norm_w[layer], rel_pos_bias[layer], w_out[layer], norm_mlp_w[layer],
                   w_mlp_in[layer], w_mlp_out[layer], tm)
    return x
```

```python
import functools

import numpy as np
import jax
import jax.numpy as jnp
from jax import lax
from jax.experimental import pallas as pl
from jax.experimental.pallas import tpu as pltpu

F32 = jnp.float32
BF16 = jnp.bfloat16

LANES = 128
SUBLANES = 8
GRID_W = 64
SSM_HEADS = 16
SSM_HEAD_DIM = 64
D_SSM = SSM_HEADS * SSM_HEAD_DIM
SSM_GROUPS = 2
GROUP_W = D_SSM // SSM_GROUPS
D_STATE = 128
D_CONV = 5
CHUNK = 128
FWD_CHUNKS = 4
BWD_CHUNKS = 4
CONV_DIM = D_SSM + 2 * SSM_GROUPS * D_STATE
DT_LANES = 2 * SSM_HEADS
DT_COPIES = 3
NA_HEADS = 16
NA_HEAD_DIM = 64
D_NA = NA_HEADS * NA_HEAD_DIM
NA_ROWS = 8
NA_COLS = 16
NA_PAIRS = NA_HEADS // 2
NA_BLOCK = 4
NA_SPAN = NA_ROWS + NA_BLOCK
NA_BKEYS = NA_SPAN * GRID_W
LOG2E = 1.4426950408889634
EPS = 1e-5
MASKED = -1e30
CONV_HALO = 2 * SUBLANES
CONV_STRIDE = CHUNK // SUBLANES + 1
VMEM_LIMIT = 56 * 1024 * 1024


def _dot(a, b):
    return jnp.dot(a, b, preferred_element_type=F32)


def _dot_nt(a, b):
    return lax.dot_general(a, b, (((1,), (1,)), ((), ())), preferred_element_type=F32)


def _split3(x):
    hi = x.astype(BF16)
    r1 = x - hi.astype(F32)
    mid = r1.astype(BF16)
    lo = (r1 - mid.astype(F32)).astype(BF16)
    return hi, mid, lo


def _dot_sel_lhs(sel, x):
    hi, mid, lo = _split3(x)
    return _dot(sel, hi) + _dot(sel, mid) + _dot(sel, lo)


def _pack3(x):
    lane = lax.broadcasted_iota(jnp.int32, x.shape, x.ndim - 1)
    hi = x.astype(BF16).astype(F32)
    r1 = x - hi
    mid = r1.astype(BF16).astype(F32)
    return jnp.where(lane < DT_LANES, hi, jnp.where(lane < 2 * DT_LANES, mid, r1 - mid)).astype(BF16)


def _softplus(x):
    return jnp.maximum(x, 0.0) + jnp.log(1.0 + jnp.exp(-jnp.abs(x)))


def _silu(x):
    return x * (1.0 / (1.0 + jnp.exp(-x)))


def _const_spec(shape):
    nd = len(shape)
    return pl.BlockSpec(shape, lambda *_: (0,) * nd, pipeline_mode=pl.Buffered(1))


O_DT = D_SSM + CONV_DIM
O_Q = O_DT + DT_LANES
O_K = O_Q + D_NA
O_V = O_K + D_NA


def _head_rmsnorm_t(xt, gain_ref):
    tokens = xt.shape[1]
    x3 = xt.reshape(NA_HEADS, NA_HEAD_DIM, tokens)
    ms = jnp.mean(x3 * x3, axis=1, keepdims=True)
    return (x3 * lax.rsqrt(ms + EPS) * gain_ref[...]).reshape(D_NA, tokens)


def _conv_silu(uext, cout, cw_ref, cb_ref, out_ref, first):
    nslab = uext.shape[0]
    for slab in range(nslab):
        cols = slice(slab * LANES, (slab + 1) * LANES)
        bias = jnp.broadcast_to(cb_ref[:, cols], (SUBLANES, LANES))
        taps = [jnp.broadcast_to(cw_ref[k:k + 1, cols], (SUBLANES, LANES)) for k in range(D_CONV)]
        for ch in range(out_ref.shape[0] // CHUNK):
            for i in range(CONV_STRIDE):
                acc = bias
                for k in range(D_CONV):
                    start = ch * CHUNK + first + k + i
                    acc = acc + taps[k] * uext[slab, pl.ds(start, SUBLANES, stride=CONV_STRIDE), :]
                cout[ch * nslab + slab, pl.ds(i, SUBLANES, stride=CONV_STRIDE), :] = _silu(acc)
            out_ref[ch * CHUNK:(ch + 1) * CHUNK, cols] = cout[ch * nslab + slab, 0:CHUNK, :]


def _inproj_body(x_ref, xp_ref, xn_ref, nw_ref, wt_ref, qg_ref, kg_ref, cw_ref, cb_ref, *refs, tiles_per_seq):
    uext, cout = refs[-2:]
    refs = refs[:-2]
    ncast = (len(refs) - 6) // 2
    cast_in, (z_ref, xc_ref, qt_ref, k_ref, vt_ref, dt_ref), cast_out = refs[:ncast], refs[ncast:ncast + 6], refs[ncast + 6:]
    for src, dst in zip(cast_in, cast_out):
        dst[...] = src[...].astype(dst.dtype)
    tm = x_ref.shape[0]
    xe = jnp.concatenate([xp_ref[...], x_ref[...], xn_ref[...]], axis=0)
    ms = jnp.mean(xe * xe, axis=-1, keepdims=True)
    he = (xe * lax.rsqrt(ms + EPS) * nw_ref[...]).astype(BF16)
    h = he[CONV_HALO:CONV_HALO + tm]

    xbc = _dot_nt(he, wt_ref[D_SSM:O_DT, :])
    pos = pl.program_id(0) % tiles_per_seq
    for slab in range(uext.shape[0]):
        cols = slice(slab * LANES, (slab + 1) * LANES)
        uext[slab, 0:CONV_HALO, :] = jnp.where(pos > 0, xbc[0:CONV_HALO, cols], 0.0)
        uext[slab, CONV_HALO:CONV_HALO + tm, :] = xbc[CONV_HALO:CONV_HALO + tm, cols]
        uext[slab, CONV_HALO + tm:, :] = jnp.where(pos < tiles_per_seq - 1, xbc[CONV_HALO + tm:, cols], 0.0)
    _conv_silu(uext, cout, cw_ref, cb_ref, xc_ref, CONV_HALO - (D_CONV - 1) // 2)
    z_ref[...] = _silu(_dot_nt(h, wt_ref[0:D_SSM, :]))

    w_dt = wt_ref[O_DT:O_Q, :]
    pad = jnp.zeros((LANES - DT_COPIES * DT_LANES, w_dt.shape[1]), w_dt.dtype)
    dt_ref[...] = _dot_nt(h, jnp.concatenate([w_dt] * DT_COPIES + [pad], axis=0))
    qt = _head_rmsnorm_t(_dot_nt(wt_ref[O_Q:O_K, :], h), qg_ref)
    k_ref[...] = _head_rmsnorm_t(_dot_nt(wt_ref[O_K:O_V, :], h), kg_ref).T.astype(k_ref.dtype)
    vt = _dot_nt(wt_ref[O_V:, :], h)
    for j in range(vt_ref.shape[0]):
        cols = slice(j * LANES, (j + 1) * LANES)
        qt_ref[j] = qt[:, cols].astype(qt_ref.dtype)
        vt_ref[j] = vt[:, cols].astype(vt_ref.dtype)


def _inproj(x2, seq_len, norm_w, w_bf, q_gain, k_gain, conv_w, conv_b, later_weights, tm):
    n, d = x2.shape
    steps = n // tm
    row = lambda i: (i, 0)
    tiles = tm // LANES
    hpt = tm // CONV_HALO
    prev_halo = pl.BlockSpec((CONV_HALO, d), lambda i: (jnp.maximum(i * hpt - 1, 0), 0))
    next_halo = pl.BlockSpec((CONV_HALO, d), lambda i: (jnp.minimum((i + 1) * hpt, n // CONV_HALO - 1), 0))
    nslab = CONV_DIM // LANES
    tile_spec = pl.BlockSpec((tiles, D_NA, LANES), lambda i: (i, 0, 0))
    tile_shape = jax.ShapeDtypeStruct((n // LANES, D_NA, LANES), BF16)
    bf16_rows = 2 * SUBLANES
    assert all(w.shape[0] % (steps * bf16_rows) == 0 for w in later_weights)
    slabs = [pl.BlockSpec((w.shape[0] // steps, w.shape[1]), row) for w in later_weights]
    gains = [jnp.broadcast_to(g.astype(F32)[:, None], (NA_HEAD_DIM, tm)) for g in (q_gain, k_gain)]
    outs = pl.pallas_call(
        functools.partial(_inproj_body, tiles_per_seq=seq_len // tm),
        grid=(steps,),
        in_specs=[pl.BlockSpec((tm, d), row), prev_halo, next_halo, _const_spec((1, d)), _const_spec(w_bf.shape),
                  _const_spec(gains[0].shape), _const_spec(gains[1].shape),
                  _const_spec(conv_w.shape), _const_spec(conv_b.shape)] + slabs,
        out_specs=[pl.BlockSpec((tm, D_SSM), row), pl.BlockSpec((tm, CONV_DIM), row), tile_spec,
                   pl.BlockSpec((tm, D_NA), row), tile_spec, pl.BlockSpec((tm, LANES), row)] + slabs,
        out_shape=[jax.ShapeDtypeStruct((n, D_SSM), F32), jax.ShapeDtypeStruct((n, CONV_DIM), F32), tile_shape,
                   jax.ShapeDtypeStruct((n, D_NA), BF16), tile_shape, jax.ShapeDtypeStruct((n, LANES), F32)]
        + [jax.ShapeDtypeStruct(w.shape, BF16) for w in later_weights],
        scratch_shapes=[pltpu.VMEM((nslab, tm + 2 * CONV_HALO, LANES), F32),
                        pltpu.VMEM((tm // CHUNK * nslab, SUBLANES * CONV_STRIDE, LANES), F32)],
        compiler_params=pltpu.CompilerParams(
            dimension_semantics=("arbitrary",), vmem_limit_bytes=VMEM_LIMIT),
    )(x2, x2, x2, norm_w, w_bf, *gains, conv_w, conv_b, *later_weights)
    return outs[:6], outs[6:]


def _tri_masks():
    t = lax.broadcasted_iota(jnp.int32, (CHUNK, CHUNK), 0)
    s = lax.broadcasted_iota(jnp.int32, (CHUNK, CHUNK), 1)
    return s <= t, s >= t


def _dt_and_loga(dt_ref, dtb_ref, alog_ref):
    dtv = _softplus(dt_ref[0] + dtb_ref[...])
    return dtv, dtv * (-LOG2E * jnp.exp(alog_ref[...]))


def _ssd_bwd_body(xc_ref, dt_ref, dtb_ref, alog_ref, eb_ref, hb_ref, state):
    @pl.when(pl.program_id(1) == 0)
    def _():
        state[...] = jnp.zeros_like(state)

    _, upper = _tri_masks()
    upper_sel = jnp.where(upper, 1.0, 0.0).astype(BF16)
    for ch in reversed(range(BWD_CHUNKS)):
        rows = slice(ch * CHUNK, (ch + 1) * CHUNK)
        dtv, loga = _dt_and_loga(dt_ref.at[:, rows], dtb_ref, alog_ref)
        rcum = _dot_sel_lhs(upper_sel, loga)
        r0 = rcum[0:1, :]
        wgt = dtv * jnp.exp2(r0 - rcum)
        wgt_x = _dot(_pack3(wgt), eb_ref[...])
        dec_x = _dot(_pack3(jnp.broadcast_to(jnp.exp2(r0), (SUBLANES, LANES))), eb_ref[...])[0:1]

        hb_ref[0, ch] = state[...].astype(BF16)
        xw = (xc_ref[0, rows, 0:D_SSM] * wgt_x).astype(BF16)
        for g in range(SSM_GROUPS):
            gs = slice(g * GROUP_W, (g + 1) * GROUP_W)
            b_t = xc_ref[0, rows, D_SSM + g * D_STATE:D_SSM + (g + 1) * D_STATE].T.astype(BF16)
            state[:, gs] = state[:, gs] * dec_x[:, gs] + _dot(b_t, xw[:, gs])


def _ssd_bwd(xc, dt, dt_bias, a_log, sel_b):
    bsz, t, _ = xc.shape
    rows = BWD_CHUNKS * CHUNK
    nsteps = t // rows
    rev = lambda c: nsteps - 1 - c
    return pl.pallas_call(
        _ssd_bwd_body,
        grid=(bsz, nsteps),
        in_specs=[
            pl.BlockSpec((1, rows, D_SSM + SSM_GROUPS * D_STATE), lambda b, c: (b, rev(c), 0)),
            pl.BlockSpec((1, rows, LANES), lambda b, c: (b, rev(c), 0)),
            _const_spec(dt_bias.shape), _const_spec(a_log.shape), _const_spec(sel_b.shape),
        ],
        out_specs=pl.BlockSpec((1, BWD_CHUNKS, D_STATE, D_SSM), lambda b, c: (b, rev(c), 0, 0)),
        out_shape=jax.ShapeDtypeStruct((bsz, t // CHUNK, D_STATE, D_SSM), BF16),
        scratch_shapes=[pltpu.VMEM((D_STATE, D_SSM), F32)],
        compiler_params=pltpu.CompilerParams(
            dimension_semantics=("arbitrary", "arbitrary"), vmem_limit_bytes=VMEM_LIMIT),
    )(xc, dt, dt_bias, a_log, sel_b)


def _ssd_fwd_body(xc_ref, dt_ref, hb_ref, dtb_ref, alog_ref, dsk_ref, ef_ref, eb_ref, y_ref, state):
    @pl.when(pl.program_id(1) == 0)
    def _():
        state[...] = jnp.zeros_like(state)

    for sub in range(FWD_CHUNKS):
        rows = slice(sub * CHUNK, (sub + 1) * CHUNK)
        _ssd_fwd_chunk(xc_ref.at[0, rows], dt_ref.at[:, rows], hb_ref.at[0, sub], dtb_ref, alog_ref,
                       dsk_ref, ef_ref, eb_ref, y_ref.at[0, rows], state)


def _ssd_fwd_chunk(xc_ref, dt_ref, hb_ref, dtb_ref, alog_ref, dsk_ref, ef_ref, eb_ref, y_ref, state):
    dtv, loga = _dt_and_loga(dt_ref, dtb_ref, alog_ref)
    lower, upper = _tri_masks()
    cum_f = _dot_sel_lhs(jnp.where(lower, 1.0, 0.0).astype(BF16), loga)
    cum_b = _dot_sel_lhs(jnp.where(upper, 1.0, 0.0).astype(BF16), loga)
    lane = lax.broadcasted_iota(jnp.int32, (CHUNK, LANES), 1)
    cum = jnp.where((lane & SSM_HEADS) == 0, cum_f, cum_b)
    ecum_p = _pack3(jnp.exp2(cum))
    wgt_p = _pack3(dtv * jnp.exp2(cum_f[CHUNK - 1:CHUNK, :] - cum_f))
    src_t = (cum - jnp.log2(dtv)).T
    first_half = lane < SSM_HEAD_DIM

    for g in range(SSM_GROUPS):
        gs = slice(g * GROUP_W, (g + 1) * GROUP_W)
        b_f32 = xc_ref[:, D_SSM + g * D_STATE:D_SSM + (g + 1) * D_STATE]
        b_g = b_f32.astype(BF16)
        c0 = D_SSM + SSM_GROUPS * D_STATE + g * D_STATE
        c_g = xc_ref[:, c0:c0 + D_STATE].astype(BF16)
        cb = _dot_nt(c_g, b_g)
        ea_g = _dot(ecum_p, ef_ref[:, gs])
        er_g = _dot(ecum_p, eb_ref[:, gs])
        y_off = _dot(c_g, state[:, gs].astype(BF16)) * ea_g + _dot(c_g, hb_ref[:, gs]) * er_g
        xw = (xc_ref[:, gs] * _dot(wgt_p, ef_ref[:, gs])).astype(BF16)
        state[:, gs] = state[:, gs] * ea_g[CHUNK - 1:CHUNK, :] + _dot(b_f32.T.astype(BF16), xw)
        pairs = GROUP_W // LANES
        for pp in range(pairs):
            pr = g * pairs + pp
            ps = slice(pr * LANES, (pr + 1) * LANES)
            mats = []
            for h in (2 * pr, 2 * pr + 1):
                hb = SSM_HEADS + h
                seg_f = jnp.broadcast_to(cum[:, h:h + 1], (CHUNK, CHUNK)) - src_t[h:h + 1, :]
                seg_b = jnp.broadcast_to(cum[:, hb:hb + 1], (CHUNK, CHUNK)) - src_t[hb:hb + 1, :]
                d_f = jnp.exp2(jnp.where(lower, seg_f, MASKED))
                d_b = jnp.exp2(jnp.where(upper, seg_b, MASKED))
                mats.append(((d_f + d_b) * cb).astype(BF16))
            xs = xc_ref[:, ps]
            xp = xs.astype(BF16)
            zero = jnp.zeros_like(xp)
            rhs = jnp.concatenate([jnp.where(first_half, xp, zero), jnp.where(first_half, zero, xp)], axis=0)
            y_diag = _dot(jnp.concatenate(mats, axis=1), rhs)
            y_ref[:, ps] = y_diag + y_off[:, pp * LANES:(pp + 1) * LANES] + dsk_ref[:, ps] * xs


def _ssd_fwd(xc, dt, hb, dt_bias, a_log, d_skip_x, sel_f, sel_b):
    bsz, t, _ = xc.shape
    step = FWD_CHUNKS * CHUNK
    blk = lambda w: pl.BlockSpec((1, step, w), lambda b, c: (b, c, 0))
    return pl.pallas_call(
        _ssd_fwd_body,
        grid=(bsz, t // step),
        in_specs=[
            blk(CONV_DIM), blk(LANES),
            pl.BlockSpec((1, FWD_CHUNKS, D_STATE, D_SSM), lambda b, c: (b, c, 0, 0)),
            _const_spec(dt_bias.shape), _const_spec(a_log.shape), _const_spec(d_skip_x.shape),
            _const_spec(sel_f.shape), _const_spec(sel_b.shape),
        ],
        out_specs=blk(D_SSM),
        out_shape=jax.ShapeDtypeStruct((bsz, t, D_SSM), F32),
        scratch_shapes=[pltpu.VMEM((D_STATE, D_SSM), F32)],
        compiler_params=pltpu.CompilerParams(
            dimension_semantics=("arbitrary", "arbitrary"), vmem_limit_bytes=VMEM_LIMIT),
    )(xc, dt, hb, dt_bias, a_log, d_skip_x, sel_f, sel_b)


def _build_bias(g_ref, colmask_ref, bias):
    rel = _block_row_rel()
    half = lax.broadcasted_iota(jnp.int32, (GRID_W, LANES), 1) < GRID_W
    nrel = 2 * NA_ROWS

    @functools.cache
    def toeplitz(hh, r, upper_half):
        if upper_half:
            return pltpu.roll(toeplitz(hh, r, False), GRID_W, 1)
        row = g_ref[0, hh * nrel + r:hh * nrel + r + 1, :] * LOG2E
        return pltpu.roll(jnp.broadcast_to(row, (GRID_W, LANES)), 0, 1, stride=1, stride_axis=0)

    for kind in range(rel.shape[0]):
        for hh in range(2):
            for u in range(0, NA_BLOCK, 2):
                lane0 = (hh * NA_BLOCK + u) * GRID_W
                for a in range(NA_SPAN):
                    r_lo, r_hi = int(rel[kind, u, a]), int(rel[kind, u + 1, a])
                    lo = toeplitz(hh, r_lo, False) if r_lo >= 0 else MASKED
                    hi = toeplitz(hh, r_hi, True) if r_hi >= 0 else MASKED
                    tile = jnp.where(half, lo, hi) + colmask_ref[...]
                    bias[kind, a * GRID_W:(a + 1) * GRID_W, lane0:lane0 + 2 * GRID_W] = tile


def _na_body(qt_ref, k_ref, vt_ref, g_ref, colmask_ref, o_ref, sbuf, bias):
    t = k_ref.shape[1]
    rows = t // GRID_W

    @pl.when(pl.program_id(1) == 0)
    def _():
        _build_bias(g_ref, colmask_ref, bias)

    nblk = rows // NA_BLOCK
    bq = NA_BLOCK * GRID_W

    def offsets(bi):
        i0 = bi * NA_BLOCK
        rlo = jnp.clip(i0 - NA_ROWS // 2, 0, rows - NA_SPAN)
        return pl.multiple_of(i0 * GRID_W, bq), pl.multiple_of(rlo * GRID_W, 2 * GRID_W)

    def scores(bi, slot):
        kind = jnp.where(bi == 0, 0, jnp.where(bi == nblk - 1, 2, 1))
        _, koff = offsets(bi)
        tile0 = bi * (bq // LANES)
        q_t = jnp.concatenate([qt_ref[tile0 + i] for i in range(bq // LANES)], axis=1)
        zero = jnp.zeros((NA_HEAD_DIM, bq), q_t.dtype)
        qs_t = jnp.concatenate([jnp.concatenate([q_t[:NA_HEAD_DIM], zero], axis=0),
                                jnp.concatenate([zero, q_t[NA_HEAD_DIM:]], axis=0)], axis=1)
        sbuf[slot] = _dot(k_ref[0, pl.ds(koff, NA_BKEYS), :], qs_t) + bias[kind]

    head_a_dims = lax.broadcasted_iota(jnp.int32, (LANES, bq), 0) < NA_HEAD_DIM

    def attend(bi, slot):
        qoff, koff = offsets(bi)
        s = sbuf[slot]
        p = jnp.exp2(s - jnp.max(s, axis=0, keepdims=True))
        denom = jnp.sum(p, axis=0, keepdims=True)
        tile0 = koff // LANES
        v_t = jnp.concatenate([vt_ref[tile0 + i] for i in range(NA_BKEYS // LANES)], axis=1)
        o_t = _dot(v_t, p.astype(BF16)) * (1.0 / denom)
        o_ref[0, pl.ds(qoff, bq), :] = jnp.where(head_a_dims, o_t[:, 0:bq], o_t[:, bq:]).T.astype(o_ref.dtype)

    scores(0, 0)

    def pair_step(i, carry):
        b0 = 2 * i
        scores(b0 + 1, 1)
        attend(b0, 0)
        scores(jnp.minimum(b0 + 2, nblk - 1), 0)
        attend(b0 + 1, 1)
        return carry

    lax.fori_loop(0, nblk // 2, pair_step, 0, unroll=4)


def _natten(qt, k, vt, g, colmask):
    bsz, t, _ = k.shape
    nq = 2 * NA_BLOCK * GRID_W
    blk = pl.BlockSpec((1, t, LANES), lambda p, b: (b, 0, p))
    tiles = pl.BlockSpec((t // LANES, LANES, LANES), lambda p, b: (b, p, 0))
    return pl.pallas_call(
        _na_body,
        grid=(NA_PAIRS, bsz),
        in_specs=[tiles, blk, tiles,
                  pl.BlockSpec((1,) + g.shape[1:], lambda p, b: (p, 0, 0)), _const_spec(colmask.shape)],
        out_specs=blk,
        out_shape=jax.ShapeDtypeStruct((bsz, t, D_NA), BF16),
        scratch_shapes=[pltpu.VMEM((2, NA_BKEYS, nq), F32), pltpu.VMEM((3, NA_BKEYS, nq), F32)],
        compiler_params=pltpu.CompilerParams(
            dimension_semantics=("arbitrary", "arbitrary"), vmem_limit_bytes=VMEM_LIMIT),
    )(qt, k, vt, g, colmask)


def _block_row_rel():
    rel = -np.ones((3, NA_BLOCK, NA_SPAN), np.int64)
    for u in range(NA_BLOCK):
        for a in range(NA_SPAN):
            if a < NA_ROWS:
                rel[0, u, a] = a - u + NA_ROWS - 1
            if u <= a < u + NA_ROWS:
                rel[1, u, a] = a - u + NA_ROWS - 1 - NA_ROWS // 2
            if a >= NA_SPAN - NA_ROWS:
                rel[2, u, a] = a - u + NA_BLOCK - NA_SPAN + NA_ROWS - 1
    return rel


def _na_bias_rows(rpb):
    centre = NA_COLS - 1
    rev = rpb[:, :, ::-1]
    rows = jnp.concatenate(
        [rev[:, :, centre:], jnp.zeros(rpb.shape[:2] + (LANES - rpb.shape[2],), rpb.dtype), rev[:, :, :centre]],
        axis=2).astype(F32)
    rows = jnp.pad(rows, ((0, 0), (0, 2 * NA_ROWS - rpb.shape[1]), (0, 0)))
    return rows.reshape(NA_PAIRS, 2 * 2 * NA_ROWS, LANES)


def _na_col_mask():
    j = np.arange(GRID_W)
    c0 = np.clip(j - NA_COLS // 2, 0, GRID_W - NA_COLS)
    cc = np.arange(GRID_W)
    valid = (cc[:, None] >= c0[None, :]) & (cc[:, None] < c0[None, :] + NA_COLS)
    return jnp.asarray(np.tile(np.where(valid, 0.0, MASKED), (1, 2)), F32)


FF_STEP = 512


def _outmlp_body(x_ref, ys_ref, gate_ref, gw_ref, yn_ref, wo_ref, nw_ref, w1_ref, w2_ref, o_ref):
    dmix = ys_ref.shape[-1]
    groups = []
    for g in range(SSM_GROUPS):
        gs = slice(g * GROUP_W, (g + 1) * GROUP_W)
        gated = ys_ref[:, gs] * gate_ref[:, gs]
        gms = jnp.mean(gated * gated, axis=-1, keepdims=True)
        groups.append((gated * lax.rsqrt(gms + EPS) * gw_ref[:, gs]).astype(BF16))
    y_ssm = jnp.concatenate(groups, axis=1)
    x1 = x_ref[...] + _dot(y_ssm, wo_ref[0:dmix, :]) + _dot(yn_ref[...], wo_ref[dmix:, :])
    ms = jnp.mean(x1 * x1, axis=-1, keepdims=True)
    h = (x1 * lax.rsqrt(ms + EPS) * nw_ref[...]).astype(BF16)
    o_ref[...] = x1
    for f in range(0, w1_ref.shape[1], FF_STEP):
        u = jnp.maximum(_dot(h, w1_ref[:, f:f + FF_STEP]), 0.0)
        o_ref[...] += _dot((u * u).astype(BF16), w2_ref[f:f + FF_STEP, :])


def _outmlp(x2, ys, gate, gate_norm_w, yn, w_out, norm_w, w1, w2, tm):
    n, d = x2.shape
    row = lambda i: (i, 0)
    return pl.pallas_call(
        _outmlp_body,
        grid=(n // tm,),
        in_specs=[
            pl.BlockSpec((tm, d), row), pl.BlockSpec((tm, ys.shape[1]), row), pl.BlockSpec((tm, gate.shape[1]), row),
            _const_spec(gate_norm_w.shape), pl.BlockSpec((tm, yn.shape[1]), row),
            _const_spec(w_out.shape), _const_spec((1, d)), _const_spec(w1.shape), _const_spec(w2.shape),
        ],
        out_specs=pl.BlockSpec((tm, d), row),
        out_shape=jax.ShapeDtypeStruct((n, d), F32),
        compiler_params=pltpu.CompilerParams(
            dimension_semantics=("arbitrary",), vmem_limit_bytes=VMEM_LIMIT),
    )(x2, ys, gate, gate_norm_w, yn, w_out, norm_w, w1, w2)


def _head_select(first_lane):
    sel = np.zeros((LANES, D_SSM), np.float32)
    for rep in range(DT_COPIES):
        for h in range(SSM_HEADS):
            sel[rep * DT_LANES + first_lane + h, h * SSM_HEAD_DIM:(h + 1) * SSM_HEAD_DIM] = 1.0
    return jnp.asarray(sel, BF16)


def _dt_lanes(fwd, bwd):
    v = jnp.tile(jnp.concatenate([fwd, bwd]).astype(F32), DT_COPIES)
    return jnp.pad(v, (0, LANES - v.shape[0]))[None, :]


def _layer(x, norm_mix_w, w_in, conv_w, conv_b, dt_bias_fwd, dt_bias_bwd, a_log_fwd, a_log_bwd, d_skip,
           ssm_norm_w, q_norm_w, k_norm_w, rel_pos_bias, w_out, norm_mlp_w, w_mlp_in, w_mlp_out, tm):
    bsz, t, d = x.shape
    n = bsz * t
    x2 = x.reshape(n, d)
    rows = t // GRID_W
    assert t % tm == 0 and tm % CHUNK == 0, (t, tm)
    assert t % (max(FWD_CHUNKS, BWD_CHUNKS) * CHUNK) == 0, t
    assert t % GRID_W == 0 and rows % (2 * NA_BLOCK) == 0 and rows >= NA_SPAN, t
    assert w_in.shape == (d, O_V + D_NA), w_in.shape

    (gate, xc, qt, k, vt, dt), (w_out_bf, w1_bf, w2_bf) = _inproj(
        x2, t, norm_mix_w[None, :], w_in.T.astype(BF16), q_norm_w.astype(F32) * (NA_HEAD_DIM ** -0.5 * LOG2E), k_norm_w,
        conv_w.astype(F32), conv_b.astype(F32)[None, :], (w_out, w_mlp_in, w_mlp_out), tm)
    r3 = lambda a: a.reshape(bsz, t, a.shape[-1])
    xc = r3(xc)

    dt_bias = _dt_lanes(dt_bias_fwd, dt_bias_bwd)
    a_log = _dt_lanes(a_log_fwd, a_log_bwd)
    sel_f, sel_b = _head_select(0), _head_select(SSM_HEADS)
    hb = _ssd_bwd(xc, r3(dt), dt_bias, a_log, sel_b)
    y_raw = _ssd_fwd(xc, r3(dt), hb, dt_bias, a_log, jnp.repeat(d_skip.astype(F32), SSM_HEAD_DIM)[None, :],
                     sel_f, sel_b)

    y_na = _natten(qt, r3(k), vt, _na_bias_rows(rel_pos_bias), _na_col_mask())

    out = _outmlp(x2, y_raw.reshape(n, D_SSM), gate, ssm_norm_w[None, :].astype(F32), y_na.reshape(n, D_NA),
                  w_out_bf, norm_mlp_w[None, :], w1_bf, w2_bf, tm)
    return out.reshape(bsz, t, d)


def kernel(x, norm_mix_w, w_in, conv_w, conv_b, dt_bias_fwd, dt_bias_bwd, a_log_fwd, a_log_bwd, d_skip,
           ssm_norm_w, q_norm_w, k_norm_w, rel_pos_bias, w_out, norm_mlp_w, w_mlp_in, w_mlp_out):
    tm = min(512, x.shape[0] * x.shape[1])
    for layer in range(norm_mix_w.shape[0]):
        x = _layer(x, norm_mix_w[layer], w_in[layer], conv_w[layer], conv_b[layer], dt_bias_fwd[layer],
                   dt_bias_bwd[layer], a_log_fwd[layer], a_log_bwd[layer], d_skip[layer], ssm_norm_w[layer],
                   q_norm_w[layer], k_norm_w[layer], rel_pos_bias[layer], w_out[layer], norm_mlp_w[layer],
                   w_mlp_in[layer], w_mlp_out[layer], tm)
    return x
```

```python
import functools

import numpy as np
import jax
import jax.numpy as jnp
from jax import lax
from jax.experimental import pallas as pl
from jax.experimental.pallas import tpu as pltpu

F32 = jnp.float32
BF16 = jnp.bfloat16

LANES = 128
SUBLANES = 8
GRID_W = 64
SSM_HEADS = 16
SSM_HEAD_DIM = 64
D_SSM = SSM_HEADS * SSM_HEAD_DIM
SSM_GROUPS = 2
GROUP_W = D_SSM // SSM_GROUPS
D_STATE = 128
D_CONV = 5
CHUNK = 128
FWD_CHUNKS = 4
BWD_CHUNKS = 4
CONV_DIM = D_SSM + 2 * SSM_GROUPS * D_STATE
DT_LANES = 2 * SSM_HEADS
DT_COPIES = 3
NA_HEADS = 16
NA_HEAD_DIM = 64
D_NA = NA_HEADS * NA_HEAD_DIM
NA_ROWS = 8
NA_COLS = 16
NA_PAIRS = NA_HEADS // 2
NA_BLOCK = 4
NA_SPAN = NA_ROWS + NA_BLOCK
NA_BKEYS = NA_SPAN * GRID_W
LOG2E = 1.4426950408889634
EPS = 1e-5
MASKED = -1e30
CONV_HALO = 2 * SUBLANES
CONV_STRIDE = CHUNK // SUBLANES + 1
VMEM_LIMIT = 56 * 1024 * 1024


def _dot(a, b):
    return jnp.dot(a, b, preferred_element_type=F32)


def _dot_nt(a, b):
    return lax.dot_general(a, b, (((1,), (1,)), ((), ())), preferred_element_type=F32)


def _split3(x):
    hi = x.astype(BF16)
    r1 = x - hi.astype(F32)
    mid = r1.astype(BF16)
    lo = (r1 - mid.astype(F32)).astype(BF16)
    return hi, mid, lo


def _dot_sel_lhs(sel, x):
    hi, mid, lo = _split3(x)
    return _dot(sel, hi) + _dot(sel, mid) + _dot(sel, lo)


def _pack3(x):
    lane = lax.broadcasted_iota(jnp.int32, x.shape, x.ndim - 1)
    hi = x.astype(BF16).astype(F32)
    r1 = x - hi
    mid = r1.astype(BF16).astype(F32)
    return jnp.where(lane < DT_LANES, hi, jnp.where(lane < 2 * DT_LANES, mid, r1 - mid)).astype(BF16)


def _softplus(x):
    return jnp.maximum(x, 0.0) + jnp.log(1.0 + jnp.exp(-jnp.abs(x)))


def _silu(x):
    half = 0.5 * x
    return half + half * jnp.tanh(half)


def _const_spec(shape):
    nd = len(shape)
    return pl.BlockSpec(shape, lambda *_: (0,) * nd, pipeline_mode=pl.Buffered(1))


O_DT = D_SSM + CONV_DIM
O_Q = O_DT + DT_LANES
O_K = O_Q + D_NA
O_V = O_K + D_NA


def _head_rmsnorm_t(xt, gain_ref):
    tokens = xt.shape[1]
    x3 = xt.reshape(NA_HEADS, NA_HEAD_DIM, tokens)
    ms = jnp.mean(x3 * x3, axis=1, keepdims=True)
    return (x3 * lax.rsqrt(ms + EPS) * gain_ref[...]).reshape(D_NA, tokens)


def _conv_silu(uext, cout, cw_ref, cb_ref, out_ref, first):
    nslab = uext.shape[0]
    for slab in range(nslab):
        cols = slice(slab * LANES, (slab + 1) * LANES)
        bias = jnp.broadcast_to(cb_ref[:, cols], (SUBLANES, LANES))
        taps = [jnp.broadcast_to(cw_ref[k:k + 1, cols], (SUBLANES, LANES)) for k in range(D_CONV)]
        for ch in range(out_ref.shape[0] // CHUNK):
            for i in range(CONV_STRIDE):
                acc = bias
                for k in range(D_CONV):
                    start = ch * CHUNK + first + k + i
                    acc = acc + taps[k] * uext[slab, pl.ds(start, SUBLANES, stride=CONV_STRIDE), :]
                cout[ch * nslab + slab, pl.ds(i, SUBLANES, stride=CONV_STRIDE), :] = _silu(acc)
            out_ref[ch * CHUNK:(ch + 1) * CHUNK, cols] = cout[ch * nslab + slab, 0:CHUNK, :]


def _inproj_body(x_ref, xp_ref, xn_ref, nw_ref, wt_ref, qg_ref, kg_ref, cw_ref, cb_ref, *refs, tiles_per_seq):
    uext, cout = refs[-2:]
    refs = refs[:-2]
    ncast = (len(refs) - 6) // 2
    cast_in, (z_ref, xc_ref, qt_ref, k_ref, vt_ref, dt_ref), cast_out = refs[:ncast], refs[ncast:ncast + 6], refs[ncast + 6:]
    for src, dst in zip(cast_in, cast_out):
        dst[...] = src[...].astype(dst.dtype)
    tm = x_ref.shape[0]
    xe = jnp.concatenate([xp_ref[...], x_ref[...], xn_ref[...]], axis=0)
    ms = jnp.mean(xe * xe, axis=-1, keepdims=True)
    he = (xe * lax.rsqrt(ms + EPS) * nw_ref[...]).astype(BF16)
    h = he[CONV_HALO:CONV_HALO + tm]

    xbc = _dot_nt(he, wt_ref[D_SSM:O_DT, :])
    pos = pl.program_id(0) % tiles_per_seq
    for slab in range(uext.shape[0]):
        cols = slice(slab * LANES, (slab + 1) * LANES)
        uext[slab, 0:CONV_HALO, :] = jnp.where(pos > 0, xbc[0:CONV_HALO, cols], 0.0)
        uext[slab, CONV_HALO:CONV_HALO + tm, :] = xbc[CONV_HALO:CONV_HALO + tm, cols]
        uext[slab, CONV_HALO + tm:, :] = jnp.where(pos < tiles_per_seq - 1, xbc[CONV_HALO + tm:, cols], 0.0)
    _conv_silu(uext, cout, cw_ref, cb_ref, xc_ref, CONV_HALO - (D_CONV - 1) // 2)
    z_ref[...] = _silu(_dot_nt(h, wt_ref[0:D_SSM, :]))

    w_dt = wt_ref[O_DT:O_Q, :]
    pad = jnp.zeros((LANES - DT_COPIES * DT_LANES, w_dt.shape[1]), w_dt.dtype)
    dt_ref[...] = _dot_nt(h, jnp.concatenate([w_dt] * DT_COPIES + [pad], axis=0))
    qt = _head_rmsnorm_t(_dot_nt(wt_ref[O_Q:O_K, :], h), qg_ref)
    k_ref[...] = _head_rmsnorm_t(_dot_nt(wt_ref[O_K:O_V, :], h), kg_ref).T.astype(k_ref.dtype)
    vt = _dot_nt(wt_ref[O_V:, :], h)
    for j in range(vt_ref.shape[0]):
        cols = slice(j * LANES, (j + 1) * LANES)
        qt_ref[j] = qt[:, cols].astype(qt_ref.dtype)
        vt_ref[j] = vt[:, cols].astype(vt_ref.dtype)


def _inproj(x2, seq_len, norm_w, w_bf, q_gain, k_gain, conv_w, conv_b, later_weights, tm):
    n, d = x2.shape
    steps = n // tm
    row = lambda i: (i, 0)
    tiles = tm // LANES
    hpt = tm // CONV_HALO
    prev_halo = pl.BlockSpec((CONV_HALO, d), lambda i: (jnp.maximum(i * hpt - 1, 0), 0))
    next_halo = pl.BlockSpec((CONV_HALO, d), lambda i: (jnp.minimum((i + 1) * hpt, n // CONV_HALO - 1), 0))
    nslab = CONV_DIM // LANES
    tile_spec = pl.BlockSpec((tiles, D_NA, LANES), lambda i: (i, 0, 0))
    tile_shape = jax.ShapeDtypeStruct((n // LANES, D_NA, LANES), BF16)
    bf16_rows = 2 * SUBLANES
    assert all(w.shape[0] % (steps * bf16_rows) == 0 for w in later_weights)
    slabs = [pl.BlockSpec((w.shape[0] // steps, w.shape[1]), row) for w in later_weights]
    gains = [jnp.broadcast_to(g.astype(F32)[:, None], (NA_HEAD_DIM, tm)) for g in (q_gain, k_gain)]
    outs = pl.pallas_call(
        functools.partial(_inproj_body, tiles_per_seq=seq_len // tm),
        grid=(steps,),
        in_specs=[pl.BlockSpec((tm, d), row), prev_halo, next_halo, _const_spec((1, d)), _const_spec(w_bf.shape),
                  _const_spec(gains[0].shape), _const_spec(gains[1].shape),
                  _const_spec(conv_w.shape), _const_spec(conv_b.shape)] + slabs,
        out_specs=[pl.BlockSpec((tm, D_SSM), row), pl.BlockSpec((tm, CONV_DIM), row), tile_spec,
                   pl.BlockSpec((tm, D_NA), row), tile_spec, pl.BlockSpec((tm, LANES), row)] + slabs,
        out_shape=[jax.ShapeDtypeStruct((n, D_SSM), F32), jax.ShapeDtypeStruct((n, CONV_DIM), F32), tile_shape,
                   jax.ShapeDtypeStruct((n, D_NA), BF16), tile_shape, jax.ShapeDtypeStruct((n, LANES), F32)]
        + [jax.ShapeDtypeStruct(w.shape, BF16) for w in later_weights],
        scratch_shapes=[pltpu.VMEM((nslab, tm + 2 * CONV_HALO, LANES), F32),
                        pltpu.VMEM((tm // CHUNK * nslab, SUBLANES * CONV_STRIDE, LANES), F32)],
        compiler_params=pltpu.CompilerParams(
            dimension_semantics=("arbitrary",), vmem_limit_bytes=VMEM_LIMIT),
    )(x2, x2, x2, norm_w, w_bf, *gains, conv_w, conv_b, *later_weights)
    return outs[:6], outs[6:]


def _tri_masks():
    t = lax.broadcasted_iota(jnp.int32, (CHUNK, CHUNK), 0)
    s = lax.broadcasted_iota(jnp.int32, (CHUNK, CHUNK), 1)
    return s <= t, s >= t


def _dt_and_loga(dt_ref, dtb_ref, alog_ref):
    dtv = _softplus(dt_ref[0] + dtb_ref[...])
    return dtv, dtv * (-LOG2E * jnp.exp(alog_ref[...]))


def _ssd_bwd_body(xc_ref, dt_ref, dtb_ref, alog_ref, eb_ref, hb_ref, state):
    @pl.when(pl.program_id(1) == 0)
    def _():
        state[...] = jnp.zeros_like(state)

    _, upper = _tri_masks()
    upper_sel = jnp.where(upper, 1.0, 0.0).astype(BF16)
    for ch in reversed(range(BWD_CHUNKS)):
        rows = slice(ch * CHUNK, (ch + 1) * CHUNK)
        dtv, loga = _dt_and_loga(dt_ref.at[:, rows], dtb_ref, alog_ref)
        rcum = _dot_sel_lhs(upper_sel, loga)
        r0 = rcum[0:1, :]
        wgt = dtv * jnp.exp2(r0 - rcum)
        wgt_x = _dot(_pack3(wgt), eb_ref[...])
        dec_x = _dot(_pack3(jnp.broadcast_to(jnp.exp2(r0), (SUBLANES, LANES))), eb_ref[...])[0:1]

        hb_ref[0, ch] = state[...].astype(BF16)
        xw = (xc_ref[0, rows, 0:D_SSM] * wgt_x).astype(BF16)
        for g in range(SSM_GROUPS):
            gs = slice(g * GROUP_W, (g + 1) * GROUP_W)
            b_t = xc_ref[0, rows, D_SSM + g * D_STATE:D_SSM + (g + 1) * D_STATE].T.astype(BF16)
            state[:, gs] = state[:, gs] * dec_x[:, gs] + _dot(b_t, xw[:, gs])


def _ssd_bwd(xc, dt, dt_bias, a_log, sel_b):
    bsz, t, _ = xc.shape
    rows = BWD_CHUNKS * CHUNK
    nsteps = t // rows
    rev = lambda c: nsteps - 1 - c
    return pl.pallas_call(
        _ssd_bwd_body,
        grid=(bsz, nsteps),
        in_specs=[
            pl.BlockSpec((1, rows, D_SSM + SSM_GROUPS * D_STATE), lambda b, c: (b, rev(c), 0)),
            pl.BlockSpec((1, rows, LANES), lambda b, c: (b, rev(c), 0)),
            _const_spec(dt_bias.shape), _const_spec(a_log.shape), _const_spec(sel_b.shape),
        ],
        out_specs=pl.BlockSpec((1, BWD_CHUNKS, D_STATE, D_SSM), lambda b, c: (b, rev(c), 0, 0)),
        out_shape=jax.ShapeDtypeStruct((bsz, t // CHUNK, D_STATE, D_SSM), BF16),
        scratch_shapes=[pltpu.VMEM((D_STATE, D_SSM), F32)],
        compiler_params=pltpu.CompilerParams(
            dimension_semantics=("arbitrary", "arbitrary"), vmem_limit_bytes=VMEM_LIMIT),
    )(xc, dt, dt_bias, a_log, sel_b)


def _ssd_fwd_body(xc_ref, dt_ref, hb_ref, dtb_ref, alog_ref, dsk_ref, ef_ref, eb_ref, y_ref, state):
    @pl.when(pl.program_id(1) == 0)
    def _():
        state[...] = jnp.zeros_like(state)

    for sub in range(FWD_CHUNKS):
        rows = slice(sub * CHUNK, (sub + 1) * CHUNK)
        _ssd_fwd_chunk(xc_ref.at[0, rows], dt_ref.at[:, rows], hb_ref.at[0, sub], dtb_ref, alog_ref,
                       dsk_ref, ef_ref, eb_ref, y_ref.at[0, rows], state)


def _ssd_fwd_chunk(xc_ref, dt_ref, hb_ref, dtb_ref, alog_ref, dsk_ref, ef_ref, eb_ref, y_ref, state):
    dtv, loga = _dt_and_loga(dt_ref, dtb_ref, alog_ref)
    lower, upper = _tri_masks()
    cum_f = _dot_sel_lhs(jnp.where(lower, 1.0, 0.0).astype(BF16), loga)
    cum_b = _dot_sel_lhs(jnp.where(upper, 1.0, 0.0).astype(BF16), loga)
    lane = lax.broadcasted_iota(jnp.int32, (CHUNK, LANES), 1)
    cum = jnp.where((lane & SSM_HEADS) == 0, cum_f, cum_b)
    ecum_p = _pack3(jnp.exp2(cum))
    wgt_p = _pack3(dtv * jnp.exp2(cum_f[CHUNK - 1:CHUNK, :] - cum_f))
    src_t = (cum - jnp.log2(dtv)).T
    first_half = lane < SSM_HEAD_DIM

    for g in range(SSM_GROUPS):
        gs = slice(g * GROUP_W, (g + 1) * GROUP_W)
        b_f32 = xc_ref[:, D_SSM + g * D_STATE:D_SSM + (g + 1) * D_STATE]
        b_g = b_f32.astype(BF16)
        c0 = D_SSM + SSM_GROUPS * D_STATE + g * D_STATE
        c_g = xc_ref[:, c0:c0 + D_STATE].astype(BF16)
        cb = _dot_nt(c_g, b_g)
        ea_g = _dot(ecum_p, ef_ref[:, gs])
        er_g = _dot(ecum_p, eb_ref[:, gs])
        y_off = _dot(c_g, state[:, gs].astype(BF16)) * ea_g + _dot(c_g, hb_ref[:, gs]) * er_g
        xw = (xc_ref[:, gs] * _dot(wgt_p, ef_ref[:, gs])).astype(BF16)
        state[:, gs] = state[:, gs] * ea_g[CHUNK - 1:CHUNK, :] + _dot(b_f32.T.astype(BF16), xw)
        pairs = GROUP_W // LANES
        for pp in range(pairs):
            pr = g * pairs + pp
            ps = slice(pr * LANES, (pr + 1) * LANES)
            mats = []
            for h in (2 * pr, 2 * pr + 1):
                hb = SSM_HEADS + h
                seg_f = jnp.broadcast_to(cum[:, h:h + 1], (CHUNK, CHUNK)) - src_t[h:h + 1, :]
                seg_b = jnp.broadcast_to(cum[:, hb:hb + 1], (CHUNK, CHUNK)) - src_t[hb:hb + 1, :]
                d_f = jnp.exp2(jnp.where(lower, seg_f, MASKED))
                d_b = jnp.exp2(jnp.where(upper, seg_b, MASKED))
                mats.append(((d_f + d_b) * cb).astype(BF16))
            xs = xc_ref[:, ps]
            xp = xs.astype(BF16)
            zero = jnp.zeros_like(xp)
            rhs = jnp.concatenate([jnp.where(first_half, xp, zero), jnp.where(first_half, zero, xp)], axis=0)
            y_diag = _dot(jnp.concatenate(mats, axis=1), rhs)
            y_ref[:, ps] = y_diag + y_off[:, pp * LANES:(pp + 1) * LANES] + dsk_ref[:, ps] * xs


def _ssd_fwd(xc, dt, hb, dt_bias, a_log, d_skip_x, sel_f, sel_b):
    bsz, t, _ = xc.shape
    step = FWD_CHUNKS * CHUNK
    blk = lambda w: pl.BlockSpec((1, step, w), lambda b, c: (b, c, 0))
    return pl.pallas_call(
        _ssd_fwd_body,
        grid=(bsz, t // step),
        in_specs=[
            blk(CONV_DIM), blk(LANES),
            pl.BlockSpec((1, FWD_CHUNKS, D_STATE, D_SSM), lambda b, c: (b, c, 0, 0)),
            _const_spec(dt_bias.shape), _const_spec(a_log.shape), _const_spec(d_skip_x.shape),
            _const_spec(sel_f.shape), _const_spec(sel_b.shape),
        ],
        out_specs=blk(D_SSM),
        out_shape=jax.ShapeDtypeStruct((bsz, t, D_SSM), F32),
        scratch_shapes=[pltpu.VMEM((D_STATE, D_SSM), F32)],
        compiler_params=pltpu.CompilerParams(
            dimension_semantics=("arbitrary", "arbitrary"), vmem_limit_bytes=VMEM_LIMIT),
    )(xc, dt, hb, dt_bias, a_log, d_skip_x, sel_f, sel_b)


def _build_bias(g_ref, colmask_ref, bias):
    rel = _block_row_rel()
    half = lax.broadcasted_iota(jnp.int32, (GRID_W, LANES), 1) < GRID_W
    nrel = 2 * NA_ROWS

    @functools.cache
    def toeplitz(hh, r, upper_half):
        if upper_half:
            return pltpu.roll(toeplitz(hh, r, False), GRID_W, 1)
        row = g_ref[0, hh * nrel + r:hh * nrel + r + 1, :] * LOG2E
        return pltpu.roll(jnp.broadcast_to(row, (GRID_W, LANES)), 0, 1, stride=1, stride_axis=0)

    for kind in range(rel.shape[0]):
        for hh in range(2):
            for u in range(0, NA_BLOCK, 2):
                lane0 = (hh * NA_BLOCK + u) * GRID_W
                for a in range(NA_SPAN):
                    r_lo, r_hi = int(rel[kind, u, a]), int(rel[kind, u + 1, a])
                    lo = toeplitz(hh, r_lo, False) if r_lo >= 0 else MASKED
                    hi = toeplitz(hh, r_hi, True) if r_hi >= 0 else MASKED
                    tile = jnp.where(half, lo, hi) + colmask_ref[...]
                    bias[kind, a * GRID_W:(a + 1) * GRID_W, lane0:lane0 + 2 * GRID_W] = tile


def _na_body(qt_ref, k_ref, vt_ref, g_ref, colmask_ref, o_ref, sbuf, bias):
    t = k_ref.shape[1]
    rows = t // GRID_W

    @pl.when(pl.program_id(1) == 0)
    def _():
        _build_bias(g_ref, colmask_ref, bias)

    nblk = rows // NA_BLOCK
    bq = NA_BLOCK * GRID_W

    def offsets(bi):
        i0 = bi * NA_BLOCK
        rlo = jnp.clip(i0 - NA_ROWS // 2, 0, rows - NA_SPAN)
        return pl.multiple_of(i0 * GRID_W, bq), pl.multiple_of(rlo * GRID_W, 2 * GRID_W)

    def scores(bi, slot):
        kind = jnp.where(bi == 0, 0, jnp.where(bi == nblk - 1, 2, 1))
        _, koff = offsets(bi)
        tile0 = bi * (bq // LANES)
        q_t = jnp.concatenate([qt_ref[tile0 + i] for i in range(bq // LANES)], axis=1)
        zero = jnp.zeros((NA_HEAD_DIM, bq), q_t.dtype)
        qs_t = jnp.concatenate([jnp.concatenate([q_t[:NA_HEAD_DIM], zero], axis=0),
                                jnp.concatenate([zero, q_t[NA_HEAD_DIM:]], axis=0)], axis=1)
        sbuf[slot] = _dot(k_ref[0, pl.ds(koff, NA_BKEYS), :], qs_t) + bias[kind]

    head_a_dims = lax.broadcasted_iota(jnp.int32, (LANES, bq), 0) < NA_HEAD_DIM

    def attend(bi, slot):
        qoff, koff = offsets(bi)
        s = sbuf[slot]
        p = jnp.exp2(s - jnp.max(s, axis=0, keepdims=True))
        denom = jnp.sum(p, axis=0, keepdims=True)
        tile0 = koff // LANES
        v_t = jnp.concatenate([vt_ref[tile0 + i] for i in range(NA_BKEYS // LANES)], axis=1)
        o_t = _dot(v_t, p.astype(BF16)) * (1.0 / denom)
        o_ref[0, pl.ds(qoff, bq), :] = jnp.where(head_a_dims, o_t[:, 0:bq], o_t[:, bq:]).T.astype(o_ref.dtype)

    scores(0, 0)

    def pair_step(i, carry):
        b0 = 2 * i
        scores(b0 + 1, 1)
        attend(b0, 0)
        scores(jnp.minimum(b0 + 2, nblk - 1), 0)
        attend(b0 + 1, 1)
        return carry

    lax.fori_loop(0, nblk // 2, pair_step, 0, unroll=4)


def _natten(qt, k, vt, g, colmask):
    bsz, t, _ = k.shape
    nq = 2 * NA_BLOCK * GRID_W
    blk = pl.BlockSpec((1, t, LANES), lambda p, b: (b, 0, p))
    tiles = pl.BlockSpec((t // LANES, LANES, LANES), lambda p, b: (b, p, 0))
    return pl.pallas_call(
        _na_body,
        grid=(NA_PAIRS, bsz),
        in_specs=[tiles, blk, tiles,
                  pl.BlockSpec((1,) + g.shape[1:], lambda p, b: (p, 0, 0)), _const_spec(colmask.shape)],
        out_specs=blk,
        out_shape=jax.ShapeDtypeStruct((bsz, t, D_NA), BF16),
        scratch_shapes=[pltpu.VMEM((2, NA_BKEYS, nq), F32), pltpu.VMEM((3, NA_BKEYS, nq), F32)],
        compiler_params=pltpu.CompilerParams(
            dimension_semantics=("arbitrary", "arbitrary"), vmem_limit_bytes=VMEM_LIMIT),
    )(qt, k, vt, g, colmask)


def _block_row_rel():
    rel = -np.ones((3, NA_BLOCK, NA_SPAN), np.int64)
    for u in range(NA_BLOCK):
        for a in range(NA_SPAN):
            if a < NA_ROWS:
                rel[0, u, a] = a - u + NA_ROWS - 1
            if u <= a < u + NA_ROWS:
                rel[1, u, a] = a - u + NA_ROWS - 1 - NA_ROWS // 2
            if a >= NA_SPAN - NA_ROWS:
                rel[2, u, a] = a - u + NA_BLOCK - NA_SPAN + NA_ROWS - 1
    return rel


def _na_bias_rows(rpb):
    centre = NA_COLS - 1
    rev = rpb[:, :, ::-1]
    rows = jnp.concatenate(
        [rev[:, :, centre:], jnp.zeros(rpb.shape[:2] + (LANES - rpb.shape[2],), rpb.dtype), rev[:, :, :centre]],
        axis=2).astype(F32)
    rows = jnp.pad(rows, ((0, 0), (0, 2 * NA_ROWS - rpb.shape[1]), (0, 0)))
    return rows.reshape(NA_PAIRS, 2 * 2 * NA_ROWS, LANES)


def _na_col_mask():
    j = np.arange(GRID_W)
    c0 = np.clip(j - NA_COLS // 2, 0, GRID_W - NA_COLS)
    cc = np.arange(GRID_W)
    valid = (cc[:, None] >= c0[None, :]) & (cc[:, None] < c0[None, :] + NA_COLS)
    return jnp.asarray(np.tile(np.where(valid, 0.0, MASKED), (1, 2)), F32)


FF_STEP = 512


def _outmlp_body(x_ref, ys_ref, gate_ref, gw_ref, yn_ref, wo_ref, nw_ref, w1_ref, w2_ref, o_ref):
    dmix = ys_ref.shape[-1]
    groups = []
    for g in range(SSM_GROUPS):
        gs = slice(g * GROUP_W, (g + 1) * GROUP_W)
        gated = ys_ref[:, gs] * gate_ref[:, gs]
        gms = jnp.mean(gated * gated, axis=-1, keepdims=True)
        groups.append((gated * lax.rsqrt(gms + EPS) * gw_ref[:, gs]).astype(BF16))
    y_ssm = jnp.concatenate(groups, axis=1)
    x1 = x_ref[...] + _dot(y_ssm, wo_ref[0:dmix, :]) + _dot(yn_ref[...], wo_ref[dmix:, :])
    ms = jnp.mean(x1 * x1, axis=-1, keepdims=True)
    h = (x1 * lax.rsqrt(ms + EPS) * nw_ref[...]).astype(BF16)
    o_ref[...] = x1
    for f in range(0, w1_ref.shape[1], FF_STEP):
        u = jnp.maximum(_dot(h, w1_ref[:, f:f + FF_STEP]), 0.0)
        o_ref[...] += _dot((u * u).astype(BF16), w2_ref[f:f + FF_STEP, :])


def _outmlp(x2, ys, gate, gate_norm_w, yn, w_out, norm_w, w1, w2, tm):
    n, d = x2.shape
    row = lambda i: (i, 0)
    return pl.pallas_call(
        _outmlp_body,
        grid=(n // tm,),
        in_specs=[
            pl.BlockSpec((tm, d), row), pl.BlockSpec((tm, ys.shape[1]), row), pl.BlockSpec((tm, gate.shape[1]), row),
            _const_spec(gate_norm_w.shape), pl.BlockSpec((tm, yn.shape[1]), row),
            _const_spec(w_out.shape), _const_spec((1, d)), _const_spec(w1.shape), _const_spec(w2.shape),
        ],
        out_specs=pl.BlockSpec((tm, d), row),
        out_shape=jax.ShapeDtypeStruct((n, d), F32),
        compiler_params=pltpu.CompilerParams(
            dimension_semantics=("arbitrary",), vmem_limit_bytes=VMEM_LIMIT),
    )(x2, ys, gate, gate_norm_w, yn, w_out, norm_w, w1, w2)


def _head_select(first_lane):
    sel = np.zeros((LANES, D_SSM), np.float32)
    for rep in range(DT_COPIES):
        for h in range(SSM_HEADS):
            sel[rep * DT_LANES + first_lane + h, h * SSM_HEAD_DIM:(h + 1) * SSM_HEAD_DIM] = 1.0
    return jnp.asarray(sel, BF16)


def _dt_lanes(fwd, bwd):
    v = jnp.tile(jnp.concatenate([fwd, bwd]).astype(F32), DT_COPIES)
    return jnp.pad(v, (0, LANES - v.shape[0]))[None, :]


def _layer(x, norm_mix_w, w_in, conv_w, conv_b, dt_bias_fwd, dt_bias_bwd, a_log_fwd, a_log_bwd, d_skip,
           ssm_norm_w, q_norm_w, k_norm_w, rel_pos_bias, w_out, norm_mlp_w, w_mlp_in, w_mlp_out, tm):
    bsz, t, d = x.shape
    n = bsz * t
    x2 = x.reshape(n, d)
    rows = t // GRID_W
    assert t % tm == 0 and tm % CHUNK == 0, (t, tm)
    assert t % (max(FWD_CHUNKS, BWD_CHUNKS) * CHUNK) == 0, t
    assert t % GRID_W == 0 and rows % (2 * NA_BLOCK) == 0 and rows >= NA_SPAN, t
    assert w_in.shape == (d, O_V + D_NA), w_in.shape

    (gate, xc, qt, k, vt, dt), (w_out_bf, w1_bf, w2_bf) = _inproj(
        x2, t, norm_mix_w[None, :], w_in.T.astype(BF16), q_norm_w.astype(F32) * (NA_HEAD_DIM ** -0.5 * LOG2E), k_norm_w,
        conv_w.astype(F32), conv_b.astype(F32)[None, :], (w_out, w_mlp_in, w_mlp_out), tm)
    r3 = lambda a: a.reshape(bsz, t, a.shape[-1])
    xc = r3(xc)

    dt_bias = _dt_lanes(dt_bias_fwd, dt_bias_bwd)
    a_log = _dt_lanes(a_log_fwd, a_log_bwd)
    sel_f, sel_b = _head_select(0), _head_select(SSM_HEADS)
    hb = _ssd_bwd(xc, r3(dt), dt_bias, a_log, sel_b)
    y_raw = _ssd_fwd(xc, r3(dt), hb, dt_bias, a_log, jnp.repeat(d_skip.astype(F32), SSM_HEAD_DIM)[None, :],
                     sel_f, sel_b)

    y_na = _natten(qt, r3(k), vt, _na_bias_rows(rel_pos_bias), _na_col_mask())

    out = _outmlp(x2, y_raw.reshape(n, D_SSM), gate, ssm_norm_w[None, :].astype(F32), y_na.reshape(n, D_NA),
                  w_out_bf, norm_mlp_w[None, :], w1_bf, w2_bf, tm)
    return out.reshape(bsz, t, d)


def kernel(x, norm_mix_w, w_in, conv_w, conv_b, dt_bias_fwd, dt_bias_bwd, a_log_fwd, a_log_bwd, d_skip,
           ssm_norm_w, q_norm_w, k_norm_w, rel_pos_bias, w_out, norm_mlp_w, w_mlp_in, w_mlp_out):
    tm = min(512, x.shape[0] * x.shape[1])
    for layer in range(norm_mix_w.shape[0]):
        x = _layer(x, norm_mix_w[layer], w_in[layer], conv_w[layer], conv_b[layer], dt_bias_fwd[layer],
                   dt_bias_bwd[layer], a_log_fwd[layer], a_log_bwd[layer], d_skip[layer], ssm_norm_w[layer],
                   q_norm_w[layer], k_norm_w[layer], rel_pos_bias[layer], w_out[layer], norm_mlp_w[layer],
                   w_mlp_in[layer], w_mlp_out[layer], tm)
    return x
```

```python
import functools

import numpy as np
import jax
import jax.numpy as jnp
from jax import lax
from jax.experimental import pallas as pl
from jax.experimental.pallas import tpu as pltpu

F32 = jnp.float32
BF16 = jnp.bfloat16

LANES = 128
SUBLANES = 8
GRID_W = 64
SSM_HEADS = 16
SSM_HEAD_DIM = 64
D_SSM = SSM_HEADS * SSM_HEAD_DIM
SSM_GROUPS = 2
GROUP_W = D_SSM // SSM_GROUPS
D_STATE = 128
D_CONV = 5
CHUNK = 128
FWD_CHUNKS = 8
BWD_CHUNKS = 8
CONV_DIM = D_SSM + 2 * SSM_GROUPS * D_STATE
DT_LANES = 2 * SSM_HEADS
DT_COPIES = 3
NA_HEADS = 16
NA_HEAD_DIM = 64
D_NA = NA_HEADS * NA_HEAD_DIM
NA_ROWS = 8
NA_COLS = 16
NA_PAIRS = NA_HEADS // 2
NA_BLOCK = 4
NA_SPAN = NA_ROWS + NA_BLOCK
NA_BKEYS = NA_SPAN * GRID_W
LOG2E = 1.4426950408889634
EPS = 1e-5
MASKED = -1e30
CONV_HALO = 2 * SUBLANES
CONV_STRIDE = CHUNK // SUBLANES + 1
VMEM_LIMIT = 56 * 1024 * 1024


def _dot(a, b):
    return jnp.dot(a, b, preferred_element_type=F32)


def _dot_nt(a, b):
    return lax.dot_general(a, b, (((1,), (1,)), ((), ())), preferred_element_type=F32)


def _split3(x):
    hi = x.astype(BF16)
    r1 = x - hi.astype(F32)
    mid = r1.astype(BF16)
    lo = (r1 - mid.astype(F32)).astype(BF16)
    return hi, mid, lo


def _dot_sel_lhs(sel, x):
    hi, mid, lo = _split3(x)
    return _dot(sel, hi) + _dot(sel, mid) + _dot(sel, lo)


def _pack3(x):
    lane = lax.broadcasted_iota(jnp.int32, x.shape, x.ndim - 1)
    hi = x.astype(BF16).astype(F32)
    r1 = x - hi
    mid = r1.astype(BF16).astype(F32)
    return jnp.where(lane < DT_LANES, hi, jnp.where(lane < 2 * DT_LANES, mid, r1 - mid)).astype(BF16)


def _softplus(x):
    return jnp.maximum(x, 0.0) + jnp.log(1.0 + jnp.exp(-jnp.abs(x)))


def _silu(x):
    half = 0.5 * x
    return half + half * jnp.tanh(half)


def _const_spec(shape):
    nd = len(shape)
    return pl.BlockSpec(shape, lambda *_: (0,) * nd, pipeline_mode=pl.Buffered(1))


O_DT = D_SSM + CONV_DIM
O_Q = O_DT + DT_LANES
O_K = O_Q + D_NA
O_V = O_K + D_NA


def _head_rmsnorm_t(xt, gain_ref):
    tokens = xt.shape[1]
    x3 = xt.reshape(NA_HEADS, NA_HEAD_DIM, tokens)
    ms = jnp.mean(x3 * x3, axis=1, keepdims=True)
    return (x3 * lax.rsqrt(ms + EPS) * gain_ref[...]).reshape(D_NA, tokens)


def _conv_silu(uext, cout, cw_ref, cb_ref, out_ref, first):
    nslab = uext.shape[0]
    for slab in range(nslab):
        cols = slice(slab * LANES, (slab + 1) * LANES)
        bias = jnp.broadcast_to(cb_ref[:, cols], (SUBLANES, LANES))
        taps = [jnp.broadcast_to(cw_ref[k:k + 1, cols], (SUBLANES, LANES)) for k in range(D_CONV)]
        for ch in range(out_ref.shape[0] // CHUNK):
            for i in range(CONV_STRIDE):
                acc = bias
                for k in range(D_CONV):
                    start = ch * CHUNK + first + k + i
                    acc = acc + taps[k] * uext[slab, pl.ds(start, SUBLANES, stride=CONV_STRIDE), :]
                cout[ch * nslab + slab, pl.ds(i, SUBLANES, stride=CONV_STRIDE), :] = _silu(acc)
            out_ref[ch * CHUNK:(ch + 1) * CHUNK, cols] = cout[ch * nslab + slab, 0:CHUNK, :]


def _inproj_body(x_ref, xp_ref, xn_ref, nw_ref, wt_ref, qg_ref, kg_ref, cw_ref, cb_ref, *refs, tiles_per_seq):
    uext, cout = refs[-2:]
    refs = refs[:-2]
    ncast = (len(refs) - 6) // 2
    cast_in, (z_ref, xc_ref, qt_ref, k_ref, vt_ref, dt_ref), cast_out = refs[:ncast], refs[ncast:ncast + 6], refs[ncast + 6:]
    for src, dst in zip(cast_in, cast_out):
        dst[...] = src[...].astype(dst.dtype)
    tm = x_ref.shape[0]
    xe = jnp.concatenate([xp_ref[...], x_ref[...], xn_ref[...]], axis=0)
    ms = jnp.mean(xe * xe, axis=-1, keepdims=True)
    he = (xe * lax.rsqrt(ms + EPS) * nw_ref[...]).astype(BF16)
    h = he[CONV_HALO:CONV_HALO + tm]

    xbc = _dot_nt(he, wt_ref[D_SSM:O_DT, :])
    pos = pl.program_id(0) % tiles_per_seq
    for slab in range(uext.shape[0]):
        cols = slice(slab * LANES, (slab + 1) * LANES)
        uext[slab, 0:CONV_HALO, :] = jnp.where(pos > 0, xbc[0:CONV_HALO, cols], 0.0)
        uext[slab, CONV_HALO:CONV_HALO + tm, :] = xbc[CONV_HALO:CONV_HALO + tm, cols]
        uext[slab, CONV_HALO + tm:, :] = jnp.where(pos < tiles_per_seq - 1, xbc[CONV_HALO + tm:, cols], 0.0)
    _conv_silu(uext, cout, cw_ref, cb_ref, xc_ref, CONV_HALO - (D_CONV - 1) // 2)
    z_ref[...] = _silu(_dot_nt(h, wt_ref[0:D_SSM, :]))

    w_dt = wt_ref[O_DT:O_Q, :]
    pad = jnp.zeros((LANES - DT_COPIES * DT_LANES, w_dt.shape[1]), w_dt.dtype)
    dt_ref[...] = _dot_nt(h, jnp.concatenate([w_dt] * DT_COPIES + [pad], axis=0))
    qt = _head_rmsnorm_t(_dot_nt(wt_ref[O_Q:O_K, :], h), qg_ref)
    k_ref[...] = _head_rmsnorm_t(_dot_nt(wt_ref[O_K:O_V, :], h), kg_ref).T.astype(k_ref.dtype)
    vt = _dot_nt(wt_ref[O_V:, :], h)
    for j in range(vt_ref.shape[0]):
        cols = slice(j * LANES, (j + 1) * LANES)
        qt_ref[j] = qt[:, cols].astype(qt_ref.dtype)
        vt_ref[j] = vt[:, cols].astype(vt_ref.dtype)


def _inproj(x2, seq_len, norm_w, w_bf, q_gain, k_gain, conv_w, conv_b, later_weights, tm):
    n, d = x2.shape
    steps = n // tm
    row = lambda i: (i, 0)
    tiles = tm // LANES
    hpt = tm // CONV_HALO
    prev_halo = pl.BlockSpec((CONV_HALO, d), lambda i: (jnp.maximum(i * hpt - 1, 0), 0))
    next_halo = pl.BlockSpec((CONV_HALO, d), lambda i: (jnp.minimum((i + 1) * hpt, n // CONV_HALO - 1), 0))
    nslab = CONV_DIM // LANES
    tile_spec = pl.BlockSpec((tiles, D_NA, LANES), lambda i: (i, 0, 0))
    tile_shape = jax.ShapeDtypeStruct((n // LANES, D_NA, LANES), BF16)
    bf16_rows = 2 * SUBLANES
    assert all(w.shape[0] % (steps * bf16_rows) == 0 for w in later_weights)
    slabs = [pl.BlockSpec((w.shape[0] // steps, w.shape[1]), row) for w in later_weights]
    gains = [jnp.broadcast_to(g.astype(F32)[:, None], (NA_HEAD_DIM, tm)) for g in (q_gain, k_gain)]
    outs = pl.pallas_call(
        functools.partial(_inproj_body, tiles_per_seq=seq_len // tm),
        grid=(steps,),
        in_specs=[pl.BlockSpec((tm, d), row), prev_halo, next_halo, _const_spec((1, d)), _const_spec(w_bf.shape),
                  _const_spec(gains[0].shape), _const_spec(gains[1].shape),
                  _const_spec(conv_w.shape), _const_spec(conv_b.shape)] + slabs,
        out_specs=[pl.BlockSpec((tm, D_SSM), row), pl.BlockSpec((tm, CONV_DIM), row), tile_spec,
                   pl.BlockSpec((tm, D_NA), row), tile_spec, pl.BlockSpec((tm, LANES), row)] + slabs,
        out_shape=[jax.ShapeDtypeStruct((n, D_SSM), F32), jax.ShapeDtypeStruct((n, CONV_DIM), F32), tile_shape,
                   jax.ShapeDtypeStruct((n, D_NA), BF16), tile_shape, jax.ShapeDtypeStruct((n, LANES), F32)]
        + [jax.ShapeDtypeStruct(w.shape, BF16) for w in later_weights],
        scratch_shapes=[pltpu.VMEM((nslab, tm + 2 * CONV_HALO, LANES), F32),
                        pltpu.VMEM((tm // CHUNK * nslab, SUBLANES * CONV_STRIDE, LANES), F32)],
        compiler_params=pltpu.CompilerParams(
            dimension_semantics=("arbitrary",), vmem_limit_bytes=VMEM_LIMIT),
    )(x2, x2, x2, norm_w, w_bf, *gains, conv_w, conv_b, *later_weights)
    return outs[:6], outs[6:]


def _tri_masks():
    t = lax.broadcasted_iota(jnp.int32, (CHUNK, CHUNK), 0)
    s = lax.broadcasted_iota(jnp.int32, (CHUNK, CHUNK), 1)
    return s <= t, s >= t


def _dt_and_loga(dt_ref, dtb_ref, alog_ref):
    dtv = _softplus(dt_ref[0] + dtb_ref[...])
    return dtv, dtv * (-LOG2E * jnp.exp(alog_ref[...]))


def _ssd_bwd_body(xc_ref, dt_ref, dtb_ref, alog_ref, eb_ref, hb_ref, state):
    @pl.when(pl.program_id(1) == 0)
    def _():
        state[...] = jnp.zeros_like(state)

    _, upper = _tri_masks()
    upper_sel = jnp.where(upper, 1.0, 0.0).astype(BF16)
    for ch in reversed(range(BWD_CHUNKS)):
        rows = slice(ch * CHUNK, (ch + 1) * CHUNK)
        dtv, loga = _dt_and_loga(dt_ref.at[:, rows], dtb_ref, alog_ref)
        rcum = _dot_sel_lhs(upper_sel, loga)
        r0 = rcum[0:1, :]
        wgt = dtv * jnp.exp2(r0 - rcum)
        wgt_x = _dot(_pack3(wgt), eb_ref[...])
        dec_x = _dot(_pack3(jnp.broadcast_to(jnp.exp2(r0), (SUBLANES, LANES))), eb_ref[...])[0:1]

        hb_ref[0, ch] = state[...].astype(BF16)
        xw = (xc_ref[0, rows, 0:D_SSM] * wgt_x).astype(BF16)
        for g in range(SSM_GROUPS):
            gs = slice(g * GROUP_W, (g + 1) * GROUP_W)
            b_t = xc_ref[0, rows, D_SSM + g * D_STATE:D_SSM + (g + 1) * D_STATE].T.astype(BF16)
            state[:, gs] = state[:, gs] * dec_x[:, gs] + _dot(b_t, xw[:, gs])


def _ssd_bwd(xc, dt, dt_bias, a_log, sel_b):
    bsz, t, _ = xc.shape
    rows = BWD_CHUNKS * CHUNK
    nsteps = t // rows
    rev = lambda c: nsteps - 1 - c
    return pl.pallas_call(
        _ssd_bwd_body,
        grid=(bsz, nsteps),
        in_specs=[
            pl.BlockSpec((1, rows, D_SSM + SSM_GROUPS * D_STATE), lambda b, c: (b, rev(c), 0)),
            pl.BlockSpec((1, rows, LANES), lambda b, c: (b, rev(c), 0)),
            _const_spec(dt_bias.shape), _const_spec(a_log.shape), _const_spec(sel_b.shape),
        ],
        out_specs=pl.BlockSpec((1, BWD_CHUNKS, D_STATE, D_SSM), lambda b, c: (b, rev(c), 0, 0)),
        out_shape=jax.ShapeDtypeStruct((bsz, t // CHUNK, D_STATE, D_SSM), BF16),
        scratch_shapes=[pltpu.VMEM((D_STATE, D_SSM), F32)],
        compiler_params=pltpu.CompilerParams(
            dimension_semantics=("arbitrary", "arbitrary"), vmem_limit_bytes=VMEM_LIMIT),
    )(xc, dt, dt_bias, a_log, sel_b)


def _ssd_fwd_body(xc_ref, dt_ref, hb_ref, dtb_ref, alog_ref, dsk_ref, ef_ref, eb_ref, y_ref, state):
    @pl.when(pl.program_id(1) == 0)
    def _():
        state[...] = jnp.zeros_like(state)

    for sub in range(FWD_CHUNKS):
        rows = slice(sub * CHUNK, (sub + 1) * CHUNK)
        _ssd_fwd_chunk(xc_ref.at[0, rows], dt_ref.at[:, rows], hb_ref.at[0, sub], dtb_ref, alog_ref,
                       dsk_ref, ef_ref, eb_ref, y_ref.at[0, rows], state)


def _ssd_fwd_chunk(xc_ref, dt_ref, hb_ref, dtb_ref, alog_ref, dsk_ref, ef_ref, eb_ref, y_ref, state):
    dtv, loga = _dt_and_loga(dt_ref, dtb_ref, alog_ref)
    lower, upper = _tri_masks()
    cum_f = _dot_sel_lhs(jnp.where(lower, 1.0, 0.0).astype(BF16), loga)
    cum_b = _dot_sel_lhs(jnp.where(upper, 1.0, 0.0).astype(BF16), loga)
    lane = lax.broadcasted_iota(jnp.int32, (CHUNK, LANES), 1)
    cum = jnp.where((lane & SSM_HEADS) == 0, cum_f, cum_b)
    ecum_p = _pack3(jnp.exp2(cum))
    wgt_p = _pack3(dtv * jnp.exp2(cum_f[CHUNK - 1:CHUNK, :] - cum_f))
    src_t = (cum - jnp.log2(dtv)).T
    first_half = lane < SSM_HEAD_DIM

    for g in range(SSM_GROUPS):
        gs = slice(g * GROUP_W, (g + 1) * GROUP_W)
        b_f32 = xc_ref[:, D_SSM + g * D_STATE:D_SSM + (g + 1) * D_STATE]
        b_g = b_f32.astype(BF16)
        c0 = D_SSM + SSM_GROUPS * D_STATE + g * D_STATE
        c_g = xc_ref[:, c0:c0 + D_STATE].astype(BF16)
        cb = _dot_nt(c_g, b_g)
        ea_g = _dot(ecum_p, ef_ref[:, gs])
        er_g = _dot(ecum_p, eb_ref[:, gs])
        y_off = _dot(c_g, state[:, gs].astype(BF16)) * ea_g + _dot(c_g, hb_ref[:, gs]) * er_g
        xw = (xc_ref[:, gs] * _dot(wgt_p, ef_ref[:, gs])).astype(BF16)
        state[:, gs] = state[:, gs] * ea_g[CHUNK - 1:CHUNK, :] + _dot(b_f32.T.astype(BF16), xw)
        pairs = GROUP_W // LANES
        for pp in range(pairs):
            pr = g * pairs + pp
            ps = slice(pr * LANES, (pr + 1) * LANES)
            mats = []
            for h in (2 * pr, 2 * pr + 1):
                hb = SSM_HEADS + h
                seg_f = jnp.broadcast_to(cum[:, h:h + 1], (CHUNK, CHUNK)) - src_t[h:h + 1, :]
                seg_b = jnp.broadcast_to(cum[:, hb:hb + 1], (CHUNK, CHUNK)) - src_t[hb:hb + 1, :]
                d_f = jnp.exp2(jnp.where(lower, seg_f, MASKED))
                d_b = jnp.exp2(jnp.where(upper, seg_b, MASKED))
                mats.append(((d_f + d_b) * cb).astype(BF16))
            xs = xc_ref[:, ps]
            xp = xs.astype(BF16)
            zero = jnp.zeros_like(xp)
            rhs = jnp.concatenate([jnp.where(first_half, xp, zero), jnp.where(first_half, zero, xp)], axis=0)
            y_diag = _dot(jnp.concatenate(mats, axis=1), rhs)
            y_ref[:, ps] = y_diag + y_off[:, pp * LANES:(pp + 1) * LANES] + dsk_ref[:, ps] * xs


def _ssd_fwd(xc, dt, hb, dt_bias, a_log, d_skip_x, sel_f, sel_b):
    bsz, t, _ = xc.shape
    step = FWD_CHUNKS * CHUNK
    blk = lambda w: pl.BlockSpec((1, step, w), lambda b, c: (b, c, 0))
    return pl.pallas_call(
        _ssd_fwd_body,
        grid=(bsz, t // step),
        in_specs=[
            blk(CONV_DIM), blk(LANES),
            pl.BlockSpec((1, FWD_CHUNKS, D_STATE, D_SSM), lambda b, c: (b, c, 0, 0)),
            _const_spec(dt_bias.shape), _const_spec(a_log.shape), _const_spec(d_skip_x.shape),
            _const_spec(sel_f.shape), _const_spec(sel_b.shape),
        ],
        out_specs=blk(D_SSM),
        out_shape=jax.ShapeDtypeStruct((bsz, t, D_SSM), F32),
        scratch_shapes=[pltpu.VMEM((D_STATE, D_SSM), F32)],
        compiler_params=pltpu.CompilerParams(
            dimension_semantics=("arbitrary", "arbitrary"), vmem_limit_bytes=VMEM_LIMIT),
    )(xc, dt, hb, dt_bias, a_log, d_skip_x, sel_f, sel_b)


def _build_bias(g_ref, colmask_ref, bias):
    rel = _block_row_rel()
    half = lax.broadcasted_iota(jnp.int32, (GRID_W, LANES), 1) < GRID_W
    nrel = 2 * NA_ROWS

    @functools.cache
    def toeplitz(hh, r, upper_half):
        if upper_half:
            return pltpu.roll(toeplitz(hh, r, False), GRID_W, 1)
        row = g_ref[0, hh * nrel + r:hh * nrel + r + 1, :] * LOG2E
        return pltpu.roll(jnp.broadcast_to(row, (GRID_W, LANES)), 0, 1, stride=1, stride_axis=0)

    for kind in range(rel.shape[0]):
        for hh in range(2):
            for u in range(0, NA_BLOCK, 2):
                lane0 = (hh * NA_BLOCK + u) * GRID_W
                for a in range(NA_SPAN):
                    r_lo, r_hi = int(rel[kind, u, a]), int(rel[kind, u + 1, a])
                    lo = toeplitz(hh, r_lo, False) if r_lo >= 0 else MASKED
                    hi = toeplitz(hh, r_hi, True) if r_hi >= 0 else MASKED
                    tile = jnp.where(half, lo, hi) + colmask_ref[...]
                    bias[kind, a * GRID_W:(a + 1) * GRID_W, lane0:lane0 + 2 * GRID_W] = tile


def _na_body(qt_ref, k_ref, vt_ref, g_ref, colmask_ref, o_ref, sbuf, bias):
    t = k_ref.shape[1]
    rows = t // GRID_W

    @pl.when(pl.program_id(1) == 0)
    def _():
        _build_bias(g_ref, colmask_ref, bias)

    nblk = rows // NA_BLOCK
    bq = NA_BLOCK * GRID_W

    def offsets(bi):
        i0 = bi * NA_BLOCK
        rlo = jnp.clip(i0 - NA_ROWS // 2, 0, rows - NA_SPAN)
        return pl.multiple_of(i0 * GRID_W, bq), pl.multiple_of(rlo * GRID_W, 2 * GRID_W)

    def scores(bi, slot):
        kind = jnp.where(bi == 0, 0, jnp.where(bi == nblk - 1, 2, 1))
        _, koff = offsets(bi)
        tile0 = bi * (bq // LANES)
        q_t = jnp.concatenate([qt_ref[tile0 + i] for i in range(bq // LANES)], axis=1)
        zero = jnp.zeros((NA_HEAD_DIM, bq), q_t.dtype)
        qs_t = jnp.concatenate([jnp.concatenate([q_t[:NA_HEAD_DIM], zero], axis=0),
                                jnp.concatenate([zero, q_t[NA_HEAD_DIM:]], axis=0)], axis=1)
        sbuf[slot] = _dot(k_ref[0, pl.ds(koff, NA_BKEYS), :], qs_t) + bias[kind]

    head_a_dims = lax.broadcasted_iota(jnp.int32, (LANES, bq), 0) < NA_HEAD_DIM

    def attend(bi, slot):
        qoff, koff = offsets(bi)
        s = sbuf[slot]
        p = jnp.exp2(s - jnp.max(s, axis=0, keepdims=True))
        denom = jnp.sum(p, axis=0, keepdims=True)
        tile0 = koff // LANES
        v_t = jnp.concatenate([vt_ref[tile0 + i] for i in range(NA_BKEYS // LANES)], axis=1)
        o_t = _dot(v_t, p.astype(BF16)) * (1.0 / denom)
        o_ref[0, pl.ds(qoff, bq), :] = jnp.where(head_a_dims, o_t[:, 0:bq], o_t[:, bq:]).T.astype(o_ref.dtype)

    scores(0, 0)

    def pair_step(i, carry):
        b0 = 2 * i
        scores(b0 + 1, 1)
        attend(b0, 0)
        scores(jnp.minimum(b0 + 2, nblk - 1), 0)
        attend(b0 + 1, 1)
        return carry

    lax.fori_loop(0, nblk // 2, pair_step, 0, unroll=4)


def _natten(qt, k, vt, g, colmask):
    bsz, t, _ = k.shape
    nq = 2 * NA_BLOCK * GRID_W
    blk = pl.BlockSpec((1, t, LANES), lambda p, b: (b, 0, p))
    tiles = pl.BlockSpec((t // LANES, LANES, LANES), lambda p, b: (b, p, 0))
    return pl.pallas_call(
        _na_body,
        grid=(NA_PAIRS, bsz),
        in_specs=[tiles, blk, tiles,
                  pl.BlockSpec((1,) + g.shape[1:], lambda p, b: (p, 0, 0)), _const_spec(colmask.shape)],
        out_specs=blk,
        out_shape=jax.ShapeDtypeStruct((bsz, t, D_NA), BF16),
        scratch_shapes=[pltpu.VMEM((2, NA_BKEYS, nq), F32), pltpu.VMEM((3, NA_BKEYS, nq), F32)],
        compiler_params=pltpu.CompilerParams(
            dimension_semantics=("arbitrary", "arbitrary"), vmem_limit_bytes=VMEM_LIMIT),
    )(qt, k, vt, g, colmask)


def _block_row_rel():
    rel = -np.ones((3, NA_BLOCK, NA_SPAN), np.int64)
    for u in range(NA_BLOCK):
        for a in range(NA_SPAN):
            if a < NA_ROWS:
                rel[0, u, a] = a - u + NA_ROWS - 1
            if u <= a < u + NA_ROWS:
                rel[1, u, a] = a - u + NA_ROWS - 1 - NA_ROWS // 2
            if a >= NA_SPAN - NA_ROWS:
                rel[2, u, a] = a - u + NA_BLOCK - NA_SPAN + NA_ROWS - 1
    return rel


def _na_bias_rows(rpb):
    centre = NA_COLS - 1
    rev = rpb[:, :, ::-1]
    rows = jnp.concatenate(
        [rev[:, :, centre:], jnp.zeros(rpb.shape[:2] + (LANES - rpb.shape[2],), rpb.dtype), rev[:, :, :centre]],
        axis=2).astype(F32)
    rows = jnp.pad(rows, ((0, 0), (0, 2 * NA_ROWS - rpb.shape[1]), (0, 0)))
    return rows.reshape(NA_PAIRS, 2 * 2 * NA_ROWS, LANES)


def _na_col_mask():
    j = np.arange(GRID_W)
    c0 = np.clip(j - NA_COLS // 2, 0, GRID_W - NA_COLS)
    cc = np.arange(GRID_W)
    valid = (cc[:, None] >= c0[None, :]) & (cc[:, None] < c0[None, :] + NA_COLS)
    return jnp.asarray(np.tile(np.where(valid, 0.0, MASKED), (1, 2)), F32)


FF_STEP = 512


def _outmlp_body(x_ref, ys_ref, gate_ref, gw_ref, yn_ref, wo_ref, nw_ref, w1_ref, w2_ref, o_ref):
    dmix = ys_ref.shape[-1]
    groups = []
    for g in range(SSM_GROUPS):
        gs = slice(g * GROUP_W, (g + 1) * GROUP_W)
        gated = ys_ref[:, gs] * gate_ref[:, gs]
        gms = jnp.mean(gated * gated, axis=-1, keepdims=True)
        groups.append((gated * lax.rsqrt(gms + EPS) * gw_ref[:, gs]).astype(BF16))
    y_ssm = jnp.concatenate(groups, axis=1)
    x1 = x_ref[...] + _dot(y_ssm, wo_ref[0:dmix, :]) + _dot(yn_ref[...], wo_ref[dmix:, :])
    ms = jnp.mean(x1 * x1, axis=-1, keepdims=True)
    h = (x1 * lax.rsqrt(ms + EPS) * nw_ref[...]).astype(BF16)
    o_ref[...] = x1
    for f in range(0, w1_ref.shape[1], FF_STEP):
        u = jnp.maximum(_dot(h, w1_ref[:, f:f + FF_STEP]), 0.0)
        o_ref[...] += _dot((u * u).astype(BF16), w2_ref[f:f + FF_STEP, :])


def _outmlp(x2, ys, gate, gate_norm_w, yn, w_out, norm_w, w1, w2, tm):
    n, d = x2.shape
    row = lambda i: (i, 0)
    return pl.pallas_call(
        _outmlp_body,
        grid=(n // tm,),
        in_specs=[
            pl.BlockSpec((tm, d), row), pl.BlockSpec((tm, ys.shape[1]), row), pl.BlockSpec((tm, gate.shape[1]), row),
            _const_spec(gate_norm_w.shape), pl.BlockSpec((tm, yn.shape[1]), row),
            _const_spec(w_out.shape), _const_spec((1, d)), _const_spec(w1.shape), _const_spec(w2.shape),
        ],
        out_specs=pl.BlockSpec((tm, d), row),
        out_shape=jax.ShapeDtypeStruct((n, d), F32),
        compiler_params=pltpu.CompilerParams(
            dimension_semantics=("arbitrary",), vmem_limit_bytes=VMEM_LIMIT),
    )(x2, ys, gate, gate_norm_w, yn, w_out, norm_w, w1, w2)


def _head_select(first_lane):
    sel = np.zeros((LANES, D_SSM), np.float32)
    for rep in range(DT_COPIES):
        for h in range(SSM_HEADS):
            sel[rep * DT_LANES + first_lane + h, h * SSM_HEAD_DIM:(h + 1) * SSM_HEAD_DIM] = 1.0
    return jnp.asarray(sel, BF16)


def _dt_lanes(fwd, bwd):
    v = jnp.tile(jnp.concatenate([fwd, bwd]).astype(F32), DT_COPIES)
    return jnp.pad(v, (0, LANES - v.shape[0]))[None, :]


def _layer(x, norm_mix_w, w_in, conv_w, conv_b, dt_bias_fwd, dt_bias_bwd, a_log_fwd, a_log_bwd, d_skip,
           ssm_norm_w, q_norm_w, k_norm_w, rel_pos_bias, w_out, norm_mlp_w, w_mlp_in, w_mlp_out, tm):
    bsz, t, d = x.shape
    n = bsz * t
    x2 = x.reshape(n, d)
    rows = t // GRID_W
    assert t % tm == 0 and tm % CHUNK == 0, (t, tm)
    assert t % (max(FWD_CHUNKS, BWD_CHUNKS) * CHUNK) == 0, t
    assert t % GRID_W == 0 and rows % (2 * NA_BLOCK) == 0 and rows >= NA_SPAN, t
    assert w_in.shape == (d, O_V + D_NA), w_in.shape

    (gate, xc, qt, k, vt, dt), (w_out_bf, w1_bf, w2_bf) = _inproj(
        x2, t, norm_mix_w[None, :], w_in.T.astype(BF16), q_norm_w.astype(F32) * (NA_HEAD_DIM ** -0.5 * LOG2E), k_norm_w,
        conv_w.astype(F32), conv_b.astype(F32)[None, :], (w_out, w_mlp_in, w_mlp_out), tm)
    r3 = lambda a: a.reshape(bsz, t, a.shape[-1])
    xc = r3(xc)

    dt_bias = _dt_lanes(dt_bias_fwd, dt_bias_bwd)
    a_log = _dt_lanes(a_log_fwd, a_log_bwd)
    sel_f, sel_b = _head_select(0), _head_select(SSM_HEADS)
    hb = _ssd_bwd(xc, r3(dt), dt_bias, a_log, sel_b)
    y_raw = _ssd_fwd(xc, r3(dt), hb, dt_bias, a_log, jnp.repeat(d_skip.astype(F32), SSM_HEAD_DIM)[None, :],
                     sel_f, sel_b)

    y_na = _natten(qt, r3(k), vt, _na_bias_rows(rel_pos_bias), _na_col_mask())

    out = _outmlp(x2, y_raw.reshape(n, D_SSM), gate, ssm_norm_w[None, :].astype(F32), y_na.reshape(n, D_NA),
                  w_out_bf, norm_mlp_w[None, :], w1_bf, w2_bf, tm)
    return out.reshape(bsz, t, d)


def kernel(x, norm_mix_w, w_in, conv_w, conv_b, dt_bias_fwd, dt_bias_bwd, a_log_fwd, a_log_bwd, d_skip,
           ssm_norm_w, q_norm_w, k_norm_w, rel_pos_bias, w_out, norm_mlp_w, w_mlp_in, w_mlp_out):
    tm = min(512, x.shape[0] * x.shape[1])
    for layer in range(norm_mix_w.shape[0]):
        x = _layer(x, norm_mix_w[layer], w_in[layer], conv_w[layer], conv_b[layer], dt_bias_fwd[layer],
                   dt_bias_bwd[layer], a_log_fwd[layer], a_log_bwd[layer], d_skip[layer], ssm_norm_w[layer],
                   q_norm_w[layer], k_norm_w[layer], rel_pos_bias[layer], w_out[layer], norm_mlp_w[layer],
                   w_mlp_in[layer], w_mlp_out[layer], tm)
    return x
```

```python
import functools

import numpy as np
import jax
import jax.numpy as jnp
from jax import lax
from jax.experimental import pallas as pl
from jax.experimental.pallas import tpu as pltpu

F32 = jnp.float32
BF16 = jnp.bfloat16

LANES = 128
SUBLANES = 8
GRID_W = 64
SSM_HEADS = 16
SSM_HEAD_DIM = 64
D_SSM = SSM_HEADS * SSM_HEAD_DIM
SSM_GROUPS = 2
GROUP_W = D_SSM // SSM_GROUPS
D_STATE = 128
D_CONV = 5
CHUNK = 128
FWD_CHUNKS = 4
BWD_CHUNKS = 4
CONV_DIM = D_SSM + 2 * SSM_GROUPS * D_STATE
DT_LANES = 2 * SSM_HEADS
DT_COPIES = 3
NA_HEADS = 16
NA_HEAD_DIM = 64
D_NA = NA_HEADS * NA_HEAD_DIM
NA_ROWS = 8
NA_COLS = 16
NA_PAIRS = NA_HEADS // 2
NA_BLOCK = 4
NA_SPAN = NA_ROWS + NA_BLOCK
NA_BKEYS = NA_SPAN * GRID_W
LOG2E = 1.4426950408889634
EPS = 1e-5
MASKED = -1e30
CONV_HALO = 2 * SUBLANES
CONV_STRIDE = CHUNK // SUBLANES + 1
VMEM_LIMIT = 56 * 1024 * 1024


def _dot(a, b):
    return jnp.dot(a, b, preferred_element_type=F32)


def _dot_nt(a, b):
    return lax.dot_general(a, b, (((1,), (1,)), ((), ())), preferred_element_type=F32)


def _split3(x):
    hi = x.astype(BF16)
    r1 = x - hi.astype(F32)
    mid = r1.astype(BF16)
    lo = (r1 - mid.astype(F32)).astype(BF16)
    return hi, mid, lo


def _dot_sel_lhs(sel, x):
    hi, mid, lo = _split3(x)
    return _dot(sel, hi) + _dot(sel, mid) + _dot(sel, lo)


def _pack3(x):
    lane = lax.broadcasted_iota(jnp.int32, x.shape, x.ndim - 1)
    hi = x.astype(BF16).astype(F32)
    r1 = x - hi
    mid = r1.astype(BF16).astype(F32)
    return jnp.where(lane < DT_LANES, hi, jnp.where(lane < 2 * DT_LANES, mid, r1 - mid)).astype(BF16)


def _softplus(x):
    return jnp.maximum(x, 0.0) + jnp.log(1.0 + jnp.exp(-jnp.abs(x)))


def _silu(x):
    half = 0.5 * x
    return half + half * jnp.tanh(half)


def _const_spec(shape):
    nd = len(shape)
    return pl.BlockSpec(shape, lambda *_: (0,) * nd, pipeline_mode=pl.Buffered(1))


O_DT = D_SSM + CONV_DIM
O_Q = O_DT + DT_LANES
O_K = O_Q + D_NA
O_V = O_K + D_NA


def _head_rmsnorm_t(xt, gain_ref):
    tokens = xt.shape[1]
    x3 = xt.reshape(NA_HEADS, NA_HEAD_DIM, tokens)
    ms = jnp.mean(x3 * x3, axis=1, keepdims=True)
    return (x3 * lax.rsqrt(ms + EPS) * gain_ref[...]).reshape(D_NA, tokens)


def _conv_silu(uext, cout, cw_ref, cb_ref, out_ref, first):
    nslab = uext.shape[0]
    for slab in range(nslab):
        cols = slice(slab * LANES, (slab + 1) * LANES)
        bias = jnp.broadcast_to(cb_ref[:, cols], (SUBLANES, LANES))
        taps = [jnp.broadcast_to(cw_ref[k:k + 1, cols], (SUBLANES, LANES)) for k in range(D_CONV)]
        for ch in range(out_ref.shape[0] // CHUNK):
            for i in range(CONV_STRIDE):
                acc = bias
                for k in range(D_CONV):
                    start = ch * CHUNK + first + k + i
                    acc = acc + taps[k] * uext[slab, pl.ds(start, SUBLANES, stride=CONV_STRIDE), :]
                cout[ch * nslab + slab, pl.ds(i, SUBLANES, stride=CONV_STRIDE), :] = _silu(acc)
            out_ref[ch * CHUNK:(ch + 1) * CHUNK, cols] = cout[ch * nslab + slab, 0:CHUNK, :]


def _inproj_body(x_ref, xp_ref, xn_ref, nw_ref, wt_ref, qg_ref, kg_ref, cw_ref, cb_ref, *refs, tiles_per_seq):
    uext, cout = refs[-2:]
    refs = refs[:-2]
    ncast = (len(refs) - 6) // 2
    cast_in, (z_ref, xc_ref, qt_ref, k_ref, vt_ref, dt_ref), cast_out = refs[:ncast], refs[ncast:ncast + 6], refs[ncast + 6:]
    for src, dst in zip(cast_in, cast_out):
        dst[...] = src[...].astype(dst.dtype)
    tm = x_ref.shape[0]
    xe = jnp.concatenate([xp_ref[...], x_ref[...], xn_ref[...]], axis=0)
    ms = jnp.mean(xe * xe, axis=-1, keepdims=True)
    he = (xe * lax.rsqrt(ms + EPS) * nw_ref[...]).astype(BF16)
    h = he[CONV_HALO:CONV_HALO + tm]

    xbc = _dot_nt(he, wt_ref[D_SSM:O_DT, :])
    pos = pl.program_id(0) % tiles_per_seq
    for slab in range(uext.shape[0]):
        cols = slice(slab * LANES, (slab + 1) * LANES)
        uext[slab, 0:CONV_HALO, :] = jnp.where(pos > 0, xbc[0:CONV_HALO, cols], 0.0)
        uext[slab, CONV_HALO:CONV_HALO + tm, :] = xbc[CONV_HALO:CONV_HALO + tm, cols]
        uext[slab, CONV_HALO + tm:, :] = jnp.where(pos < tiles_per_seq - 1, xbc[CONV_HALO + tm:, cols], 0.0)
    _conv_silu(uext, cout, cw_ref, cb_ref, xc_ref, CONV_HALO - (D_CONV - 1) // 2)
    z_ref[...] = _silu(_dot_nt(h, wt_ref[0:D_SSM, :]))

    w_dt = wt_ref[O_DT:O_Q, :]
    pad = jnp.zeros((LANES - DT_COPIES * DT_LANES, w_dt.shape[1]), w_dt.dtype)
    dt_ref[...] = _dot_nt(h, jnp.concatenate([w_dt] * DT_COPIES + [pad], axis=0))
    qt = _head_rmsnorm_t(_dot_nt(wt_ref[O_Q:O_K, :], h), qg_ref)
    k_ref[...] = _head_rmsnorm_t(_dot_nt(wt_ref[O_K:O_V, :], h), kg_ref).T.astype(k_ref.dtype)
    vt = _dot_nt(wt_ref[O_V:, :], h)
    for j in range(vt_ref.shape[0]):
        cols = slice(j * LANES, (j + 1) * LANES)
        qt_ref[j] = qt[:, cols].astype(qt_ref.dtype)
        vt_ref[j] = vt[:, cols].astype(vt_ref.dtype)


def _inproj(x2, seq_len, norm_w, w_bf, q_gain, k_gain, conv_w, conv_b, later_weights, tm):
    n, d = x2.shape
    steps = n // tm
    row = lambda i: (i, 0)
    tiles = tm // LANES
    hpt = tm // CONV_HALO
    prev_halo = pl.BlockSpec((CONV_HALO, d), lambda i: (jnp.maximum(i * hpt - 1, 0), 0))
    next_halo = pl.BlockSpec((CONV_HALO, d), lambda i: (jnp.minimum((i + 1) * hpt, n // CONV_HALO - 1), 0))
    nslab = CONV_DIM // LANES
    tile_spec = pl.BlockSpec((tiles, D_NA, LANES), lambda i: (i, 0, 0))
    tile_shape = jax.ShapeDtypeStruct((n // LANES, D_NA, LANES), BF16)
    bf16_rows = 2 * SUBLANES
    assert all(w.shape[0] % (steps * bf16_rows) == 0 for w in later_weights)
    slabs = [pl.BlockSpec((w.shape[0] // steps, w.shape[1]), row) for w in later_weights]
    gains = [jnp.broadcast_to(g.astype(F32)[:, None], (NA_HEAD_DIM, tm)) for g in (q_gain, k_gain)]
    outs = pl.pallas_call(
        functools.partial(_inproj_body, tiles_per_seq=seq_len // tm),
        grid=(steps,),
        in_specs=[pl.BlockSpec((tm, d), row), prev_halo, next_halo, _const_spec((1, d)), _const_spec(w_bf.shape),
                  _const_spec(gains[0].shape), _const_spec(gains[1].shape),
                  _const_spec(conv_w.shape), _const_spec(conv_b.shape)] + slabs,
        out_specs=[pl.BlockSpec((tm, D_SSM), row), pl.BlockSpec((tm, CONV_DIM), row), tile_spec,
                   pl.BlockSpec((tm, D_NA), row), tile_spec, pl.BlockSpec((tm, LANES), row)] + slabs,
        out_shape=[jax.ShapeDtypeStruct((n, D_SSM), F32), jax.ShapeDtypeStruct((n, CONV_DIM), F32), tile_shape,
                   jax.ShapeDtypeStruct((n, D_NA), BF16), tile_shape, jax.ShapeDtypeStruct((n, LANES), F32)]
        + [jax.ShapeDtypeStruct(w.shape, BF16) for w in later_weights],
        scratch_shapes=[pltpu.VMEM((nslab, tm + 2 * CONV_HALO, LANES), F32),
                        pltpu.VMEM((tm // CHUNK * nslab, SUBLANES * CONV_STRIDE, LANES), F32)],
        compiler_params=pltpu.CompilerParams(
            dimension_semantics=("arbitrary",), vmem_limit_bytes=VMEM_LIMIT),
    )(x2, x2, x2, norm_w, w_bf, *gains, conv_w, conv_b, *later_weights)
    return outs[:6], outs[6:]


def _tri_masks():
    t = lax.broadcasted_iota(jnp.int32, (CHUNK, CHUNK), 0)
    s = lax.broadcasted_iota(jnp.int32, (CHUNK, CHUNK), 1)
    return s <= t, s >= t


def _dt_and_loga(dt_ref, dtb_ref, alog_ref):
    dtv = _softplus(dt_ref[0] + dtb_ref[...])
    return dtv, dtv * (-LOG2E * jnp.exp(alog_ref[...]))


def _ssd_bwd_body(xc_ref, dt_ref, dtb_ref, alog_ref, eb_ref, hb_ref, state):
    @pl.when(pl.program_id(1) == 0)
    def _():
        state[...] = jnp.zeros_like(state)

    _, upper = _tri_masks()
    upper_sel = jnp.where(upper, 1.0, 0.0).astype(BF16)
    for ch in reversed(range(BWD_CHUNKS)):
        rows = slice(ch * CHUNK, (ch + 1) * CHUNK)
        dtv, loga = _dt_and_loga(dt_ref.at[:, rows], dtb_ref, alog_ref)
        rcum = _dot_sel_lhs(upper_sel, loga)
        r0 = rcum[0:1, :]
        wgt = dtv * jnp.exp2(r0 - rcum)
        wgt_x = _dot(_pack3(wgt), eb_ref[...])
        dec_x = _dot(_pack3(jnp.broadcast_to(jnp.exp2(r0), (SUBLANES, LANES))), eb_ref[...])[0:1]

        hb_ref[0, ch] = state[...].astype(BF16)
        xw = (xc_ref[0, rows, 0:D_SSM] * wgt_x).astype(BF16)
        for g in range(SSM_GROUPS):
            gs = slice(g * GROUP_W, (g + 1) * GROUP_W)
            b_t = xc_ref[0, rows, D_SSM + g * D_STATE:D_SSM + (g + 1) * D_STATE].T.astype(BF16)
            state[:, gs] = state[:, gs] * dec_x[:, gs] + _dot(b_t, xw[:, gs])


def _ssd_bwd(xc, dt, dt_bias, a_log, sel_b):
    bsz, t, _ = xc.shape
    rows = BWD_CHUNKS * CHUNK
    nsteps = t // rows
    rev = lambda c: nsteps - 1 - c
    return pl.pallas_call(
        _ssd_bwd_body,
        grid=(bsz, nsteps),
        in_specs=[
            pl.BlockSpec((1, rows, D_SSM + SSM_GROUPS * D_STATE), lambda b, c: (b, rev(c), 0)),
            pl.BlockSpec((1, rows, LANES), lambda b, c: (b, rev(c), 0)),
            _const_spec(dt_bias.shape), _const_spec(a_log.shape), _const_spec(sel_b.shape),
        ],
        out_specs=pl.BlockSpec((1, BWD_CHUNKS, D_STATE, D_SSM), lambda b, c: (b, rev(c), 0, 0)),
        out_shape=jax.ShapeDtypeStruct((bsz, t // CHUNK, D_STATE, D_SSM), BF16),
        scratch_shapes=[pltpu.VMEM((D_STATE, D_SSM), F32)],
        compiler_params=pltpu.CompilerParams(
            dimension_semantics=("arbitrary", "arbitrary"), vmem_limit_bytes=VMEM_LIMIT),
    )(xc, dt, dt_bias, a_log, sel_b)


def _ssd_fwd_body(xc_ref, dt_ref, hb_ref, dtb_ref, alog_ref, dsk_ref, ef_ref, eb_ref, y_ref, state):
    @pl.when(pl.program_id(1) == 0)
    def _():
        state[...] = jnp.zeros_like(state)

    for sub in range(FWD_CHUNKS):
        rows = slice(sub * CHUNK, (sub + 1) * CHUNK)
        _ssd_fwd_chunk(xc_ref.at[0, rows], dt_ref.at[:, rows], hb_ref.at[0, sub], dtb_ref, alog_ref,
                       dsk_ref, ef_ref, eb_ref, y_ref.at[0, rows], state)


def _ssd_fwd_chunk(xc_ref, dt_ref, hb_ref, dtb_ref, alog_ref, dsk_ref, ef_ref, eb_ref, y_ref, state):
    dtv, loga = _dt_and_loga(dt_ref, dtb_ref, alog_ref)
    lower, upper = _tri_masks()
    cum_f = _dot_sel_lhs(jnp.where(lower, 1.0, 0.0).astype(BF16), loga)
    cum_b = _dot_sel_lhs(jnp.where(upper, 1.0, 0.0).astype(BF16), loga)
    lane = lax.broadcasted_iota(jnp.int32, (CHUNK, LANES), 1)
    cum = jnp.where((lane & SSM_HEADS) == 0, cum_f, cum_b)
    ecum_p = _pack3(jnp.exp2(cum))
    wgt_p = _pack3(dtv * jnp.exp2(cum_f[CHUNK - 1:CHUNK, :] - cum_f))
    src_t = (cum - jnp.log2(dtv)).T
    first_half = lane < SSM_HEAD_DIM

    for g in range(SSM_GROUPS):
        gs = slice(g * GROUP_W, (g + 1) * GROUP_W)
        b_f32 = xc_ref[:, D_SSM + g * D_STATE:D_SSM + (g + 1) * D_STATE]
        b_g = b_f32.astype(BF16)
        c0 = D_SSM + SSM_GROUPS * D_STATE + g * D_STATE
        c_g = xc_ref[:, c0:c0 + D_STATE].astype(BF16)
        cb = _dot_nt(c_g, b_g)
        ea_g = _dot(ecum_p, ef_ref[:, gs])
        er_g = _dot(ecum_p, eb_ref[:, gs])
        y_off = _dot(c_g, state[:, gs].astype(BF16)) * ea_g + _dot(c_g, hb_ref[:, gs]) * er_g
        xw = (xc_ref[:, gs] * _dot(wgt_p, ef_ref[:, gs])).astype(BF16)
        state[:, gs] = state[:, gs] * ea_g[CHUNK - 1:CHUNK, :] + _dot(b_f32.T.astype(BF16), xw)
        pairs = GROUP_W // LANES
        for pp in range(pairs):
            pr = g * pairs + pp
            ps = slice(pr * LANES, (pr + 1) * LANES)
            mats = []
            for h in (2 * pr, 2 * pr + 1):
                hb = SSM_HEADS + h
                seg_f = jnp.broadcast_to(cum[:, h:h + 1], (CHUNK, CHUNK)) - src_t[h:h + 1, :]
                seg_b = jnp.broadcast_to(cum[:, hb:hb + 1], (CHUNK, CHUNK)) - src_t[hb:hb + 1, :]
                d_f = jnp.exp2(jnp.where(lower, seg_f, MASKED))
                d_b = jnp.exp2(jnp.where(upper, seg_b, MASKED))
                mats.append(((d_f + d_b) * cb).astype(BF16))
            xs = xc_ref[:, ps]
            xp = xs.astype(BF16)
            zero = jnp.zeros_like(xp)
            rhs = jnp.concatenate([jnp.where(first_half, xp, zero), jnp.where(first_half, zero, xp)], axis=0)
            y_diag = _dot(jnp.concatenate(mats, axis=1), rhs)
            y_ref[:, ps] = y_diag + y_off[:, pp * LANES:(pp + 1) * LANES] + dsk_ref[:, ps] * xs


def _ssd_fwd(xc, dt, hb, dt_bias, a_log, d_skip_x, sel_f, sel_b):
    bsz, t, _ = xc.shape
    step = FWD_CHUNKS * CHUNK
    blk = lambda w: pl.BlockSpec((1, step, w), lambda b, c: (b, c, 0))
    return pl.pallas_call(
        _ssd_fwd_body,
        grid=(bsz, t // step),
        in_specs=[
            blk(CONV_DIM), blk(LANES),
            pl.BlockSpec((1, FWD_CHUNKS, D_STATE, D_SSM), lambda b, c: (b, c, 0, 0)),
            _const_spec(dt_bias.shape), _const_spec(a_log.shape), _const_spec(d_skip_x.shape),
            _const_spec(sel_f.shape), _const_spec(sel_b.shape),
        ],
        out_specs=blk(D_SSM),
        out_shape=jax.ShapeDtypeStruct((bsz, t, D_SSM), F32),
        scratch_shapes=[pltpu.VMEM((D_STATE, D_SSM), F32)],
        compiler_params=pltpu.CompilerParams(
            dimension_semantics=("arbitrary", "arbitrary"), vmem_limit_bytes=VMEM_LIMIT),
    )(xc, dt, hb, dt_bias, a_log, d_skip_x, sel_f, sel_b)


def _build_bias(g_ref, colmask_ref, bias):
    rel = _block_row_rel()
    half = lax.broadcasted_iota(jnp.int32, (GRID_W, LANES), 1) < GRID_W
    nrel = 2 * NA_ROWS

    @functools.cache
    def toeplitz(hh, r, upper_half):
        if upper_half:
            return pltpu.roll(toeplitz(hh, r, False), GRID_W, 1)
        row = g_ref[0, hh * nrel + r:hh * nrel + r + 1, :] * LOG2E
        return pltpu.roll(jnp.broadcast_to(row, (GRID_W, LANES)), 0, 1, stride=1, stride_axis=0)

    for kind in range(rel.shape[0]):
        for hh in range(2):
            for u in range(0, NA_BLOCK, 2):
                lane0 = (hh * NA_BLOCK + u) * GRID_W
                for a in range(NA_SPAN):
                    r_lo, r_hi = int(rel[kind, u, a]), int(rel[kind, u + 1, a])
                    lo = toeplitz(hh, r_lo, False) if r_lo >= 0 else MASKED
                    hi = toeplitz(hh, r_hi, True) if r_hi >= 0 else MASKED
                    tile = jnp.where(half, lo, hi) + colmask_ref[...]
                    bias[kind, a * GRID_W:(a + 1) * GRID_W, lane0:lane0 + 2 * GRID_W] = tile


def _na_body(qt_ref, k_ref, vt_ref, g_ref, colmask_ref, o_ref, sbuf, bias):
    t = k_ref.shape[1]
    rows = t // GRID_W

    @pl.when(pl.program_id(1) == 0)
    def _():
        _build_bias(g_ref, colmask_ref, bias)

    nblk = rows // NA_BLOCK
    bq = NA_BLOCK * GRID_W

    def offsets(bi):
        i0 = bi * NA_BLOCK
        rlo = jnp.clip(i0 - NA_ROWS // 2, 0, rows - NA_SPAN)
        return pl.multiple_of(i0 * GRID_W, bq), pl.multiple_of(rlo * GRID_W, 2 * GRID_W)

    def scores(bi, slot):
        kind = jnp.where(bi == 0, 0, jnp.where(bi == nblk - 1, 2, 1))
        _, koff = offsets(bi)
        tile0 = bi * (bq // LANES)
        q_t = jnp.concatenate([qt_ref[tile0 + i] for i in range(bq // LANES)], axis=1)
        zero = jnp.zeros((NA_HEAD_DIM, bq), q_t.dtype)
        qs_t = jnp.concatenate([jnp.concatenate([q_t[:NA_HEAD_DIM], zero], axis=0),
                                jnp.concatenate([zero, q_t[NA_HEAD_DIM:]], axis=0)], axis=1)
        sbuf[slot] = _dot(k_ref[0, pl.ds(koff, NA_BKEYS), :], qs_t) + bias[kind]

    def attend(bi, slot):
        qoff, koff = offsets(bi)
        s = sbuf[slot]
        p = jnp.exp2(s - jnp.max(s, axis=0, keepdims=True))
        denom = jnp.sum(p, axis=0, keepdims=True)
        tile0 = koff // LANES
        v_t = jnp.concatenate([vt_ref[tile0 + i] for i in range(NA_BKEYS // LANES)], axis=1)
        pw = p.astype(BF16)
        rden = 1.0 / denom
        o_a = _dot(v_t[:NA_HEAD_DIM], pw[:, :bq]) * rden[:, :bq]
        o_b = _dot(v_t[NA_HEAD_DIM:], pw[:, bq:]) * rden[:, bq:]
        o_ref[0, pl.ds(qoff, bq), :] = jnp.concatenate([o_a, o_b], axis=0).T.astype(o_ref.dtype)

    scores(0, 0)

    def pair_step(i, carry):
        b0 = 2 * i
        scores(b0 + 1, 1)
        attend(b0, 0)
        scores(jnp.minimum(b0 + 2, nblk - 1), 0)
        attend(b0 + 1, 1)
        return carry

    lax.fori_loop(0, nblk // 2, pair_step, 0, unroll=4)


def _natten(qt, k, vt, g, colmask):
    bsz, t, _ = k.shape
    nq = 2 * NA_BLOCK * GRID_W
    blk = pl.BlockSpec((1, t, LANES), lambda p, b: (b, 0, p))
    tiles = pl.BlockSpec((t // LANES, LANES, LANES), lambda p, b: (b, p, 0))
    return pl.pallas_call(
        _na_body,
        grid=(NA_PAIRS, bsz),
        in_specs=[tiles, blk, tiles,
                  pl.BlockSpec((1,) + g.shape[1:], lambda p, b: (p, 0, 0)), _const_spec(colmask.shape)],
        out_specs=blk,
        out_shape=jax.ShapeDtypeStruct((bsz, t, D_NA), BF16),
        scratch_shapes=[pltpu.VMEM((2, NA_BKEYS, nq), F32), pltpu.VMEM((3, NA_BKEYS, nq), F32)],
        compiler_params=pltpu.CompilerParams(
            dimension_semantics=("arbitrary", "arbitrary"), vmem_limit_bytes=VMEM_LIMIT),
    )(qt, k, vt, g, colmask)


def _block_row_rel():
    rel = -np.ones((3, NA_BLOCK, NA_SPAN), np.int64)
    for u in range(NA_BLOCK):
        for a in range(NA_SPAN):
            if a < NA_ROWS:
                rel[0, u, a] = a - u + NA_ROWS - 1
            if u <= a < u + NA_ROWS:
                rel[1, u, a] = a - u + NA_ROWS - 1 - NA_ROWS // 2
            if a >= NA_SPAN - NA_ROWS:
                rel[2, u, a] = a - u + NA_BLOCK - NA_SPAN + NA_ROWS - 1
    return rel


def _na_bias_rows(rpb):
    centre = NA_COLS - 1
    rev = rpb[:, :, ::-1]
    rows = jnp.concatenate(
        [rev[:, :, centre:], jnp.zeros(rpb.shape[:2] + (LANES - rpb.shape[2],), rpb.dtype), rev[:, :, :centre]],
        axis=2).astype(F32)
    rows = jnp.pad(rows, ((0, 0), (0, 2 * NA_ROWS - rpb.shape[1]), (0, 0)))
    return rows.reshape(NA_PAIRS, 2 * 2 * NA_ROWS, LANES)


def _na_col_mask():
    j = np.arange(GRID_W)
    c0 = np.clip(j - NA_COLS // 2, 0, GRID_W - NA_COLS)
    cc = np.arange(GRID_W)
    valid = (cc[:, None] >= c0[None, :]) & (cc[:, None] < c0[None, :] + NA_COLS)
    return jnp.asarray(np.tile(np.where(valid, 0.0, MASKED), (1, 2)), F32)


FF_STEP = 512


def _outmlp_body(x_ref, ys_ref, gate_ref, gw_ref, yn_ref, wo_ref, nw_ref, w1_ref, w2_ref, o_ref):
    dmix = ys_ref.shape[-1]
    groups = []
    for g in range(SSM_GROUPS):
        gs = slice(g * GROUP_W, (g + 1) * GROUP_W)
        gated = ys_ref[:, gs] * gate_ref[:, gs]
        gms = jnp.mean(gated * gated, axis=-1, keepdims=True)
        groups.append((gated * lax.rsqrt(gms + EPS) * gw_ref[:, gs]).astype(BF16))
    y_ssm = jnp.concatenate(groups, axis=1)
    x1 = x_ref[...] + _dot(y_ssm, wo_ref[0:dmix, :]) + _dot(yn_ref[...], wo_ref[dmix:, :])
    ms = jnp.mean(x1 * x1, axis=-1, keepdims=True)
    h = (x1 * lax.rsqrt(ms + EPS) * nw_ref[...]).astype(BF16)
    o_ref[...] = x1
    for f in range(0, w1_ref.shape[1], FF_STEP):
        u = jnp.maximum(_dot(h, w1_ref[:, f:f + FF_STEP]), 0.0)
        o_ref[...] += _dot((u * u).astype(BF16), w2_ref[f:f + FF_STEP, :])


def _outmlp(x2, ys, gate, gate_norm_w, yn, w_out, norm_w, w1, w2, tm):
    n, d = x2.shape
    row = lambda i: (i, 0)
    return pl.pallas_call(
        _outmlp_body,
        grid=(n // tm,),
        in_specs=[
            pl.BlockSpec((tm, d), row), pl.BlockSpec((tm, ys.shape[1]), row), pl.BlockSpec((tm, gate.shape[1]), row),
            _const_spec(gate_norm_w.shape), pl.BlockSpec((tm, yn.shape[1]), row),
            _const_spec(w_out.shape), _const_spec((1, d)), _const_spec(w1.shape), _const_spec(w2.shape),
        ],
        out_specs=pl.BlockSpec((tm, d), row),
        out_shape=jax.ShapeDtypeStruct((n, d), F32),
        compiler_params=pltpu.CompilerParams(
            dimension_semantics=("arbitrary",), vmem_limit_bytes=VMEM_LIMIT),
    )(x2, ys, gate, gate_norm_w, yn, w_out, norm_w, w1, w2)


def _head_select(first_lane):
    sel = np.zeros((LANES, D_SSM), np.float32)
    for rep in range(DT_COPIES):
        for h in range(SSM_HEADS):
            sel[rep * DT_LANES + first_lane + h, h * SSM_HEAD_DIM:(h + 1) * SSM_HEAD_DIM] = 1.0
    return jnp.asarray(sel, BF16)


def _dt_lanes(fwd, bwd):
    v = jnp.tile(jnp.concatenate([fwd, bwd]).astype(F32), DT_COPIES)
    return jnp.pad(v, (0, LANES - v.shape[0]))[None, :]


def _layer(x, norm_mix_w, w_in, conv_w, conv_b, dt_bias_fwd, dt_bias_bwd, a_log_fwd, a_log_bwd, d_skip,
           ssm_norm_w, q_norm_w, k_norm_w, rel_pos_bias, w_out, norm_mlp_w, w_mlp_in, w_mlp_out, tm):
    bsz, t, d = x.shape
    n = bsz * t
    x2 = x.reshape(n, d)
    rows = t // GRID_W
    assert t % tm == 0 and tm % CHUNK == 0, (t, tm)
    assert t % (max(FWD_CHUNKS, BWD_CHUNKS) * CHUNK) == 0, t
    assert t % GRID_W == 0 and rows % (2 * NA_BLOCK) == 0 and rows >= NA_SPAN, t
    assert w_in.shape == (d, O_V + D_NA), w_in.shape

    (gate, xc, qt, k, vt, dt), (w_out_bf, w1_bf, w2_bf) = _inproj(
        x2, t, norm_mix_w[None, :], w_in.T.astype(BF16), q_norm_w.astype(F32) * (NA_HEAD_DIM ** -0.5 * LOG2E), k_norm_w,
        conv_w.astype(F32), conv_b.astype(F32)[None, :], (w_out, w_mlp_in, w_mlp_out), tm)
    r3 = lambda a: a.reshape(bsz, t, a.shape[-1])
    xc = r3(xc)

    dt_bias = _dt_lanes(dt_bias_fwd, dt_bias_bwd)
    a_log = _dt_lanes(a_log_fwd, a_log_bwd)
    sel_f, sel_b = _head_select(0), _head_select(SSM_HEADS)
    hb = _ssd_bwd(xc, r3(dt), dt_bias, a_log, sel_b)
    y_raw = _ssd_fwd(xc, r3(dt), hb, dt_bias, a_log, jnp.repeat(d_skip.astype(F32), SSM_HEAD_DIM)[None, :],
                     sel_f, sel_b)

    y_na = _natten(qt, r3(k), vt, _na_bias_rows(rel_pos_bias), _na_col_mask())

    out = _outmlp(x2, y_raw.reshape(n, D_SSM), gate, ssm_norm_w[None, :].astype(F32), y_na.reshape(n, D_NA),
                  w_out_bf, norm_mlp_w[None, :], w1_bf, w2_bf, tm)
    return out.reshape(bsz, t, d)


def kernel(x, norm_mix_w, w_in, conv_w, conv_b, dt_bias_fwd, dt_bias_bwd, a_log_fwd, a_log_bwd, d_skip,
           ssm_norm_w, q_norm_w, k_norm_w, rel_pos_bias, w_out, norm_mlp_w, w_mlp_in, w_mlp_out):
    tm = min(512, x.shape[0] * x.shape[1])
    for layer in range(norm_mix_w.shape[0]):
        x = _layer(x, norm_mix_w[layer], w_in[layer], conv_w[layer], conv_b[layer], dt_bias_fwd[layer],
                   dt_bias_bwd[layer], a_log_fwd[layer], a_log_bwd[layer], d_skip[layer], ssm_norm_w[layer],
                   q_norm_w[layer], k_norm_w[layer], rel_pos_bias[layer], w_out[layer], norm_mlp_w[layer],
                   w_mlp_in[layer], w_mlp_out[layer], tm)
    return x
```

```python
import functools

import numpy as np
import jax
import jax.numpy as jnp
from jax import lax
from jax.experimental import pallas as pl
from jax.experimental.pallas import tpu as pltpu

F32 = jnp.float32
BF16 = jnp.bfloat16

LANES = 128
SUBLANES = 8
GRID_W = 64
SSM_HEADS = 16
SSM_HEAD_DIM = 64
D_SSM = SSM_HEADS * SSM_HEAD_DIM
SSM_GROUPS = 2
GROUP_W = D_SSM // SSM_GROUPS
D_STATE = 128
D_CONV = 5
CHUNK = 128
FWD_CHUNKS = 4
BWD_CHUNKS = 4
CONV_DIM = D_SSM + 2 * SSM_GROUPS * D_STATE
DT_LANES = 2 * SSM_HEADS
DT_COPIES = 3
NA_HEADS = 16
NA_HEAD_DIM = 64
D_NA = NA_HEADS * NA_HEAD_DIM
NA_ROWS = 8
NA_COLS = 16
NA_PAIRS = NA_HEADS // 2
NA_BLOCK = 4
NA_SPAN = NA_ROWS + NA_BLOCK
NA_BKEYS = NA_SPAN * GRID_W
LOG2E = 1.4426950408889634
EPS = 1e-5
MASKED = -1e30
CONV_HALO = 2 * SUBLANES
CONV_STRIDE = CHUNK // SUBLANES + 1
VMEM_LIMIT = 56 * 1024 * 1024


def _dot(a, b):
    return jnp.dot(a, b, preferred_element_type=F32)


def _dot_nt(a, b):
    return lax.dot_general(a, b, (((1,), (1,)), ((), ())), preferred_element_type=F32)


def _split3(x):
    hi = x.astype(BF16)
    r1 = x - hi.astype(F32)
    mid = r1.astype(BF16)
    lo = (r1 - mid.astype(F32)).astype(BF16)
    return hi, mid, lo


def _dot_sel_lhs(sel, x):
    hi, mid, lo = _split3(x)
    return _dot(sel, hi) + _dot(sel, mid) + _dot(sel, lo)


def _pack3(x):
    lane = lax.broadcasted_iota(jnp.int32, x.shape, x.ndim - 1)
    hi = x.astype(BF16).astype(F32)
    r1 = x - hi
    mid = r1.astype(BF16).astype(F32)
    return jnp.where(lane < DT_LANES, hi, jnp.where(lane < 2 * DT_LANES, mid, r1 - mid)).astype(BF16)


def _softplus(x):
    return jnp.maximum(x, 0.0) + jnp.log(1.0 + jnp.exp(-jnp.abs(x)))


def _silu(x):
    half = 0.5 * x
    return half + half * jnp.tanh(half)


def _const_spec(shape):
    nd = len(shape)
    return pl.BlockSpec(shape, lambda *_: (0,) * nd, pipeline_mode=pl.Buffered(1))


O_DT = D_SSM + CONV_DIM
O_Q = O_DT + DT_LANES
O_K = O_Q + D_NA
O_V = O_K + D_NA


def _head_rmsnorm_t(xt, gain_ref):
    tokens = xt.shape[1]
    x3 = xt.reshape(NA_HEADS, NA_HEAD_DIM, tokens)
    ms = jnp.mean(x3 * x3, axis=1, keepdims=True)
    return (x3 * lax.rsqrt(ms + EPS) * gain_ref[...]).reshape(D_NA, tokens)


def _conv_silu(uext, cout, cw_ref, cb_ref, out_ref, first):
    nslab = uext.shape[0]
    for slab in range(nslab):
        cols = slice(slab * LANES, (slab + 1) * LANES)
        bias = jnp.broadcast_to(cb_ref[:, cols], (SUBLANES, LANES))
        taps = [jnp.broadcast_to(cw_ref[k:k + 1, cols], (SUBLANES, LANES)) for k in range(D_CONV)]
        for ch in range(out_ref.shape[0] // CHUNK):
            for i in range(CONV_STRIDE):
                acc = bias
                for k in range(D_CONV):
                    start = ch * CHUNK + first + k + i
                    acc = acc + taps[k] * uext[slab, pl.ds(start, SUBLANES, stride=CONV_STRIDE), :]
                cout[ch * nslab + slab, pl.ds(i, SUBLANES, stride=CONV_STRIDE), :] = _silu(acc)
            out_ref[ch * CHUNK:(ch + 1) * CHUNK, cols] = cout[ch * nslab + slab, 0:CHUNK, :]


def _inproj_body(x_ref, xp_ref, xn_ref, nw_ref, wt_ref, qg_ref, kg_ref, cw_ref, cb_ref, *refs, tiles_per_seq):
    uext, cout = refs[-2:]
    refs = refs[:-2]
    ncast = (len(refs) - 6) // 2
    cast_in, (z_ref, xc_ref, qt_ref, k_ref, vt_ref, dt_ref), cast_out = refs[:ncast], refs[ncast:ncast + 6], refs[ncast + 6:]
    for src, dst in zip(cast_in, cast_out):
        dst[...] = src[...].astype(dst.dtype)
    tm = x_ref.shape[0]
    xe = jnp.concatenate([xp_ref[...], x_ref[...], xn_ref[...]], axis=0)
    ms = jnp.mean(xe * xe, axis=-1, keepdims=True)
    he = (xe * lax.rsqrt(ms + EPS) * nw_ref[...]).astype(BF16)
    h = he[CONV_HALO:CONV_HALO + tm]

    xbc = _dot_nt(he, wt_ref[D_SSM:O_DT, :])
    pos = pl.program_id(0) % tiles_per_seq
    for slab in range(uext.shape[0]):
        cols = slice(slab * LANES, (slab + 1) * LANES)
        uext[slab, 0:CONV_HALO, :] = jnp.where(pos > 0, xbc[0:CONV_HALO, cols], 0.0)
        uext[slab, CONV_HALO:CONV_HALO + tm, :] = xbc[CONV_HALO:CONV_HALO + tm, cols]
        uext[slab, CONV_HALO + tm:, :] = jnp.where(pos < tiles_per_seq - 1, xbc[CONV_HALO + tm:, cols], 0.0)
    _conv_silu(uext, cout, cw_ref, cb_ref, xc_ref, CONV_HALO - (D_CONV - 1) // 2)
    z_ref[...] = _silu(_dot_nt(h, wt_ref[0:D_SSM, :]))

    w_dt = wt_ref[O_DT:O_Q, :]
    pad = jnp.zeros((LANES - DT_COPIES * DT_LANES, w_dt.shape[1]), w_dt.dtype)
    dt_ref[...] = _dot_nt(h, jnp.concatenate([w_dt] * DT_COPIES + [pad], axis=0))
    qt = _head_rmsnorm_t(_dot_nt(wt_ref[O_Q:O_K, :], h), qg_ref)
    kn = _head_rmsnorm_t(_dot_nt(wt_ref[O_K:O_V, :], h), kg_ref).T.astype(k_ref.dtype)
    for p in range(k_ref.shape[0]):
        k_ref[p] = kn[:, p * LANES:(p + 1) * LANES]
    vt = _dot_nt(wt_ref[O_V:, :], h)
    for j in range(vt_ref.shape[0]):
        cols = slice(j * LANES, (j + 1) * LANES)
        qt_ref[j] = qt[:, cols].astype(qt_ref.dtype)
        vt_ref[j] = vt[:, cols].astype(vt_ref.dtype)


def _inproj(x2, seq_len, norm_w, w_bf, q_gain, k_gain, conv_w, conv_b, later_weights, tm):
    n, d = x2.shape
    steps = n // tm
    row = lambda i: (i, 0)
    tiles = tm // LANES
    hpt = tm // CONV_HALO
    prev_halo = pl.BlockSpec((CONV_HALO, d), lambda i: (jnp.maximum(i * hpt - 1, 0), 0))
    next_halo = pl.BlockSpec((CONV_HALO, d), lambda i: (jnp.minimum((i + 1) * hpt, n // CONV_HALO - 1), 0))
    nslab = CONV_DIM // LANES
    tile_spec = pl.BlockSpec((tiles, D_NA, LANES), lambda i: (i, 0, 0))
    tile_shape = jax.ShapeDtypeStruct((n // LANES, D_NA, LANES), BF16)
    bf16_rows = 2 * SUBLANES
    assert all(w.shape[0] % (steps * bf16_rows) == 0 for w in later_weights)
    slabs = [pl.BlockSpec((w.shape[0] // steps, w.shape[1]), row) for w in later_weights]
    gains = [jnp.broadcast_to(g.astype(F32)[:, None], (NA_HEAD_DIM, tm)) for g in (q_gain, k_gain)]
    outs = pl.pallas_call(
        functools.partial(_inproj_body, tiles_per_seq=seq_len // tm),
        grid=(steps,),
        in_specs=[pl.BlockSpec((tm, d), row), prev_halo, next_halo, _const_spec((1, d)), _const_spec(w_bf.shape),
                  _const_spec(gains[0].shape), _const_spec(gains[1].shape),
                  _const_spec(conv_w.shape), _const_spec(conv_b.shape)] + slabs,
        out_specs=[pl.BlockSpec((tm, D_SSM), row), pl.BlockSpec((tm, CONV_DIM), row), tile_spec,
                   pl.BlockSpec((NA_PAIRS, tm, LANES), lambda i: (0, i, 0)), tile_spec,
                   pl.BlockSpec((tm, LANES), row)] + slabs,
        out_shape=[jax.ShapeDtypeStruct((n, D_SSM), F32), jax.ShapeDtypeStruct((n, CONV_DIM), F32), tile_shape,
                   jax.ShapeDtypeStruct((NA_PAIRS, n, LANES), BF16), tile_shape,
                   jax.ShapeDtypeStruct((n, LANES), F32)]
        + [jax.ShapeDtypeStruct(w.shape, BF16) for w in later_weights],
        scratch_shapes=[pltpu.VMEM((nslab, tm + 2 * CONV_HALO, LANES), F32),
                        pltpu.VMEM((tm // CHUNK * nslab, SUBLANES * CONV_STRIDE, LANES), F32)],
        compiler_params=pltpu.CompilerParams(
            dimension_semantics=("arbitrary",), vmem_limit_bytes=VMEM_LIMIT),
    )(x2, x2, x2, norm_w, w_bf, *gains, conv_w, conv_b, *later_weights)
    return outs[:6], outs[6:]


def _tri_masks():
    t = lax.broadcasted_iota(jnp.int32, (CHUNK, CHUNK), 0)
    s = lax.broadcasted_iota(jnp.int32, (CHUNK, CHUNK), 1)
    return s <= t, s >= t


def _dt_and_loga(dt_ref, dtb_ref, alog_ref):
    dtv = _softplus(dt_ref[0] + dtb_ref[...])
    return dtv, dtv * (-LOG2E * jnp.exp(alog_ref[...]))


def _ssd_bwd_body(xc_ref, dt_ref, dtb_ref, alog_ref, eb_ref, hb_ref, state):
    @pl.when(pl.program_id(1) == 0)
    def _():
        state[...] = jnp.zeros_like(state)

    _, upper = _tri_masks()
    upper_sel = jnp.where(upper, 1.0, 0.0).astype(BF16)
    for ch in reversed(range(BWD_CHUNKS)):
        rows = slice(ch * CHUNK, (ch + 1) * CHUNK)
        dtv, loga = _dt_and_loga(dt_ref.at[:, rows], dtb_ref, alog_ref)
        rcum = _dot_sel_lhs(upper_sel, loga)
        r0 = rcum[0:1, :]
        wgt = dtv * jnp.exp2(r0 - rcum)
        wgt_x = _dot(_pack3(wgt), eb_ref[...])
        dec_x = _dot(_pack3(jnp.broadcast_to(jnp.exp2(r0), (SUBLANES, LANES))), eb_ref[...])[0:1]

        hb_ref[0, ch] = state[...].astype(BF16)
        xw = (xc_ref[0, rows, 0:D_SSM] * wgt_x).astype(BF16)
        for g in range(SSM_GROUPS):
            gs = slice(g * GROUP_W, (g + 1) * GROUP_W)
            b_t = xc_ref[0, rows, D_SSM + g * D_STATE:D_SSM + (g + 1) * D_STATE].T.astype(BF16)
            state[:, gs] = state[:, gs] * dec_x[:, gs] + _dot(b_t, xw[:, gs])


def _ssd_bwd(xc, dt, dt_bias, a_log, sel_b):
    bsz, t, _ = xc.shape
    rows = BWD_CHUNKS * CHUNK
    nsteps = t // rows
    rev = lambda c: nsteps - 1 - c
    return pl.pallas_call(
        _ssd_bwd_body,
        grid=(bsz, nsteps),
        in_specs=[
            pl.BlockSpec((1, rows, D_SSM + SSM_GROUPS * D_STATE), lambda b, c: (b, rev(c), 0)),
            pl.BlockSpec((1, rows, LANES), lambda b, c: (b, rev(c), 0)),
            _const_spec(dt_bias.shape), _const_spec(a_log.shape), _const_spec(sel_b.shape),
        ],
        out_specs=pl.BlockSpec((1, BWD_CHUNKS, D_STATE, D_SSM), lambda b, c: (b, rev(c), 0, 0)),
        out_shape=jax.ShapeDtypeStruct((bsz, t // CHUNK, D_STATE, D_SSM), BF16),
        scratch_shapes=[pltpu.VMEM((D_STATE, D_SSM), F32)],
        compiler_params=pltpu.CompilerParams(
            dimension_semantics=("arbitrary", "arbitrary"), vmem_limit_bytes=VMEM_LIMIT),
    )(xc, dt, dt_bias, a_log, sel_b)


def _ssd_fwd_body(xc_ref, dt_ref, hb_ref, dtb_ref, alog_ref, dsk_ref, ef_ref, eb_ref, y_ref, state):
    @pl.when(pl.program_id(1) == 0)
    def _():
        state[...] = jnp.zeros_like(state)

    for sub in range(FWD_CHUNKS):
        rows = slice(sub * CHUNK, (sub + 1) * CHUNK)
        _ssd_fwd_chunk(xc_ref.at[0, rows], dt_ref.at[:, rows], hb_ref.at[0, sub], dtb_ref, alog_ref,
                       dsk_ref, ef_ref, eb_ref, y_ref.at[0, rows], state)


def _ssd_fwd_chunk(xc_ref, dt_ref, hb_ref, dtb_ref, alog_ref, dsk_ref, ef_ref, eb_ref, y_ref, state):
    dtv, loga = _dt_and_loga(dt_ref, dtb_ref, alog_ref)
    lower, upper = _tri_masks()
    cum_f = _dot_sel_lhs(jnp.where(lower, 1.0, 0.0).astype(BF16), loga)
    cum_b = _dot_sel_lhs(jnp.where(upper, 1.0, 0.0).astype(BF16), loga)
    lane = lax.broadcasted_iota(jnp.int32, (CHUNK, LANES), 1)
    cum = jnp.where((lane & SSM_HEADS) == 0, cum_f, cum_b)
    ecum_p = _pack3(jnp.exp2(cum))
    wgt_p = _pack3(dtv * jnp.exp2(cum_f[CHUNK - 1:CHUNK, :] - cum_f))
    src_t = (cum - jnp.log2(dtv)).T
    first_half = lane < SSM_HEAD_DIM

    for g in range(SSM_GROUPS):
        gs = slice(g * GROUP_W, (g + 1) * GROUP_W)
        b_f32 = xc_ref[:, D_SSM + g * D_STATE:D_SSM + (g + 1) * D_STATE]
        b_g = b_f32.astype(BF16)
        c0 = D_SSM + SSM_GROUPS * D_STATE + g * D_STATE
        c_g = xc_ref[:, c0:c0 + D_STATE].astype(BF16)
        cb = _dot_nt(c_g, b_g)
        ea_g = _dot(ecum_p, ef_ref[:, gs])
        er_g = _dot(ecum_p, eb_ref[:, gs])
        y_off = _dot(c_g, state[:, gs].astype(BF16)) * ea_g + _dot(c_g, hb_ref[:, gs]) * er_g
        xw = (xc_ref[:, gs] * _dot(wgt_p, ef_ref[:, gs])).astype(BF16)
        state[:, gs] = state[:, gs] * ea_g[CHUNK - 1:CHUNK, :] + _dot(b_f32.T.astype(BF16), xw)
        pairs = GROUP_W // LANES
        for pp in range(pairs):
            pr = g * pairs + pp
            ps = slice(pr * LANES, (pr + 1) * LANES)
            mats = []
            for h in (2 * pr, 2 * pr + 1):
                hb = SSM_HEADS + h
                seg_f = jnp.broadcast_to(cum[:, h:h + 1], (CHUNK, CHUNK)) - src_t[h:h + 1, :]
                seg_b = jnp.broadcast_to(cum[:, hb:hb + 1], (CHUNK, CHUNK)) - src_t[hb:hb + 1, :]
                d_f = jnp.exp2(jnp.where(lower, seg_f, MASKED))
                d_b = jnp.exp2(jnp.where(upper, seg_b, MASKED))
                mats.append(((d_f + d_b) * cb).astype(BF16))
            xs = xc_ref[:, ps]
            xp = xs.astype(BF16)
            zero = jnp.zeros_like(xp)
            rhs = jnp.concatenate([jnp.where(first_half, xp, zero), jnp.where(first_half, zero, xp)], axis=0)
            y_diag = _dot(jnp.concatenate(mats, axis=1), rhs)
            y_ref[:, ps] = y_diag + y_off[:, pp * LANES:(pp + 1) * LANES] + dsk_ref[:, ps] * xs


def _ssd_fwd(xc, dt, hb, dt_bias, a_log, d_skip_x, sel_f, sel_b):
    bsz, t, _ = xc.shape
    step = FWD_CHUNKS * CHUNK
    blk = lambda w: pl.BlockSpec((1, step, w), lambda b, c: (b, c, 0))
    return pl.pallas_call(
        _ssd_fwd_body,
        grid=(bsz, t // step),
        in_specs=[
            blk(CONV_DIM), blk(LANES),
            pl.BlockSpec((1, FWD_CHUNKS, D_STATE, D_SSM), lambda b, c: (b, c, 0, 0)),
            _const_spec(dt_bias.shape), _const_spec(a_log.shape), _const_spec(d_skip_x.shape),
            _const_spec(sel_f.shape), _const_spec(sel_b.shape),
        ],
        out_specs=blk(D_SSM),
        out_shape=jax.ShapeDtypeStruct((bsz, t, D_SSM), F32),
        scratch_shapes=[pltpu.VMEM((D_STATE, D_SSM), F32)],
        compiler_params=pltpu.CompilerParams(
            dimension_semantics=("arbitrary", "arbitrary"), vmem_limit_bytes=VMEM_LIMIT),
    )(xc, dt, hb, dt_bias, a_log, d_skip_x, sel_f, sel_b)


def _build_bias(g_ref, colmask_ref, bias):
    rel = _block_row_rel()
    half = lax.broadcasted_iota(jnp.int32, (GRID_W, LANES), 1) < GRID_W
    nrel = 2 * NA_ROWS

    @functools.cache
    def toeplitz(hh, r, upper_half):
        if upper_half:
            return pltpu.roll(toeplitz(hh, r, False), GRID_W, 1)
        row = g_ref[0, hh * nrel + r:hh * nrel + r + 1, :] * LOG2E
        return pltpu.roll(jnp.broadcast_to(row, (GRID_W, LANES)), 0, 1, stride=1, stride_axis=0)

    for kind in range(rel.shape[0]):
        for hh in range(2):
            for u in range(0, NA_BLOCK, 2):
                lane0 = (hh * NA_BLOCK + u) * GRID_W
                for a in range(NA_SPAN):
                    r_lo, r_hi = int(rel[kind, u, a]), int(rel[kind, u + 1, a])
                    lo = toeplitz(hh, r_lo, False) if r_lo >= 0 else MASKED
                    hi = toeplitz(hh, r_hi, True) if r_hi >= 0 else MASKED
                    tile = jnp.where(half, lo, hi) + colmask_ref[...]
                    bias[kind, a * GRID_W:(a + 1) * GRID_W, lane0:lane0 + 2 * GRID_W] = tile


def _na_body(qt_ref, k_ref, vt_ref, g_ref, colmask_ref, o_ref, sbuf, bias):
    t = k_ref.shape[1]
    rows = t // GRID_W

    @pl.when(pl.program_id(1) == 0)
    def _():
        _build_bias(g_ref, colmask_ref, bias)

    nblk = rows // NA_BLOCK
    bq = NA_BLOCK * GRID_W

    def offsets(bi):
        i0 = bi * NA_BLOCK
        rlo = jnp.clip(i0 - NA_ROWS // 2, 0, rows - NA_SPAN)
        return pl.multiple_of(i0 * GRID_W, bq), pl.multiple_of(rlo * GRID_W, 2 * GRID_W)

    def scores(bi, slot):
        kind = jnp.where(bi == 0, 0, jnp.where(bi == nblk - 1, 2, 1))
        _, koff = offsets(bi)
        tile0 = bi * (bq // LANES)
        q_t = jnp.concatenate([qt_ref[tile0 + i] for i in range(bq // LANES)], axis=1)
        zero = jnp.zeros((NA_HEAD_DIM, bq), q_t.dtype)
        qs_t = jnp.concatenate([jnp.concatenate([q_t[:NA_HEAD_DIM], zero], axis=0),
                                jnp.concatenate([zero, q_t[NA_HEAD_DIM:]], axis=0)], axis=1)
        sbuf[slot] = _dot(k_ref[0, pl.ds(koff, NA_BKEYS), :], qs_t) + bias[kind]

    head_a_dims = lax.broadcasted_iota(jnp.int32, (LANES, bq), 0) < NA_HEAD_DIM

    def attend(bi, slot):
        qoff, koff = offsets(bi)
        s = sbuf[slot]
        p = jnp.exp2(s - jnp.max(s, axis=0, keepdims=True))
        denom = jnp.sum(p, axis=0, keepdims=True)
        tile0 = koff // LANES
        v_t = jnp.concatenate([vt_ref[tile0 + i] for i in range(NA_BKEYS // LANES)], axis=1)
        o_t = _dot(v_t, p.astype(BF16)) * (1.0 / denom)
        o_ref[0, pl.ds(qoff, bq), :] = jnp.where(head_a_dims, o_t[:, 0:bq], o_t[:, bq:]).T.astype(o_ref.dtype)

    scores(0, 0)

    def pair_step(i, carry):
        b0 = 2 * i
        scores(b0 + 1, 1)
        attend(b0, 0)
        scores(jnp.minimum(b0 + 2, nblk - 1), 0)
        attend(b0 + 1, 1)
        return carry

    lax.fori_loop(0, nblk // 2, pair_step, 0, unroll=4)


def _natten(qt, k, vt, g, colmask, t):
    bsz = k.shape[1] // t
    nq = 2 * NA_BLOCK * GRID_W
    blk = pl.BlockSpec((1, t, LANES), lambda p, b: (p, b, 0))
    tiles = pl.BlockSpec((t // LANES, LANES, LANES), lambda p, b: (b, p, 0))
    return pl.pallas_call(
        _na_body,
        grid=(NA_PAIRS, bsz),
        in_specs=[tiles, blk, tiles,
                  pl.BlockSpec((1,) + g.shape[1:], lambda p, b: (p, 0, 0)), _const_spec(colmask.shape)],
        out_specs=blk,
        out_shape=jax.ShapeDtypeStruct(k.shape, BF16),
        scratch_shapes=[pltpu.VMEM((2, NA_BKEYS, nq), F32), pltpu.VMEM((3, NA_BKEYS, nq), F32)],
        compiler_params=pltpu.CompilerParams(
            dimension_semantics=("arbitrary", "arbitrary"), vmem_limit_bytes=VMEM_LIMIT),
    )(qt, k, vt, g, colmask)


def _block_row_rel():
    rel = -np.ones((3, NA_BLOCK, NA_SPAN), np.int64)
    for u in range(NA_BLOCK):
        for a in range(NA_SPAN):
            if a < NA_ROWS:
                rel[0, u, a] = a - u + NA_ROWS - 1
            if u <= a < u + NA_ROWS:
                rel[1, u, a] = a - u + NA_ROWS - 1 - NA_ROWS // 2
            if a >= NA_SPAN - NA_ROWS:
                rel[2, u, a] = a - u + NA_BLOCK - NA_SPAN + NA_ROWS - 1
    return rel


def _na_bias_rows(rpb):
    centre = NA_COLS - 1
    rev = rpb[:, :, ::-1]
    rows = jnp.concatenate(
        [rev[:, :, centre:], jnp.zeros(rpb.shape[:2] + (LANES - rpb.shape[2],), rpb.dtype), rev[:, :, :centre]],
        axis=2).astype(F32)
    rows = jnp.pad(rows, ((0, 0), (0, 2 * NA_ROWS - rpb.shape[1]), (0, 0)))
    return rows.reshape(NA_PAIRS, 2 * 2 * NA_ROWS, LANES)


def _na_col_mask():
    j = np.arange(GRID_W)
    c0 = np.clip(j - NA_COLS // 2, 0, GRID_W - NA_COLS)
    cc = np.arange(GRID_W)
    valid = (cc[:, None] >= c0[None, :]) & (cc[:, None] < c0[None, :] + NA_COLS)
    return jnp.asarray(np.tile(np.where(valid, 0.0, MASKED), (1, 2)), F32)


FF_STEP = 512


def _outmlp_body(x_ref, ys_ref, gate_ref, gw_ref, yn_ref, wo_ref, nw_ref, w1_ref, w2_ref, o_ref):
    dmix = ys_ref.shape[-1]
    groups = []
    for g in range(SSM_GROUPS):
        gs = slice(g * GROUP_W, (g + 1) * GROUP_W)
        gated = ys_ref[:, gs] * gate_ref[:, gs]
        gms = jnp.mean(gated * gated, axis=-1, keepdims=True)
        groups.append((gated * lax.rsqrt(gms + EPS) * gw_ref[:, gs]).astype(BF16))
    y_ssm = jnp.concatenate(groups, axis=1)
    y_na = jnp.concatenate([yn_ref[p] for p in range(yn_ref.shape[0])], axis=1)
    x1 = x_ref[...] + _dot(y_ssm, wo_ref[0:dmix, :]) + _dot(y_na, wo_ref[dmix:, :])
    ms = jnp.mean(x1 * x1, axis=-1, keepdims=True)
    h = (x1 * lax.rsqrt(ms + EPS) * nw_ref[...]).astype(BF16)
    o_ref[...] = x1
    for f in range(0, w1_ref.shape[1], FF_STEP):
        u = jnp.maximum(_dot(h, w1_ref[:, f:f + FF_STEP]), 0.0)
        o_ref[...] += _dot((u * u).astype(BF16), w2_ref[f:f + FF_STEP, :])


def _outmlp(x2, ys, gate, gate_norm_w, yn, w_out, norm_w, w1, w2, tm):
    n, d = x2.shape
    row = lambda i: (i, 0)
    return pl.pallas_call(
        _outmlp_body,
        grid=(n // tm,),
        in_specs=[
            pl.BlockSpec((tm, d), row), pl.BlockSpec((tm, ys.shape[1]), row), pl.BlockSpec((tm, gate.shape[1]), row),
            _const_spec(gate_norm_w.shape), pl.BlockSpec((yn.shape[0], tm, yn.shape[2]), lambda i: (0, i, 0)),
            _const_spec(w_out.shape), _const_spec((1, d)), _const_spec(w1.shape), _const_spec(w2.shape),
        ],
        out_specs=pl.BlockSpec((tm, d), row),
        out_shape=jax.ShapeDtypeStruct((n, d), F32),
        compiler_params=pltpu.CompilerParams(
            dimension_semantics=("arbitrary",), vmem_limit_bytes=VMEM_LIMIT),
    )(x2, ys, gate, gate_norm_w, yn, w_out, norm_w, w1, w2)


def _head_select(first_lane):
    sel = np.zeros((LANES, D_SSM), np.float32)
    for rep in range(DT_COPIES):
        for h in range(SSM_HEADS):
            sel[rep * DT_LANES + first_lane + h, h * SSM_HEAD_DIM:(h + 1) * SSM_HEAD_DIM] = 1.0
    return jnp.asarray(sel, BF16)


def _dt_lanes(fwd, bwd):
    v = jnp.tile(jnp.concatenate([fwd, bwd]).astype(F32), DT_COPIES)
    return jnp.pad(v, (0, LANES - v.shape[0]))[None, :]


def _layer(x, norm_mix_w, w_in, conv_w, conv_b, dt_bias_fwd, dt_bias_bwd, a_log_fwd, a_log_bwd, d_skip,
           ssm_norm_w, q_norm_w, k_norm_w, rel_pos_bias, w_out, norm_mlp_w, w_mlp_in, w_mlp_out, tm):
    bsz, t, d = x.shape
    n = bsz * t
    x2 = x.reshape(n, d)
    rows = t // GRID_W
    assert t % tm == 0 and tm % CHUNK == 0, (t, tm)
    assert t % (max(FWD_CHUNKS, BWD_CHUNKS) * CHUNK) == 0, t
    assert t % GRID_W == 0 and rows % (2 * NA_BLOCK) == 0 and rows >= NA_SPAN, t
    assert w_in.shape == (d, O_V + D_NA), w_in.shape

    (gate, xc, qt, k, vt, dt), (w_out_bf, w1_bf, w2_bf) = _inproj(
        x2, t, norm_mix_w[None, :], w_in.T.astype(BF16), q_norm_w.astype(F32) * (NA_HEAD_DIM ** -0.5 * LOG2E), k_norm_w,
        conv_w.astype(F32), conv_b.astype(F32)[None, :], (w_out, w_mlp_in, w_mlp_out), tm)
    r3 = lambda a: a.reshape(bsz, t, a.shape[-1])
    xc = r3(xc)

    dt_bias = _dt_lanes(dt_bias_fwd, dt_bias_bwd)
    a_log = _dt_lanes(a_log_fwd, a_log_bwd)
    sel_f, sel_b = _head_select(0), _head_select(SSM_HEADS)
    hb = _ssd_bwd(xc, r3(dt), dt_bias, a_log, sel_b)
    y_raw = _ssd_fwd(xc, r3(dt), hb, dt_bias, a_log, jnp.repeat(d_skip.astype(F32), SSM_HEAD_DIM)[None, :],
                     sel_f, sel_b)

    y_na = _natten(qt, k, vt, _na_bias_rows(rel_pos_bias), _na_col_mask(), t)

    out = _outmlp(x2, y_raw.reshape(n, D_SSM), gate, ssm_norm_w[None, :].astype(F32), y_na,
                  w_out_bf, norm_mlp_w[None, :], w1_bf, w2_bf, tm)
    return out.reshape(bsz, t, d)


def kernel(x, norm_mix_w, w_in, conv_w, conv_b, dt_bias_fwd, dt_bias_bwd, a_log_fwd, a_log_bwd, d_skip,
           ssm_norm_w, q_norm_w, k_norm_w, rel_pos_bias, w_out, norm_mlp_w, w_mlp_in, w_mlp_out):
    tm = min(512, x.shape[0] * x.shape[1])
    for layer in range(norm_mix_w.shape[0]):
        x = _layer(x, norm_mix_w[layer], w_in[layer], conv_w[layer], conv_b[layer], dt_bias_fwd[layer],
                   dt_bias_bwd[layer], a_log_fwd[layer], a_log_bwd[layer], d_skip[layer], ssm_norm_w[layer],
                   q_norm_w[layer], k_norm_w[layer], rel_pos_bias[layer], w_out[layer], norm_mlp_w[layer],
                   w_mlp_in[layer], w_mlp_out[layer], tm)
    return x
```

```python
import functools

import numpy as np
import jax
import jax.numpy as jnp
from jax import lax
from jax.experimental import pallas as pl
from jax.experimental.pallas import tpu as pltpu

F32 = jnp.float32
BF16 = jnp.bfloat16

LANES = 128
SUBLANES = 8
GRID_W = 64
SSM_HEADS = 16
SSM_HEAD_DIM = 64
D_SSM = SSM_HEADS * SSM_HEAD_DIM
SSM_GROUPS = 2
GROUP_W = D_SSM // SSM_GROUPS
D_STATE = 128
D_CONV = 5
CHUNK = 128
FWD_CHUNKS = 4
BWD_CHUNKS = 4
CONV_DIM = D_SSM + 2 * SSM_GROUPS * D_STATE
DT_LANES = 2 * SSM_HEADS
DT_COPIES = 3
NA_HEADS = 16
NA_HEAD_DIM = 64
D_NA = NA_HEADS * NA_HEAD_DIM
NA_ROWS = 8
NA_COLS = 16
NA_PAIRS = NA_HEADS // 2
NA_BLOCK = 4
NA_SPAN = NA_ROWS + NA_BLOCK
NA_BKEYS = NA_SPAN * GRID_W
LOG2E = 1.4426950408889634
EPS = 1e-5
MASKED = -1e30
CONV_HALO = 2 * SUBLANES
CONV_STRIDE = CHUNK // SUBLANES + 1
VMEM_LIMIT = 56 * 1024 * 1024


def _dot(a, b):
    return jnp.dot(a, b, preferred_element_type=F32)


def _dot_nt(a, b):
    return lax.dot_general(a, b, (((1,), (1,)), ((), ())), preferred_element_type=F32)


def _split3(x):
    hi = x.astype(BF16)
    r1 = x - hi.astype(F32)
    mid = r1.astype(BF16)
    lo = (r1 - mid.astype(F32)).astype(BF16)
    return hi, mid, lo


def _dot_sel_lhs(sel, x):
    hi, mid, lo = _split3(x)
    return _dot(sel, hi) + _dot(sel, mid) + _dot(sel, lo)


def _pack3(x):
    lane = lax.broadcasted_iota(jnp.int32, x.shape, x.ndim - 1)
    hi = x.astype(BF16).astype(F32)
    r1 = x - hi
    mid = r1.astype(BF16).astype(F32)
    return jnp.where(lane < DT_LANES, hi, jnp.where(lane < 2 * DT_LANES, mid, r1 - mid)).astype(BF16)


def _softplus(x):
    return jnp.maximum(x, 0.0) + jnp.log(1.0 + jnp.exp(-jnp.abs(x)))


def _silu(x):
    half = 0.5 * x
    return half + half * jnp.tanh(half)


def _const_spec(shape):
    nd = len(shape)
    return pl.BlockSpec(shape, lambda *_: (0,) * nd, pipeline_mode=pl.Buffered(1))


O_DT = D_SSM + CONV_DIM
O_Q = O_DT + DT_LANES
O_K = O_Q + D_NA
O_V = O_K + D_NA


def _head_rmsnorm_t(xt, gain_ref):
    tokens = xt.shape[1]
    x3 = xt.reshape(NA_HEADS, NA_HEAD_DIM, tokens)
    ms = jnp.mean(x3 * x3, axis=1, keepdims=True)
    return (x3 * lax.rsqrt(ms + EPS) * gain_ref[...]).reshape(D_NA, tokens)


def _conv_silu(uext, cout, cw_ref, cb_ref, out_ref, first):
    nslab = uext.shape[0]
    for slab in range(nslab):
        cols = slice(slab * LANES, (slab + 1) * LANES)
        bias = jnp.broadcast_to(cb_ref[:, cols], (SUBLANES, LANES))
        taps = [jnp.broadcast_to(cw_ref[k:k + 1, cols], (SUBLANES, LANES)) for k in range(D_CONV)]
        for ch in range(out_ref.shape[0] // CHUNK):
            for i in range(CONV_STRIDE):
                acc = bias
                for k in range(D_CONV):
                    start = ch * CHUNK + first + k + i
                    acc = acc + taps[k] * uext[slab, pl.ds(start, SUBLANES, stride=CONV_STRIDE), :]
                cout[ch * nslab + slab, pl.ds(i, SUBLANES, stride=CONV_STRIDE), :] = _silu(acc)
            out_ref[ch * CHUNK:(ch + 1) * CHUNK, cols] = cout[ch * nslab + slab, 0:CHUNK, :]


def _inproj_body(x_ref, xp_ref, xn_ref, nw_ref, wt_ref, qg_ref, kg_ref, cw_ref, cb_ref, *refs, tiles_per_seq):
    uext, cout = refs[-2:]
    refs = refs[:-2]
    ncast = (len(refs) - 6) // 2
    cast_in, (z_ref, xc_ref, qt_ref, k_ref, vt_ref, dt_ref), cast_out = refs[:ncast], refs[ncast:ncast + 6], refs[ncast + 6:]
    for src, dst in zip(cast_in, cast_out):
        dst[...] = src[...].astype(dst.dtype)
    tm = x_ref.shape[0]
    xe = jnp.concatenate([xp_ref[...], x_ref[...], xn_ref[...]], axis=0)
    ms = jnp.mean(xe * xe, axis=-1, keepdims=True)
    he = (xe * lax.rsqrt(ms + EPS) * nw_ref[...]).astype(BF16)
    h = he[CONV_HALO:CONV_HALO + tm]

    xbc = _dot_nt(he, wt_ref[D_SSM:O_DT, :])
    pos = pl.program_id(0) % tiles_per_seq
    for slab in range(uext.shape[0]):
        cols = slice(slab * LANES, (slab + 1) * LANES)
        uext[slab, 0:CONV_HALO, :] = jnp.where(pos > 0, xbc[0:CONV_HALO, cols], 0.0)
        uext[slab, CONV_HALO:CONV_HALO + tm, :] = xbc[CONV_HALO:CONV_HALO + tm, cols]
        uext[slab, CONV_HALO + tm:, :] = jnp.where(pos < tiles_per_seq - 1, xbc[CONV_HALO + tm:, cols], 0.0)
    _conv_silu(uext, cout, cw_ref, cb_ref, xc_ref, CONV_HALO - (D_CONV - 1) // 2)
    z_ref[...] = _silu(_dot_nt(h, wt_ref[0:D_SSM, :]))

    w_dt = wt_ref[O_DT:O_Q, :]
    pad = jnp.zeros((LANES - DT_COPIES * DT_LANES, w_dt.shape[1]), w_dt.dtype)
    dt_ref[...] = _dot_nt(h, jnp.concatenate([w_dt] * DT_COPIES + [pad], axis=0))
    qt = _head_rmsnorm_t(_dot_nt(wt_ref[O_Q:O_K, :], h), qg_ref)
    k_ref[...] = _head_rmsnorm_t(_dot_nt(wt_ref[O_K:O_V, :], h), kg_ref).T.astype(k_ref.dtype)
    vt = _dot_nt(wt_ref[O_V:, :], h)
    for j in range(vt_ref.shape[0]):
        cols = slice(j * LANES, (j + 1) * LANES)
        qt_ref[j] = qt[:, cols].astype(qt_ref.dtype)
        vt_ref[j] = vt[:, cols].astype(vt_ref.dtype)


def _inproj(x2, seq_len, norm_w, w_bf, q_gain, k_gain, conv_w, conv_b, later_weights, tm):
    n, d = x2.shape
    steps = n // tm
    row = lambda i: (i, 0)
    tiles = tm // LANES
    hpt = tm // CONV_HALO
    prev_halo = pl.BlockSpec((CONV_HALO, d), lambda i: (jnp.maximum(i * hpt - 1, 0), 0))
    next_halo = pl.BlockSpec((CONV_HALO, d), lambda i: (jnp.minimum((i + 1) * hpt, n // CONV_HALO - 1), 0))
    nslab = CONV_DIM // LANES
    tile_spec = pl.BlockSpec((tiles, D_NA, LANES), lambda i: (i, 0, 0))
    tile_shape = jax.ShapeDtypeStruct((n // LANES, D_NA, LANES), BF16)
    bf16_rows = 2 * SUBLANES
    assert all(w.shape[0] % (steps * bf16_rows) == 0 for w in later_weights)
    slabs = [pl.BlockSpec((w.shape[0] // steps, w.shape[1]), row) for w in later_weights]
    gains = [jnp.broadcast_to(g.astype(F32)[:, None], (NA_HEAD_DIM, tm)) for g in (q_gain, k_gain)]
    outs = pl.pallas_call(
        functools.partial(_inproj_body, tiles_per_seq=seq_len // tm),
        grid=(steps,),
        in_specs=[pl.BlockSpec((tm, d), row), prev_halo, next_halo, _const_spec((1, d)), _const_spec(w_bf.shape),
                  _const_spec(gains[0].shape), _const_spec(gains[1].shape),
                  _const_spec(conv_w.shape), _const_spec(conv_b.shape)] + slabs,
        out_specs=[pl.BlockSpec((tm, D_SSM), row), pl.BlockSpec((tm, CONV_DIM), row), tile_spec,
                   pl.BlockSpec((tm, D_NA), row), tile_spec, pl.BlockSpec((tm, LANES), row)] + slabs,
        out_shape=[jax.ShapeDtypeStruct((n, D_SSM), F32), jax.ShapeDtypeStruct((n, CONV_DIM), F32), tile_shape,
                   jax.ShapeDtypeStruct((n, D_NA), BF16), tile_shape, jax.ShapeDtypeStruct((n, LANES), F32)]
        + [jax.ShapeDtypeStruct(w.shape, BF16) for w in later_weights],
        scratch_shapes=[pltpu.VMEM((nslab, tm + 2 * CONV_HALO, LANES), F32),
                        pltpu.VMEM((tm // CHUNK * nslab, SUBLANES * CONV_STRIDE, LANES), F32)],
        compiler_params=pltpu.CompilerParams(
            dimension_semantics=("arbitrary",), vmem_limit_bytes=VMEM_LIMIT),
    )(x2, x2, x2, norm_w, w_bf, *gains, conv_w, conv_b, *later_weights)
    return outs[:6], outs[6:]


def _tri_masks():
    t = lax.broadcasted_iota(jnp.int32, (CHUNK, CHUNK), 0)
    s = lax.broadcasted_iota(jnp.int32, (CHUNK, CHUNK), 1)
    return s <= t, s >= t


def _dt_and_loga(dt_ref, dtb_ref, alog_ref):
    dtv = _softplus(dt_ref[0] + dtb_ref[...])
    return dtv, dtv * (-LOG2E * jnp.exp(alog_ref[...]))


def _ssd_bwd_body(xc_ref, dt_ref, dtb_ref, alog_ref, eb_ref, hb_ref, state):
    @pl.when(pl.program_id(1) == 0)
    def _():
        state[...] = jnp.zeros_like(state)

    _, upper = _tri_masks()
    upper_sel = jnp.where(upper, 1.0, 0.0).astype(BF16)
    for ch in reversed(range(BWD_CHUNKS)):
        rows = slice(ch * CHUNK, (ch + 1) * CHUNK)
        dtv, loga = _dt_and_loga(dt_ref.at[:, rows], dtb_ref, alog_ref)
        rcum = _dot_sel_lhs(upper_sel, loga)
        r0 = rcum[0:1, :]
        wgt = dtv * jnp.exp2(r0 - rcum)
        wgt_x = _dot(_pack3(wgt), eb_ref[...])
        dec_x = _dot(_pack3(jnp.broadcast_to(jnp.exp2(r0), (SUBLANES, LANES))), eb_ref[...])[0:1]

        hb_ref[0, ch] = state[...].astype(BF16)
        xw = (xc_ref[0, rows, 0:D_SSM] * wgt_x).astype(BF16)
        for g in range(SSM_GROUPS):
            gs = slice(g * GROUP_W, (g + 1) * GROUP_W)
            b_t = xc_ref[0, rows, D_SSM + g * D_STATE:D_SSM + (g + 1) * D_STATE].T.astype(BF16)
            state[:, gs] = state[:, gs] * dec_x[:, gs] + _dot(b_t, xw[:, gs])


def _ssd_bwd(xc, dt, dt_bias, a_log, sel_b):
    bsz, t, _ = xc.shape
    rows = BWD_CHUNKS * CHUNK
    nsteps = t // rows
    rev = lambda c: nsteps - 1 - c
    return pl.pallas_call(
        _ssd_bwd_body,
        grid=(bsz, nsteps),
        in_specs=[
            pl.BlockSpec((1, rows, D_SSM + SSM_GROUPS * D_STATE), lambda b, c: (b, rev(c), 0)),
            pl.BlockSpec((1, rows, LANES), lambda b, c: (b, rev(c), 0)),
            _const_spec(dt_bias.shape), _const_spec(a_log.shape), _const_spec(sel_b.shape),
        ],
        out_specs=pl.BlockSpec((1, BWD_CHUNKS, D_STATE, D_SSM), lambda b, c: (b, rev(c), 0, 0)),
        out_shape=jax.ShapeDtypeStruct((bsz, t // CHUNK, D_STATE, D_SSM), BF16),
        scratch_shapes=[pltpu.VMEM((D_STATE, D_SSM), F32)],
        compiler_params=pltpu.CompilerParams(
            dimension_semantics=("arbitrary", "arbitrary"), vmem_limit_bytes=VMEM_LIMIT),
    )(xc, dt, dt_bias, a_log, sel_b)


def _ssd_fwd_body(xc_ref, dt_ref, hb_ref, dtb_ref, alog_ref, dsk_ref, ef_ref, eb_ref, y_ref, state):
    @pl.when(pl.program_id(1) == 0)
    def _():
        state[...] = jnp.zeros_like(state)

    for sub in range(FWD_CHUNKS):
        rows = slice(sub * CHUNK, (sub + 1) * CHUNK)
        _ssd_fwd_chunk(xc_ref.at[0, rows], dt_ref.at[:, rows], hb_ref.at[0, sub], dtb_ref, alog_ref,
                       dsk_ref, ef_ref, eb_ref, y_ref.at[0, rows], state)


def _ssd_fwd_chunk(xc_ref, dt_ref, hb_ref, dtb_ref, alog_ref, dsk_ref, ef_ref, eb_ref, y_ref, state):
    dtv, loga = _dt_and_loga(dt_ref, dtb_ref, alog_ref)
    lower, upper = _tri_masks()
    cum_f = _dot_sel_lhs(jnp.where(lower, 1.0, 0.0).astype(BF16), loga)
    cum_b = _dot_sel_lhs(jnp.where(upper, 1.0, 0.0).astype(BF16), loga)
    lane = lax.broadcasted_iota(jnp.int32, (CHUNK, LANES), 1)
    cum = jnp.where((lane & SSM_HEADS) == 0, cum_f, cum_b)
    ecum_p = _pack3(jnp.exp2(cum))
    wgt_p = _pack3(dtv * jnp.exp2(cum_f[CHUNK - 1:CHUNK, :] - cum_f))
    src_t = (cum - jnp.log2(dtv)).T
    first_half = lane < SSM_HEAD_DIM

    for g in range(SSM_GROUPS):
        gs = slice(g * GROUP_W, (g + 1) * GROUP_W)
        b_f32 = xc_ref[:, D_SSM + g * D_STATE:D_SSM + (g + 1) * D_STATE]
        b_g = b_f32.astype(BF16)
        c0 = D_SSM + SSM_GROUPS * D_STATE + g * D_STATE
        c_g = xc_ref[:, c0:c0 + D_STATE].astype(BF16)
        cb = _dot_nt(c_g, b_g)
        ea_g = _dot(ecum_p, ef_ref[:, gs])
        er_g = _dot(ecum_p, eb_ref[:, gs])
        y_off = _dot(c_g, state[:, gs].astype(BF16)) * ea_g + _dot(c_g, hb_ref[:, gs]) * er_g
        xw = (xc_ref[:, gs] * _dot(wgt_p, ef_ref[:, gs])).astype(BF16)
        state[:, gs] = state[:, gs] * ea_g[CHUNK - 1:CHUNK, :] + _dot(b_f32.T.astype(BF16), xw)
        pairs = GROUP_W // LANES
        for pp in range(pairs):
            pr = g * pairs + pp
            ps = slice(pr * LANES, (pr + 1) * LANES)
            mats = []
            for h in (2 * pr, 2 * pr + 1):
                hb = SSM_HEADS + h
                seg_f = jnp.broadcast_to(cum[:, h:h + 1], (CHUNK, CHUNK)) - src_t[h:h + 1, :]
                seg_b = jnp.broadcast_to(cum[:, hb:hb + 1], (CHUNK, CHUNK)) - src_t[hb:hb + 1, :]
                d_f = jnp.exp2(jnp.where(lower, seg_f, MASKED))
                d_b = jnp.exp2(jnp.where(upper, seg_b, MASKED))
                mats.append(((d_f + d_b) * cb).astype(BF16))
            xs = xc_ref[:, ps]
            xp = xs.astype(BF16)
            zero = jnp.zeros_like(xp)
            rhs = jnp.concatenate([jnp.where(first_half, xp, zero), jnp.where(first_half, zero, xp)], axis=0)
            y_diag = _dot(jnp.concatenate(mats, axis=1), rhs)
            y_ref[:, ps] = y_diag + y_off[:, pp * LANES:(pp + 1) * LANES] + dsk_ref[:, ps] * xs


def _ssd_fwd(xc, dt, hb, dt_bias, a_log, d_skip_x, sel_f, sel_b):
    bsz, t, _ = xc.shape
    step = FWD_CHUNKS * CHUNK
    blk = lambda w: pl.BlockSpec((1, step, w), lambda b, c: (b, c, 0))
    return pl.pallas_call(
        _ssd_fwd_body,
        grid=(bsz, t // step),
        in_specs=[
            blk(CONV_DIM), blk(LANES),
            pl.BlockSpec((1, FWD_CHUNKS, D_STATE, D_SSM), lambda b, c: (b, c, 0, 0)),
            _const_spec(dt_bias.shape), _const_spec(a_log.shape), _const_spec(d_skip_x.shape),
            _const_spec(sel_f.shape), _const_spec(sel_b.shape),
        ],
        out_specs=blk(D_SSM),
        out_shape=jax.ShapeDtypeStruct((bsz, t, D_SSM), F32),
        scratch_shapes=[pltpu.VMEM((D_STATE, D_SSM), F32)],
        compiler_params=pltpu.CompilerParams(
            dimension_semantics=("arbitrary", "arbitrary"), vmem_limit_bytes=VMEM_LIMIT),
    )(xc, dt, hb, dt_bias, a_log, d_skip_x, sel_f, sel_b)


def _build_bias(g_ref, colmask_ref, bias):
    rel = _block_row_rel()
    half = lax.broadcasted_iota(jnp.int32, (GRID_W, LANES), 1) < GRID_W
    nrel = 2 * NA_ROWS

    @functools.cache
    def toeplitz(hh, r, upper_half):
        if upper_half:
            return pltpu.roll(toeplitz(hh, r, False), GRID_W, 1)
        row = g_ref[0, hh * nrel + r:hh * nrel + r + 1, :] * LOG2E
        return pltpu.roll(jnp.broadcast_to(row, (GRID_W, LANES)), 0, 1, stride=1, stride_axis=0)

    for kind in range(rel.shape[0]):
        for hh in range(2):
            for u in range(0, NA_BLOCK, 2):
                lane0 = (hh * NA_BLOCK + u) * GRID_W
                for a in range(NA_SPAN):
                    r_lo, r_hi = int(rel[kind, u, a]), int(rel[kind, u + 1, a])
                    lo = toeplitz(hh, r_lo, False) if r_lo >= 0 else MASKED
                    hi = toeplitz(hh, r_hi, True) if r_hi >= 0 else MASKED
                    tile = jnp.where(half, lo, hi) + colmask_ref[...]
                    bias[kind, a * GRID_W:(a + 1) * GRID_W, lane0:lane0 + 2 * GRID_W] = tile


def _na_body(qt_ref, k_ref, vt_ref, g_ref, colmask_ref, o_ref, sbuf, bias):
    t = k_ref.shape[1]
    rows = t // GRID_W

    @pl.when(pl.program_id(1) == 0)
    def _():
        _build_bias(g_ref, colmask_ref, bias)

    nblk = rows // NA_BLOCK
    bq = NA_BLOCK * GRID_W

    def offsets(bi):
        i0 = bi * NA_BLOCK
        rlo = jnp.clip(i0 - NA_ROWS // 2, 0, rows - NA_SPAN)
        return pl.multiple_of(i0 * GRID_W, bq), pl.multiple_of(rlo * GRID_W, 2 * GRID_W)

    def scores(bi, slot):
        kind = jnp.where(bi == 0, 0, jnp.where(bi == nblk - 1, 2, 1))
        _, koff = offsets(bi)
        tile0 = bi * (bq // LANES)
        q_t = jnp.concatenate([qt_ref[tile0 + i] for i in range(bq // LANES)], axis=1)
        zero = jnp.zeros((NA_HEAD_DIM, bq), q_t.dtype)
        qs_t = jnp.concatenate([jnp.concatenate([q_t[:NA_HEAD_DIM], zero], axis=0),
                                jnp.concatenate([zero, q_t[NA_HEAD_DIM:]], axis=0)], axis=1)
        sbuf[slot] = _dot(k_ref[0, pl.ds(koff, NA_BKEYS), :], qs_t) + bias[kind]

    head_a_dims = lax.broadcasted_iota(jnp.int32, (LANES, bq), 0) < NA_HEAD_DIM

    def attend(bi, slot):
        qoff, koff = offsets(bi)
        s = sbuf[slot]
        p = jnp.exp2(s - jnp.max(s, axis=0, keepdims=True))
        denom = jnp.sum(p, axis=0, keepdims=True)
        tile0 = koff // LANES
        v_t = jnp.concatenate([vt_ref[tile0 + i] for i in range(NA_BKEYS // LANES)], axis=1)
        o_t = _dot(v_t, p.astype(BF16)) * (1.0 / denom)
        o_ref[0, pl.ds(qoff, bq), :] = jnp.where(head_a_dims, o_t[:, 0:bq], o_t[:, bq:]).T.astype(o_ref.dtype)

    scores(0, 0)

    def pair_step(i, carry):
        b0 = 2 * i
        scores(b0 + 1, 1)
        attend(b0, 0)
        scores(jnp.minimum(b0 + 2, nblk - 1), 0)
        attend(b0 + 1, 1)
        return carry

    lax.fori_loop(0, nblk // 2, pair_step, 0, unroll=4)


def _natten(qt, k, vt, g, colmask):
    bsz, t, _ = k.shape
    nq = 2 * NA_BLOCK * GRID_W
    blk = pl.BlockSpec((1, t, LANES), lambda p, b: (b, 0, p))
    tiles = pl.BlockSpec((t // LANES, LANES, LANES), lambda p, b: (b, p, 0))
    return pl.pallas_call(
        _na_body,
        grid=(NA_PAIRS, bsz),
        in_specs=[tiles, blk, tiles,
                  pl.BlockSpec((1,) + g.shape[1:], lambda p, b: (p, 0, 0)), _const_spec(colmask.shape)],
        out_specs=blk,
        out_shape=jax.ShapeDtypeStruct((bsz, t, D_NA), BF16),
        scratch_shapes=[pltpu.VMEM((2, NA_BKEYS, nq), F32), pltpu.VMEM((3, NA_BKEYS, nq), F32)],
        compiler_params=pltpu.CompilerParams(
            dimension_semantics=("arbitrary", "arbitrary"), vmem_limit_bytes=VMEM_LIMIT),
    )(qt, k, vt, g, colmask)


def _block_row_rel():
    rel = -np.ones((3, NA_BLOCK, NA_SPAN), np.int64)
    for u in range(NA_BLOCK):
        for a in range(NA_SPAN):
            if a < NA_ROWS:
                rel[0, u, a] = a - u + NA_ROWS - 1
            if u <= a < u + NA_ROWS:
                rel[1, u, a] = a - u + NA_ROWS - 1 - NA_ROWS // 2
            if a >= NA_SPAN - NA_ROWS:
                rel[2, u, a] = a - u + NA_BLOCK - NA_SPAN + NA_ROWS - 1
    return rel


def _na_bias_rows(rpb):
    centre = NA_COLS - 1
    rev = rpb[:, :, ::-1]
    rows = jnp.concatenate(
        [rev[:, :, centre:], jnp.zeros(rpb.shape[:2] + (LANES - rpb.shape[2],), rpb.dtype), rev[:, :, :centre]],
        axis=2).astype(F32)
    rows = jnp.pad(rows, ((0, 0), (0, 2 * NA_ROWS - rpb.shape[1]), (0, 0)))
    return rows.reshape(NA_PAIRS, 2 * 2 * NA_ROWS, LANES)


def _na_col_mask():
    j = np.arange(GRID_W)
    c0 = np.clip(j - NA_COLS // 2, 0, GRID_W - NA_COLS)
    cc = np.arange(GRID_W)
    valid = (cc[:, None] >= c0[None, :]) & (cc[:, None] < c0[None, :] + NA_COLS)
    return jnp.asarray(np.tile(np.where(valid, 0.0, MASKED), (1, 2)), F32)


FF_STEP = 1024


def _outmlp_body(x_ref, ys_ref, gate_ref, gw_ref, yn_ref, wo_ref, nw_ref, w1_ref, w2_ref, o_ref):
    dmix = ys_ref.shape[-1]
    groups = []
    for g in range(SSM_GROUPS):
        gs = slice(g * GROUP_W, (g + 1) * GROUP_W)
        gated = ys_ref[:, gs] * gate_ref[:, gs]
        gms = jnp.mean(gated * gated, axis=-1, keepdims=True)
        groups.append((gated * lax.rsqrt(gms + EPS) * gw_ref[:, gs]).astype(BF16))
    y_ssm = jnp.concatenate(groups, axis=1)
    x1 = x_ref[...] + _dot(y_ssm, wo_ref[0:dmix, :]) + _dot(yn_ref[...], wo_ref[dmix:, :])
    ms = jnp.mean(x1 * x1, axis=-1, keepdims=True)
    h = (x1 * lax.rsqrt(ms + EPS) * nw_ref[...]).astype(BF16)
    o_ref[...] = x1
    for f in range(0, w1_ref.shape[1], FF_STEP):
        u = jnp.maximum(_dot(h, w1_ref[:, f:f + FF_STEP]), 0.0)
        o_ref[...] += _dot((u * u).astype(BF16), w2_ref[f:f + FF_STEP, :])


def _outmlp(x2, ys, gate, gate_norm_w, yn, w_out, norm_w, w1, w2, tm):
    n, d = x2.shape
    row = lambda i: (i, 0)
    return pl.pallas_call(
        _outmlp_body,
        grid=(n // tm,),
        in_specs=[
            pl.BlockSpec((tm, d), row), pl.BlockSpec((tm, ys.shape[1]), row), pl.BlockSpec((tm, gate.shape[1]), row),
            _const_spec(gate_norm_w.shape), pl.BlockSpec((tm, yn.shape[1]), row),
            _const_spec(w_out.shape), _const_spec((1, d)), _const_spec(w1.shape), _const_spec(w2.shape),
        ],
        out_specs=pl.BlockSpec((tm, d), row),
        out_shape=jax.ShapeDtypeStruct((n, d), F32),
        compiler_params=pltpu.CompilerParams(
            dimension_semantics=("arbitrary",), vmem_limit_bytes=VMEM_LIMIT),
    )(x2, ys, gate, gate_norm_w, yn, w_out, norm_w, w1, w2)


def _head_select(first_lane):
    sel = np.zeros((LANES, D_SSM), np.float32)
    for rep in range(DT_COPIES):
        for h in range(SSM_HEADS):
            sel[rep * DT_LANES + first_lane + h, h * SSM_HEAD_DIM:(h + 1) * SSM_HEAD_DIM] = 1.0
    return jnp.asarray(sel, BF16)


def _dt_lanes(fwd, bwd):
    v = jnp.tile(jnp.concatenate([fwd, bwd]).astype(F32), DT_COPIES)
    return jnp.pad(v, (0, LANES - v.shape[0]))[None, :]


def _layer(x, norm_mix_w, w_in, conv_w, conv_b, dt_bias_fwd, dt_bias_bwd, a_log_fwd, a_log_bwd, d_skip,
           ssm_norm_w, q_norm_w, k_norm_w, rel_pos_bias, w_out, norm_mlp_w, w_mlp_in, w_mlp_out, tm):
    bsz, t, d = x.shape
    n = bsz * t
    x2 = x.reshape(n, d)
    rows = t // GRID_W
    assert t % tm == 0 and tm % CHUNK == 0, (t, tm)
    assert t % (max(FWD_CHUNKS, BWD_CHUNKS) * CHUNK) == 0, t
    assert t % GRID_W == 0 and rows % (2 * NA_BLOCK) == 0 and rows >= NA_SPAN, t
    assert w_in.shape == (d, O_V + D_NA), w_in.shape

    (gate, xc, qt, k, vt, dt), (w_out_bf, w1_bf, w2_bf) = _inproj(
        x2, t, norm_mix_w[None, :], w_in.T.astype(BF16), q_norm_w.astype(F32) * (NA_HEAD_DIM ** -0.5 * LOG2E), k_norm_w,
        conv_w.astype(F32), conv_b.astype(F32)[None, :], (w_out, w_mlp_in, w_mlp_out), tm)
    r3 = lambda a: a.reshape(bsz, t, a.shape[-1])
    xc = r3(xc)

    dt_bias = _dt_lanes(dt_bias_fwd, dt_bias_bwd)
    a_log = _dt_lanes(a_log_fwd, a_log_bwd)
    sel_f, sel_b = _head_select(0), _head_select(SSM_HEADS)
    hb = _ssd_bwd(xc, r3(dt), dt_bias, a_log, sel_b)
    y_raw = _ssd_fwd(xc, r3(dt), hb, dt_bias, a_log, jnp.repeat(d_skip.astype(F32), SSM_HEAD_DIM)[None, :],
                     sel_f, sel_b)

    y_na = _natten(qt, r3(k), vt, _na_bias_rows(rel_pos_bias), _na_col_mask())

    out = _outmlp(x2, y_raw.reshape(n, D_SSM), gate, ssm_norm_w[None, :].astype(F32), y_na.reshape(n, D_NA),
                  w_out_bf, norm_mlp_w[None, :], w1_bf, w2_bf, tm)
    return out.reshape(bsz, t, d)


def kernel(x, norm_mix_w, w_in, conv_w, conv_b, dt_bias_fwd, dt_bias_bwd, a_log_fwd, a_log_bwd, d_skip,
           ssm_norm_w, q_norm_w, k_norm_w, rel_pos_bias, w_out, norm_mlp_w, w_mlp_in, w_mlp_out):
    tm = min(512, x.shape[0] * x.shape[1])
    for layer in range(norm_mix_w.shape[0]):
        x = _layer(x, norm_mix_w[layer], w_in[layer], conv_w[layer], conv_b[layer], dt_bias_fwd[layer],
                   dt_bias_bwd[layer], a_log_fwd[layer], a_log_bwd[layer], d_skip[layer], ssm_norm_w[layer],
                   q_norm_w[layer], k_norm_w[layer], rel_pos_bias[layer], w_out[layer], norm_mlp_w[layer],
                   w_mlp_in[layer], w_mlp_out[layer], tm)
    return x
```

```python
import functools

import numpy as np
import jax
import jax.numpy as jnp
from jax import lax
from jax.experimental import pallas as pl
from jax.experimental.pallas import tpu as pltpu

F32 = jnp.float32
BF16 = jnp.bfloat16

LANES = 128
SUBLANES = 8
GRID_W = 64
SSM_HEADS = 16
SSM_HEAD_DIM = 64
D_SSM = SSM_HEADS * SSM_HEAD_DIM
SSM_GROUPS = 2
GROUP_W = D_SSM // SSM_GROUPS
D_STATE = 128
D_CONV = 5
CHUNK = 128
FWD_CHUNKS = 4
BWD_CHUNKS = 4
CONV_DIM = D_SSM + 2 * SSM_GROUPS * D_STATE
DT_LANES = 2 * SSM_HEADS
DT_COPIES = 3
NA_HEADS = 16
NA_HEAD_DIM = 64
D_NA = NA_HEADS * NA_HEAD_DIM
NA_ROWS = 8
NA_COLS = 16
NA_PAIRS = NA_HEADS // 2
NA_BLOCK = 4
NA_SPAN = NA_ROWS + NA_BLOCK
NA_BKEYS = NA_SPAN * GRID_W
LOG2E = 1.4426950408889634
EPS = 1e-5
MASKED = -1e30
CONV_HALO = 2 * SUBLANES
CONV_STRIDE = CHUNK // SUBLANES + 1
VMEM_LIMIT = 56 * 1024 * 1024


def _dot(a, b):
    return jnp.dot(a, b, preferred_element_type=F32)


def _dot_nt(a, b):
    return lax.dot_general(a, b, (((1,), (1,)), ((), ())), preferred_element_type=F32)


def _split3(x):
    hi = x.astype(BF16)
    r1 = x - hi.astype(F32)
    mid = r1.astype(BF16)
    lo = (r1 - mid.astype(F32)).astype(BF16)
    return hi, mid, lo


def _dot_sel_lhs(sel, x):
    hi, mid, lo = _split3(x)
    return _dot(sel, hi) + _dot(sel, mid) + _dot(sel, lo)


def _pack3(x):
    lane = lax.broadcasted_iota(jnp.int32, x.shape, x.ndim - 1)
    hi = x.astype(BF16).astype(F32)
    r1 = x - hi
    mid = r1.astype(BF16).astype(F32)
    return jnp.where(lane < DT_LANES, hi, jnp.where(lane < 2 * DT_LANES, mid, r1 - mid)).astype(BF16)


def _softplus(x):
    return jnp.maximum(x, 0.0) + jnp.log(1.0 + jnp.exp(-jnp.abs(x)))


def _silu(x):
    half = 0.5 * x
    return half + half * jnp.tanh(half)


def _const_spec(shape):
    nd = len(shape)
    return pl.BlockSpec(shape, lambda *_: (0,) * nd, pipeline_mode=pl.Buffered(1))


O_DT = D_SSM + CONV_DIM
O_Q = O_DT + DT_LANES
O_K = O_Q + D_NA
O_V = O_K + D_NA


def _head_rmsnorm_t(xt, gain_ref):
    tokens = xt.shape[1]
    x3 = xt.reshape(NA_HEADS, NA_HEAD_DIM, tokens)
    ms = jnp.mean(x3 * x3, axis=1, keepdims=True)
    return (x3 * lax.rsqrt(ms + EPS) * gain_ref[...]).reshape(D_NA, tokens)


def _conv_silu(uext, cout, cw_ref, cb_ref, out_ref, first):
    nslab = uext.shape[0]
    for slab in range(nslab):
        cols = slice(slab * LANES, (slab + 1) * LANES)
        bias = jnp.broadcast_to(cb_ref[:, cols], (SUBLANES, LANES))
        taps = [jnp.broadcast_to(cw_ref[k:k + 1, cols], (SUBLANES, LANES)) for k in range(D_CONV)]
        for ch in range(out_ref.shape[0] // CHUNK):
            for i in range(CONV_STRIDE):
                acc = bias
                for k in range(D_CONV):
                    start = ch * CHUNK + first + k + i
                    acc = acc + taps[k] * uext[slab, pl.ds(start, SUBLANES, stride=CONV_STRIDE), :]
                cout[ch * nslab + slab, pl.ds(i, SUBLANES, stride=CONV_STRIDE), :] = _silu(acc)
            out_ref[ch * CHUNK:(ch + 1) * CHUNK, cols] = cout[ch * nslab + slab, 0:CHUNK, :]


def _inproj_body(x_ref, xp_ref, xn_ref, nw_ref, wt_ref, qg_ref, kg_ref, cw_ref, cb_ref, *refs, tiles_per_seq):
    uext, cout = refs[-2:]
    refs = refs[:-2]
    ncast = (len(refs) - 6) // 2
    cast_in, (z_ref, xc_ref, qt_ref, k_ref, vt_ref, dt_ref), cast_out = refs[:ncast], refs[ncast:ncast + 6], refs[ncast + 6:]
    for src, dst in zip(cast_in, cast_out):
        dst[...] = src[...].astype(dst.dtype)
    tm = x_ref.shape[0]
    xe = jnp.concatenate([xp_ref[...], x_ref[...], xn_ref[...]], axis=0)
    ms = jnp.mean(xe * xe, axis=-1, keepdims=True)
    he = (xe * lax.rsqrt(ms + EPS) * nw_ref[...]).astype(BF16)
    h = he[CONV_HALO:CONV_HALO + tm]

    xbc = _dot_nt(he, wt_ref[D_SSM:O_DT, :])
    pos = pl.program_id(0) % tiles_per_seq
    for slab in range(uext.shape[0]):
        cols = slice(slab * LANES, (slab + 1) * LANES)
        uext[slab, 0:CONV_HALO, :] = jnp.where(pos > 0, xbc[0:CONV_HALO, cols], 0.0)
        uext[slab, CONV_HALO:CONV_HALO + tm, :] = xbc[CONV_HALO:CONV_HALO + tm, cols]
        uext[slab, CONV_HALO + tm:, :] = jnp.where(pos < tiles_per_seq - 1, xbc[CONV_HALO + tm:, cols], 0.0)
    _conv_silu(uext, cout, cw_ref, cb_ref, xc_ref, CONV_HALO - (D_CONV - 1) // 2)
    z_ref[...] = _silu(_dot_nt(h, wt_ref[0:D_SSM, :]))

    w_dt = wt_ref[O_DT:O_Q, :]
    pad = jnp.zeros((LANES - DT_COPIES * DT_LANES, w_dt.shape[1]), w_dt.dtype)
    dt_ref[...] = _dot_nt(h, jnp.concatenate([w_dt] * DT_COPIES + [pad], axis=0))
    qt = _head_rmsnorm_t(_dot_nt(wt_ref[O_Q:O_K, :], h), qg_ref)
    k_ref[...] = _head_rmsnorm_t(_dot_nt(wt_ref[O_K:O_V, :], h), kg_ref).T.astype(k_ref.dtype)
    vt = _dot_nt(wt_ref[O_V:, :], h)
    for j in range(vt_ref.shape[0]):
        cols = slice(j * LANES, (j + 1) * LANES)
        qt_ref[j] = qt[:, cols].astype(qt_ref.dtype)
        vt_ref[j] = vt[:, cols].astype(vt_ref.dtype)


def _inproj(x2, seq_len, norm_w, w_bf, q_gain, k_gain, conv_w, conv_b, later_weights, tm):
    n, d = x2.shape
    steps = n // tm
    row = lambda i: (i, 0)
    tiles = tm // LANES
    hpt = tm // CONV_HALO
    prev_halo = pl.BlockSpec((CONV_HALO, d), lambda i: (jnp.maximum(i * hpt - 1, 0), 0))
    next_halo = pl.BlockSpec((CONV_HALO, d), lambda i: (jnp.minimum((i + 1) * hpt, n // CONV_HALO - 1), 0))
    nslab = CONV_DIM // LANES
    tile_spec = pl.BlockSpec((tiles, D_NA, LANES), lambda i: (i, 0, 0))
    tile_shape = jax.ShapeDtypeStruct((n // LANES, D_NA, LANES), BF16)
    bf16_rows = 2 * SUBLANES
    assert all(w.shape[0] % (steps * bf16_rows) == 0 for w in later_weights)
    slabs = [pl.BlockSpec((w.shape[0] // steps, w.shape[1]), row) for w in later_weights]
    gains = [jnp.broadcast_to(g.astype(F32)[:, None], (NA_HEAD_DIM, tm)) for g in (q_gain, k_gain)]
    outs = pl.pallas_call(
        functools.partial(_inproj_body, tiles_per_seq=seq_len // tm),
        grid=(steps,),
        in_specs=[pl.BlockSpec((tm, d), row), prev_halo, next_halo, _const_spec((1, d)), _const_spec(w_bf.shape),
                  _const_spec(gains[0].shape), _const_spec(gains[1].shape),
                  _const_spec(conv_w.shape), _const_spec(conv_b.shape)] + slabs,
        out_specs=[pl.BlockSpec((tm, D_SSM), row), pl.BlockSpec((tm, CONV_DIM), row), tile_spec,
                   pl.BlockSpec((tm, D_NA), row), tile_spec, pl.BlockSpec((tm, LANES), row)] + slabs,
        out_shape=[jax.ShapeDtypeStruct((n, D_SSM), F32), jax.ShapeDtypeStruct((n, CONV_DIM), F32), tile_shape,
                   jax.ShapeDtypeStruct((n, D_NA), BF16), tile_shape, jax.ShapeDtypeStruct((n, LANES), F32)]
        + [jax.ShapeDtypeStruct(w.shape, BF16) for w in later_weights],
        scratch_shapes=[pltpu.VMEM((nslab, tm + 2 * CONV_HALO, LANES), F32),
                        pltpu.VMEM((tm // CHUNK * nslab, SUBLANES * CONV_STRIDE, LANES), F32)],
        compiler_params=pltpu.CompilerParams(
            dimension_semantics=("arbitrary",), vmem_limit_bytes=VMEM_LIMIT),
    )(x2, x2, x2, norm_w, w_bf, *gains, conv_w, conv_b, *later_weights)
    return outs[:6], outs[6:]


def _tri_masks():
    t = lax.broadcasted_iota(jnp.int32, (CHUNK, CHUNK), 0)
    s = lax.broadcasted_iota(jnp.int32, (CHUNK, CHUNK), 1)
    return s <= t, s >= t


def _dt_and_loga(dt_ref, dtb_ref, alog_ref):
    dtv = _softplus(dt_ref[0] + dtb_ref[...])
    return dtv, dtv * (-LOG2E * jnp.exp(alog_ref[...]))


def _ssd_bwd_body(xc_ref, dt_ref, dtb_ref, alog_ref, eb_ref, hb_ref, state):
    @pl.when(pl.program_id(1) == 0)
    def _():
        state[...] = jnp.zeros_like(state)

    _, upper = _tri_masks()
    upper_sel = jnp.where(upper, 1.0, 0.0).astype(BF16)
    for ch in reversed(range(BWD_CHUNKS)):
        rows = slice(ch * CHUNK, (ch + 1) * CHUNK)
        dtv, loga = _dt_and_loga(dt_ref.at[:, rows], dtb_ref, alog_ref)
        rcum = _dot_sel_lhs(upper_sel, loga)
        r0 = rcum[0:1, :]
        wgt = dtv * jnp.exp2(r0 - rcum)
        wgt_x = _dot(_pack3(wgt), eb_ref[...])
        dec_x = _dot(_pack3(jnp.broadcast_to(jnp.exp2(r0), (SUBLANES, LANES))), eb_ref[...])[0:1]

        hb_ref[0, ch] = state[...].astype(BF16)
        xw = (xc_ref[0, rows, 0:D_SSM] * wgt_x).astype(BF16)
        for g in range(SSM_GROUPS):
            gs = slice(g * GROUP_W, (g + 1) * GROUP_W)
            b_t = xc_ref[0, rows, D_SSM + g * D_STATE:D_SSM + (g + 1) * D_STATE].T.astype(BF16)
            state[:, gs] = state[:, gs] * dec_x[:, gs] + _dot(b_t, xw[:, gs])


def _ssd_bwd(xc, dt, dt_bias, a_log, sel_b):
    bsz, t, _ = xc.shape
    rows = BWD_CHUNKS * CHUNK
    nsteps = t // rows
    rev = lambda c: nsteps - 1 - c
    return pl.pallas_call(
        _ssd_bwd_body,
        grid=(bsz, nsteps),
        in_specs=[
            pl.BlockSpec((1, rows, D_SSM + SSM_GROUPS * D_STATE), lambda b, c: (b, rev(c), 0)),
            pl.BlockSpec((1, rows, LANES), lambda b, c: (b, rev(c), 0)),
            _const_spec(dt_bias.shape), _const_spec(a_log.shape), _const_spec(sel_b.shape),
        ],
        out_specs=pl.BlockSpec((1, BWD_CHUNKS, D_STATE, D_SSM), lambda b, c: (b, rev(c), 0, 0)),
        out_shape=jax.ShapeDtypeStruct((bsz, t // CHUNK, D_STATE, D_SSM), BF16),
        scratch_shapes=[pltpu.VMEM((D_STATE, D_SSM), F32)],
        compiler_params=pltpu.CompilerParams(
            dimension_semantics=("arbitrary", "arbitrary"), vmem_limit_bytes=VMEM_LIMIT),
    )(xc, dt, dt_bias, a_log, sel_b)


def _ssd_fwd_body(xc_ref, dt_ref, hb_ref, dtb_ref, alog_ref, dsk_ref, ef_ref, eb_ref, y_ref, state):
    @pl.when(pl.program_id(1) == 0)
    def _():
        state[...] = jnp.zeros_like(state)

    for sub in range(FWD_CHUNKS):
        rows = slice(sub * CHUNK, (sub + 1) * CHUNK)
        _ssd_fwd_chunk(xc_ref.at[0, rows], dt_ref.at[:, rows], hb_ref.at[0, sub], dtb_ref, alog_ref,
                       dsk_ref, ef_ref, eb_ref, y_ref.at[0, rows], state)


def _ssd_fwd_chunk(xc_ref, dt_ref, hb_ref, dtb_ref, alog_ref, dsk_ref, ef_ref, eb_ref, y_ref, state):
    dtv, loga = _dt_and_loga(dt_ref, dtb_ref, alog_ref)
    lower, upper = _tri_masks()
    cum_f = _dot_sel_lhs(jnp.where(lower, 1.0, 0.0).astype(BF16), loga)
    cum_b = _dot_sel_lhs(jnp.where(upper, 1.0, 0.0).astype(BF16), loga)
    lane = lax.broadcasted_iota(jnp.int32, (CHUNK, LANES), 1)
    cum = jnp.where((lane & SSM_HEADS) == 0, cum_f, cum_b)
    ecum_p = _pack3(jnp.exp2(cum))
    wgt_p = _pack3(dtv * jnp.exp2(cum_f[CHUNK - 1:CHUNK, :] - cum_f))
    src_t = (cum - jnp.log2(dtv)).T
    first_half = lane < SSM_HEAD_DIM

    for g in range(SSM_GROUPS):
        gs = slice(g * GROUP_W, (g + 1) * GROUP_W)
        b_f32 = xc_ref[:, D_SSM + g * D_STATE:D_SSM + (g + 1) * D_STATE]
        b_g = b_f32.astype(BF16)
        c0 = D_SSM + SSM_GROUPS * D_STATE + g * D_STATE
        c_g = xc_ref[:, c0:c0 + D_STATE].astype(BF16)
        cb = _dot_nt(c_g, b_g)
        ea_g = _dot(ecum_p, ef_ref[:, gs])
        er_g = _dot(ecum_p, eb_ref[:, gs])
        y_off = _dot(c_g, state[:, gs].astype(BF16)) * ea_g + _dot(c_g, hb_ref[:, gs]) * er_g
        xw = (xc_ref[:, gs] * _dot(wgt_p, ef_ref[:, gs])).astype(BF16)
        state[:, gs] = state[:, gs] * ea_g[CHUNK - 1:CHUNK, :] + _dot(b_f32.T.astype(BF16), xw)
        pairs = GROUP_W // LANES
        for pp in range(pairs):
            pr = g * pairs + pp
            ps = slice(pr * LANES, (pr + 1) * LANES)
            mats = []
            for h in (2 * pr, 2 * pr + 1):
                hb = SSM_HEADS + h
                seg_f = jnp.broadcast_to(cum[:, h:h + 1], (CHUNK, CHUNK)) - src_t[h:h + 1, :]
                seg_b = jnp.broadcast_to(cum[:, hb:hb + 1], (CHUNK, CHUNK)) - src_t[hb:hb + 1, :]
                d_f = jnp.exp2(jnp.where(lower, seg_f, MASKED))
                d_b = jnp.exp2(jnp.where(upper, seg_b, MASKED))
                mats.append(((d_f + d_b) * cb).astype(BF16))
            xs = xc_ref[:, ps]
            xp = xs.astype(BF16)
            zero = jnp.zeros_like(xp)
            rhs = jnp.concatenate([jnp.where(first_half, xp, zero), jnp.where(first_half, zero, xp)], axis=0)
            y_diag = _dot(jnp.concatenate(mats, axis=1), rhs)
            y_ref[:, ps] = y_diag + y_off[:, pp * LANES:(pp + 1) * LANES] + dsk_ref[:, ps] * xs


def _ssd_fwd(xc, dt, hb, dt_bias, a_log, d_skip_x, sel_f, sel_b):
    bsz, t, _ = xc.shape
    step = FWD_CHUNKS * CHUNK
    blk = lambda w: pl.BlockSpec((1, step, w), lambda b, c: (b, c, 0))
    return pl.pallas_call(
        _ssd_fwd_body,
        grid=(bsz, t // step),
        in_specs=[
            blk(CONV_DIM), blk(LANES),
            pl.BlockSpec((1, FWD_CHUNKS, D_STATE, D_SSM), lambda b, c: (b, c, 0, 0)),
            _const_spec(dt_bias.shape), _const_spec(a_log.shape), _const_spec(d_skip_x.shape),
            _const_spec(sel_f.shape), _const_spec(sel_b.shape),
        ],
        out_specs=blk(D_SSM),
        out_shape=jax.ShapeDtypeStruct((bsz, t, D_SSM), F32),
        scratch_shapes=[pltpu.VMEM((D_STATE, D_SSM), F32)],
        compiler_params=pltpu.CompilerParams(
            dimension_semantics=("arbitrary", "arbitrary"), vmem_limit_bytes=VMEM_LIMIT),
    )(xc, dt, hb, dt_bias, a_log, d_skip_x, sel_f, sel_b)


def _build_bias(g_ref, colmask_ref, bias):
    rel = _block_row_rel()
    half = lax.broadcasted_iota(jnp.int32, (GRID_W, LANES), 1) < GRID_W
    nrel = 2 * NA_ROWS

    @functools.cache
    def toeplitz(hh, r, upper_half):
        if upper_half:
            return pltpu.roll(toeplitz(hh, r, False), GRID_W, 1)
        row = g_ref[0, hh * nrel + r:hh * nrel + r + 1, :] * LOG2E
        return pltpu.roll(jnp.broadcast_to(row, (GRID_W, LANES)), 0, 1, stride=1, stride_axis=0)

    for kind in range(rel.shape[0]):
        for hh in range(2):
            for u in range(0, NA_BLOCK, 2):
                lane0 = (hh * NA_BLOCK + u) * GRID_W
                for a in range(NA_SPAN):
                    r_lo, r_hi = int(rel[kind, u, a]), int(rel[kind, u + 1, a])
                    lo = toeplitz(hh, r_lo, False) if r_lo >= 0 else MASKED
                    hi = toeplitz(hh, r_hi, True) if r_hi >= 0 else MASKED
                    tile = jnp.where(half, lo, hi) + colmask_ref[...]
                    bias[kind, a * GRID_W:(a + 1) * GRID_W, lane0:lane0 + 2 * GRID_W] = tile


def _na_body(qt_ref, k_ref, vt_ref, g_ref, colmask_ref, o_ref, sbuf, bias):
    t = k_ref.shape[1]
    rows = t // GRID_W

    @pl.when(pl.program_id(1) == 0)
    def _():
        _build_bias(g_ref, colmask_ref, bias)

    nblk = rows // NA_BLOCK
    bq = NA_BLOCK * GRID_W

    def offsets(bi):
        i0 = bi * NA_BLOCK
        rlo = jnp.clip(i0 - NA_ROWS // 2, 0, rows - NA_SPAN)
        return pl.multiple_of(i0 * GRID_W, bq), pl.multiple_of(rlo * GRID_W, 2 * GRID_W)

    def scores(bi, slot):
        kind = jnp.where(bi == 0, 0, jnp.where(bi == nblk - 1, 2, 1))
        _, koff = offsets(bi)
        tile0 = bi * (bq // LANES)
        q_t = jnp.concatenate([qt_ref[tile0 + i] for i in range(bq // LANES)], axis=1)
        zero = jnp.zeros((NA_HEAD_DIM, bq), q_t.dtype)
        qs_t = jnp.concatenate([jnp.concatenate([q_t[:NA_HEAD_DIM], zero], axis=0),
                                jnp.concatenate([zero, q_t[NA_HEAD_DIM:]], axis=0)], axis=1)
        sbuf[slot] = _dot(k_ref[0, pl.ds(koff, NA_BKEYS), :], qs_t) + bias[kind]

    head_a_dims = lax.broadcasted_iota(jnp.int32, (LANES, bq), 0) < NA_HEAD_DIM

    def attend(bi, slot):
        qoff, koff = offsets(bi)
        s = sbuf[slot]
        p = jnp.exp2(s - jnp.max(s, axis=0, keepdims=True))
        denom = jnp.sum(p, axis=0, keepdims=True)
        tile0 = koff // LANES
        v_t = jnp.concatenate([vt_ref[tile0 + i] for i in range(NA_BKEYS // LANES)], axis=1)
        o_t = _dot(v_t, p.astype(BF16)) * (1.0 / denom)
        o_ref[0, pl.ds(qoff, bq), :] = jnp.where(head_a_dims, o_t[:, 0:bq], o_t[:, bq:]).T.astype(o_ref.dtype)

    scores(0, 0)

    def pair_step(i, carry):
        b0 = 2 * i
        scores(b0 + 1, 1)
        attend(b0, 0)
        scores(jnp.minimum(b0 + 2, nblk - 1), 0)
        attend(b0 + 1, 1)
        return carry

    lax.fori_loop(0, nblk // 2, pair_step, 0, unroll=8)


def _natten(qt, k, vt, g, colmask):
    bsz, t, _ = k.shape
    nq = 2 * NA_BLOCK * GRID_W
    blk = pl.BlockSpec((1, t, LANES), lambda p, b: (b, 0, p))
    tiles = pl.BlockSpec((t // LANES, LANES, LANES), lambda p, b: (b, p, 0))
    return pl.pallas_call(
        _na_body,
        grid=(NA_PAIRS, bsz),
        in_specs=[tiles, blk, tiles,
                  pl.BlockSpec((1,) + g.shape[1:], lambda p, b: (p, 0, 0)), _const_spec(colmask.shape)],
        out_specs=blk,
        out_shape=jax.ShapeDtypeStruct((bsz, t, D_NA), BF16),
        scratch_shapes=[pltpu.VMEM((2, NA_BKEYS, nq), F32), pltpu.VMEM((3, NA_BKEYS, nq), F32)],
        compiler_params=pltpu.CompilerParams(
            dimension_semantics=("arbitrary", "arbitrary"), vmem_limit_bytes=VMEM_LIMIT),
    )(qt, k, vt, g, colmask)


def _block_row_rel():
    rel = -np.ones((3, NA_BLOCK, NA_SPAN), np.int64)
    for u in range(NA_BLOCK):
        for a in range(NA_SPAN):
            if a < NA_ROWS:
                rel[0, u, a] = a - u + NA_ROWS - 1
            if u <= a < u + NA_ROWS:
                rel[1, u, a] = a - u + NA_ROWS - 1 - NA_ROWS // 2
            if a >= NA_SPAN - NA_ROWS:
                rel[2, u, a] = a - u + NA_BLOCK - NA_SPAN + NA_ROWS - 1
    return rel


def _na_bias_rows(rpb):
    centre = NA_COLS - 1
    rev = rpb[:, :, ::-1]
    rows = jnp.concatenate(
        [rev[:, :, centre:], jnp.zeros(rpb.shape[:2] + (LANES - rpb.shape[2],), rpb.dtype), rev[:, :, :centre]],
        axis=2).astype(F32)
    rows = jnp.pad(rows, ((0, 0), (0, 2 * NA_ROWS - rpb.shape[1]), (0, 0)))
    return rows.reshape(NA_PAIRS, 2 * 2 * NA_ROWS, LANES)


def _na_col_mask():
    j = np.arange(GRID_W)
    c0 = np.clip(j - NA_COLS // 2, 0, GRID_W - NA_COLS)
    cc = np.arange(GRID_W)
    valid = (cc[:, None] >= c0[None, :]) & (cc[:, None] < c0[None, :] + NA_COLS)
    return jnp.asarray(np.tile(np.where(valid, 0.0, MASKED), (1, 2)), F32)


FF_STEP = 512


def _outmlp_body(x_ref, ys_ref, gate_ref, gw_ref, yn_ref, wo_ref, nw_ref, w1_ref, w2_ref, o_ref):
    dmix = ys_ref.shape[-1]
    groups = []
    for g in range(SSM_GROUPS):
        gs = slice(g * GROUP_W, (g + 1) * GROUP_W)
        gated = ys_ref[:, gs] * gate_ref[:, gs]
        gms = jnp.mean(gated * gated, axis=-1, keepdims=True)
        groups.append((gated * lax.rsqrt(gms + EPS) * gw_ref[:, gs]).astype(BF16))
    y_ssm = jnp.concatenate(groups, axis=1)
    x1 = x_ref[...] + _dot(y_ssm, wo_ref[0:dmix, :]) + _dot(yn_ref[...], wo_ref[dmix:, :])
    ms = jnp.mean(x1 * x1, axis=-1, keepdims=True)
    h = (x1 * lax.rsqrt(ms + EPS) * nw_ref[...]).astype(BF16)
    o_ref[...] = x1
    for f in range(0, w1_ref.shape[1], FF_STEP):
        u = jnp.maximum(_dot(h, w1_ref[:, f:f + FF_STEP]), 0.0)
        o_ref[...] += _dot((u * u).astype(BF16), w2_ref[f:f + FF_STEP, :])


def _outmlp(x2, ys, gate, gate_norm_w, yn, w_out, norm_w, w1, w2, tm):
    n, d = x2.shape
    row = lambda i: (i, 0)
    return pl.pallas_call(
        _outmlp_body,
        grid=(n // tm,),
        in_specs=[
            pl.BlockSpec((tm, d), row), pl.BlockSpec((tm, ys.shape[1]), row), pl.BlockSpec((tm, gate.shape[1]), row),
            _const_spec(gate_norm_w.shape), pl.BlockSpec((tm, yn.shape[1]), row),
            _const_spec(w_out.shape), _const_spec((1, d)), _const_spec(w1.shape), _const_spec(w2.shape),
        ],
        out_specs=pl.BlockSpec((tm, d), row),
        out_shape=jax.ShapeDtypeStruct((n, d), F32),
        compiler_params=pltpu.CompilerParams(
            dimension_semantics=("arbitrary",), vmem_limit_bytes=VMEM_LIMIT),
    )(x2, ys, gate, gate_norm_w, yn, w_out, norm_w, w1, w2)


def _head_select(first_lane):
    sel = np.zeros((LANES, D_SSM), np.float32)
    for rep in range(DT_COPIES):
        for h in range(SSM_HEADS):
            sel[rep * DT_LANES + first_lane + h, h * SSM_HEAD_DIM:(h + 1) * SSM_HEAD_DIM] = 1.0
    return jnp.asarray(sel, BF16)


def _dt_lanes(fwd, bwd):
    v = jnp.tile(jnp.concatenate([fwd, bwd]).astype(F32), DT_COPIES)
    return jnp.pad(v, (0, LANES - v.shape[0]))[None, :]


def _layer(x, norm_mix_w, w_in, conv_w, conv_b, dt_bias_fwd, dt_bias_bwd, a_log_fwd, a_log_bwd, d_skip,
           ssm_norm_w, q_norm_w, k_norm_w, rel_pos_bias, w_out, norm_mlp_w, w_mlp_in, w_mlp_out, tm):
    bsz, t, d = x.shape
    n = bsz * t
    x2 = x.reshape(n, d)
    rows = t // GRID_W
    assert t % tm == 0 and tm % CHUNK == 0, (t, tm)
    assert t % (max(FWD_CHUNKS, BWD_CHUNKS) * CHUNK) == 0, t
    assert t % GRID_W == 0 and rows % (2 * NA_BLOCK) == 0 and rows >= NA_SPAN, t
    assert w_in.shape == (d, O_V + D_NA), w_in.shape

    (gate, xc, qt, k, vt, dt), (w_out_bf, w1_bf, w2_bf) = _inproj(
        x2, t, norm_mix_w[None, :], w_in.T.astype(BF16), q_norm_w.astype(F32) * (NA_HEAD_DIM ** -0.5 * LOG2E), k_norm_w,
        conv_w.astype(F32), conv_b.astype(F32)[None, :], (w_out, w_mlp_in, w_mlp_out), tm)
    r3 = lambda a: a.reshape(bsz, t, a.shape[-1])
    xc = r3(xc)

    dt_bias = _dt_lanes(dt_bias_fwd, dt_bias_bwd)
    a_log = _dt_lanes(a_log_fwd, a_log_bwd)
    sel_f, sel_b = _head_select(0), _head_select(SSM_HEADS)
    hb = _ssd_bwd(xc, r3(dt), dt_bias, a_log, sel_b)
    y_raw = _ssd_fwd(xc, r3(dt), hb, dt_bias, a_log, jnp.repeat(d_skip.astype(F32), SSM_HEAD_DIM)[None, :],
                     sel_f, sel_b)

    y_na = _natten(qt, r3(k), vt, _na_bias_rows(rel_pos_bias), _na_col_mask())

    out = _outmlp(x2, y_raw.reshape(n, D_SSM), gate, ssm_norm_w[None, :].astype(F32), y_na.reshape(n, D_NA),
                  w_out_bf, norm_mlp_w[None, :], w1_bf, w2_bf, tm)
    return out.reshape(bsz, t, d)


def kernel(x, norm_mix_w, w_in, conv_w, conv_b, dt_bias_fwd, dt_bias_bwd, a_log_fwd, a_log_bwd, d_skip,
           ssm_norm_w, q_norm_w, k_norm_w, rel_pos_bias, w_out, norm_mlp_w, w_mlp_in, w_mlp_out):
    tm = min(512, x.shape[0] * x.shape[1])
    for layer in range(norm_mix_w.shape[0]):
        x = _layer(x, norm_mix_w[layer], w_in[layer], conv_w[layer], conv_b[layer], dt_bias_fwd[layer],
                   dt_bias_bwd[layer], a_log_fwd[layer], a_log_bwd[layer], d_skip[layer], ssm_norm_w[layer],
                   q_norm_w[layer], k_norm_w[layer], rel_pos_bias[layer], w_out[layer], norm_mlp_w[layer],
                   w_mlp_in[layer], w_mlp_out[layer], tm)
    return x
```

```python
import functools

import numpy as np
import jax
import jax.numpy as jnp
from jax import lax
from jax.experimental import pallas as pl
from jax.experimental.pallas import tpu as pltpu

F32 = jnp.float32
BF16 = jnp.bfloat16

LANES = 128
SUBLANES = 8
GRID_W = 64
SSM_HEADS = 16
SSM_HEAD_DIM = 64
D_SSM = SSM_HEADS * SSM_HEAD_DIM
SSM_GROUPS = 2
GROUP_W = D_SSM // SSM_GROUPS
D_STATE = 128
D_CONV = 5
CHUNK = 128
FWD_CHUNKS = 4
BWD_CHUNKS = 4
CONV_DIM = D_SSM + 2 * SSM_GROUPS * D_STATE
DT_LANES = 2 * SSM_HEADS
DT_COPIES = 3
NA_HEADS = 16
NA_HEAD_DIM = 64
D_NA = NA_HEADS * NA_HEAD_DIM
NA_ROWS = 8
NA_COLS = 16
NA_PAIRS = NA_HEADS // 2
NA_BLOCK = 4
NA_SPAN = NA_ROWS + NA_BLOCK
NA_BKEYS = NA_SPAN * GRID_W
LOG2E = 1.4426950408889634
EPS = 1e-5
MASKED = -1e30
CONV_HALO = 2 * SUBLANES
CONV_STRIDE = CHUNK // SUBLANES + 1
VMEM_LIMIT = 56 * 1024 * 1024


def _dot(a, b):
    return jnp.dot(a, b, preferred_element_type=F32)


def _dot_nt(a, b):
    return lax.dot_general(a, b, (((1,), (1,)), ((), ())), preferred_element_type=F32)


def _split3(x):
    hi = x.astype(BF16)
    r1 = x - hi.astype(F32)
    mid = r1.astype(BF16)
    lo = (r1 - mid.astype(F32)).astype(BF16)
    return hi, mid, lo


def _dot_sel_lhs(sel, x):
    hi, mid, lo = _split3(x)
    return _dot(sel, hi) + _dot(sel, mid) + _dot(sel, lo)


def _pack3(x):
    lane = lax.broadcasted_iota(jnp.int32, x.shape, x.ndim - 1)
    hi = x.astype(BF16).astype(F32)
    r1 = x - hi
    mid = r1.astype(BF16).astype(F32)
    return jnp.where(lane < DT_LANES, hi, jnp.where(lane < 2 * DT_LANES, mid, r1 - mid)).astype(BF16)


def _softplus(x):
    return jnp.maximum(x, 0.0) + jnp.log(1.0 + jnp.exp(-jnp.abs(x)))


def _silu(x):
    half = 0.5 * x
    return half + half * jnp.tanh(half)


def _const_spec(shape):
    nd = len(shape)
    return pl.BlockSpec(shape, lambda *_: (0,) * nd, pipeline_mode=pl.Buffered(1))


O_DT = D_SSM + CONV_DIM
O_Q = O_DT + DT_LANES
O_K = O_Q + D_NA
O_V = O_K + D_NA


def _head_rmsnorm_t(xt, gain_ref):
    tokens = xt.shape[1]
    x3 = xt.reshape(NA_HEADS, NA_HEAD_DIM, tokens)
    ms = jnp.mean(x3 * x3, axis=1, keepdims=True)
    return (x3 * lax.rsqrt(ms + EPS) * gain_ref[...]).reshape(D_NA, tokens)


def _conv_silu(uext, cout, cw_ref, cb_ref, out_ref, first):
    nslab = uext.shape[0]
    for slab in range(nslab):
        cols = slice(slab * LANES, (slab + 1) * LANES)
        bias = jnp.broadcast_to(cb_ref[:, cols], (SUBLANES, LANES))
        taps = [jnp.broadcast_to(cw_ref[k:k + 1, cols], (SUBLANES, LANES)) for k in range(D_CONV)]
        for ch in range(out_ref.shape[0] // CHUNK):
            for i in range(CONV_STRIDE):
                acc = bias
                for k in range(D_CONV):
                    start = ch * CHUNK + first + k + i
                    acc = acc + taps[k] * uext[slab, pl.ds(start, SUBLANES, stride=CONV_STRIDE), :]
                cout[ch * nslab + slab, pl.ds(i, SUBLANES, stride=CONV_STRIDE), :] = _silu(acc)
            out_ref[ch * CHUNK:(ch + 1) * CHUNK, cols] = cout[ch * nslab + slab, 0:CHUNK, :]


def _inproj_body(x_ref, xp_ref, xn_ref, nw_ref, wt_ref, qg_ref, kg_ref, cw_ref, cb_ref, *refs, tiles_per_seq):
    uext, cout = refs[-2:]
    refs = refs[:-2]
    ncast = (len(refs) - 6) // 2
    cast_in, (z_ref, xc_ref, qt_ref, k_ref, vt_ref, dt_ref), cast_out = refs[:ncast], refs[ncast:ncast + 6], refs[ncast + 6:]
    for src, dst in zip(cast_in, cast_out):
        dst[...] = src[...].astype(dst.dtype)
    tm = x_ref.shape[0]
    xe = jnp.concatenate([xp_ref[...], x_ref[...], xn_ref[...]], axis=0)
    ms = jnp.mean(xe * xe, axis=-1, keepdims=True)
    he = (xe * lax.rsqrt(ms + EPS) * nw_ref[...]).astype(BF16)
    h = he[CONV_HALO:CONV_HALO + tm]

    xbc = _dot_nt(he, wt_ref[D_SSM:O_DT, :])
    pos = pl.program_id(0) % tiles_per_seq
    for slab in range(uext.shape[0]):
        cols = slice(slab * LANES, (slab + 1) * LANES)
        uext[slab, 0:CONV_HALO, :] = jnp.where(pos > 0, xbc[0:CONV_HALO, cols], 0.0)
        uext[slab, CONV_HALO:CONV_HALO + tm, :] = xbc[CONV_HALO:CONV_HALO + tm, cols]
        uext[slab, CONV_HALO + tm:, :] = jnp.where(pos < tiles_per_seq - 1, xbc[CONV_HALO + tm:, cols], 0.0)
    _conv_silu(uext, cout, cw_ref, cb_ref, xc_ref, CONV_HALO - (D_CONV - 1) // 2)
    z_ref[...] = _silu(_dot_nt(h, wt_ref[0:D_SSM, :]))

    w_dt = wt_ref[O_DT:O_Q, :]
    pad = jnp.zeros((LANES - DT_COPIES * DT_LANES, w_dt.shape[1]), w_dt.dtype)
    dt_ref[...] = _dot_nt(h, jnp.concatenate([w_dt] * DT_COPIES + [pad], axis=0))
    qt = _head_rmsnorm_t(_dot_nt(wt_ref[O_Q:O_K, :], h), qg_ref)
    k_ref[...] = _head_rmsnorm_t(_dot_nt(wt_ref[O_K:O_V, :], h), kg_ref).T.astype(k_ref.dtype)
    vt = _dot_nt(wt_ref[O_V:, :], h)
    for j in range(vt_ref.shape[0]):
        cols = slice(j * LANES, (j + 1) * LANES)
        qt_ref[j] = qt[:, cols].astype(qt_ref.dtype)
        vt_ref[j] = vt[:, cols].astype(vt_ref.dtype)


def _inproj(x2, seq_len, norm_w, w_bf, q_gain, k_gain, conv_w, conv_b, later_weights, tm):
    n, d = x2.shape
    steps = n // tm
    row = lambda i: (i, 0)
    tiles = tm // LANES
    hpt = tm // CONV_HALO
    prev_halo = pl.BlockSpec((CONV_HALO, d), lambda i: (jnp.maximum(i * hpt - 1, 0), 0))
    next_halo = pl.BlockSpec((CONV_HALO, d), lambda i: (jnp.minimum((i + 1) * hpt, n // CONV_HALO - 1), 0))
    nslab = CONV_DIM // LANES
    tile_spec = pl.BlockSpec((tiles, D_NA, LANES), lambda i: (i, 0, 0))
    tile_shape = jax.ShapeDtypeStruct((n // LANES, D_NA, LANES), BF16)
    bf16_rows = 2 * SUBLANES
    assert all(w.shape[0] % (steps * bf16_rows) == 0 for w in later_weights)
    slabs = [pl.BlockSpec((w.shape[0] // steps, w.shape[1]), row) for w in later_weights]
    gains = [jnp.broadcast_to(g.astype(F32)[:, None], (NA_HEAD_DIM, tm)) for g in (q_gain, k_gain)]
    outs = pl.pallas_call(
        functools.partial(_inproj_body, tiles_per_seq=seq_len // tm),
        grid=(steps,),
        in_specs=[pl.BlockSpec((tm, d), row), prev_halo, next_halo, _const_spec((1, d)), _const_spec(w_bf.shape),
                  _const_spec(gains[0].shape), _const_spec(gains[1].shape),
                  _const_spec(conv_w.shape), _const_spec(conv_b.shape)] + slabs,
        out_specs=[pl.BlockSpec((tm, D_SSM), row), pl.BlockSpec((tm, CONV_DIM), row), tile_spec,
                   pl.BlockSpec((tm, D_NA), row), tile_spec, pl.BlockSpec((tm, LANES), row)] + slabs,
        out_shape=[jax.ShapeDtypeStruct((n, D_SSM), F32), jax.ShapeDtypeStruct((n, CONV_DIM), F32), tile_shape,
                   jax.ShapeDtypeStruct((n, D_NA), BF16), tile_shape, jax.ShapeDtypeStruct((n, LANES), F32)]
        + [jax.ShapeDtypeStruct(w.shape, BF16) for w in later_weights],
        scratch_shapes=[pltpu.VMEM((nslab, tm + 2 * CONV_HALO, LANES), F32),
                        pltpu.VMEM((tm // CHUNK * nslab, SUBLANES * CONV_STRIDE, LANES), F32)],
        compiler_params=pltpu.CompilerParams(
            dimension_semantics=("arbitrary",), vmem_limit_bytes=VMEM_LIMIT),
    )(x2, x2, x2, norm_w, w_bf, *gains, conv_w, conv_b, *later_weights)
    return outs[:6], outs[6:]


def _tri_masks():
    t = lax.broadcasted_iota(jnp.int32, (CHUNK, CHUNK), 0)
    s = lax.broadcasted_iota(jnp.int32, (CHUNK, CHUNK), 1)
    return s <= t, s >= t


def _dt_and_loga(dt_ref, dtb_ref, alog_ref):
    dtv = _softplus(dt_ref[0] + dtb_ref[...])
    return dtv, dtv * (-LOG2E * jnp.exp(alog_ref[...]))


def _ssd_bwd_body(xc_ref, dt_ref, dtb_ref, alog_ref, eb_ref, hb_ref, state):
    @pl.when(pl.program_id(1) == 0)
    def _():
        state[...] = jnp.zeros_like(state)

    _, upper = _tri_masks()
    upper_sel = jnp.where(upper, 1.0, 0.0).astype(BF16)
    for ch in reversed(range(BWD_CHUNKS)):
        rows = slice(ch * CHUNK, (ch + 1) * CHUNK)
        dtv, loga = _dt_and_loga(dt_ref.at[:, rows], dtb_ref, alog_ref)
        rcum = _dot_sel_lhs(upper_sel, loga)
        r0 = rcum[0:1, :]
        wgt = dtv * jnp.exp2(r0 - rcum)
        wgt_x = _dot(_pack3(wgt), eb_ref[...])
        dec_x = _dot(_pack3(jnp.broadcast_to(jnp.exp2(r0), (SUBLANES, LANES))), eb_ref[...])[0:1]

        hb_ref[0, ch] = state[...].astype(BF16)
        xw = (xc_ref[0, rows, 0:D_SSM] * wgt_x).astype(BF16)
        for g in range(SSM_GROUPS):
            gs = slice(g * GROUP_W, (g + 1) * GROUP_W)
            b_t = xc_ref[0, rows, D_SSM + g * D_STATE:D_SSM + (g + 1) * D_STATE].T.astype(BF16)
            state[:, gs] = state[:, gs] * dec_x[:, gs] + _dot(b_t, xw[:, gs])


def _ssd_bwd(xc, dt, dt_bias, a_log, sel_b):
    bsz, t, _ = xc.shape
    rows = BWD_CHUNKS * CHUNK
    nsteps = t // rows
    rev = lambda c: nsteps - 1 - c
    return pl.pallas_call(
        _ssd_bwd_body,
        grid=(bsz, nsteps),
        in_specs=[
            pl.BlockSpec((1, rows, D_SSM + SSM_GROUPS * D_STATE), lambda b, c: (b, rev(c), 0)),
            pl.BlockSpec((1, rows, LANES), lambda b, c: (b, rev(c), 0)),
            _const_spec(dt_bias.shape), _const_spec(a_log.shape), _const_spec(sel_b.shape),
        ],
        out_specs=pl.BlockSpec((1, BWD_CHUNKS, D_STATE, D_SSM), lambda b, c: (b, rev(c), 0, 0)),
        out_shape=jax.ShapeDtypeStruct((bsz, t // CHUNK, D_STATE, D_SSM), BF16),
        scratch_shapes=[pltpu.VMEM((D_STATE, D_SSM), F32)],
        compiler_params=pltpu.CompilerParams(
            dimension_semantics=("arbitrary", "arbitrary"), vmem_limit_bytes=VMEM_LIMIT),
    )(xc, dt, dt_bias, a_log, sel_b)


def _ssd_fwd_body(xc_ref, dt_ref, hb_ref, dtb_ref, alog_ref, dsk_ref, ef_ref, eb_ref, y_ref, state):
    @pl.when(pl.program_id(1) == 0)
    def _():
        state[...] = jnp.zeros_like(state)

    for sub in range(FWD_CHUNKS):
        rows = slice(sub * CHUNK, (sub + 1) * CHUNK)
        _ssd_fwd_chunk(xc_ref.at[0, rows], dt_ref.at[:, rows], hb_ref.at[0, sub], dtb_ref, alog_ref,
                       dsk_ref, ef_ref, eb_ref, y_ref.at[0, rows], state)


def _ssd_fwd_chunk(xc_ref, dt_ref, hb_ref, dtb_ref, alog_ref, dsk_ref, ef_ref, eb_ref, y_ref, state):
    dtv, loga = _dt_and_loga(dt_ref, dtb_ref, alog_ref)
    lower, upper = _tri_masks()
    cum_f = _dot_sel_lhs(jnp.where(lower, 1.0, 0.0).astype(BF16), loga)
    cum_b = _dot_sel_lhs(jnp.where(upper, 1.0, 0.0).astype(BF16), loga)
    lane = lax.broadcasted_iota(jnp.int32, (CHUNK, LANES), 1)
    cum = jnp.where((lane & SSM_HEADS) == 0, cum_f, cum_b)
    ecum_p = _pack3(jnp.exp2(cum))
    wgt_p = _pack3(dtv * jnp.exp2(cum_f[CHUNK - 1:CHUNK, :] - cum_f))
    src_t = (cum - jnp.log2(dtv)).T
    first_half = lane < SSM_HEAD_DIM

    for g in range(SSM_GROUPS):
        gs = slice(g * GROUP_W, (g + 1) * GROUP_W)
        b_f32 = xc_ref[:, D_SSM + g * D_STATE:D_SSM + (g + 1) * D_STATE]
        b_g = b_f32.astype(BF16)
        c0 = D_SSM + SSM_GROUPS * D_STATE + g * D_STATE
        c_g = xc_ref[:, c0:c0 + D_STATE].astype(BF16)
        cb = _dot_nt(c_g, b_g)
        ea_g = _dot(ecum_p, ef_ref[:, gs])
        er_g = _dot(ecum_p, eb_ref[:, gs])
        y_off = _dot(c_g, state[:, gs].astype(BF16)) * ea_g + _dot(c_g, hb_ref[:, gs]) * er_g
        xw = (xc_ref[:, gs] * _dot(wgt_p, ef_ref[:, gs])).astype(BF16)
        state[:, gs] = state[:, gs] * ea_g[CHUNK - 1:CHUNK, :] + _dot(b_f32.T.astype(BF16), xw)
        pairs = GROUP_W // LANES
        for pp in range(pairs):
            pr = g * pairs + pp
            ps = slice(pr * LANES, (pr + 1) * LANES)
            mats = []
            for h in (2 * pr, 2 * pr + 1):
                hb = SSM_HEADS + h
                seg_f = jnp.broadcast_to(cum[:, h:h + 1], (CHUNK, CHUNK)) - src_t[h:h + 1, :]
                seg_b = jnp.broadcast_to(cum[:, hb:hb + 1], (CHUNK, CHUNK)) - src_t[hb:hb + 1, :]
                d_f = jnp.exp2(jnp.where(lower, seg_f, MASKED))
                d_b = jnp.exp2(jnp.where(upper, seg_b, MASKED))
                mats.append(((d_f + d_b) * cb).astype(BF16))
            xs = xc_ref[:, ps]
            xp = xs.astype(BF16)
            zero = jnp.zeros_like(xp)
            rhs = jnp.concatenate([jnp.where(first_half, xp, zero), jnp.where(first_half, zero, xp)], axis=0)
            y_diag = _dot(jnp.concatenate(mats, axis=1), rhs)
            y_ref[:, ps] = y_diag + y_off[:, pp * LANES:(pp + 1) * LANES] + dsk_ref[:, ps] * xs


def _ssd_fwd(xc, dt, hb, dt_bias, a_log, d_skip_x, sel_f, sel_b):
    bsz, t, _ = xc.shape
    step = FWD_CHUNKS * CHUNK
    blk = lambda w: pl.BlockSpec((1, step, w), lambda b, c: (b, c, 0))
    return pl.pallas_call(
        _ssd_fwd_body,
        grid=(bsz, t // step),
        in_specs=[
            blk(CONV_DIM), blk(LANES),
            pl.BlockSpec((1, FWD_CHUNKS, D_STATE, D_SSM), lambda b, c: (b, c, 0, 0)),
            _const_spec(dt_bias.shape), _const_spec(a_log.shape), _const_spec(d_skip_x.shape),
            _const_spec(sel_f.shape), _const_spec(sel_b.shape),
        ],
        out_specs=blk(D_SSM),
        out_shape=jax.ShapeDtypeStruct((bsz, t, D_SSM), F32),
        scratch_shapes=[pltpu.VMEM((D_STATE, D_SSM), F32)],
        compiler_params=pltpu.CompilerParams(
            dimension_semantics=("arbitrary", "arbitrary"), vmem_limit_bytes=VMEM_LIMIT),
    )(xc, dt, hb, dt_bias, a_log, d_skip_x, sel_f, sel_b)


def _build_bias(g_ref, colmask_ref, bias):
    rel = _block_row_rel()
    half = lax.broadcasted_iota(jnp.int32, (GRID_W, LANES), 1) < GRID_W
    nrel = 2 * NA_ROWS

    @functools.cache
    def toeplitz(hh, r, upper_half):
        if upper_half:
            return pltpu.roll(toeplitz(hh, r, False), GRID_W, 1)
        row = g_ref[0, hh * nrel + r:hh * nrel + r + 1, :] * LOG2E
        return pltpu.roll(jnp.broadcast_to(row, (GRID_W, LANES)), 0, 1, stride=1, stride_axis=0)

    for kind in range(rel.shape[0]):
        for hh in range(2):
            for u in range(0, NA_BLOCK, 2):
                lane0 = (hh * NA_BLOCK + u) * GRID_W
                for a in range(NA_SPAN):
                    r_lo, r_hi = int(rel[kind, u, a]), int(rel[kind, u + 1, a])
                    lo = toeplitz(hh, r_lo, False) if r_lo >= 0 else MASKED
                    hi = toeplitz(hh, r_hi, True) if r_hi >= 0 else MASKED
                    tile = jnp.where(half, lo, hi) + colmask_ref[...]
                    bias[kind, a * GRID_W:(a + 1) * GRID_W, lane0:lane0 + 2 * GRID_W] = tile


def _na_body(qt_ref, k_ref, vt_ref, g_ref, colmask_ref, o_ref, sbuf, bias):
    t = k_ref.shape[1]
    rows = t // GRID_W

    @pl.when(pl.program_id(1) == 0)
    def _():
        _build_bias(g_ref, colmask_ref, bias)

    nblk = rows // NA_BLOCK
    bq = NA_BLOCK * GRID_W

    def offsets(bi):
        i0 = bi * NA_BLOCK
        rlo = jnp.clip(i0 - NA_ROWS // 2, 0, rows - NA_SPAN)
        return pl.multiple_of(i0 * GRID_W, bq), pl.multiple_of(rlo * GRID_W, 2 * GRID_W)

    def scores(bi, slot):
        kind = jnp.where(bi == 0, 0, jnp.where(bi == nblk - 1, 2, 1))
        _, koff = offsets(bi)
        tile0 = bi * (bq // LANES)
        q_t = jnp.concatenate([qt_ref[tile0 + i] for i in range(bq // LANES)], axis=1)
        zero = jnp.zeros((NA_HEAD_DIM, bq), q_t.dtype)
        qs_t = jnp.concatenate([jnp.concatenate([q_t[:NA_HEAD_DIM], zero], axis=0),
                                jnp.concatenate([zero, q_t[NA_HEAD_DIM:]], axis=0)], axis=1)
        sbuf[slot] = _dot(k_ref[0, pl.ds(koff, NA_BKEYS), :], qs_t) + bias[kind]

    def attend(bi, slot):
        qoff, koff = offsets(bi)
        s = sbuf[slot]
        p = jnp.exp2(s - jnp.max(s, axis=0, keepdims=True))
        denom = jnp.sum(p, axis=0, keepdims=True)
        tile0 = koff // LANES
        v_t = jnp.concatenate([vt_ref[tile0 + i] for i in range(NA_BKEYS // LANES)], axis=1)
        pw = p.astype(BF16)
        rden = 1.0 / denom
        o_a = _dot(v_t[:NA_HEAD_DIM], pw[:, :bq]) * rden[:, :bq]
        o_b = _dot(v_t[NA_HEAD_DIM:], pw[:, bq:]) * rden[:, bq:]
        o_ref[0, pl.ds(qoff, bq), :] = jnp.concatenate([o_a, o_b], axis=0).T.astype(o_ref.dtype)

    scores(0, 0)

    def pair_step(i, carry):
        b0 = 2 * i
        scores(b0 + 1, 1)
        attend(b0, 0)
        scores(jnp.minimum(b0 + 2, nblk - 1), 0)
        attend(b0 + 1, 1)
        return carry

    lax.fori_loop(0, nblk // 2, pair_step, 0, unroll=8)


def _natten(qt, k, vt, g, colmask):
    bsz, t, _ = k.shape
    nq = 2 * NA_BLOCK * GRID_W
    blk = pl.BlockSpec((1, t, LANES), lambda p, b: (b, 0, p))
    tiles = pl.BlockSpec((t // LANES, LANES, LANES), lambda p, b: (b, p, 0))
    return pl.pallas_call(
        _na_body,
        grid=(NA_PAIRS, bsz),
        in_specs=[tiles, blk, tiles,
                  pl.BlockSpec((1,) + g.shape[1:], lambda p, b: (p, 0, 0)), _const_spec(colmask.shape)],
        out_specs=blk,
        out_shape=jax.ShapeDtypeStruct((bsz, t, D_NA), BF16),
        scratch_shapes=[pltpu.VMEM((2, NA_BKEYS, nq), F32), pltpu.VMEM((3, NA_BKEYS, nq), F32)],
        compiler_params=pltpu.CompilerParams(
            dimension_semantics=("arbitrary", "arbitrary"), vmem_limit_bytes=VMEM_LIMIT),
    )(qt, k, vt, g, colmask)


def _block_row_rel():
    rel = -np.ones((3, NA_BLOCK, NA_SPAN), np.int64)
    for u in range(NA_BLOCK):
        for a in range(NA_SPAN):
            if a < NA_ROWS:
                rel[0, u, a] = a - u + NA_ROWS - 1
            if u <= a < u + NA_ROWS:
                rel[1, u, a] = a - u + NA_ROWS - 1 - NA_ROWS // 2
            if a >= NA_SPAN - NA_ROWS:
                rel[2, u, a] = a - u + NA_BLOCK - NA_SPAN + NA_ROWS - 1
    return rel


def _na_bias_rows(rpb):
    centre = NA_COLS - 1
    rev = rpb[:, :, ::-1]
    rows = jnp.concatenate(
        [rev[:, :, centre:], jnp.zeros(rpb.shape[:2] + (LANES - rpb.shape[2],), rpb.dtype), rev[:, :, :centre]],
        axis=2).astype(F32)
    rows = jnp.pad(rows, ((0, 0), (0, 2 * NA_ROWS - rpb.shape[1]), (0, 0)))
    return rows.reshape(NA_PAIRS, 2 * 2 * NA_ROWS, LANES)


def _na_col_mask():
    j = np.arange(GRID_W)
    c0 = np.clip(j - NA_COLS // 2, 0, GRID_W - NA_COLS)
    cc = np.arange(GRID_W)
    valid = (cc[:, None] >= c0[None, :]) & (cc[:, None] < c0[None, :] + NA_COLS)
    return jnp.asarray(np.tile(np.where(valid, 0.0, MASKED), (1, 2)), F32)


FF_STEP = 512


def _outmlp_body(x_ref, ys_ref, gate_ref, gw_ref, yn_ref, wo_ref, nw_ref, w1_ref, w2_ref, o_ref):
    dmix = ys_ref.shape[-1]
    groups = []
    for g in range(SSM_GROUPS):
        gs = slice(g * GROUP_W, (g + 1) * GROUP_W)
        gated = ys_ref[:, gs] * gate_ref[:, gs]
        gms = jnp.mean(gated * gated, axis=-1, keepdims=True)
        groups.append((gated * lax.rsqrt(gms + EPS) * gw_ref[:, gs]).astype(BF16))
    y_ssm = jnp.concatenate(groups, axis=1)
    x1 = x_ref[...] + _dot(y_ssm, wo_ref[0:dmix, :]) + _dot(yn_ref[...], wo_ref[dmix:, :])
    ms = jnp.mean(x1 * x1, axis=-1, keepdims=True)
    h = (x1 * lax.rsqrt(ms + EPS) * nw_ref[...]).astype(BF16)
    o_ref[...] = x1
    for f in range(0, w1_ref.shape[1], FF_STEP):
        u = jnp.maximum(_dot(h, w1_ref[:, f:f + FF_STEP]), 0.0)
        o_ref[...] += _dot((u * u).astype(BF16), w2_ref[f:f + FF_STEP, :])


def _outmlp(x2, ys, gate, gate_norm_w, yn, w_out, norm_w, w1, w2, tm):
    n, d = x2.shape
    row = lambda i: (i, 0)
    return pl.pallas_call(
        _outmlp_body,
        grid=(n // tm,),
        in_specs=[
            pl.BlockSpec((tm, d), row), pl.BlockSpec((tm, ys.shape[1]), row), pl.BlockSpec((tm, gate.shape[1]), row),
            _const_spec(gate_norm_w.shape), pl.BlockSpec((tm, yn.shape[1]), row),
            _const_spec(w_out.shape), _const_spec((1, d)), _const_spec(w1.shape), _const_spec(w2.shape),
        ],
        out_specs=pl.BlockSpec((tm, d), row),
        out_shape=jax.ShapeDtypeStruct((n, d), F32),
        compiler_params=pltpu.CompilerParams(
            dimension_semantics=("arbitrary",), vmem_limit_bytes=VMEM_LIMIT),
    )(x2, ys, gate, gate_norm_w, yn, w_out, norm_w, w1, w2)


def _head_select(first_lane):
    sel = np.zeros((LANES, D_SSM), np.float32)
    for rep in range(DT_COPIES):
        for h in range(SSM_HEADS):
            sel[rep * DT_LANES + first_lane + h, h * SSM_HEAD_DIM:(h + 1) * SSM_HEAD_DIM] = 1.0
    return jnp.asarray(sel, BF16)


def _dt_lanes(fwd, bwd):
    v = jnp.tile(jnp.concatenate([fwd, bwd]).astype(F32), DT_COPIES)
    return jnp.pad(v, (0, LANES - v.shape[0]))[None, :]


def _layer(x, norm_mix_w, w_in, conv_w, conv_b, dt_bias_fwd, dt_bias_bwd, a_log_fwd, a_log_bwd, d_skip,
           ssm_norm_w, q_norm_w, k_norm_w, rel_pos_bias, w_out, norm_mlp_w, w_mlp_in, w_mlp_out, tm):
    bsz, t, d = x.shape
    n = bsz * t
    x2 = x.reshape(n, d)
    rows = t // GRID_W
    assert t % tm == 0 and tm % CHUNK == 0, (t, tm)
    assert t % (max(FWD_CHUNKS, BWD_CHUNKS) * CHUNK) == 0, t
    assert t % GRID_W == 0 and rows % (2 * NA_BLOCK) == 0 and rows >= NA_SPAN, t
    assert w_in.shape == (d, O_V + D_NA), w_in.shape

    (gate, xc, qt, k, vt, dt), (w_out_bf, w1_bf, w2_bf) = _inproj(
        x2, t, norm_mix_w[None, :], w_in.T.astype(BF16), q_norm_w.astype(F32) * (NA_HEAD_DIM ** -0.5 * LOG2E), k_norm_w,
        conv_w.astype(F32), conv_b.astype(F32)[None, :], (w_out, w_mlp_in, w_mlp_out), tm)
    r3 = lambda a: a.reshape(bsz, t, a.shape[-1])
    xc = r3(xc)

    dt_bias = _dt_lanes(dt_bias_fwd, dt_bias_bwd)
    a_log = _dt_lanes(a_log_fwd, a_log_bwd)
    sel_f, sel_b = _head_select(0), _head_select(SSM_HEADS)
    hb = _ssd_bwd(xc, r3(dt), dt_bias, a_log, sel_b)
    y_raw = _ssd_fwd(xc, r3(dt), hb, dt_bias, a_log, jnp.repeat(d_skip.astype(F32), SSM_HEAD_DIM)[None, :],
                     sel_f, sel_b)

    y_na = _natten(qt, r3(k), vt, _na_bias_rows(rel_pos_bias), _na_col_mask())

    out = _outmlp(x2, y_raw.reshape(n, D_SSM), gate, ssm_norm_w[None, :].astype(F32), y_na.reshape(n, D_NA),
                  w_out_bf, norm_mlp_w[None, :], w1_bf, w2_bf, tm)
    return out.reshape(bsz, t, d)


def kernel(x, norm_mix_w, w_in, conv_w, conv_b, dt_bias_fwd, dt_bias_bwd, a_log_fwd, a_log_bwd, d_skip,
           ssm_norm_w, q_norm_w, k_norm_w, rel_pos_bias, w_out, norm_mlp_w, w_mlp_in, w_mlp_out):
    tm = min(512, x.shape[0] * x.shape[1])
    for layer in range(norm_mix_w.shape[0]):
        x = _layer(x, norm_mix_w[layer], w_in[layer], conv_w[layer], conv_b[layer], dt_bias_fwd[layer],
                   dt_bias_bwd[layer], a_log_fwd[layer], a_log_bwd[layer], d_skip[layer], ssm_norm_w[layer],
                   q_norm_w[layer], k_norm_w[layer], rel_pos_bias[layer], w_out[layer], norm_mlp_w[layer],
                   w_mlp_in[layer], w_mlp_out[layer], tm)
    return x
```
